```python
import jax, jax.numpy as jnp
from jax import lax
import numpy as np

D_MODEL = 1024
BATCH = 8
SEQ = 4096
DEPTH = 2

GRID_W = 64
CTX_LEN = 256
N_EVEN = (DEPTH + 1) // 2
N_ODD = DEPTH // 2
N_MOD = 6
EPS = 1e-6

FOURIER_HEADS = 4
FOURIER_HEAD_DIM = D_MODEL // 8
FOURIER_WIDTH = FOURIER_HEADS * FOURIER_HEAD_DIM
CONV_HEADS = 4
CONV_WIDTH = D_MODEL // 2
CONV_K = 3
EVEN_IN = FOURIER_WIDTH + 3 * CONV_WIDTH
EVEN_MIX = FOURIER_WIDTH + CONV_WIDTH

POOL_WINDOWS = (2, 4, 8, 16)
POOL_GROUPS = 4
POOL_GROUP_DIM = D_MODEL // 8
POOL_WIDTH = POOL_GROUPS * POOL_GROUP_DIM
HEAD_DIM = 64
N_Q_HEADS = (D_MODEL // 2) // HEAD_DIM
N_KV_HEADS = N_Q_HEADS // 4
Q_PER_KV = N_Q_HEADS // N_KV_HEADS
ATTN_WIDTH = N_Q_HEADS * HEAD_DIM
KV_WIDTH = N_KV_HEADS * HEAD_DIM
ODD_IN = POOL_WIDTH + ATTN_WIDTH + 2 * KV_WIDTH
ODD_MIX = POOL_WIDTH + ATTN_WIDTH
WINDOW = 128
BLOCK = 128
ROPE_THETA = 10000.0

N_EXPERTS = 16
N_GROUPS = 4
EXPERTS_PER_GROUP = N_EXPERTS // N_GROUPS
TOP_K = 2
D_EXPERT = D_MODEL // 2

kernel_name = "hybrid_fourier_conv_pool_swa_moe_dit"


def rmsnorm(x, g):
    xf = x.astype(jnp.float32)
    y = xf * lax.rsqrt(jnp.mean(xf * xf, axis=-1, keepdims=True) + EPS)
    return (y * g.astype(jnp.float32)).astype(x.dtype)


def modulate(h, shift, scale):
    return h * (1 + scale) + shift


def ada_mod(s, w, b):
    return jnp.split(s @ w + b, N_MOD, axis=-1)


def fourier_mix(u):
    f = jnp.fft.fft2(u.astype(jnp.float32), axes=(1, 3), norm="ortho")
    return jnp.real(f).astype(u.dtype)


def short_conv(u, w):
    n = u.shape[1]
    p = CONV_K // 2
    up = jnp.pad(u, ((0, 0), (p, p), (0, 0)))
    out = up[:, 0:n] * w[0]
    for k in range(1, CONV_K):
        out = out + up[:, k:k + n] * w[k]
    return out


def even_mixer(h, w_in, conv_w, w_out):
    b, n, _ = h.shape
    proj = h @ w_in
    u_f, g_b, g_c, u_c = jnp.split(
        proj, [FOURIER_WIDTH, FOURIER_WIDTH + CONV_WIDTH, FOURIER_WIDTH + 2 * CONV_WIDTH], axis=-1)
    y_f = fourier_mix(u_f.reshape(b, n, FOURIER_HEADS, FOURIER_HEAD_DIM)).reshape(b, n, FOURIER_WIDTH)
    y_c = g_b * short_conv(g_c * u_c, conv_w)
    return jnp.concatenate([y_f, y_c], axis=-1) @ w_out


def multiscale_pool(u):
    n = u.shape[1]
    uf = u.astype(jnp.float32)
    cs = jnp.pad(jnp.cumsum(uf, axis=1), ((0, 0), (1, 0), (0, 0)))
    t = jnp.arange(n)
    outs = []
    for gi, w in enumerate(POOL_WINDOWS):
        r = w // 2
        lo = jnp.clip(t - r, 0, n)
        hi = jnp.clip(t + r + 1, 0, n)
        sl = slice(gi * POOL_GROUP_DIM, (gi + 1) * POOL_GROUP_DIM)
        seg = cs[:, :, sl]
        cnt = (hi - lo).astype(jnp.float32)[None, :, None]
        outs.append((seg[:, hi] - seg[:, lo]) / cnt - uf[:, :, sl])
    return jnp.concatenate(outs, axis=-1).astype(u.dtype)


def pool_branch(u, pool_w, pool_scale):
    b, n, _ = u.shape
    p = multiscale_pool(u).reshape(b, n, POOL_GROUPS, POOL_GROUP_DIM)
    y = jnp.einsum('bngc,gcd->bngd', p, pool_w).reshape(b, n, POOL_WIDTH)
    return y * pool_scale


def axial_rope_tables(row_ids, col_ids):
    quarter = HEAD_DIM // 4
    inv = ROPE_THETA ** (-jnp.arange(quarter, dtype=jnp.float32) / quarter)
    ang = jnp.stack([row_ids.astype(jnp.float32)[:, None] * inv,
                     col_ids.astype(jnp.float32)[:, None] * inv], axis=1)
    return jnp.cos(ang), jnp.sin(ang)


def apply_rope(x, cos, sin):
    xf = x.astype(jnp.float32).reshape(*x.shape[:-1], 2, 2, HEAD_DIM // 4)
    x1, x2 = xf[..., 0, :], xf[..., 1, :]
    c = cos[None, :, None]
    s = sin[None, :, None]
    out = jnp.stack([x1 * c - x2 * s, x2 * c + x1 * s], axis=-2)
    return out.reshape(x.shape).astype(x.dtype)


def latent_attention(q, k, v, k_ctx, v_ctx, sink):
    b, n, _, _ = q.shape
    nb = n // BLOCK
    span = 3 * BLOCK
    scale = HEAD_DIM ** -0.5
    qb = q.reshape(b, nb, BLOCK, N_KV_HEADS, Q_PER_KV, HEAD_DIM)
    pad = ((0, 0), (BLOCK, BLOCK), (0, 0), (0, 0))
    kp, vp = jnp.pad(k, pad), jnp.pad(v, pad)

    def band(z):
        return jnp.concatenate(
            [z[:, i * BLOCK:i * BLOCK + n].reshape(b, nb, BLOCK, N_KV_HEADS, HEAD_DIM) for i in range(3)],
            axis=2)

    kb, vb = band(kp), band(vp)
    s_lat = jnp.einsum('bnqkgd,bnskd->bnkgqs', qb, kb).astype(jnp.float32) * scale
    qi = jnp.arange(BLOCK)[:, None]
    kj = jnp.arange(span)[None, :]
    rel = kj - BLOCK - qi
    key_pos = jnp.arange(nb)[:, None, None] * BLOCK - BLOCK + kj[None]
    valid = (jnp.abs(rel) <= WINDOW)[None] & (key_pos >= 0) & (key_pos < n)
    s_lat = jnp.where(valid[None, :, None, None], s_lat, -jnp.inf)
    s_ctx = jnp.einsum('bnqkgd,blkd->bnkgql', qb, k_ctx).astype(jnp.float32) * scale
    sink_b = jnp.broadcast_to(
        sink.astype(jnp.float32).reshape(N_KV_HEADS, Q_PER_KV)[None, None, :, :, None, None],
        (b, nb, N_KV_HEADS, Q_PER_KV, BLOCK, 1))
    p = jax.nn.softmax(jnp.concatenate([s_lat, s_ctx, sink_b], axis=-1), axis=-1)
    l = k_ctx.shape[1]
    p_lat = p[..., :span].astype(v.dtype)
    p_ctx = p[..., span:span + l].astype(v.dtype)
    o = (jnp.einsum('bnkgqs,bnskd->bnqkgd', p_lat, vb)
         + jnp.einsum('bnkgql,blkd->bnqkgd', p_ctx, v_ctx))
    return o.reshape(b, n, ATTN_WIDTH)


def context_attention(q, k, v, sink):
    b, l, _, _ = q.shape
    scale = HEAD_DIM ** -0.5
    qg = q.reshape(b, l, N_KV_HEADS, Q_PER_KV, HEAD_DIM)
    s = jnp.einsum('blkgd,bmkd->bkglm', qg, k).astype(jnp.float32) * scale
    sink_b = jnp.broadcast_to(
        sink.astype(jnp.float32).reshape(N_KV_HEADS, Q_PER_KV)[None, :, :, None, None],
        (b, N_KV_HEADS, Q_PER_KV, l, 1))
    p = jax.nn.softmax(jnp.concatenate([s, sink_b], axis=-1), axis=-1)[..., :l].astype(v.dtype)
    o = jnp.einsum('bkglm,bmkd->blkgd', p, v)
    return o.reshape(b, l, ATTN_WIDTH)


def odd_mixer(a, ac, cos, sin, w_in, pool_w, pool_scale, sink, w_out, need_ctx):
    b, n, _ = a.shape
    l = ac.shape[1]
    kv0 = POOL_WIDTH + ATTN_WIDTH
    proj = a @ w_in
    u_pool = proj[..., :POOL_WIDTH]
    q = apply_rope(proj[..., POOL_WIDTH:kv0].reshape(b, n, N_Q_HEADS, HEAD_DIM), cos, sin)
    k = apply_rope(proj[..., kv0:kv0 + KV_WIDTH].reshape(b, n, N_KV_HEADS, HEAD_DIM), cos, sin)
    v = proj[..., kv0 + KV_WIDTH:].reshape(b, n, N_KV_HEADS, HEAD_DIM)
    if need_ctx:
        proj_c = ac @ w_in
        kv_c = proj_c[..., kv0:]
    else:
        kv_c = ac @ w_in[:, kv0:]
    k_c = kv_c[..., :KV_WIDTH].reshape(b, l, N_KV_HEADS, HEAD_DIM)
    v_c = kv_c[..., KV_WIDTH:].reshape(b, l, N_KV_HEADS, HEAD_DIM)
    y = jnp.concatenate([pool_branch(u_pool, pool_w, pool_scale),
                         latent_attention(q, k, v, k_c, v_c, sink)], axis=-1) @ w_out
    if need_ctx:
        q_c = proj_c[..., POOL_WIDTH:kv0].reshape(b, l, N_Q_HEADS, HEAD_DIM)
        yc = jnp.concatenate([pool_branch(proj_c[..., :POOL_WIDTH], pool_w, pool_scale),
                              context_attention(q_c, k_c, v_c, sink)], axis=-1) @ w_out
        return y, yc
    return y, None


def grouped_moe(h, router_w, router_b, w_gate, w_up, w_down):
    t = h.shape[0]
    aff = jax.nn.sigmoid((h @ router_w).astype(jnp.float32))
    sel = aff + router_b.astype(jnp.float32)
    group_score = lax.top_k(sel.reshape(t, N_GROUPS, EXPERTS_PER_GROUP), 2)[0].sum(-1)
    best = jnp.argmax(group_score, axis=-1)
    in_group = (jnp.arange(N_EXPERTS) // EXPERTS_PER_GROUP)[None, :] == best[:, None]
    _, idx = lax.top_k(jnp.where(in_group, sel, -jnp.inf), TOP_K)
    w_sel = jnp.take_along_axis(aff, idx, axis=-1)
    w_sel = w_sel / jnp.sum(w_sel, axis=-1, keepdims=True)
    combine = jnp.sum(jax.nn.one_hot(idx, N_EXPERTS, dtype=jnp.float32) * w_sel[..., None], axis=1)
    combine = combine.astype(h.dtype)
    out = jnp.zeros_like(h)
    for e in range(N_EXPERTS):
        hid = jax.nn.silu(h @ w_gate[e]) * (h @ w_up[e])
        out = out + combine[:, e:e + 1] * (hid @ w_down[e])
    return out


def setup_inputs(seed: int = 0) -> dict:
    key = jax.random.key(seed)
    ks = jax.random.split(key, 24)
    f32 = jnp.float32
    nrm = lambda k, shape, s: jax.random.normal(k, shape, f32) * s
    D = D_MODEL
    return {
        "x": nrm(ks[0], (BATCH, SEQ, D), 1.0),
        "c": nrm(ks[1], (BATCH, D), 1.0),
        "ctx": nrm(ks[2], (BATCH, CTX_LEN, D), 1.0),
        "c_ctx": nrm(ks[3], (D,), 1.0),
        "ada_w": nrm(ks[4], (DEPTH, D, N_MOD * D), 0.5 * D ** -0.5),
        "ada_b": nrm(ks[5], (DEPTH, N_MOD * D), 0.02),
        "norm_mix_g": 1.0 + nrm(ks[6], (DEPTH, D), 0.02),
        "norm_ffn_g": 1.0 + nrm(ks[7], (DEPTH, D), 0.02),
        "even_w_in": nrm(ks[8], (N_EVEN, D, EVEN_IN), D ** -0.5),
        "even_conv_w": nrm(ks[9], (N_EVEN, CONV_K, CONV_WIDTH), CONV_K ** -0.5),
        "even_w_out": nrm(ks[10], (N_EVEN, EVEN_MIX, D), EVEN_MIX ** -0.5),
        "odd_w_in": nrm(ks[11], (N_ODD, D, ODD_IN), D ** -0.5),
        "odd_pool_w": nrm(ks[12], (N_ODD, POOL_GROUPS, POOL_GROUP_DIM, POOL_GROUP_DIM), POOL_GROUP_DIM ** -0.5),
        "odd_pool_scale": 1.0 + nrm(ks[13], (N_ODD, POOL_WIDTH), 0.02),
        "odd_sink": nrm(ks[14], (N_ODD, N_Q_HEADS), 0.5),
        "odd_w_out": nrm(ks[15], (N_ODD, ODD_MIX, D), ODD_MIX ** -0.5),
        "router_w": nrm(ks[16], (D, N_EXPERTS), D ** -0.5),
        "router_b": nrm(ks[17], (N_EXPERTS,), 0.01),
        "moe_w_gate": nrm(ks[18], (DEPTH, N_EXPERTS, D, D_EXPERT), D ** -0.5),
        "moe_w_up": nrm(ks[19], (DEPTH, N_EXPERTS, D, D_EXPERT), D ** -0.5),
        "moe_w_down": nrm(ks[20], (DEPTH, N_EXPERTS, D_EXPERT, D), D_EXPERT ** -0.5),
        "final_g": 1.0 + nrm(ks[21], (D,), 0.02),
    }


def reference(x, c, ctx, c_ctx, ada_w, ada_b, norm_mix_g, norm_ffn_g,
              even_w_in, even_conv_w, even_w_out,
              odd_w_in, odd_pool_w, odd_pool_scale, odd_sink, odd_w_out,
              router_w, router_b, moe_w_gate, moe_w_up, moe_w_down, final_g):
    b, n, d = x.shape
    l = ctx.shape[1]
    rows = n // GRID_W
    row_ids = jnp.repeat(jnp.arange(rows), GRID_W)
    col_ids = jnp.tile(jnp.arange(GRID_W), rows)
    cos, sin = axial_rope_tables(row_ids, col_ids)
    s_lat = jax.nn.silu(c)
    s_ctx = jax.nn.silu(c_ctx)
    h, hc = x, ctx
    for layer in range(DEPTH):
        last = layer == DEPTH - 1
        j = layer // 2
        m = [z[:, None, :] for z in ada_mod(s_lat, ada_w[layer], ada_b[layer])]
        mc = ada_mod(s_ctx, ada_w[layer], ada_b[layer])
        a = modulate(rmsnorm(h, norm_mix_g[layer]), m[0], m[1])
        if layer % 2 == 0:
            y = even_mixer(a, even_w_in[j], even_conv_w[j], even_w_out[j])
            if not last:
                ac = modulate(rmsnorm(hc, norm_mix_g[layer]), mc[0], mc[1])
                yc = even_mixer(ac, even_w_in[j], even_conv_w[j], even_w_out[j])
        else:
            ac = modulate(rmsnorm(hc, norm_mix_g[layer]), mc[0], mc[1])
            y, yc = odd_mixer(a, ac, cos, sin, odd_w_in[j], odd_pool_w[j], odd_pool_scale[j],
                              odd_sink[j], odd_w_out[j], not last)
        h = h + m[2] * y
        f = modulate(rmsnorm(h, norm_ffn_g[layer]), m[3], m[4])
        if not last:
            hc = hc + mc[2] * yc
            fc = modulate(rmsnorm(hc, norm_ffn_g[layer]), mc[3], mc[4])
            tokens = jnp.concatenate([f.reshape(b * n, d), fc.reshape(b * l, d)], axis=0)
            out = grouped_moe(tokens, router_w, router_b,
                              moe_w_gate[layer], moe_w_up[layer], moe_w_down[layer])
            h = h + m[5] * out[:b * n].reshape(b, n, d)
            hc = hc + mc[5] * out[b * n:].reshape(b, l, d)
        else:
            out = grouped_moe(f.reshape(b * n, d), router_w, router_b,
                              moe_w_gate[layer], moe_w_up[layer], moe_w_down[layer])
            h = h + m[5] * out.reshape(b, n, d)
    return rmsnorm(h, final_g)
```

```python
import functools
import math

import numpy as np
import jax
import jax.numpy as jnp
from jax import lax
from jax.experimental import pallas as pl
from jax.experimental.pallas import tpu as pltpu

F32 = jnp.float32
BF16 = jnp.bfloat16

D_MODEL = 1024
GRID_W = 64
EPS = 1e-6
N_MOD = 6
FOURIER_HEADS = 4
FOURIER_HEAD_DIM = 128
FOURIER_WIDTH = 512
CONV_WIDTH = 512
CONV_K = 3
POOL_WINDOWS = (2, 4, 8, 16)
POOL_GROUP_DIM = 128
POOL_WIDTH = 512
POOL_HALO = 8
HEAD_DIM = 64
N_Q_HEADS = 8
N_KV_HEADS = 2
ATTN_WIDTH = 512
KV_WIDTH = 128
ATTN_BLOCK = 128
ROPE_THETA = 10000.0
N_EXPERTS = 16
N_GROUPS = 4
EXPERTS_PER_GROUP = 4
D_EXPERT = 512
N_PAIRS = 6
N_BUCKETS = N_GROUPS * N_PAIRS

LANES = 128
VMEM_LIMIT_BYTES = 48 * 1024 * 1024

HIGHEST = lax.Precision.HIGHEST


def _cparams(sem):
    return pltpu.CompilerParams(dimension_semantics=sem, vmem_limit_bytes=VMEM_LIMIT_BYTES)


def _rms_mod(x, g, shift, scale):
    y = x * lax.rsqrt(jnp.mean(x * x, axis=-1, keepdims=True) + EPS) * g
    return y * (1.0 + scale) + shift


def _dot(a, b):
    return jnp.dot(a, b, preferred_element_type=F32)


def _dot_nt(a, b):
    return lax.dot_general(a, b, (((1,), (1,)), ((), ())), preferred_element_type=F32)


def _ada_kernel(s_ref, w_ref, b_ref, o_ref):
    s = s_ref[...]
    s = s * jax.nn.sigmoid(s)
    o_ref[0] = jnp.dot(s, w_ref[0], preferred_element_type=F32, precision=HIGHEST) + b_ref[0]


def _ada_call(s_rows, ada_w, ada_b):
    depth, d, n6 = ada_w.shape
    r = s_rows.shape[0]
    tn = 1536
    return pl.pallas_call(
        _ada_kernel,
        grid=(depth, n6 // tn),
        in_specs=[
            pl.BlockSpec((r, d), lambda l, j: (0, 0)),
            pl.BlockSpec((1, d, tn), lambda l, j: (l, 0, j)),
            pl.BlockSpec((1, 1, tn), lambda l, j: (l, 0, j)),
        ],
        out_specs=pl.BlockSpec((1, r, tn), lambda l, j: (l, 0, j)),
        out_shape=jax.ShapeDtypeStruct((depth, r, n6), F32),
        compiler_params=_cparams(("arbitrary", "arbitrary")),
        name="ada_mod",
    )(s_rows, ada_w, ada_b.reshape(depth, 1, n6))


def _rope_group(x, cos, sin_signed):
    lane = lax.broadcasted_iota(jnp.int32, x.shape, 1)
    first_half = (lane % 32) < 16
    partner = jnp.where(first_half, pltpu.roll(x, LANES - 16, 1), pltpu.roll(x, 16, 1))
    return x * cos + partner * sin_signed


def _inproj_kernel(*refs, mode, add_moe):
    it = iter(refs)
    h_ref = next(it)
    if add_moe:
        moe_ref, gate_ref = next(it), next(it)
    g_ref, sh_ref, sc_ref, w_ref = next(it), next(it), next(it), next(it)
    x = h_ref[0]
    if add_moe:
        x = x + gate_ref[0] * moe_ref[0]
    a = _rms_mod(x, g_ref[...], sh_ref[0], sc_ref[0])
    proj = _dot(a.astype(BF16), w_ref[...])
    if mode == "even":
        cs_ref = next(it)
        uc_ref, us_ref, gb_ref, gu_ref = next(it), next(it), next(it), next(it)
        uf = proj[:, :FOURIER_WIDTH].astype(BF16)
        cs = cs_ref[...].astype(BF16)
        for hh in range(FOURIER_HEADS):
            cols = slice(hh * LANES, (hh + 1) * LANES)
            r = _dot(uf[:, cols], cs)
            uc_ref[0, :, cols] = r[:, :LANES].astype(BF16)
            us_ref[0, :, cols] = r[:, LANES:].astype(BF16)
        c0 = FOURIER_WIDTH
        gb_ref[0] = proj[:, c0:c0 + CONV_WIDTH]
        gu_ref[0] = proj[:, c0 + CONV_WIDTH:c0 + 2 * CONV_WIDTH] * proj[:, c0 + 2 * CONV_WIDTH:]
    elif mode == "odd":
        cq_ref, sq_ref, ck_ref, sk_ref = next(it), next(it), next(it), next(it)
        hn_ref, up_ref, q_ref, kd_ref, vd_ref = next(it), next(it), next(it), next(it), next(it)
        hn_ref[0] = x
        up_ref[0] = proj[:, :POOL_WIDTH]
        c0 = POOL_WIDTH
        for gi in range(ATTN_WIDTH // LANES):
            cols = slice(c0 + gi * LANES, c0 + (gi + 1) * LANES)
            q_ref[0, :, gi * LANES:(gi + 1) * LANES] = _rope_group(
                proj[:, cols], cq_ref[...], sq_ref[...]).astype(BF16)
        c0 += ATTN_WIDTH
        for gi in range(2 * KV_WIDTH // LANES):
            cols = slice(c0 + gi * LANES, c0 + (gi + 1) * LANES)
            kd_ref[0, :, gi * LANES:(gi + 1) * LANES] = _rope_group(
                proj[:, cols], ck_ref[...], sk_ref[...]).astype(BF16)
        c0 += 2 * KV_WIDTH
        vd_ref[0] = proj[:, c0:].astype(BF16)
    else:
        kv_ref = next(it)
        kv_ref[0] = proj.astype(BF16)


def _mod_spec(arr):
    if arr.shape[0] > 1:
        return pl.BlockSpec((1, 1, arr.shape[2]), lambda b, i: (b, 0, 0))
    return pl.BlockSpec((1, 1, arr.shape[2]), lambda b, i: (0, 0, 0))


def _inproj_call(h, g, shift, scale, w, *, mode, tm, moe=None, gate=None, extra=()):
    b, n, d = h.shape
    nout = w.shape[1]
    add_moe = moe is not None
    row = lambda bb, i: (bb, i, 0)
    full2 = lambda bb, i: (0, 0)
    args = [h]
    in_specs = [pl.BlockSpec((1, tm, d), row)]
    if add_moe:
        args += [moe, gate]
        in_specs += [pl.BlockSpec((1, tm, d), row), _mod_spec(gate)]
    args += [g.reshape(1, d), shift, scale, w]
    in_specs += [pl.BlockSpec((1, d), full2), _mod_spec(shift), _mod_spec(scale),
                 pl.BlockSpec((d, nout), full2)]
    if mode == "even":
        args += list(extra)
        in_specs += [pl.BlockSpec(extra[0].shape, full2)]
        out_shape = [jax.ShapeDtypeStruct((b, n, FOURIER_WIDTH), BF16),
                     jax.ShapeDtypeStruct((b, n, FOURIER_WIDTH), BF16),
                     jax.ShapeDtypeStruct((b, n, CONV_WIDTH), F32),
                     jax.ShapeDtypeStruct((b, n, CONV_WIDTH), F32)]
        out_specs = [pl.BlockSpec((1, tm, 512), row)] * 4
    elif mode == "odd":
        args += list(extra)
        in_specs += [pl.BlockSpec((tm, LANES), lambda bb, i: (i, 0))] * 4
        out_shape = [jax.ShapeDtypeStruct((b, n, d), F32),
                     jax.ShapeDtypeStruct((b, n, POOL_WIDTH), F32),
                     jax.ShapeDtypeStruct((b, n, ATTN_WIDTH), BF16),
                     jax.ShapeDtypeStruct((b, n, 2 * KV_WIDTH), BF16),
                     jax.ShapeDtypeStruct((b, n, 2 * KV_WIDTH), BF16)]
        out_specs = [pl.BlockSpec((1, tm, d), row), pl.BlockSpec((1, tm, POOL_WIDTH), row),
                     pl.BlockSpec((1, tm, ATTN_WIDTH), row),
                     pl.BlockSpec((1, tm, 2 * KV_WIDTH), row), pl.BlockSpec((1, tm, 2 * KV_WIDTH), row)]
    else:
        out_shape = [jax.ShapeDtypeStruct((b, n, nout), BF16)]
        out_specs = [pl.BlockSpec((1, tm, nout), row)]
    return pl.pallas_call(
        functools.partial(_inproj_kernel, mode=mode, add_moe=add_moe),
        grid=(b, n // tm),
        in_specs=in_specs,
        out_specs=out_specs,
        out_shape=out_shape,
        compiler_params=_cparams(("arbitrary", "arbitrary")),
        name="inproj_" + mode,
    )(*args)


DFT_RADIX = 64


def _dft_kernel(ca_ref, sa_ref, cb_ref, sb_ref, uc_ref, us_ref, o_ref, c_scr, s_scr, *, norm):
    @pl.when(pl.program_id(1) == 0)
    def _():
        cb, sb = cb_ref[...], sb_ref[...]
        for r in range(ca_ref.shape[0]):
            ca, sa = ca_ref[r:r + 1, :], sa_ref[r:r + 1, :]
            rows = slice(r * DFT_RADIX, (r + 1) * DFT_RADIX)
            c_scr[rows, :] = (ca * cb - sa * sb).astype(BF16)
            s_scr[rows, :] = (-(sa * cb + ca * sb)).astype(BF16)

    acc = _dot(c_scr[...], uc_ref[0]) + _dot(s_scr[...], us_ref[0])
    o_ref[0] = (acc * norm).astype(BF16)


def _dft_call(tabs, uc, us, *, tm):
    b, n, wdt = uc.shape
    r_tile = tm // DFT_RADIX
    norm = 1.0 / math.sqrt(n * FOURIER_HEAD_DIM)
    a_spec = pl.BlockSpec((r_tile, n), lambda i, bb: (i, 0))
    b_spec = pl.BlockSpec((DFT_RADIX, n), lambda i, bb: (0, 0))
    u_spec = pl.BlockSpec((1, n, wdt), lambda i, bb: (bb, 0, 0))
    return pl.pallas_call(
        functools.partial(_dft_kernel, norm=norm),
        grid=(n // tm, b),
        in_specs=[a_spec, a_spec, b_spec, b_spec, u_spec, u_spec],
        out_specs=pl.BlockSpec((1, tm, wdt), lambda i, bb: (bb, i, 0)),
        out_shape=jax.ShapeDtypeStruct((b, n, wdt), BF16),
        scratch_shapes=[pltpu.VMEM((tm, n), BF16), pltpu.VMEM((tm, n), BF16)],
        compiler_params=_cparams(("arbitrary", "arbitrary")),
        name="dft_rows",
    )(*tabs, uc, us)


def _tail(y, h_ref, gate_ref, g2_ref, sh_ref, sc_ref, rwt_ref, hout_ref, f_ref, lg_ref):
    hn = h_ref[0] + gate_ref[0] * y
    hout_ref[0] = hn
    f = _rms_mod(hn, g2_ref[...], sh_ref[0], sc_ref[0])
    f_ref[0] = f.astype(BF16)
    lg_ref[0] = lax.dot_general(rwt_ref[...], f, (((1,), (1,)), ((), ())),
                                preferred_element_type=F32, precision=HIGHEST)


def _tail_specs(h, gate, shift, scale, tm):
    b, n, d = h.shape
    row = lambda bb, i: (bb, i, 0)
    in_specs = [pl.BlockSpec((1, tm, d), row), _mod_spec(gate),
                pl.BlockSpec((1, d), lambda bb, i: (0, 0)), _mod_spec(shift), _mod_spec(scale),
                pl.BlockSpec((N_EXPERTS, d), lambda bb, i: (0, 0))]
    out_shape = [jax.ShapeDtypeStruct((b, n, d), F32), jax.ShapeDtypeStruct((b, n, d), BF16),
                 jax.ShapeDtypeStruct((b, N_EXPERTS, n), F32)]
    out_specs = [pl.BlockSpec((1, tm, d), row), pl.BlockSpec((1, tm, d), row),
                 pl.BlockSpec((1, N_EXPERTS, tm), lambda bb, i: (bb, 0, i))]
    return in_specs, out_shape, out_specs


def _even_out_kernel(yf_ref, gb_ref, gu_ref, gp_ref, gn_ref, cw_ref, wo_ref,
                     h_ref, gate_ref, g2_ref, sh_ref, sc_ref, rwt_ref,
                     hout_ref, f_ref, lg_ref):
    i = pl.program_id(1)
    last = pl.num_programs(1) - 1
    gu = gu_ref[0]
    tm = gu.shape[0]
    prev = jnp.where(i > 0, gp_ref[0, 7:8, :], 0.0)
    nxt = jnp.where(i < last, gn_ref[0, 0:1, :], 0.0)
    row = lax.broadcasted_iota(jnp.int32, gu.shape, 0)
    up = jnp.where(row == 0, prev, pltpu.roll(gu, 1, 0))
    dn = jnp.where(row == tm - 1, nxt, pltpu.roll(gu, tm - 1, 0))
    conv = up * cw_ref[0:1, :] + gu * cw_ref[1:2, :] + dn * cw_ref[2:3, :]
    yc = (gb_ref[0] * conv).astype(BF16)
    y = _dot(yf_ref[0], wo_ref[:FOURIER_WIDTH, :]) + _dot(yc, wo_ref[FOURIER_WIDTH:, :])
    _tail(y, h_ref, gate_ref, g2_ref, sh_ref, sc_ref, rwt_ref, hout_ref, f_ref, lg_ref)


def _even_out_call(yf, gb, gu, conv_w, w_out, h, gate, g2, shift, scale, rwt, *, tm):
    b, n, d = h.shape
    row = lambda bb, i: (bb, i, 0)
    nb8 = n // 8
    t8 = tm // 8
    tin, out_shape, out_specs = _tail_specs(h, gate, shift, scale, tm)
    in_specs = [pl.BlockSpec((1, tm, 512), row), pl.BlockSpec((1, tm, 512), row),
                pl.BlockSpec((1, tm, 512), row),
                pl.BlockSpec((1, 8, 512), lambda bb, i: (bb, jnp.maximum(i * t8 - 1, 0), 0)),
                pl.BlockSpec((1, 8, 512), lambda bb, i: (bb, jnp.minimum((i + 1) * t8, nb8 - 1), 0)),
                pl.BlockSpec((CONV_K, CONV_WIDTH), lambda bb, i: (0, 0)),
                pl.BlockSpec((d, d), lambda bb, i: (0, 0))] + tin
    return pl.pallas_call(
        _even_out_kernel,
        grid=(b, n // tm),
        in_specs=in_specs,
        out_specs=out_specs,
        out_shape=out_shape,
        compiler_params=_cparams(("arbitrary", "arbitrary")),
        name="even_out",
    )(yf, gb, gu, gu, gu, conv_w, w_out, h, gate, g2.reshape(1, d), shift, scale, rwt)


def _odd_out_kernel(sink_ref, up_ref, upp_ref, upn_ref, q_ref, kc_ref, kp_ref, kn_ref,
                    vc_ref, vp_ref, vn_ref, kvx_ref, pw_ref, ps_ref, wo_ref,
                    h_ref, gate_ref, g2_ref, sh_ref, sc_ref, rwt_ref,
                    hout_ref, f_ref, lg_ref, ext_ref, mix_ref, *, n_total):
    i = pl.program_id(1)
    last = pl.num_programs(1) - 1
    tq = q_ref.shape[1]
    nsub = tq // ATTN_BLOCK

    u = up_ref[0]
    ext_ref[0:POOL_HALO, :] = jnp.where(i > 0, upp_ref[0], 0.0)
    ext_ref[POOL_HALO:POOL_HALO + tq, :] = u
    ext_ref[POOL_HALO + tq:, :] = jnp.where(i < last, upn_ref[0], 0.0)
    t = i * tq + lax.broadcasted_iota(jnp.int32, (tq, LANES), 0)
    for gi, win in enumerate(POOL_WINDOWS):
        r = win // 2
        cols = slice(gi * LANES, (gi + 1) * LANES)
        acc = ext_ref[POOL_HALO - r:POOL_HALO - r + tq, cols]
        for dlt in range(-r + 1, r + 1):
            acc = acc + ext_ref[POOL_HALO + dlt:POOL_HALO + dlt + tq, cols]
        cnt = (jnp.minimum(t + r + 1, n_total) - jnp.maximum(t - r, 0)).astype(F32)
        p = acc / cnt - u[:, cols]
        y = _dot(p.astype(BF16), pw_ref[gi]) * ps_ref[:, cols]
        mix_ref[:, cols] = y.astype(BF16)

    kwin = jnp.concatenate([kp_ref[0], kc_ref[0], kn_ref[0]], axis=0)
    vwin = jnp.concatenate([vp_ref[0], vc_ref[0], vn_ref[0]], axis=0)
    kvx = kvx_ref[0]
    kx, vx = kvx[:, :2 * KV_WIDTH], kvx[:, 2 * KV_WIDTH:]
    low = (lax.broadcasted_iota(jnp.int32, (1, 2 * KV_WIDTH), 1) % LANES) < HEAD_DIM
    zero = jnp.zeros((), BF16)
    k_half = (jnp.where(low, kwin, zero), jnp.where(low, zero, kwin))
    v_half = (jnp.where(low, vwin, zero), jnp.where(low, zero, vwin))
    kx_half = (jnp.where(low, kx, zero), jnp.where(low, zero, kx))
    vx_half = (jnp.where(low, vx, zero), jnp.where(low, zero, vx))

    span = 3 * ATTN_BLOCK
    rows2 = 2 * ATTN_BLOCK
    qi = lax.broadcasted_iota(jnp.int32, (rows2, span), 0) % ATTN_BLOCK
    kj = lax.broadcasted_iota(jnp.int32, (rows2, span), 1)
    in_prev = kj < ATTN_BLOCK
    in_next = kj >= 2 * ATTN_BLOCK
    neg_inf = jnp.float32(-jnp.inf)
    band_bias = (jnp.where(in_prev & (kj < qi), neg_inf, 0.0)
                 + jnp.where(in_next & (kj - 2 * ATTN_BLOCK > qi), neg_inf, 0.0))
    top_rows = lax.broadcasted_iota(jnp.int32, (rows2, 1), 0) < ATTN_BLOCK

    for j in range(nsub):
        blk = i * nsub + j
        prev_bias = jnp.where(blk > 0, 0.0, neg_inf)
        next_bias = jnp.where(blk < (n_total // ATTN_BLOCK) - 1, 0.0, neg_inf)
        bias = band_bias + jnp.where(in_prev, prev_bias, 0.0) + jnp.where(in_next, next_bias, 0.0)
        r0 = j * ATTN_BLOCK
        for kh in range(N_KV_HEADS):
            kcols = slice(kh * LANES, (kh + 1) * LANES)
            xq = jnp.concatenate(
                [q_ref[0, r0:r0 + ATTN_BLOCK, (2 * kh + pr) * LANES:(2 * kh + pr + 1) * LANES]
                 for pr in range(2)], axis=0)
            o_pair = None
            for half in range(2):
                ks = k_half[half][r0:r0 + span, kcols]
                vs = v_half[half][r0:r0 + span, kcols]
                s1 = _dot_nt(xq, ks) + bias
                s2 = _dot_nt(xq, kx_half[half][:, kcols])
                head0 = kh * 4 + half
                snk = jnp.where(top_rows, sink_ref[head0], sink_ref[head0 + 2])
                m = jnp.maximum(jnp.maximum(jnp.max(s1, axis=-1, keepdims=True),
                                            jnp.max(s2, axis=-1, keepdims=True)), snk)
                e1 = jnp.exp(s1 - m)
                e2 = jnp.exp(s2 - m)
                den = (jnp.sum(e1, axis=-1, keepdims=True) + jnp.sum(e2, axis=-1, keepdims=True)
                       + jnp.exp(snk - m))
                o = _dot(e1.astype(BF16), vs) + _dot(e2.astype(BF16), vx_half[half][:, kcols])
                o = o / den
                o_pair = o if o_pair is None else o_pair + o
            for pr in range(2):
                c0 = POOL_WIDTH + (2 * kh + pr) * LANES
                mix_ref[r0:r0 + ATTN_BLOCK, c0:c0 + LANES] = (
                    o_pair[pr * ATTN_BLOCK:(pr + 1) * ATTN_BLOCK].astype(BF16))

    y = _dot(mix_ref[...], wo_ref[...])
    _tail(y, h_ref, gate_ref, g2_ref, sh_ref, sc_ref, rwt_ref, hout_ref, f_ref, lg_ref)


def _odd_out_call(sink, up, q, kd, vd, kvx, pool_w, pool_scale, w_out,
                  h, gate, g2, shift, scale, rwt, *, tq):
    b, n, d = h.shape
    row = lambda bb, i: (bb, i, 0)
    nb8, t8 = n // POOL_HALO, tq // POOL_HALO
    nbk, tk = n // ATTN_BLOCK, tq // ATTN_BLOCK
    prev8 = lambda bb, i: (bb, jnp.maximum(i * t8 - 1, 0), 0)
    next8 = lambda bb, i: (bb, jnp.minimum((i + 1) * t8, nb8 - 1), 0)
    prevk = lambda bb, i: (bb, jnp.maximum(i * tk - 1, 0), 0)
    nextk = lambda bb, i: (bb, jnp.minimum((i + 1) * tk, nbk - 1), 0)
    kvw = 2 * KV_WIDTH
    tin, out_shape, out_specs = _tail_specs(h, gate, shift, scale, tq)
    in_specs = [pl.BlockSpec(memory_space=pltpu.SMEM),
                pl.BlockSpec((1, tq, POOL_WIDTH), row),
                pl.BlockSpec((1, POOL_HALO, POOL_WIDTH), prev8),
                pl.BlockSpec((1, POOL_HALO, POOL_WIDTH), next8),
                pl.BlockSpec((1, tq, ATTN_WIDTH), row),
                pl.BlockSpec((1, tq, kvw), row),
                pl.BlockSpec((1, ATTN_BLOCK, kvw), prevk),
                pl.BlockSpec((1, ATTN_BLOCK, kvw), nextk),
                pl.BlockSpec((1, tq, kvw), row),
                pl.BlockSpec((1, ATTN_BLOCK, kvw), prevk),
                pl.BlockSpec((1, ATTN_BLOCK, kvw), nextk),
                pl.BlockSpec((1, kvx.shape[1], 2 * kvw), lambda bb, i: (bb, 0, 0)),
                pl.BlockSpec(pool_w.shape, lambda bb, i: (0, 0, 0)),
                pl.BlockSpec((1, POOL_WIDTH), lambda bb, i: (0, 0)),
                pl.BlockSpec((d, d), lambda bb, i: (0, 0))] + tin
    return pl.pallas_call(
        functools.partial(_odd_out_kernel, n_total=n),
        grid=(b, n // tq),
        in_specs=in_specs,
        out_specs=out_specs,
        out_shape=out_shape,
        scratch_shapes=[pltpu.VMEM((tq + 2 * POOL_HALO, POOL_WIDTH), F32),
                        pltpu.VMEM((tq, d), BF16)],
        compiler_params=_cparams(("arbitrary", "arbitrary")),
        name="odd_out",
    )(sink, up, up, up, q, kd, kd, kd, vd, vd, vd, kvx, pool_w, pool_scale.reshape(1, POOL_WIDTH),
      w_out, h, gate, g2.reshape(1, d), shift, scale, rwt)


def _moe_kernel(e_lo_ref, e_hi_ref, valid_ref, x_ref, wl_ref, wh_ref,
                g1_ref, u1_ref, d1_ref, g2_ref, u2_ref, d2_ref, o_ref):
    j = pl.program_id(0)

    @pl.when(valid_ref[j] != 0)
    def _():
        x = x_ref[...]

        def expert(g_ref, u_ref, d_ref):
            gate = _dot(x, g_ref[0])
            hid = gate * jax.nn.sigmoid(gate) * _dot(x, u_ref[0])
            return _dot(hid.astype(BF16), d_ref[0])

        o_lo = expert(g1_ref, u1_ref, d1_ref)
        o_hi = expert(g2_ref, u2_ref, d2_ref)
        o_ref[...] = wl_ref[:, 0:1] * o_lo + wh_ref[:, 0:1] * o_hi

    @pl.when(valid_ref[j] == 0)
    def _():
        o_ref[...] = jnp.zeros(o_ref.shape, o_ref.dtype)


def _moe_call(tile_lo, tile_hi, tile_valid, xs, w_lo, w_hi, wg, wu, wd, *, tm):
    p, d = xs.shape
    ntiles = p // tm
    lo3 = lambda j, lo, hi, v: (lo[j], 0, 0)
    hi3 = lambda j, lo, hi, v: (hi[j], 0, 0)
    rowm = lambda j, lo, hi, v: (j, 0)
    grid_spec = pltpu.PrefetchScalarGridSpec(
        num_scalar_prefetch=3,
        grid=(ntiles,),
        in_specs=[pl.BlockSpec((tm, d), rowm),
                  pl.BlockSpec((tm, LANES), rowm), pl.BlockSpec((tm, LANES), rowm),
                  pl.BlockSpec((1, d, D_EXPERT), lo3), pl.BlockSpec((1, d, D_EXPERT), lo3),
                  pl.BlockSpec((1, D_EXPERT, d), lo3),
                  pl.BlockSpec((1, d, D_EXPERT), hi3), pl.BlockSpec((1, d, D_EXPERT), hi3),
                  pl.BlockSpec((1, D_EXPERT, d), hi3)],
        out_specs=pl.BlockSpec((tm, d), rowm),
    )
    return pl.pallas_call(
        _moe_kernel,
        grid_spec=grid_spec,
        out_shape=jax.ShapeDtypeStruct((p, d), F32),
        compiler_params=_cparams(("arbitrary",)),
        name="moe_pairs",
    )(tile_lo, tile_hi, tile_valid, xs, w_lo, w_hi, wg, wu, wd, wg, wu, wd)


def _final_kernel(h_ref, moe_ref, gate_ref, g_ref, o_ref):
    x = h_ref[0] + gate_ref[0] * moe_ref[0]
    o_ref[0] = x * lax.rsqrt(jnp.mean(x * x, axis=-1, keepdims=True) + EPS) * g_ref[...]


def _final_call(h, moe, gate, g, *, tm):
    b, n, d = h.shape
    row = lambda bb, i: (bb, i, 0)
    return pl.pallas_call(
        _final_kernel,
        grid=(b, n // tm),
        in_specs=[pl.BlockSpec((1, tm, d), row), pl.BlockSpec((1, tm, d), row), _mod_spec(gate),
                  pl.BlockSpec((1, d), lambda bb, i: (0, 0))],
        out_specs=pl.BlockSpec((1, tm, d), row),
        out_shape=jax.ShapeDtypeStruct((b, n, d), F32),
        compiler_params=_cparams(("arbitrary", "arbitrary")),
        name="final_norm",
    )(h, moe, gate, g.reshape(1, d))


def _channel_dft_table():
    c = np.arange(FOURIER_HEAD_DIM)
    ang = 2.0 * np.pi * ((c[:, None] * c[None, :]) % FOURIER_HEAD_DIM) / FOURIER_HEAD_DIM
    return jnp.asarray(np.concatenate([np.cos(ang), np.sin(ang)], axis=1), F32)


def _position_dft_tables(n):
    n1 = n // DFT_RADIX
    t = np.arange(n)
    a = 2.0 * np.pi * ((np.arange(n1)[:, None] * t[None, :]) % n1) / n1
    bb = 2.0 * np.pi * ((np.arange(DFT_RADIX)[:, None] * t[None, :]) % n) / n
    return tuple(jnp.asarray(v, F32) for v in (np.cos(a), np.sin(a), np.cos(bb), np.sin(bb)))


def _rope_tables(n):
    quarter = HEAD_DIM // 4
    inv = ROPE_THETA ** (-jnp.arange(quarter, dtype=F32) / quarter)
    t = jnp.arange(n)
    ang_r = (t // GRID_W).astype(F32)[:, None] * inv
    ang_c = (t % GRID_W).astype(F32)[:, None] * inv
    cos = jnp.concatenate([jnp.cos(ang_r)] * 2 + [jnp.cos(ang_c)] * 2, axis=1)
    sin = jnp.concatenate([-jnp.sin(ang_r), jnp.sin(ang_r), -jnp.sin(ang_c), jnp.sin(ang_c)], axis=1)
    return jnp.tile(cos, (1, 2)), jnp.tile(sin, (1, 2))


_PAIR_OF = np.full((EXPERTS_PER_GROUP, EXPERTS_PER_GROUP), 0, np.int32)
_PAIR_LO = np.zeros((N_PAIRS,), np.int32)
_PAIR_HI = np.zeros((N_PAIRS,), np.int32)
_p = 0
for _a in range(EXPERTS_PER_GROUP):
    for _b in range(_a + 1, EXPERTS_PER_GROUP):
        _PAIR_OF[_a, _b] = _PAIR_OF[_b, _a] = _p
        _PAIR_LO[_p], _PAIR_HI[_p] = _a, _b
        _p += 1


def _route(logits, router_b, tm):
    t = logits.shape[0]
    aff = jax.nn.sigmoid(logits)
    sel = aff + router_b.astype(F32)
    sel_g = sel.reshape(t, N_GROUPS, EXPERTS_PER_GROUP)
    pair_sum = jnp.stack([sel_g[:, :, _PAIR_LO[p]] + sel_g[:, :, _PAIR_HI[p]] for p in range(N_PAIRS)],
                         axis=-1).reshape(t, N_BUCKETS)
    bucket = jnp.argmax(pair_sum, axis=-1).astype(jnp.int32)
    onehot = bucket[:, None] == jnp.arange(N_BUCKETS, dtype=jnp.int32)[None, :]
    aff_g = aff.reshape(t, N_GROUPS, EXPERTS_PER_GROUP)
    aff_lo = jnp.stack([aff_g[:, :, _PAIR_LO[p]] for p in range(N_PAIRS)], axis=-1).reshape(t, N_BUCKETS)
    aff_hi = jnp.stack([aff_g[:, :, _PAIR_HI[p]] for p in range(N_PAIRS)], axis=-1).reshape(t, N_BUCKETS)
    a_lo = jnp.sum(jnp.where(onehot, aff_lo, 0.0), axis=-1)
    a_hi = jnp.sum(jnp.where(onehot, aff_hi, 0.0), axis=-1)
    w_lo = a_lo / (a_lo + a_hi)
    w_hi = a_hi / (a_lo + a_hi)

    chunk = LANES
    oh = onehot.astype(F32).reshape(t // chunk, chunk, N_BUCKETS)
    strict_lower = jnp.asarray(np.tril(np.ones((chunk, chunk), np.float32), -1))
    within = jnp.einsum("ij,cjb->cib", strict_lower, oh)
    tot = jnp.sum(oh, axis=1)
    before = jnp.cumsum(tot, axis=0) - tot
    rank = jnp.sum((within + before[:, None, :]) * oh, axis=-1).reshape(t).astype(jnp.int32)
    counts = jnp.sum(tot, axis=0).astype(jnp.int32)

    ntiles = t // tm + N_BUCKETS
    tiles_per = (counts + tm - 1) // tm
    tile_end = jnp.cumsum(tiles_per)
    tile_start = tile_end - tiles_per
    pos = jnp.sum(jnp.where(onehot, (tile_start * tm)[None, :], 0), axis=-1) + rank
    p = ntiles * tm
    src = jnp.zeros((p,), jnp.int32).at[pos].set(jnp.arange(t, dtype=jnp.int32), unique_indices=True)
    tile_ids = jnp.arange(ntiles, dtype=jnp.int32)
    used = tile_end[-1]
    tile_bucket = jnp.sum((tile_ids[:, None] >= tile_end[None, :]).astype(jnp.int32), axis=1)
    last_bucket = jnp.sum((jnp.maximum(used - 1, 0) >= tile_end).astype(jnp.int32))
    tile_valid = (tile_ids < used).astype(jnp.int32)
    tile_bucket = jnp.where(tile_valid == 1, tile_bucket, last_bucket)
    grp = tile_bucket // N_PAIRS
    pr = tile_bucket % N_PAIRS
    tile_lo = grp * EXPERTS_PER_GROUP + jnp.asarray(_PAIR_LO)[pr]
    tile_hi = grp * EXPERTS_PER_GROUP + jnp.asarray(_PAIR_HI)[pr]
    return src, pos, w_lo, w_hi, tile_lo.astype(jnp.int32), tile_hi.astype(jnp.int32), tile_valid


def _moe_layer(f_tokens, logits, router_b, wg, wu, wd, *, tm):
    src, pos, w_lo, w_hi, tile_lo, tile_hi, tile_valid = _route(logits, router_b, tm)
    xs = jnp.take(f_tokens, src, axis=0)
    wl = jnp.broadcast_to(jnp.take(w_lo, src)[:, None], (src.shape[0], LANES))
    wh = jnp.broadcast_to(jnp.take(w_hi, src)[:, None], (src.shape[0], LANES))
    out_sorted = _moe_call(tile_lo, tile_hi, tile_valid, xs, wl, wh, wg, wu, wd, tm=tm)
    return out_sorted, pos


def _forward(x, c, ctx, c_ctx, ada_w, ada_b, norm_mix_g, norm_ffn_g, even_w_in, even_conv_w, even_w_out,
             odd_w_in, odd_pool_w, odd_pool_scale, odd_sink, odd_w_out, router_w, router_b,
             moe_w_gate, moe_w_up, moe_w_down, final_g, *, tm_lat, tm_ctx, tq, tm_dft, tm_moe):
    b, n, d = x.shape
    l = ctx.shape[1]

    rows = ((b + 1 + 7) // 8) * 8
    s_rows = jnp.zeros((rows, d), F32).at[:b].set(c).at[b].set(c_ctx)
    mods = _ada_call(s_rows, ada_w, ada_b)

    def mod_vecs(layer):
        m = mods[layer, :b].reshape(b, N_MOD, 1, d)
        mc = mods[layer, b].reshape(N_MOD, 1, 1, d)
        return [m[:, k] for k in range(N_MOD)], [mc[k] for k in range(N_MOD)]

    rwt = router_w.T
    cs_tab = _channel_dft_table()

    m, mc = mod_vecs(0)
    w_in0 = even_w_in[0].astype(BF16)
    w_out0 = even_w_out[0].astype(BF16)

    def even_stream(h, mv, tm):
        nn = h.shape[1]
        uc, us, gb, gu = _inproj_call(h, norm_mix_g[0], mv[0], mv[1], w_in0, mode="even", tm=tm,
                                      extra=(cs_tab,))
        yf = _dft_call(_position_dft_tables(nn), uc, us, tm=min(tm_dft, nn))
        return _even_out_call(yf, gb, gu, even_conv_w[0], w_out0, h, mv[2], norm_ffn_g[0],
                              mv[3], mv[4], rwt, tm=tm)

    h1, f_lat, lg_lat = even_stream(x, m, tm_lat)
    hc1, f_ctx, lg_ctx = even_stream(ctx, mc, tm_ctx)

    tokens = jnp.concatenate([f_lat.reshape(b * n, d), f_ctx.reshape(b * l, d)], axis=0)
    logits = jnp.concatenate([lg_lat.transpose(0, 2, 1).reshape(b * n, N_EXPERTS),
                              lg_ctx.transpose(0, 2, 1).reshape(b * l, N_EXPERTS)], axis=0)
    moe0, pos0 = _moe_layer(tokens, logits, router_b, moe_w_gate[0].astype(BF16), moe_w_up[0].astype(BF16),
                            moe_w_down[0].astype(BF16), tm=tm_moe)
    moe_lat = jnp.take(moe0, pos0[:b * n], axis=0).reshape(b, n, d)
    moe_ctx = jnp.take(moe0, pos0[b * n:], axis=0).reshape(b, l, d)
    gate_lat0, gate_ctx0 = m[5], mc[5]

    m, mc = mod_vecs(1)
    w_in1 = odd_w_in[0]
    kv0 = POOL_WIDTH + ATTN_WIDTH
    wk, wv = w_in1[:, kv0:kv0 + KV_WIDTH], w_in1[:, kv0 + KV_WIDTH:]

    def dup_heads(wm):
        return jnp.concatenate([wm[:, :HEAD_DIM], wm[:, :HEAD_DIM], wm[:, HEAD_DIM:], wm[:, HEAD_DIM:]], axis=1)

    w_kv_dup = jnp.concatenate([dup_heads(wk), dup_heads(wv)], axis=1)
    w_lat1 = jnp.concatenate([w_in1[:, :kv0], w_kv_dup], axis=1).astype(BF16)
    w_out1 = odd_w_out[0].astype(BF16)

    cos_t, sin_t = _rope_tables(n)
    q_scale = HEAD_DIM ** -0.5
    h1b, up, q, kd, vd = _inproj_call(h1, norm_mix_g[1], m[0], m[1], w_lat1, mode="odd", tm=tm_lat,
                                      moe=moe_lat, gate=gate_lat0,
                                      extra=(cos_t * q_scale, sin_t * q_scale, cos_t, sin_t))
    (kvx,) = _inproj_call(hc1, norm_mix_g[1], mc[0], mc[1], w_kv_dup.astype(BF16), mode="plain", tm=tm_ctx,
                          moe=moe_ctx, gate=gate_ctx0)
    h2, f2, lg2 = _odd_out_call(odd_sink[0], up, q, kd, vd, kvx, odd_pool_w[0].astype(BF16),
                                odd_pool_scale[0], w_out1, h1b, m[2], norm_ffn_g[1], m[3], m[4], rwt, tq=tq)
    moe1, pos1 = _moe_layer(f2.reshape(b * n, d), lg2.transpose(0, 2, 1).reshape(b * n, N_EXPERTS), router_b,
                            moe_w_gate[1].astype(BF16), moe_w_up[1].astype(BF16), moe_w_down[1].astype(BF16),
                            tm=tm_moe)
    return _final_call(h2, jnp.take(moe1, pos1, axis=0).reshape(b, n, d), m[5], final_g, tm=tm_lat)


def kernel(x, c, ctx, c_ctx, ada_w, ada_b, norm_mix_g, norm_ffn_g, even_w_in, even_conv_w, even_w_out,
           odd_w_in, odd_pool_w, odd_pool_scale, odd_sink, odd_w_out, router_w, router_b,
           moe_w_gate, moe_w_up, moe_w_down, final_g):
    return _forward(x, c, ctx, c_ctx, ada_w, ada_b, norm_mix_g, norm_ffn_g, even_w_in, even_conv_w,
                    even_w_out, odd_w_in, odd_pool_w, odd_pool_scale, odd_sink, odd_w_out, router_w,
                    router_b, moe_w_gate, moe_w_up, moe_w_down, final_g,
                    tm_lat=512, tm_ctx=256, tq=256, tm_dft=512, tm_moe=512)
```

```python
import functools
import math

import numpy as np
import jax
import jax.numpy as jnp
from jax import lax
from jax.experimental import pallas as pl
from jax.experimental.pallas import tpu as pltpu

F32 = jnp.float32
BF16 = jnp.bfloat16

D_MODEL = 1024
GRID_W = 64
EPS = 1e-6
N_MOD = 6
FOURIER_HEADS = 4
FOURIER_HEAD_DIM = 128
FOURIER_WIDTH = 512
CONV_WIDTH = 512
CONV_K = 3
POOL_WINDOWS = (2, 4, 8, 16)
POOL_GROUP_DIM = 128
POOL_WIDTH = 512
POOL_HALO = 8
HEAD_DIM = 64
N_Q_HEADS = 8
N_KV_HEADS = 2
ATTN_WIDTH = 512
KV_WIDTH = 128
ATTN_BLOCK = 128
ROPE_THETA = 10000.0
N_EXPERTS = 16
N_GROUPS = 4
EXPERTS_PER_GROUP = 4
D_EXPERT = 512
N_PAIRS = 6
N_BUCKETS = N_GROUPS * N_PAIRS

_PAIRS = [(a, b) for a in range(EXPERTS_PER_GROUP) for b in range(a + 1, EXPERTS_PER_GROUP)]
_BUCKET_LO = [(k // N_PAIRS) * EXPERTS_PER_GROUP + _PAIRS[k % N_PAIRS][0] for k in range(N_BUCKETS)]
_BUCKET_HI = [(k // N_PAIRS) * EXPERTS_PER_GROUP + _PAIRS[k % N_PAIRS][1] for k in range(N_BUCKETS)]

LANES = 128
ROW_WIDTH = D_MODEL + LANES
INFO_BUCKET, INFO_W_LO, INFO_W_HI = 0, 1, 2
VMEM_LIMIT_BYTES = 48 * 1024 * 1024

HIGHEST = lax.Precision.HIGHEST


def _cparams(sem):
    return pltpu.CompilerParams(dimension_semantics=sem, vmem_limit_bytes=VMEM_LIMIT_BYTES)


def _rms_mod(x, g, shift, scale):
    y = x * lax.rsqrt(jnp.mean(x * x, axis=-1, keepdims=True) + EPS) * g
    return y * (1.0 + scale) + shift


def _dot(a, b):
    return jnp.dot(a, b, preferred_element_type=F32)


def _dot_nt(a, b):
    return lax.dot_general(a, b, (((1,), (1,)), ((), ())), preferred_element_type=F32)


def _ada_kernel(s_ref, w_ref, b_ref, o_ref):
    s = s_ref[...]
    s = s * jax.nn.sigmoid(s)
    o_ref[0] = jnp.dot(s, w_ref[0], preferred_element_type=F32, precision=HIGHEST) + b_ref[0]


def _ada_call(s_rows, ada_w, ada_b):
    depth, d, n6 = ada_w.shape
    r = s_rows.shape[0]
    tn = 1536
    return pl.pallas_call(
        _ada_kernel,
        grid=(depth, n6 // tn),
        in_specs=[
            pl.BlockSpec((r, d), lambda l, j: (0, 0)),
            pl.BlockSpec((1, d, tn), lambda l, j: (l, 0, j)),
            pl.BlockSpec((1, 1, tn), lambda l, j: (l, 0, j)),
        ],
        out_specs=pl.BlockSpec((1, r, tn), lambda l, j: (l, 0, j)),
        out_shape=jax.ShapeDtypeStruct((depth, r, n6), F32),
        compiler_params=_cparams(("arbitrary", "arbitrary")),
        name="ada_mod",
    )(s_rows, ada_w, ada_b.reshape(depth, 1, n6))


def _rope_group(x, cos, sin_signed):
    lane = lax.broadcasted_iota(jnp.int32, x.shape, 1)
    first_half = (lane % 32) < 16
    partner = jnp.where(first_half, pltpu.roll(x, LANES - 16, 1), pltpu.roll(x, 16, 1))
    return x * cos + partner * sin_signed


def _inproj_kernel(*refs, mode, add_moe):
    it = iter(refs)
    h_ref = next(it)
    if add_moe:
        moe_ref, gate_ref = next(it), next(it)
    g_ref, sh_ref, sc_ref, w_ref = next(it), next(it), next(it), next(it)
    x = h_ref[0]
    if add_moe:
        x = x + gate_ref[0] * moe_ref[...]
    a = _rms_mod(x, g_ref[...], sh_ref[0], sc_ref[0])
    proj = _dot(a.astype(BF16), w_ref[...])
    if mode == "even":
        cs_ref = next(it)
        uc_ref, us_ref, gb_ref, gu_ref = next(it), next(it), next(it), next(it)
        uf = proj[:, :FOURIER_WIDTH].astype(BF16)
        cs = cs_ref[...].astype(BF16)
        for hh in range(FOURIER_HEADS):
            cols = slice(hh * LANES, (hh + 1) * LANES)
            r = _dot(uf[:, cols], cs)
            uc_ref[0, :, cols] = r[:, :LANES].astype(BF16)
            us_ref[0, :, cols] = r[:, LANES:].astype(BF16)
        c0 = FOURIER_WIDTH
        gb_ref[0] = proj[:, c0:c0 + CONV_WIDTH]
        gu_ref[0] = proj[:, c0 + CONV_WIDTH:c0 + 2 * CONV_WIDTH] * proj[:, c0 + 2 * CONV_WIDTH:]
    elif mode == "odd":
        cq_ref, sq_ref, ck_ref, sk_ref = next(it), next(it), next(it), next(it)
        hn_ref, up_ref, q_ref, kd_ref, vd_ref = next(it), next(it), next(it), next(it), next(it)
        hn_ref[0] = x
        up_ref[0] = proj[:, :POOL_WIDTH]
        c0 = POOL_WIDTH
        for gi in range(ATTN_WIDTH // LANES):
            cols = slice(c0 + gi * LANES, c0 + (gi + 1) * LANES)
            q_ref[0, :, gi * LANES:(gi + 1) * LANES] = _rope_group(
                proj[:, cols], cq_ref[...], sq_ref[...]).astype(BF16)
        c0 += ATTN_WIDTH
        for gi in range(2 * KV_WIDTH // LANES):
            cols = slice(c0 + gi * LANES, c0 + (gi + 1) * LANES)
            kd_ref[0, :, gi * LANES:(gi + 1) * LANES] = _rope_group(
                proj[:, cols], ck_ref[...], sk_ref[...]).astype(BF16)
        c0 += 2 * KV_WIDTH
        vd_ref[0] = proj[:, c0:].astype(BF16)
    else:
        kv_ref = next(it)
        kv_ref[0] = proj.astype(BF16)


def _mod_spec(arr):
    if arr.shape[0] > 1:
        return pl.BlockSpec((1, 1, arr.shape[2]), lambda b, i: (b, 0, 0))
    return pl.BlockSpec((1, 1, arr.shape[2]), lambda b, i: (0, 0, 0))


def _flat_rows_spec(tm, d, nt, row0):
    tile0 = row0 // tm
    return pl.BlockSpec((tm, d), lambda bb, i: (tile0 + bb * nt + i, 0))


def _inproj_call(h, g, shift, scale, w, *, mode, tm, moe=None, moe_row0=0, gate=None, extra=()):
    b, n, d = h.shape
    nout = w.shape[1]
    add_moe = moe is not None
    row = lambda bb, i: (bb, i, 0)
    full2 = lambda bb, i: (0, 0)
    args = [h]
    in_specs = [pl.BlockSpec((1, tm, d), row)]
    if add_moe:
        args += [moe, gate]
        in_specs += [_flat_rows_spec(tm, d, n // tm, moe_row0), _mod_spec(gate)]
    args += [g.reshape(1, d), shift, scale, w]
    in_specs += [pl.BlockSpec((1, d), full2), _mod_spec(shift), _mod_spec(scale),
                 pl.BlockSpec((d, nout), full2)]
    if mode == "even":
        args += list(extra)
        in_specs += [pl.BlockSpec(extra[0].shape, full2)]
        out_shape = [jax.ShapeDtypeStruct((b, n, FOURIER_WIDTH), BF16),
                     jax.ShapeDtypeStruct((b, n, FOURIER_WIDTH), BF16),
                     jax.ShapeDtypeStruct((b, n, CONV_WIDTH), F32),
                     jax.ShapeDtypeStruct((b, n, CONV_WIDTH), F32)]
        out_specs = [pl.BlockSpec((1, tm, 512), row)] * 4
    elif mode == "odd":
        args += list(extra)
        in_specs += [pl.BlockSpec((tm, LANES), lambda bb, i: (i, 0))] * 4
        out_shape = [jax.ShapeDtypeStruct((b, n, d), F32),
                     jax.ShapeDtypeStruct((b, n, POOL_WIDTH), F32),
                     jax.ShapeDtypeStruct((b, n, ATTN_WIDTH), BF16),
                     jax.ShapeDtypeStruct((b, n, 2 * KV_WIDTH), BF16),
                     jax.ShapeDtypeStruct((b, n, 2 * KV_WIDTH), BF16)]
        out_specs = [pl.BlockSpec((1, tm, d), row), pl.BlockSpec((1, tm, POOL_WIDTH), row),
                     pl.BlockSpec((1, tm, ATTN_WIDTH), row),
                     pl.BlockSpec((1, tm, 2 * KV_WIDTH), row), pl.BlockSpec((1, tm, 2 * KV_WIDTH), row)]
    else:
        out_shape = [jax.ShapeDtypeStruct((b, n, nout), BF16)]
        out_specs = [pl.BlockSpec((1, tm, nout), row)]
    return pl.pallas_call(
        functools.partial(_inproj_kernel, mode=mode, add_moe=add_moe),
        grid=(b, n // tm),
        in_specs=in_specs,
        out_specs=out_specs,
        out_shape=out_shape,
        compiler_params=_cparams(("arbitrary", "arbitrary")),
        name="inproj_" + mode,
    )(*args)


DFT_RADIX = 64


def _dft_kernel(ca_ref, sa_ref, cb_ref, sb_ref, uc_ref, us_ref, o_ref, c_scr, s_scr, *, norm):
    @pl.when(pl.program_id(1) == 0)
    def _():
        cb, sb = cb_ref[...], sb_ref[...]
        for r in range(ca_ref.shape[0]):
            ca, sa = ca_ref[r:r + 1, :], sa_ref[r:r + 1, :]
            rows = slice(r * DFT_RADIX, (r + 1) * DFT_RADIX)
            c_scr[rows, :] = (ca * cb - sa * sb).astype(BF16)
            s_scr[rows, :] = (-(sa * cb + ca * sb)).astype(BF16)

    acc = _dot(c_scr[...], uc_ref[0]) + _dot(s_scr[...], us_ref[0])
    o_ref[0] = (acc * norm).astype(BF16)


def _dft_call(tabs, uc, us, *, tm):
    b, n, wdt = uc.shape
    r_tile = tm // DFT_RADIX
    norm = 1.0 / math.sqrt(n * FOURIER_HEAD_DIM)
    a_spec = pl.BlockSpec((r_tile, n), lambda i, bb: (i, 0))
    b_spec = pl.BlockSpec((DFT_RADIX, n), lambda i, bb: (0, 0))
    u_spec = pl.BlockSpec((1, n, wdt), lambda i, bb: (bb, 0, 0))
    return pl.pallas_call(
        functools.partial(_dft_kernel, norm=norm),
        grid=(n // tm, b),
        in_specs=[a_spec, a_spec, b_spec, b_spec, u_spec, u_spec],
        out_specs=pl.BlockSpec((1, tm, wdt), lambda i, bb: (bb, i, 0)),
        out_shape=jax.ShapeDtypeStruct((b, n, wdt), BF16),
        scratch_shapes=[pltpu.VMEM((tm, n), BF16), pltpu.VMEM((tm, n), BF16)],
        compiler_params=_cparams(("arbitrary", "arbitrary")),
        name="dft_rows",
    )(*tabs, uc, us)


def _tail(y, h_ref, gate_ref, g2_ref, sh_ref, sc_ref, rwt_ref, rb_ref, hout_ref, fx_ref, bk_ref):
    hn = h_ref[0] + gate_ref[0] * y
    hout_ref[0] = hn
    f = _rms_mod(hn, g2_ref[...], sh_ref[0], sc_ref[0])
    tm = f.shape[0]
    logits = lax.dot_general(rwt_ref[...], f, (((1,), (1,)), ((), ())),
                             preferred_element_type=F32, precision=HIGHEST)
    aff = jax.nn.sigmoid(logits)
    sel = aff + rb_ref[...]
    best = bucket = a_lo = a_hi = None
    for bkt in range(N_BUCKETS):
        lo, hi = _BUCKET_LO[bkt], _BUCKET_HI[bkt]
        pair_sum = sel[lo:lo + 1, :] + sel[hi:hi + 1, :]
        if bkt == 0:
            best, bucket = pair_sum, jnp.zeros_like(pair_sum)
            a_lo, a_hi = aff[lo:lo + 1, :], aff[hi:hi + 1, :]
        else:
            upd = pair_sum > best
            best = jnp.where(upd, pair_sum, best)
            bucket = jnp.where(upd, float(bkt), bucket)
            a_lo = jnp.where(upd, aff[lo:lo + 1, :], a_lo)
            a_hi = jnp.where(upd, aff[hi:hi + 1, :], a_hi)
    den = a_lo + a_hi
    info = jnp.concatenate([bucket, a_lo / den, a_hi / den, jnp.zeros((LANES - 3, tm), F32)], axis=0)
    fx_ref[0, :, :D_MODEL] = f
    fx_ref[0, :, D_MODEL:] = info.T
    bk_ref[...] = jnp.concatenate([bucket, jnp.zeros((7, tm), F32)], axis=0)


def _tail_specs(h, gate, shift, scale, tm):
    b, n, d = h.shape
    nt = n // tm
    row = lambda bb, i: (bb, i, 0)
    in_specs = [pl.BlockSpec((1, tm, d), row), _mod_spec(gate),
                pl.BlockSpec((1, d), lambda bb, i: (0, 0)), _mod_spec(shift), _mod_spec(scale),
                pl.BlockSpec((N_EXPERTS, d), lambda bb, i: (0, 0)),
                pl.BlockSpec((N_EXPERTS, 1), lambda bb, i: (0, 0))]
    out_shape = [jax.ShapeDtypeStruct((b, n, d), F32), jax.ShapeDtypeStruct((b, n, ROW_WIDTH), F32),
                 jax.ShapeDtypeStruct((8, b * n), F32)]
    out_specs = [pl.BlockSpec((1, tm, d), row), pl.BlockSpec((1, tm, ROW_WIDTH), row),
                 pl.BlockSpec((8, tm), lambda bb, i: (0, bb * nt + i))]
    return in_specs, out_shape, out_specs


def _even_out_kernel(yf_ref, gb_ref, gu_ref, gp_ref, gn_ref, cw_ref, wo_ref,
                     h_ref, gate_ref, g2_ref, sh_ref, sc_ref, rwt_ref, rb_ref,
                     hout_ref, fx_ref, bk_ref):
    i = pl.program_id(1)
    last = pl.num_programs(1) - 1
    gu = gu_ref[0]
    tm = gu.shape[0]
    prev = jnp.where(i > 0, gp_ref[0, 7:8, :], 0.0)
    nxt = jnp.where(i < last, gn_ref[0, 0:1, :], 0.0)
    row = lax.broadcasted_iota(jnp.int32, gu.shape, 0)
    up = jnp.where(row == 0, prev, pltpu.roll(gu, 1, 0))
    dn = jnp.where(row == tm - 1, nxt, pltpu.roll(gu, tm - 1, 0))
    conv = up * cw_ref[0:1, :] + gu * cw_ref[1:2, :] + dn * cw_ref[2:3, :]
    yc = (gb_ref[0] * conv).astype(BF16)
    y = _dot(yf_ref[0], wo_ref[:FOURIER_WIDTH, :]) + _dot(yc, wo_ref[FOURIER_WIDTH:, :])
    _tail(y, h_ref, gate_ref, g2_ref, sh_ref, sc_ref, rwt_ref, rb_ref, hout_ref, fx_ref, bk_ref)


def _even_out_call(yf, gb, gu, conv_w, w_out, h, gate, g2, shift, scale, rwt, rb, *, tm):
    b, n, d = h.shape
    row = lambda bb, i: (bb, i, 0)
    nb8 = n // 8
    t8 = tm // 8
    tin, out_shape, out_specs = _tail_specs(h, gate, shift, scale, tm)
    in_specs = [pl.BlockSpec((1, tm, 512), row), pl.BlockSpec((1, tm, 512), row),
                pl.BlockSpec((1, tm, 512), row),
                pl.BlockSpec((1, 8, 512), lambda bb, i: (bb, jnp.maximum(i * t8 - 1, 0), 0)),
                pl.BlockSpec((1, 8, 512), lambda bb, i: (bb, jnp.minimum((i + 1) * t8, nb8 - 1), 0)),
                pl.BlockSpec((CONV_K, CONV_WIDTH), lambda bb, i: (0, 0)),
                pl.BlockSpec((d, d), lambda bb, i: (0, 0))] + tin
    return pl.pallas_call(
        _even_out_kernel,
        grid=(b, n // tm),
        in_specs=in_specs,
        out_specs=out_specs,
        out_shape=out_shape,
        compiler_params=_cparams(("arbitrary", "arbitrary")),
        name="even_out",
    )(yf, gb, gu, gu, gu, conv_w, w_out, h, gate, g2.reshape(1, d), shift, scale, rwt, rb)


def _odd_out_kernel(sink_ref, up_ref, upp_ref, upn_ref, q_ref, kc_ref, kp_ref, kn_ref,
                    vc_ref, vp_ref, vn_ref, kvx_ref, pw_ref, ps_ref, wo_ref,
                    h_ref, gate_ref, g2_ref, sh_ref, sc_ref, rwt_ref, rb_ref,
                    hout_ref, fx_ref, bk_ref, ext_ref, mix_ref, *, n_total):
    i = pl.program_id(1)
    last = pl.num_programs(1) - 1
    tq = q_ref.shape[1]
    nsub = tq // ATTN_BLOCK

    u = up_ref[0]
    ext_ref[0:POOL_HALO, :] = jnp.where(i > 0, upp_ref[0], 0.0)
    ext_ref[POOL_HALO:POOL_HALO + tq, :] = u
    ext_ref[POOL_HALO + tq:, :] = jnp.where(i < last, upn_ref[0], 0.0)
    t = i * tq + lax.broadcasted_iota(jnp.int32, (tq, LANES), 0)
    for gi, win in enumerate(POOL_WINDOWS):
        r = win // 2
        cols = slice(gi * LANES, (gi + 1) * LANES)
        acc = ext_ref[POOL_HALO - r:POOL_HALO - r + tq, cols]
        for dlt in range(-r + 1, r + 1):
            acc = acc + ext_ref[POOL_HALO + dlt:POOL_HALO + dlt + tq, cols]
        cnt = (jnp.minimum(t + r + 1, n_total) - jnp.maximum(t - r, 0)).astype(F32)
        p = acc / cnt - u[:, cols]
        y = _dot(p.astype(BF16), pw_ref[gi]) * ps_ref[:, cols]
        mix_ref[:, cols] = y.astype(BF16)

    kwin = jnp.concatenate([kp_ref[0], kc_ref[0], kn_ref[0]], axis=0)
    vwin = jnp.concatenate([vp_ref[0], vc_ref[0], vn_ref[0]], axis=0)
    kvx = kvx_ref[0]
    kx, vx = kvx[:, :2 * KV_WIDTH], kvx[:, 2 * KV_WIDTH:]
    low = (lax.broadcasted_iota(jnp.int32, (1, 2 * KV_WIDTH), 1) % LANES) < HEAD_DIM
    zero = jnp.zeros((), BF16)
    k_half = (jnp.where(low, kwin, zero), jnp.where(low, zero, kwin))
    v_half = (jnp.where(low, vwin, zero), jnp.where(low, zero, vwin))
    kx_half = (jnp.where(low, kx, zero), jnp.where(low, zero, kx))
    vx_half = (jnp.where(low, vx, zero), jnp.where(low, zero, vx))

    span = 3 * ATTN_BLOCK
    rows2 = 2 * ATTN_BLOCK
    qi = lax.broadcasted_iota(jnp.int32, (rows2, span), 0) % ATTN_BLOCK
    kj = lax.broadcasted_iota(jnp.int32, (rows2, span), 1)
    in_prev = kj < ATTN_BLOCK
    in_next = kj >= 2 * ATTN_BLOCK
    neg_inf = jnp.float32(-jnp.inf)
    band_bias = (jnp.where(in_prev & (kj < qi), neg_inf, 0.0)
                 + jnp.where(in_next & (kj - 2 * ATTN_BLOCK > qi), neg_inf, 0.0))
    top_rows = lax.broadcasted_iota(jnp.int32, (rows2, 1), 0) < ATTN_BLOCK

    for j in range(nsub):
        blk = i * nsub + j
        prev_bias = jnp.where(blk > 0, 0.0, neg_inf)
        next_bias = jnp.where(blk < (n_total // ATTN_BLOCK) - 1, 0.0, neg_inf)
        bias = band_bias + jnp.where(in_prev, prev_bias, 0.0) + jnp.where(in_next, next_bias, 0.0)
        r0 = j * ATTN_BLOCK
        for kh in range(N_KV_HEADS):
            kcols = slice(kh * LANES, (kh + 1) * LANES)
            xq = jnp.concatenate(
                [q_ref[0, r0:r0 + ATTN_BLOCK, (2 * kh + pr) * LANES:(2 * kh + pr + 1) * LANES]
                 for pr in range(2)], axis=0)
            o_pair = None
            for half in range(2):
                ks = k_half[half][r0:r0 + span, kcols]
                vs = v_half[half][r0:r0 + span, kcols]
                s1 = _dot_nt(xq, ks) + bias
                s2 = _dot_nt(xq, kx_half[half][:, kcols])
                head0 = kh * 4 + half
                snk = jnp.where(top_rows, sink_ref[head0], sink_ref[head0 + 2])
                m = jnp.maximum(jnp.maximum(jnp.max(s1, axis=-1, keepdims=True),
                                            jnp.max(s2, axis=-1, keepdims=True)), snk)
                e1 = jnp.exp(s1 - m)
                e2 = jnp.exp(s2 - m)
                den = (jnp.sum(e1, axis=-1, keepdims=True) + jnp.sum(e2, axis=-1, keepdims=True)
                       + jnp.exp(snk - m))
                o = _dot(e1.astype(BF16), vs) + _dot(e2.astype(BF16), vx_half[half][:, kcols])
                o = o / den
                o_pair = o if o_pair is None else o_pair + o
            for pr in range(2):
                c0 = POOL_WIDTH + (2 * kh + pr) * LANES
                mix_ref[r0:r0 + ATTN_BLOCK, c0:c0 + LANES] = (
                    o_pair[pr * ATTN_BLOCK:(pr + 1) * ATTN_BLOCK].astype(BF16))

    y = _dot(mix_ref[...], wo_ref[...])
    _tail(y, h_ref, gate_ref, g2_ref, sh_ref, sc_ref, rwt_ref, rb_ref, hout_ref, fx_ref, bk_ref)


def _odd_out_call(sink, up, q, kd, vd, kvx, pool_w, pool_scale, w_out,
                  h, gate, g2, shift, scale, rwt, rb, *, tq):
    b, n, d = h.shape
    row = lambda bb, i: (bb, i, 0)
    nb8, t8 = n // POOL_HALO, tq // POOL_HALO
    nbk, tk = n // ATTN_BLOCK, tq // ATTN_BLOCK
    prev8 = lambda bb, i: (bb, jnp.maximum(i * t8 - 1, 0), 0)
    next8 = lambda bb, i: (bb, jnp.minimum((i + 1) * t8, nb8 - 1), 0)
    prevk = lambda bb, i: (bb, jnp.maximum(i * tk - 1, 0), 0)
    nextk = lambda bb, i: (bb, jnp.minimum((i + 1) * tk, nbk - 1), 0)
    kvw = 2 * KV_WIDTH
    tin, out_shape, out_specs = _tail_specs(h, gate, shift, scale, tq)
    in_specs = [pl.BlockSpec(memory_space=pltpu.SMEM),
                pl.BlockSpec((1, tq, POOL_WIDTH), row),
                pl.BlockSpec((1, POOL_HALO, POOL_WIDTH), prev8),
                pl.BlockSpec((1, POOL_HALO, POOL_WIDTH), next8),
                pl.BlockSpec((1, tq, ATTN_WIDTH), row),
                pl.BlockSpec((1, tq, kvw), row),
                pl.BlockSpec((1, ATTN_BLOCK, kvw), prevk),
                pl.BlockSpec((1, ATTN_BLOCK, kvw), nextk),
                pl.BlockSpec((1, tq, kvw), row),
                pl.BlockSpec((1, ATTN_BLOCK, kvw), prevk),
                pl.BlockSpec((1, ATTN_BLOCK, kvw), nextk),
                pl.BlockSpec((1, kvx.shape[1], 2 * kvw), lambda bb, i: (bb, 0, 0)),
                pl.BlockSpec(pool_w.shape, lambda bb, i: (0, 0, 0)),
                pl.BlockSpec((1, POOL_WIDTH), lambda bb, i: (0, 0)),
                pl.BlockSpec((d, d), lambda bb, i: (0, 0))] + tin
    return pl.pallas_call(
        functools.partial(_odd_out_kernel, n_total=n),
        grid=(b, n // tq),
        in_specs=in_specs,
        out_specs=out_specs,
        out_shape=out_shape,
        scratch_shapes=[pltpu.VMEM((tq + 2 * POOL_HALO, POOL_WIDTH), F32),
                        pltpu.VMEM((tq, d), BF16)],
        compiler_params=_cparams(("arbitrary", "arbitrary")),
        name="odd_out",
    )(sink, up, up, up, q, kd, kd, kd, vd, vd, vd, kvx, pool_w, pool_scale.reshape(1, POOL_WIDTH),
      w_out, h, gate, g2.reshape(1, d), shift, scale, rwt, rb)


def _moe_kernel(e_lo_ref, e_hi_ref, valid_ref, x_ref,
                g1_ref, u1_ref, d1_ref, g2_ref, u2_ref, d2_ref, o_ref):
    j = pl.program_id(0)

    @pl.when(valid_ref[j] != 0)
    def _():
        x = x_ref[:, :D_MODEL].astype(BF16)
        w_lo = x_ref[:, D_MODEL + INFO_W_LO:D_MODEL + INFO_W_LO + 1]
        w_hi = x_ref[:, D_MODEL + INFO_W_HI:D_MODEL + INFO_W_HI + 1]

        def expert(g_ref, u_ref, d_ref):
            gate = _dot(x, g_ref[0])
            hid = gate * jax.nn.sigmoid(gate) * _dot(x, u_ref[0])
            return _dot(hid.astype(BF16), d_ref[0])

        o_lo = expert(g1_ref, u1_ref, d1_ref)
        o_hi = expert(g2_ref, u2_ref, d2_ref)
        o_ref[...] = w_lo * o_lo + w_hi * o_hi

    @pl.when(valid_ref[j] == 0)
    def _():
        o_ref[...] = jnp.zeros(o_ref.shape, o_ref.dtype)


def _moe_call(tile_lo, tile_hi, tile_valid, xs, wg, wu, wd, *, tm):
    p = xs.shape[0]
    d = D_MODEL
    ntiles = p // tm
    lo3 = lambda j, lo, hi, v: (lo[j], 0, 0)
    hi3 = lambda j, lo, hi, v: (hi[j], 0, 0)
    rowm = lambda j, lo, hi, v: (j, 0)
    grid_spec = pltpu.PrefetchScalarGridSpec(
        num_scalar_prefetch=3,
        grid=(ntiles,),
        in_specs=[pl.BlockSpec((tm, ROW_WIDTH), rowm),
                  pl.BlockSpec((1, d, D_EXPERT), lo3), pl.BlockSpec((1, d, D_EXPERT), lo3),
                  pl.BlockSpec((1, D_EXPERT, d), lo3),
                  pl.BlockSpec((1, d, D_EXPERT), hi3), pl.BlockSpec((1, d, D_EXPERT), hi3),
                  pl.BlockSpec((1, D_EXPERT, d), hi3)],
        out_specs=pl.BlockSpec((tm, d), rowm),
    )
    return pl.pallas_call(
        _moe_kernel,
        grid_spec=grid_spec,
        out_shape=jax.ShapeDtypeStruct((p, d), F32),
        compiler_params=_cparams(("arbitrary",)),
        name="moe_pairs",
    )(tile_lo, tile_hi, tile_valid, xs, wg, wu, wd, wg, wu, wd)


RANK_ROWS = 32


def _rank_kernel(bk_ref, tri_ref, rk_ref, cnt_ref, carry_ref):
    @pl.when(pl.program_id(0) == 0)
    def _():
        carry_ref[...] = jnp.zeros(carry_ref.shape, F32)

    tr = bk_ref.shape[1]
    bucket = bk_ref[0:1, :]
    rows = lax.broadcasted_iota(jnp.int32, (RANK_ROWS, tr), 0).astype(F32)
    onehot = jnp.where(rows == bucket, 1.0, 0.0)
    before = _dot(onehot.astype(BF16), tri_ref[...]) + carry_ref[:, 0:1]
    rank = jnp.sum(onehot * before, axis=0, keepdims=True)
    rk_ref[...] = jnp.broadcast_to(rank, rk_ref.shape)
    carry_ref[...] = carry_ref[...] + jnp.sum(onehot, axis=1, keepdims=True)
    cnt_ref[...] = carry_ref[...]


def _rank_call(bk, *, tr):
    t = bk.shape[1]
    tri = jnp.asarray(np.triu(np.ones((tr, tr), np.float32), 1), BF16)
    return pl.pallas_call(
        _rank_kernel,
        grid=(t // tr,),
        in_specs=[pl.BlockSpec((8, tr), lambda j: (0, j)), pl.BlockSpec((tr, tr), lambda j: (0, 0))],
        out_specs=[pl.BlockSpec((8, tr), lambda j: (0, j)), pl.BlockSpec((RANK_ROWS, LANES), lambda j: (0, 0))],
        out_shape=[jax.ShapeDtypeStruct((8, t), F32), jax.ShapeDtypeStruct((RANK_ROWS, LANES), F32)],
        scratch_shapes=[pltpu.VMEM((RANK_ROWS, LANES), F32)],
        compiler_params=_cparams(("arbitrary",)),
        name="bucket_rank",
    )(bk, tri)


ROW_TILE = 512


def _wait_rows(src_hbm, dst_hbm, sem, nrows):
    pltpu.make_async_copy(src_hbm.at[pl.ds(0, nrows)], dst_hbm.at[pl.ds(0, nrows)], sem).wait()


def _scatter_rows_kernel(pos_ref, *refs, tile_starts):
    nsrc = len(tile_starts)
    srcs, out_hbm, sem = refs[:nsrc], refs[nsrc + 1], refs[nsrc + 2]
    j = pl.program_id(0)

    def issue(src_hbm, base):
        def body(r, carry):
            pltpu.make_async_copy(src_hbm.at[pl.ds(base + r, 1)],
                                  out_hbm.at[pl.ds(pos_ref[0, 0, r], 1)], sem).start()
            return carry
        lax.fori_loop(0, ROW_TILE, body, 0, unroll=8)

    for s in range(nsrc):
        first = tile_starts[s]
        if nsrc == 1:
            issue(srcs[s], j * ROW_TILE)
        else:
            in_range = (j >= first) if s == nsrc - 1 else ((j >= first) & (j < tile_starts[s + 1]))
            pl.when(in_range)(functools.partial(issue, srcs[s], (j - first) * ROW_TILE))

    _wait_rows(srcs[0], out_hbm, sem, ROW_TILE)


def _scatter_rows_call(pos, sources, nslots):
    w = sources[0].shape[1]
    t = pos.shape[0]
    ntiles = t // ROW_TILE
    tile_starts, acc = [], 0
    for s in sources:
        tile_starts.append(acc)
        acc += s.shape[0] // ROW_TILE
    any_spec = pl.BlockSpec(memory_space=pl.ANY)
    nsrc = len(sources)
    return pl.pallas_call(
        functools.partial(_scatter_rows_kernel, tile_starts=tuple(tile_starts)),
        grid=(ntiles,),
        in_specs=[pl.BlockSpec((1, 1, ROW_TILE), lambda j: (j, 0, 0), memory_space=pltpu.SMEM)]
        + [any_spec] * (nsrc + 1),
        out_specs=any_spec,
        out_shape=jax.ShapeDtypeStruct((nslots, w), F32),
        scratch_shapes=[pltpu.SemaphoreType.DMA(())],
        input_output_aliases={nsrc + 1: 0},
        compiler_params=_cparams(("arbitrary",)),
        name="scatter_rows",
    )(pos.reshape(ntiles, 1, ROW_TILE), *sources, jnp.zeros((nslots, w), F32))


def _gather_rows_kernel(pos_ref, src_hbm, out_hbm, sem):
    base = pl.program_id(0) * ROW_TILE

    def body(r, carry):
        pltpu.make_async_copy(src_hbm.at[pl.ds(pos_ref[0, 0, r], 1)],
                              out_hbm.at[pl.ds(base + r, 1)], sem).start()
        return carry
    lax.fori_loop(0, ROW_TILE, body, 0, unroll=8)
    _wait_rows(src_hbm, out_hbm, sem, ROW_TILE)


def _gather_rows_call(pos, src):
    t = pos.shape[0]
    ntiles = t // ROW_TILE
    any_spec = pl.BlockSpec(memory_space=pl.ANY)
    return pl.pallas_call(
        _gather_rows_kernel,
        grid=(ntiles,),
        in_specs=[pl.BlockSpec((1, 1, ROW_TILE), lambda j: (j, 0, 0), memory_space=pltpu.SMEM), any_spec],
        out_specs=any_spec,
        out_shape=jax.ShapeDtypeStruct((t, src.shape[1]), F32),
        scratch_shapes=[pltpu.SemaphoreType.DMA(())],
        compiler_params=_cparams(("arbitrary",)),
        name="gather_rows",
    )(pos.reshape(ntiles, 1, ROW_TILE), src)


def _final_kernel(h_ref, moe_ref, gate_ref, g_ref, o_ref):
    x = h_ref[0] + gate_ref[0] * moe_ref[...]
    o_ref[0] = x * lax.rsqrt(jnp.mean(x * x, axis=-1, keepdims=True) + EPS) * g_ref[...]


def _final_call(h, moe, gate, g, *, tm):
    b, n, d = h.shape
    row = lambda bb, i: (bb, i, 0)
    return pl.pallas_call(
        _final_kernel,
        grid=(b, n // tm),
        in_specs=[pl.BlockSpec((1, tm, d), row), _flat_rows_spec(tm, d, n // tm, 0), _mod_spec(gate),
                  pl.BlockSpec((1, d), lambda bb, i: (0, 0))],
        out_specs=pl.BlockSpec((1, tm, d), row),
        out_shape=jax.ShapeDtypeStruct((b, n, d), F32),
        compiler_params=_cparams(("arbitrary", "arbitrary")),
        name="final_norm",
    )(h, moe, gate, g.reshape(1, d))


def _channel_dft_table():
    c = np.arange(FOURIER_HEAD_DIM)
    ang = 2.0 * np.pi * ((c[:, None] * c[None, :]) % FOURIER_HEAD_DIM) / FOURIER_HEAD_DIM
    return jnp.asarray(np.concatenate([np.cos(ang), np.sin(ang)], axis=1), F32)


def _position_dft_tables(n):
    n1 = n // DFT_RADIX
    t = np.arange(n)
    a = 2.0 * np.pi * ((np.arange(n1)[:, None] * t[None, :]) % n1) / n1
    bb = 2.0 * np.pi * ((np.arange(DFT_RADIX)[:, None] * t[None, :]) % n) / n
    return tuple(jnp.asarray(v, F32) for v in (np.cos(a), np.sin(a), np.cos(bb), np.sin(bb)))


def _rope_tables(n):
    quarter = HEAD_DIM // 4
    inv = ROPE_THETA ** (-jnp.arange(quarter, dtype=F32) / quarter)
    t = jnp.arange(n)
    ang_r = (t // GRID_W).astype(F32)[:, None] * inv
    ang_c = (t % GRID_W).astype(F32)[:, None] * inv
    cos = jnp.concatenate([jnp.cos(ang_r)] * 2 + [jnp.cos(ang_c)] * 2, axis=1)
    sin = jnp.concatenate([-jnp.sin(ang_r), jnp.sin(ang_r), -jnp.sin(ang_c), jnp.sin(ang_c)], axis=1)
    return jnp.tile(cos, (1, 2)), jnp.tile(sin, (1, 2))


def _dispatch_plan(bucket, rank, counts, tm):
    t = bucket.shape[0]
    ntiles = t // tm + N_BUCKETS
    tiles_per = (counts + tm - 1) // tm
    tile_end = jnp.cumsum(tiles_per)
    tile_start = tile_end - tiles_per
    onehot = bucket[:, None] == jnp.arange(N_BUCKETS, dtype=jnp.int32)[None, :]
    pos = jnp.sum(jnp.where(onehot, (tile_start * tm)[None, :], 0), axis=-1) + rank
    tile_ids = jnp.arange(ntiles, dtype=jnp.int32)
    used = tile_end[-1]
    tile_bucket = jnp.sum((tile_ids[:, None] >= tile_end[None, :]).astype(jnp.int32), axis=1)
    last_bucket = jnp.sum((jnp.maximum(used - 1, 0) >= tile_end).astype(jnp.int32))
    tile_valid = (tile_ids < used).astype(jnp.int32)
    tile_bucket = jnp.where(tile_valid == 1, tile_bucket, last_bucket)
    tile_lo = jnp.asarray(np.asarray(_BUCKET_LO, np.int32))[tile_bucket]
    tile_hi = jnp.asarray(np.asarray(_BUCKET_HI, np.int32))[tile_bucket]
    return pos.astype(jnp.int32), tile_lo, tile_hi, tile_valid, ntiles * tm


def _moe_layer(fx_list, bk_list, wg, wu, wd, *, tm):
    bk = bk_list[0] if len(bk_list) == 1 else jnp.concatenate(bk_list, axis=1)
    rk, cnt = _rank_call(bk, tr=ROW_TILE)
    pos, tile_lo, tile_hi, tile_valid, nslots = _dispatch_plan(
        bk[0].astype(jnp.int32), rk[0].astype(jnp.int32), cnt[:N_BUCKETS, 0].astype(jnp.int32), tm)
    xs = _scatter_rows_call(pos, fx_list, nslots)
    out_sorted = _moe_call(tile_lo, tile_hi, tile_valid, xs, wg, wu, wd, tm=tm)
    return _gather_rows_call(pos, out_sorted)


def _forward(x, c, ctx, c_ctx, ada_w, ada_b, norm_mix_g, norm_ffn_g, even_w_in, even_conv_w, even_w_out,
             odd_w_in, odd_pool_w, odd_pool_scale, odd_sink, odd_w_out, router_w, router_b,
             moe_w_gate, moe_w_up, moe_w_down, final_g, *, tm_lat, tm_ctx, tq, tm_dft, tm_moe):
    b, n, d = x.shape
    l = ctx.shape[1]

    rows = ((b + 1 + 7) // 8) * 8
    s_rows = jnp.zeros((rows, d), F32).at[:b].set(c).at[b].set(c_ctx)
    mods = _ada_call(s_rows, ada_w, ada_b)

    def mod_vecs(layer):
        m = mods[layer, :b].reshape(b, N_MOD, 1, d)
        mc = mods[layer, b].reshape(N_MOD, 1, 1, d)
        return [m[:, k] for k in range(N_MOD)], [mc[k] for k in range(N_MOD)]

    rwt = router_w.T
    rb = router_b.astype(F32).reshape(N_EXPERTS, 1)
    cs_tab = _channel_dft_table()

    m, mc = mod_vecs(0)
    w_in0 = even_w_in[0].astype(BF16)
    w_out0 = even_w_out[0].astype(BF16)

    def even_stream(h, mv, tm):
        nn = h.shape[1]
        uc, us, gb, gu = _inproj_call(h, norm_mix_g[0], mv[0], mv[1], w_in0, mode="even", tm=tm,
                                      extra=(cs_tab,))
        yf = _dft_call(_position_dft_tables(nn), uc, us, tm=min(tm_dft, nn))
        return _even_out_call(yf, gb, gu, even_conv_w[0], w_out0, h, mv[2], norm_ffn_g[0],
                              mv[3], mv[4], rwt, rb, tm=tm)

    h1, fx_lat, bk_lat = even_stream(x, m, tm_lat)
    hc1, fx_ctx, bk_ctx = even_stream(ctx, mc, tm_ctx)

    moe0 = _moe_layer([fx_lat.reshape(b * n, ROW_WIDTH), fx_ctx.reshape(b * l, ROW_WIDTH)], [bk_lat, bk_ctx],
                      moe_w_gate[0].astype(BF16), moe_w_up[0].astype(BF16), moe_w_down[0].astype(BF16),
                      tm=tm_moe)
    gate_lat0, gate_ctx0 = m[5], mc[5]

    m, mc = mod_vecs(1)
    w_in1 = odd_w_in[0]
    kv0 = POOL_WIDTH + ATTN_WIDTH
    wk, wv = w_in1[:, kv0:kv0 + KV_WIDTH], w_in1[:, kv0 + KV_WIDTH:]

    def dup_heads(wm):
        return jnp.concatenate([wm[:, :HEAD_DIM], wm[:, :HEAD_DIM], wm[:, HEAD_DIM:], wm[:, HEAD_DIM:]], axis=1)

    w_kv_dup = jnp.concatenate([dup_heads(wk), dup_heads(wv)], axis=1)
    w_lat1 = jnp.concatenate([w_in1[:, :kv0], w_kv_dup], axis=1).astype(BF16)
    w_out1 = odd_w_out[0].astype(BF16)

    cos_t, sin_t = _rope_tables(n)
    q_scale = HEAD_DIM ** -0.5
    h1b, up, q, kd, vd = _inproj_call(h1, norm_mix_g[1], m[0], m[1], w_lat1, mode="odd", tm=tm_lat,
                                      moe=moe0, moe_row0=0, gate=gate_lat0,
                                      extra=(cos_t * q_scale, sin_t * q_scale, cos_t, sin_t))
    (kvx,) = _inproj_call(hc1, norm_mix_g[1], mc[0], mc[1], w_kv_dup.astype(BF16), mode="plain", tm=tm_ctx,
                          moe=moe0, moe_row0=b * n, gate=gate_ctx0)
    h2, fx2, bk2 = _odd_out_call(odd_sink[0], up, q, kd, vd, kvx, odd_pool_w[0].astype(BF16),
                                 odd_pool_scale[0], w_out1, h1b, m[2], norm_ffn_g[1], m[3], m[4], rwt, rb, tq=tq)
    moe1 = _moe_layer([fx2.reshape(b * n, ROW_WIDTH)], [bk2], moe_w_gate[1].astype(BF16),
                      moe_w_up[1].astype(BF16), moe_w_down[1].astype(BF16), tm=tm_moe)
    return _final_call(h2, moe1, m[5], final_g, tm=tm_lat)


def kernel(x, c, ctx, c_ctx, ada_w, ada_b, norm_mix_g, norm_ffn_g, even_w_in, even_conv_w, even_w_out,
           odd_w_in, odd_pool_w, odd_pool_scale, odd_sink, odd_w_out, router_w, router_b,
           moe_w_gate, moe_w_up, moe_w_down, final_g):
    return _forward(x, c, ctx, c_ctx, ada_w, ada_b, norm_mix_g, norm_ffn_g, even_w_in, even_conv_w,
                    even_w_out, odd_w_in, odd_pool_w, odd_pool_scale, odd_sink, odd_w_out, router_w,
                    router_b, moe_w_gate, moe_w_up, moe_w_down, final_g,
                    tm_lat=512, tm_ctx=256, tq=256, tm_dft=512, tm_moe=512)
```

```python
import functools
import math

import numpy as np
import jax
import jax.numpy as jnp
from jax import lax
from jax.experimental import pallas as pl
from jax.experimental.pallas import tpu as pltpu

F32 = jnp.float32
BF16 = jnp.bfloat16

D_MODEL = 1024
GRID_W = 64
EPS = 1e-6
N_MOD = 6
FOURIER_HEADS = 4
FOURIER_HEAD_DIM = 128
FOURIER_WIDTH = 512
CONV_WIDTH = 512
CONV_K = 3
POOL_WINDOWS = (2, 4, 8, 16)
POOL_GROUP_DIM = 128
POOL_WIDTH = 512
POOL_HALO = 8
HEAD_DIM = 64
N_Q_HEADS = 8
N_KV_HEADS = 2
ATTN_WIDTH = 512
KV_WIDTH = 128
ATTN_BLOCK = 128
ROPE_THETA = 10000.0
N_EXPERTS = 16
N_GROUPS = 4
EXPERTS_PER_GROUP = 4
D_EXPERT = 512
N_PAIRS = 6
N_BUCKETS = N_GROUPS * N_PAIRS

_PAIRS = [(a, b) for a in range(EXPERTS_PER_GROUP) for b in range(a + 1, EXPERTS_PER_GROUP)]
_BUCKET_LO = [(k // N_PAIRS) * EXPERTS_PER_GROUP + _PAIRS[k % N_PAIRS][0] for k in range(N_BUCKETS)]
_BUCKET_HI = [(k // N_PAIRS) * EXPERTS_PER_GROUP + _PAIRS[k % N_PAIRS][1] for k in range(N_BUCKETS)]

LANES = 128
ROW_WIDTH = D_MODEL + LANES
INFO_BUCKET, INFO_W_LO, INFO_W_HI = 0, 1, 2
VMEM_LIMIT_BYTES = 48 * 1024 * 1024

HIGHEST = lax.Precision.HIGHEST


def _cparams(sem):
    return pltpu.CompilerParams(dimension_semantics=sem, vmem_limit_bytes=VMEM_LIMIT_BYTES)


def _rms_mod(x, g, shift, scale):
    y = x * lax.rsqrt(jnp.mean(x * x, axis=-1, keepdims=True) + EPS) * g
    return y * (1.0 + scale) + shift


def _dot(a, b):
    return jnp.dot(a, b, preferred_element_type=F32)


def _dot_nt(a, b):
    return lax.dot_general(a, b, (((1,), (1,)), ((), ())), preferred_element_type=F32)


def _ada_kernel(s_ref, w_ref, b_ref, o_ref):
    s = s_ref[...]
    s = s * jax.nn.sigmoid(s)
    o_ref[0] = jnp.dot(s, w_ref[0], preferred_element_type=F32, precision=HIGHEST) + b_ref[0]


def _ada_call(s_rows, ada_w, ada_b):
    depth, d, n6 = ada_w.shape
    r = s_rows.shape[0]
    tn = 1536
    return pl.pallas_call(
        _ada_kernel,
        grid=(depth, n6 // tn),
        in_specs=[
            pl.BlockSpec((r, d), lambda l, j: (0, 0)),
            pl.BlockSpec((1, d, tn), lambda l, j: (l, 0, j)),
            pl.BlockSpec((1, 1, tn), lambda l, j: (l, 0, j)),
        ],
        out_specs=pl.BlockSpec((1, r, tn), lambda l, j: (l, 0, j)),
        out_shape=jax.ShapeDtypeStruct((depth, r, n6), F32),
        compiler_params=_cparams(("arbitrary", "arbitrary")),
        name="ada_mod",
    )(s_rows, ada_w, ada_b.reshape(depth, 1, n6))


def _rope_group(x, cos, sin_signed):
    lane = lax.broadcasted_iota(jnp.int32, x.shape, 1)
    first_half = (lane % 32) < 16
    partner = jnp.where(first_half, pltpu.roll(x, LANES - 16, 1), pltpu.roll(x, 16, 1))
    return x * cos + partner * sin_signed


def _inproj_kernel(*refs, mode, add_moe):
    it = iter(refs)
    h_ref = next(it)
    if add_moe:
        moe_ref, gate_ref = next(it), next(it)
    g_ref, sh_ref, sc_ref, w_ref = next(it), next(it), next(it), next(it)
    x = h_ref[0]
    if add_moe:
        x = x + gate_ref[0] * moe_ref[...]
    a = _rms_mod(x, g_ref[...], sh_ref[0], sc_ref[0])
    proj = _dot(a.astype(BF16), w_ref[...])
    if mode == "even":
        cs_ref = next(it)
        uc_ref, us_ref, gb_ref, gu_ref = next(it), next(it), next(it), next(it)
        uf = proj[:, :FOURIER_WIDTH].astype(BF16)
        cs = cs_ref[...].astype(BF16)
        for hh in range(FOURIER_HEADS):
            cols = slice(hh * LANES, (hh + 1) * LANES)
            r = _dot(uf[:, cols], cs)
            uc_ref[0, :, cols] = r[:, :LANES].astype(BF16)
            us_ref[0, :, cols] = r[:, LANES:].astype(BF16)
        c0 = FOURIER_WIDTH
        gb_ref[0] = proj[:, c0:c0 + CONV_WIDTH]
        gu_ref[0] = proj[:, c0 + CONV_WIDTH:c0 + 2 * CONV_WIDTH] * proj[:, c0 + 2 * CONV_WIDTH:]
    elif mode == "odd":
        cq_ref, sq_ref, ck_ref, sk_ref = next(it), next(it), next(it), next(it)
        hn_ref, up_ref, q_ref, kd_ref, vd_ref = next(it), next(it), next(it), next(it), next(it)
        hn_ref[0] = x
        up_ref[0] = proj[:, :POOL_WIDTH]
        c0 = POOL_WIDTH
        for gi in range(ATTN_WIDTH // LANES):
            cols = slice(c0 + gi * LANES, c0 + (gi + 1) * LANES)
            q_ref[0, :, gi * LANES:(gi + 1) * LANES] = _rope_group(
                proj[:, cols], cq_ref[...], sq_ref[...]).astype(BF16)
        c0 += ATTN_WIDTH
        for gi in range(2 * KV_WIDTH // LANES):
            cols = slice(c0 + gi * LANES, c0 + (gi + 1) * LANES)
            kd_ref[0, :, gi * LANES:(gi + 1) * LANES] = _rope_group(
                proj[:, cols], ck_ref[...], sk_ref[...]).astype(BF16)
        c0 += 2 * KV_WIDTH
        vd_ref[0] = proj[:, c0:].astype(BF16)
    else:
        kv_ref = next(it)
        kv_ref[0] = proj.astype(BF16)


def _mod_spec(arr):
    if arr.shape[0] > 1:
        return pl.BlockSpec((1, 1, arr.shape[2]), lambda b, i: (b, 0, 0))
    return pl.BlockSpec((1, 1, arr.shape[2]), lambda b, i: (0, 0, 0))


def _flat_rows_spec(tm, d, nt, row0):
    tile0 = row0 // tm
    return pl.BlockSpec((tm, d), lambda bb, i: (tile0 + bb * nt + i, 0))


def _inproj_call(h, g, shift, scale, w, *, mode, tm, moe=None, moe_row0=0, gate=None, extra=()):
    b, n, d = h.shape
    nout = w.shape[1]
    add_moe = moe is not None
    row = lambda bb, i: (bb, i, 0)
    full2 = lambda bb, i: (0, 0)
    args = [h]
    in_specs = [pl.BlockSpec((1, tm, d), row)]
    if add_moe:
        args += [moe, gate]
        in_specs += [_flat_rows_spec(tm, d, n // tm, moe_row0), _mod_spec(gate)]
    args += [g.reshape(1, d), shift, scale, w]
    in_specs += [pl.BlockSpec((1, d), full2), _mod_spec(shift), _mod_spec(scale),
                 pl.BlockSpec((d, nout), full2)]
    if mode == "even":
        args += list(extra)
        in_specs += [pl.BlockSpec(extra[0].shape, full2)]
        out_shape = [jax.ShapeDtypeStruct((b, n, FOURIER_WIDTH), BF16),
                     jax.ShapeDtypeStruct((b, n, FOURIER_WIDTH), BF16),
                     jax.ShapeDtypeStruct((b, n, CONV_WIDTH), F32),
                     jax.ShapeDtypeStruct((b, n, CONV_WIDTH), F32)]
        out_specs = [pl.BlockSpec((1, tm, 512), row)] * 4
    elif mode == "odd":
        args += list(extra)
        in_specs += [pl.BlockSpec((tm, LANES), lambda bb, i: (i, 0))] * 4
        out_shape = [jax.ShapeDtypeStruct((b, n, d), F32),
                     jax.ShapeDtypeStruct((b, n, POOL_WIDTH), F32),
                     jax.ShapeDtypeStruct((b, n, ATTN_WIDTH), BF16),
                     jax.ShapeDtypeStruct((b, n, 2 * KV_WIDTH), BF16),
                     jax.ShapeDtypeStruct((b, n, 2 * KV_WIDTH), BF16)]
        out_specs = [pl.BlockSpec((1, tm, d), row), pl.BlockSpec((1, tm, POOL_WIDTH), row),
                     pl.BlockSpec((1, tm, ATTN_WIDTH), row),
                     pl.BlockSpec((1, tm, 2 * KV_WIDTH), row), pl.BlockSpec((1, tm, 2 * KV_WIDTH), row)]
    else:
        out_shape = [jax.ShapeDtypeStruct((b, n, nout), BF16)]
        out_specs = [pl.BlockSpec((1, tm, nout), row)]
    return pl.pallas_call(
        functools.partial(_inproj_kernel, mode=mode, add_moe=add_moe),
        grid=(b, n // tm),
        in_specs=in_specs,
        out_specs=out_specs,
        out_shape=out_shape,
        compiler_params=_cparams(("arbitrary", "arbitrary")),
        name="inproj_" + mode,
    )(*args)


DFT_RADIX = 64


def _dft_kernel(ca_ref, sa_ref, cb_ref, sb_ref, uc_ref, us_ref, o_ref, c_scr, s_scr, *, norm):
    @pl.when(pl.program_id(1) == 0)
    def _():
        cb, sb = cb_ref[...], sb_ref[...]
        for r in range(ca_ref.shape[0]):
            ca, sa = ca_ref[r:r + 1, :], sa_ref[r:r + 1, :]
            rows = slice(r * DFT_RADIX, (r + 1) * DFT_RADIX)
            c_scr[rows, :] = (ca * cb - sa * sb).astype(BF16)
            s_scr[rows, :] = (-(sa * cb + ca * sb)).astype(BF16)

    acc = _dot(c_scr[...], uc_ref[0]) + _dot(s_scr[...], us_ref[0])
    o_ref[0] = (acc * norm).astype(BF16)


def _dft_call(tabs, uc, us, *, tm):
    b, n, wdt = uc.shape
    r_tile = tm // DFT_RADIX
    norm = 1.0 / math.sqrt(n * FOURIER_HEAD_DIM)
    a_spec = pl.BlockSpec((r_tile, n), lambda i, bb: (i, 0))
    b_spec = pl.BlockSpec((DFT_RADIX, n), lambda i, bb: (0, 0))
    u_spec = pl.BlockSpec((1, n, wdt), lambda i, bb: (bb, 0, 0))
    return pl.pallas_call(
        functools.partial(_dft_kernel, norm=norm),
        grid=(n // tm, b),
        in_specs=[a_spec, a_spec, b_spec, b_spec, u_spec, u_spec],
        out_specs=pl.BlockSpec((1, tm, wdt), lambda i, bb: (bb, i, 0)),
        out_shape=jax.ShapeDtypeStruct((b, n, wdt), BF16),
        scratch_shapes=[pltpu.VMEM((tm, n), BF16), pltpu.VMEM((tm, n), BF16)],
        compiler_params=_cparams(("arbitrary", "arbitrary")),
        name="dft_rows",
    )(*tabs, uc, us)


def _tail(y, h_ref, gate_ref, g2_ref, sh_ref, sc_ref, rwt_ref, rb_ref, hout_ref, fx_ref, bk_ref):
    hn = h_ref[0] + gate_ref[0] * y
    hout_ref[0] = hn
    f = _rms_mod(hn, g2_ref[...], sh_ref[0], sc_ref[0])
    tm = f.shape[0]
    logits = lax.dot_general(rwt_ref[...], f, (((1,), (1,)), ((), ())),
                             preferred_element_type=F32, precision=HIGHEST)
    aff = jax.nn.sigmoid(logits)
    sel = aff + rb_ref[...]
    best = bucket = a_lo = a_hi = None
    for bkt in range(N_BUCKETS):
        lo, hi = _BUCKET_LO[bkt], _BUCKET_HI[bkt]
        pair_sum = sel[lo:lo + 1, :] + sel[hi:hi + 1, :]
        if bkt == 0:
            best, bucket = pair_sum, jnp.zeros_like(pair_sum)
            a_lo, a_hi = aff[lo:lo + 1, :], aff[hi:hi + 1, :]
        else:
            upd = pair_sum > best
            best = jnp.where(upd, pair_sum, best)
            bucket = jnp.where(upd, float(bkt), bucket)
            a_lo = jnp.where(upd, aff[lo:lo + 1, :], a_lo)
            a_hi = jnp.where(upd, aff[hi:hi + 1, :], a_hi)
    den = a_lo + a_hi
    info = jnp.concatenate([bucket, a_lo / den, a_hi / den, jnp.zeros((LANES - 3, tm), F32)], axis=0)
    fx_ref[0, :, :D_MODEL] = f
    fx_ref[0, :, D_MODEL:] = info.T
    bk_ref[...] = jnp.concatenate([bucket, jnp.zeros((7, tm), F32)], axis=0)


def _tail_specs(h, gate, shift, scale, tm):
    b, n, d = h.shape
    nt = n // tm
    row = lambda bb, i: (bb, i, 0)
    in_specs = [pl.BlockSpec((1, tm, d), row), _mod_spec(gate),
                pl.BlockSpec((1, d), lambda bb, i: (0, 0)), _mod_spec(shift), _mod_spec(scale),
                pl.BlockSpec((N_EXPERTS, d), lambda bb, i: (0, 0)),
                pl.BlockSpec((N_EXPERTS, 1), lambda bb, i: (0, 0))]
    out_shape = [jax.ShapeDtypeStruct((b, n, d), F32), jax.ShapeDtypeStruct((b, n, ROW_WIDTH), F32),
                 jax.ShapeDtypeStruct((8, b * n), F32)]
    out_specs = [pl.BlockSpec((1, tm, d), row), pl.BlockSpec((1, tm, ROW_WIDTH), row),
                 pl.BlockSpec((8, tm), lambda bb, i: (0, bb * nt + i))]
    return in_specs, out_shape, out_specs


def _even_out_kernel(yf_ref, gb_ref, gu_ref, gp_ref, gn_ref, cw_ref, wo_ref,
                     h_ref, gate_ref, g2_ref, sh_ref, sc_ref, rwt_ref, rb_ref,
                     hout_ref, fx_ref, bk_ref):
    i = pl.program_id(1)
    last = pl.num_programs(1) - 1
    gu = gu_ref[0]
    tm = gu.shape[0]
    prev = jnp.where(i > 0, gp_ref[0, 7:8, :], 0.0)
    nxt = jnp.where(i < last, gn_ref[0, 0:1, :], 0.0)
    row = lax.broadcasted_iota(jnp.int32, gu.shape, 0)
    up = jnp.where(row == 0, prev, pltpu.roll(gu, 1, 0))
    dn = jnp.where(row == tm - 1, nxt, pltpu.roll(gu, tm - 1, 0))
    conv = up * cw_ref[0:1, :] + gu * cw_ref[1:2, :] + dn * cw_ref[2:3, :]
    yc = (gb_ref[0] * conv).astype(BF16)
    y = _dot(yf_ref[0], wo_ref[:FOURIER_WIDTH, :]) + _dot(yc, wo_ref[FOURIER_WIDTH:, :])
    _tail(y, h_ref, gate_ref, g2_ref, sh_ref, sc_ref, rwt_ref, rb_ref, hout_ref, fx_ref, bk_ref)


def _even_out_call(yf, gb, gu, conv_w, w_out, h, gate, g2, shift, scale, rwt, rb, *, tm):
    b, n, d = h.shape
    row = lambda bb, i: (bb, i, 0)
    nb8 = n // 8
    t8 = tm // 8
    tin, out_shape, out_specs = _tail_specs(h, gate, shift, scale, tm)
    in_specs = [pl.BlockSpec((1, tm, 512), row), pl.BlockSpec((1, tm, 512), row),
                pl.BlockSpec((1, tm, 512), row),
                pl.BlockSpec((1, 8, 512), lambda bb, i: (bb, jnp.maximum(i * t8 - 1, 0), 0)),
                pl.BlockSpec((1, 8, 512), lambda bb, i: (bb, jnp.minimum((i + 1) * t8, nb8 - 1), 0)),
                pl.BlockSpec((CONV_K, CONV_WIDTH), lambda bb, i: (0, 0)),
                pl.BlockSpec((d, d), lambda bb, i: (0, 0))] + tin
    return pl.pallas_call(
        _even_out_kernel,
        grid=(b, n // tm),
        in_specs=in_specs,
        out_specs=out_specs,
        out_shape=out_shape,
        compiler_params=_cparams(("arbitrary", "arbitrary")),
        name="even_out",
    )(yf, gb, gu, gu, gu, conv_w, w_out, h, gate, g2.reshape(1, d), shift, scale, rwt, rb)


def _odd_out_kernel(sink_ref, up_ref, upp_ref, upn_ref, q_ref, kc_ref, kp_ref, kn_ref,
                    vc_ref, vp_ref, vn_ref, kvx_ref, pw_ref, ps_ref, wo_ref,
                    h_ref, gate_ref, g2_ref, sh_ref, sc_ref, rwt_ref, rb_ref,
                    hout_ref, fx_ref, bk_ref, ext_ref, mix_ref, *, n_total):
    i = pl.program_id(1)
    last = pl.num_programs(1) - 1
    tq = q_ref.shape[1]
    nsub = tq // ATTN_BLOCK

    u = up_ref[0]
    ext_ref[0:POOL_HALO, :] = jnp.where(i > 0, upp_ref[0], 0.0)
    ext_ref[POOL_HALO:POOL_HALO + tq, :] = u
    ext_ref[POOL_HALO + tq:, :] = jnp.where(i < last, upn_ref[0], 0.0)
    t = i * tq + lax.broadcasted_iota(jnp.int32, (tq, LANES), 0)
    for gi, win in enumerate(POOL_WINDOWS):
        r = win // 2
        cols = slice(gi * LANES, (gi + 1) * LANES)
        acc = ext_ref[POOL_HALO - r:POOL_HALO - r + tq, cols]
        for dlt in range(-r + 1, r + 1):
            acc = acc + ext_ref[POOL_HALO + dlt:POOL_HALO + dlt + tq, cols]
        cnt = (jnp.minimum(t + r + 1, n_total) - jnp.maximum(t - r, 0)).astype(F32)
        p = acc / cnt - u[:, cols]
        y = _dot(p.astype(BF16), pw_ref[gi]) * ps_ref[:, cols]
        mix_ref[:, cols] = y.astype(BF16)

    kwin = jnp.concatenate([kp_ref[0], kc_ref[0], kn_ref[0]], axis=0)
    vwin = jnp.concatenate([vp_ref[0], vc_ref[0], vn_ref[0]], axis=0)
    kvx = kvx_ref[0]
    kx, vx = kvx[:, :2 * KV_WIDTH], kvx[:, 2 * KV_WIDTH:]
    low = (lax.broadcasted_iota(jnp.int32, (1, 2 * KV_WIDTH), 1) % LANES) < HEAD_DIM
    zero = jnp.zeros((), BF16)
    k_half = (jnp.where(low, kwin, zero), jnp.where(low, zero, kwin))
    v_half = (jnp.where(low, vwin, zero), jnp.where(low, zero, vwin))
    kx_half = (jnp.where(low, kx, zero), jnp.where(low, zero, kx))
    vx_half = (jnp.where(low, vx, zero), jnp.where(low, zero, vx))

    span = 3 * ATTN_BLOCK
    rows2 = 2 * ATTN_BLOCK
    qi = lax.broadcasted_iota(jnp.int32, (rows2, span), 0) % ATTN_BLOCK
    kj = lax.broadcasted_iota(jnp.int32, (rows2, span), 1)
    in_prev = kj < ATTN_BLOCK
    in_next = kj >= 2 * ATTN_BLOCK
    neg_inf = jnp.float32(-jnp.inf)
    band_bias = (jnp.where(in_prev & (kj < qi), neg_inf, 0.0)
                 + jnp.where(in_next & (kj - 2 * ATTN_BLOCK > qi), neg_inf, 0.0))
    top_rows = lax.broadcasted_iota(jnp.int32, (rows2, 1), 0) < ATTN_BLOCK

    for j in range(nsub):
        blk = i * nsub + j
        prev_bias = jnp.where(blk > 0, 0.0, neg_inf)
        next_bias = jnp.where(blk < (n_total // ATTN_BLOCK) - 1, 0.0, neg_inf)
        bias = band_bias + jnp.where(in_prev, prev_bias, 0.0) + jnp.where(in_next, next_bias, 0.0)
        r0 = j * ATTN_BLOCK
        for kh in range(N_KV_HEADS):
            kcols = slice(kh * LANES, (kh + 1) * LANES)
            xq = jnp.concatenate(
                [q_ref[0, r0:r0 + ATTN_BLOCK, (2 * kh + pr) * LANES:(2 * kh + pr + 1) * LANES]
                 for pr in range(2)], axis=0)
            o_pair = None
            for half in range(2):
                ks = k_half[half][r0:r0 + span, kcols]
                vs = v_half[half][r0:r0 + span, kcols]
                s1 = _dot_nt(xq, ks) + bias
                s2 = _dot_nt(xq, kx_half[half][:, kcols])
                head0 = kh * 4 + half
                snk = jnp.where(top_rows, sink_ref[head0], sink_ref[head0 + 2])
                m = jnp.maximum(jnp.maximum(jnp.max(s1, axis=-1, keepdims=True),
                                            jnp.max(s2, axis=-1, keepdims=True)), snk)
                e1 = jnp.exp(s1 - m)
                e2 = jnp.exp(s2 - m)
                den = (jnp.sum(e1, axis=-1, keepdims=True) + jnp.sum(e2, axis=-1, keepdims=True)
                       + jnp.exp(snk - m))
                o = _dot(e1.astype(BF16), vs) + _dot(e2.astype(BF16), vx_half[half][:, kcols])
                o = o / den
                o_pair = o if o_pair is None else o_pair + o
            for pr in range(2):
                c0 = POOL_WIDTH + (2 * kh + pr) * LANES
                mix_ref[r0:r0 + ATTN_BLOCK, c0:c0 + LANES] = (
                    o_pair[pr * ATTN_BLOCK:(pr + 1) * ATTN_BLOCK].astype(BF16))

    y = _dot(mix_ref[...], wo_ref[...])
    _tail(y, h_ref, gate_ref, g2_ref, sh_ref, sc_ref, rwt_ref, rb_ref, hout_ref, fx_ref, bk_ref)


def _odd_out_call(sink, up, q, kd, vd, kvx, pool_w, pool_scale, w_out,
                  h, gate, g2, shift, scale, rwt, rb, *, tq):
    b, n, d = h.shape
    row = lambda bb, i: (bb, i, 0)
    nb8, t8 = n // POOL_HALO, tq // POOL_HALO
    nbk, tk = n // ATTN_BLOCK, tq // ATTN_BLOCK
    prev8 = lambda bb, i: (bb, jnp.maximum(i * t8 - 1, 0), 0)
    next8 = lambda bb, i: (bb, jnp.minimum((i + 1) * t8, nb8 - 1), 0)
    prevk = lambda bb, i: (bb, jnp.maximum(i * tk - 1, 0), 0)
    nextk = lambda bb, i: (bb, jnp.minimum((i + 1) * tk, nbk - 1), 0)
    kvw = 2 * KV_WIDTH
    tin, out_shape, out_specs = _tail_specs(h, gate, shift, scale, tq)
    in_specs = [pl.BlockSpec(memory_space=pltpu.SMEM),
                pl.BlockSpec((1, tq, POOL_WIDTH), row),
                pl.BlockSpec((1, POOL_HALO, POOL_WIDTH), prev8),
                pl.BlockSpec((1, POOL_HALO, POOL_WIDTH), next8),
                pl.BlockSpec((1, tq, ATTN_WIDTH), row),
                pl.BlockSpec((1, tq, kvw), row),
                pl.BlockSpec((1, ATTN_BLOCK, kvw), prevk),
                pl.BlockSpec((1, ATTN_BLOCK, kvw), nextk),
                pl.BlockSpec((1, tq, kvw), row),
                pl.BlockSpec((1, ATTN_BLOCK, kvw), prevk),
                pl.BlockSpec((1, ATTN_BLOCK, kvw), nextk),
                pl.BlockSpec((1, kvx.shape[1], 2 * kvw), lambda bb, i: (bb, 0, 0)),
                pl.BlockSpec(pool_w.shape, lambda bb, i: (0, 0, 0)),
                pl.BlockSpec((1, POOL_WIDTH), lambda bb, i: (0, 0)),
                pl.BlockSpec((d, d), lambda bb, i: (0, 0))] + tin
    return pl.pallas_call(
        functools.partial(_odd_out_kernel, n_total=n),
        grid=(b, n // tq),
        in_specs=in_specs,
        out_specs=out_specs,
        out_shape=out_shape,
        scratch_shapes=[pltpu.VMEM((tq + 2 * POOL_HALO, POOL_WIDTH), F32),
                        pltpu.VMEM((tq, d), BF16)],
        compiler_params=_cparams(("arbitrary", "arbitrary")),
        name="odd_out",
    )(sink, up, up, up, q, kd, kd, kd, vd, vd, vd, kvx, pool_w, pool_scale.reshape(1, POOL_WIDTH),
      w_out, h, gate, g2.reshape(1, d), shift, scale, rwt, rb)


def _moe_kernel(e_lo_ref, e_hi_ref, valid_ref, x_ref,
                g1_ref, u1_ref, d1_ref, g2_ref, u2_ref, d2_ref, o_ref):
    j = pl.program_id(0)

    @pl.when(valid_ref[j] != 0)
    def _():
        x = x_ref[:, :D_MODEL].astype(BF16)
        w_lo = x_ref[:, D_MODEL + INFO_W_LO:D_MODEL + INFO_W_LO + 1]
        w_hi = x_ref[:, D_MODEL + INFO_W_HI:D_MODEL + INFO_W_HI + 1]

        def expert(g_ref, u_ref, d_ref):
            gate = _dot(x, g_ref[0])
            hid = gate * jax.nn.sigmoid(gate) * _dot(x, u_ref[0])
            return _dot(hid.astype(BF16), d_ref[0])

        o_lo = expert(g1_ref, u1_ref, d1_ref)
        o_hi = expert(g2_ref, u2_ref, d2_ref)
        o_ref[...] = w_lo * o_lo + w_hi * o_hi

    @pl.when(valid_ref[j] == 0)
    def _():
        o_ref[...] = jnp.zeros(o_ref.shape, o_ref.dtype)


def _moe_call(tile_lo, tile_hi, tile_valid, xs, wg, wu, wd, *, tm):
    p = xs.shape[0]
    d = D_MODEL
    ntiles = p // tm
    lo3 = lambda j, lo, hi, v: (lo[j], 0, 0)
    hi3 = lambda j, lo, hi, v: (hi[j], 0, 0)
    rowm = lambda j, lo, hi, v: (j, 0)
    grid_spec = pltpu.PrefetchScalarGridSpec(
        num_scalar_prefetch=3,
        grid=(ntiles,),
        in_specs=[pl.BlockSpec((tm, ROW_WIDTH), rowm),
                  pl.BlockSpec((1, d, D_EXPERT), lo3), pl.BlockSpec((1, d, D_EXPERT), lo3),
                  pl.BlockSpec((1, D_EXPERT, d), lo3),
                  pl.BlockSpec((1, d, D_EXPERT), hi3), pl.BlockSpec((1, d, D_EXPERT), hi3),
                  pl.BlockSpec((1, D_EXPERT, d), hi3)],
        out_specs=pl.BlockSpec((tm, d), rowm),
    )
    return pl.pallas_call(
        _moe_kernel,
        grid_spec=grid_spec,
        out_shape=jax.ShapeDtypeStruct((p, d), F32),
        compiler_params=_cparams(("arbitrary",)),
        name="moe_pairs",
    )(tile_lo, tile_hi, tile_valid, xs, wg, wu, wd, wg, wu, wd)


RANK_ROWS = 32


def _rank_kernel(bk_ref, tri_ref, rk_ref, cnt_ref, carry_ref):
    @pl.when(pl.program_id(0) == 0)
    def _():
        carry_ref[...] = jnp.zeros(carry_ref.shape, F32)

    tr = bk_ref.shape[1]
    bucket = bk_ref[0:1, :]
    rows = lax.broadcasted_iota(jnp.int32, (RANK_ROWS, tr), 0).astype(F32)
    onehot = jnp.where(rows == bucket, 1.0, 0.0)
    before = _dot(onehot.astype(BF16), tri_ref[...]) + carry_ref[:, 0:1]
    rank = jnp.sum(onehot * before, axis=0, keepdims=True)
    rk_ref[...] = jnp.broadcast_to(rank, rk_ref.shape)
    carry_ref[...] = carry_ref[...] + jnp.sum(onehot, axis=1, keepdims=True)
    cnt_ref[...] = carry_ref[...]


def _rank_call(bk, *, tr):
    t = bk.shape[1]
    tri = jnp.asarray(np.triu(np.ones((tr, tr), np.float32), 1), BF16)
    return pl.pallas_call(
        _rank_kernel,
        grid=(t // tr,),
        in_specs=[pl.BlockSpec((8, tr), lambda j: (0, j)), pl.BlockSpec((tr, tr), lambda j: (0, 0))],
        out_specs=[pl.BlockSpec((8, tr), lambda j: (0, j)), pl.BlockSpec((RANK_ROWS, LANES), lambda j: (0, 0))],
        out_shape=[jax.ShapeDtypeStruct((8, t), F32), jax.ShapeDtypeStruct((RANK_ROWS, LANES), F32)],
        scratch_shapes=[pltpu.VMEM((RANK_ROWS, LANES), F32)],
        compiler_params=_cparams(("arbitrary",)),
        name="bucket_rank",
    )(bk, tri)


ROW_TILE = 512


def _scatter_rows_kernel(pos_ref, *refs, tile_starts):
    nsrc = len(tile_starts)
    srcs, out_hbm, sem = refs[:nsrc], refs[nsrc + 1], refs[nsrc + 2]
    j = pl.program_id(0)

    def move(src_vmem):
        def body(r, carry):
            pltpu.make_async_copy(src_vmem.at[pl.ds(r, 1)],
                                  out_hbm.at[pl.ds(pos_ref[0, 0, r], 1)], sem).start()
            return carry
        lax.fori_loop(0, ROW_TILE, body, 0, unroll=8)
        pltpu.make_async_copy(src_vmem, out_hbm.at[pl.ds(0, ROW_TILE)], sem).wait()

    for s in range(nsrc):
        first = tile_starts[s]
        if nsrc == 1:
            move(srcs[s])
        else:
            in_range = (j >= first) if s == nsrc - 1 else ((j >= first) & (j < tile_starts[s + 1]))
            pl.when(in_range)(functools.partial(move, srcs[s]))


def _scatter_rows_call(pos, sources, nslots):
    w = sources[0].shape[1]
    t = pos.shape[0]
    ntiles = t // ROW_TILE
    tile_starts, src_specs, acc = [], [], 0
    for s in sources:
        first, count = acc, s.shape[0] // ROW_TILE
        tile_starts.append(first)
        src_specs.append(pl.BlockSpec(
            (ROW_TILE, w), lambda j, first=first, count=count: (jnp.clip(j - first, 0, count - 1), 0)))
        acc += count
    any_spec = pl.BlockSpec(memory_space=pl.ANY)
    nsrc = len(sources)
    return pl.pallas_call(
        functools.partial(_scatter_rows_kernel, tile_starts=tuple(tile_starts)),
        grid=(ntiles,),
        in_specs=[pl.BlockSpec((1, 1, ROW_TILE), lambda j: (j, 0, 0), memory_space=pltpu.SMEM)]
        + src_specs + [any_spec],
        out_specs=any_spec,
        out_shape=jax.ShapeDtypeStruct((nslots, w), F32),
        scratch_shapes=[pltpu.SemaphoreType.DMA(())],
        input_output_aliases={nsrc + 1: 0},
        compiler_params=_cparams(("arbitrary",)),
        name="scatter_rows",
    )(pos.reshape(ntiles, 1, ROW_TILE), *sources, jnp.zeros((nslots, w), F32))


def _gather_rows_kernel(pos_ref, src_hbm, out_ref, sem):
    def body(r, carry):
        pltpu.make_async_copy(src_hbm.at[pl.ds(pos_ref[0, 0, r], 1)], out_ref.at[pl.ds(r, 1)], sem).start()
        return carry
    lax.fori_loop(0, ROW_TILE, body, 0, unroll=8)
    pltpu.make_async_copy(src_hbm.at[pl.ds(0, ROW_TILE)], out_ref, sem).wait()


def _gather_rows_call(pos, src):
    t = pos.shape[0]
    w = src.shape[1]
    ntiles = t // ROW_TILE
    return pl.pallas_call(
        _gather_rows_kernel,
        grid=(ntiles,),
        in_specs=[pl.BlockSpec((1, 1, ROW_TILE), lambda j: (j, 0, 0), memory_space=pltpu.SMEM),
                  pl.BlockSpec(memory_space=pl.ANY)],
        out_specs=pl.BlockSpec((ROW_TILE, w), lambda j: (j, 0)),
        out_shape=jax.ShapeDtypeStruct((t, w), F32),
        scratch_shapes=[pltpu.SemaphoreType.DMA(())],
        compiler_params=_cparams(("arbitrary",)),
        name="gather_rows",
    )(pos.reshape(ntiles, 1, ROW_TILE), src)


def _final_kernel(h_ref, moe_ref, gate_ref, g_ref, o_ref):
    x = h_ref[0] + gate_ref[0] * moe_ref[...]
    o_ref[0] = x * lax.rsqrt(jnp.mean(x * x, axis=-1, keepdims=True) + EPS) * g_ref[...]


def _final_call(h, moe, gate, g, *, tm):
    b, n, d = h.shape
    row = lambda bb, i: (bb, i, 0)
    return pl.pallas_call(
        _final_kernel,
        grid=(b, n // tm),
        in_specs=[pl.BlockSpec((1, tm, d), row), _flat_rows_spec(tm, d, n // tm, 0), _mod_spec(gate),
                  pl.BlockSpec((1, d), lambda bb, i: (0, 0))],
        out_specs=pl.BlockSpec((1, tm, d), row),
        out_shape=jax.ShapeDtypeStruct((b, n, d), F32),
        compiler_params=_cparams(("arbitrary", "arbitrary")),
        name="final_norm",
    )(h, moe, gate, g.reshape(1, d))


def _channel_dft_table():
    c = np.arange(FOURIER_HEAD_DIM)
    ang = 2.0 * np.pi * ((c[:, None] * c[None, :]) % FOURIER_HEAD_DIM) / FOURIER_HEAD_DIM
    return jnp.asarray(np.concatenate([np.cos(ang), np.sin(ang)], axis=1), F32)


def _position_dft_tables(n):
    n1 = n // DFT_RADIX
    t = np.arange(n)
    a = 2.0 * np.pi * ((np.arange(n1)[:, None] * t[None, :]) % n1) / n1
    bb = 2.0 * np.pi * ((np.arange(DFT_RADIX)[:, None] * t[None, :]) % n) / n
    return tuple(jnp.asarray(v, F32) for v in (np.cos(a), np.sin(a), np.cos(bb), np.sin(bb)))


def _rope_tables(n):
    quarter = HEAD_DIM // 4
    inv = ROPE_THETA ** (-jnp.arange(quarter, dtype=F32) / quarter)
    t = jnp.arange(n)
    ang_r = (t // GRID_W).astype(F32)[:, None] * inv
    ang_c = (t % GRID_W).astype(F32)[:, None] * inv
    cos = jnp.concatenate([jnp.cos(ang_r)] * 2 + [jnp.cos(ang_c)] * 2, axis=1)
    sin = jnp.concatenate([-jnp.sin(ang_r), jnp.sin(ang_r), -jnp.sin(ang_c), jnp.sin(ang_c)], axis=1)
    return jnp.tile(cos, (1, 2)), jnp.tile(sin, (1, 2))


def _dispatch_plan(bucket, rank, counts, tm):
    t = bucket.shape[0]
    ntiles = t // tm + N_BUCKETS
    tiles_per = (counts + tm - 1) // tm
    tile_end = jnp.cumsum(tiles_per)
    tile_start = tile_end - tiles_per
    onehot = bucket[:, None] == jnp.arange(N_BUCKETS, dtype=jnp.int32)[None, :]
    pos = jnp.sum(jnp.where(onehot, (tile_start * tm)[None, :], 0), axis=-1) + rank
    tile_ids = jnp.arange(ntiles, dtype=jnp.int32)
    used = tile_end[-1]
    tile_bucket = jnp.sum((tile_ids[:, None] >= tile_end[None, :]).astype(jnp.int32), axis=1)
    last_bucket = jnp.sum((jnp.maximum(used - 1, 0) >= tile_end).astype(jnp.int32))
    tile_valid = (tile_ids < used).astype(jnp.int32)
    tile_bucket = jnp.where(tile_valid == 1, tile_bucket, last_bucket)
    tile_lo = jnp.asarray(np.asarray(_BUCKET_LO, np.int32))[tile_bucket]
    tile_hi = jnp.asarray(np.asarray(_BUCKET_HI, np.int32))[tile_bucket]
    return pos.astype(jnp.int32), tile_lo, tile_hi, tile_valid, ntiles * tm


def _moe_layer(fx_list, bk_list, wg, wu, wd, *, tm):
    bk = bk_list[0] if len(bk_list) == 1 else jnp.concatenate(bk_list, axis=1)
    rk, cnt = _rank_call(bk, tr=ROW_TILE)
    pos, tile_lo, tile_hi, tile_valid, nslots = _dispatch_plan(
        bk[0].astype(jnp.int32), rk[0].astype(jnp.int32), cnt[:N_BUCKETS, 0].astype(jnp.int32), tm)
    xs = _scatter_rows_call(pos, fx_list, nslots)
    out_sorted = _moe_call(tile_lo, tile_hi, tile_valid, xs, wg, wu, wd, tm=tm)
    return _gather_rows_call(pos, out_sorted)


def _forward(x, c, ctx, c_ctx, ada_w, ada_b, norm_mix_g, norm_ffn_g, even_w_in, even_conv_w, even_w_out,
             odd_w_in, odd_pool_w, odd_pool_scale, odd_sink, odd_w_out, router_w, router_b,
             moe_w_gate, moe_w_up, moe_w_down, final_g, *, tm_lat, tm_ctx, tq, tm_dft, tm_moe):
    b, n, d = x.shape
    l = ctx.shape[1]

    rows = ((b + 1 + 7) // 8) * 8
    s_rows = jnp.zeros((rows, d), F32).at[:b].set(c).at[b].set(c_ctx)
    mods = _ada_call(s_rows, ada_w, ada_b)

    def mod_vecs(layer):
        m = mods[layer, :b].reshape(b, N_MOD, 1, d)
        mc = mods[layer, b].reshape(N_MOD, 1, 1, d)
        return [m[:, k] for k in range(N_MOD)], [mc[k] for k in range(N_MOD)]

    rwt = router_w.T
    rb = router_b.astype(F32).reshape(N_EXPERTS, 1)
    cs_tab = _channel_dft_table()

    m, mc = mod_vecs(0)
    w_in0 = even_w_in[0].astype(BF16)
    w_out0 = even_w_out[0].astype(BF16)

    def even_stream(h, mv, tm):
        nn = h.shape[1]
        uc, us, gb, gu = _inproj_call(h, norm_mix_g[0], mv[0], mv[1], w_in0, mode="even", tm=tm,
                                      extra=(cs_tab,))
        yf = _dft_call(_position_dft_tables(nn), uc, us, tm=min(tm_dft, nn))
        return _even_out_call(yf, gb, gu, even_conv_w[0], w_out0, h, mv[2], norm_ffn_g[0],
                              mv[3], mv[4], rwt, rb, tm=tm)

    h1, fx_lat, bk_lat = even_stream(x, m, tm_lat)
    hc1, fx_ctx, bk_ctx = even_stream(ctx, mc, tm_ctx)

    moe0 = _moe_layer([fx_lat.reshape(b * n, ROW_WIDTH), fx_ctx.reshape(b * l, ROW_WIDTH)], [bk_lat, bk_ctx],
                      moe_w_gate[0].astype(BF16), moe_w_up[0].astype(BF16), moe_w_down[0].astype(BF16),
                      tm=tm_moe)
    gate_lat0, gate_ctx0 = m[5], mc[5]

    m, mc = mod_vecs(1)
    w_in1 = odd_w_in[0]
    kv0 = POOL_WIDTH + ATTN_WIDTH
    wk, wv = w_in1[:, kv0:kv0 + KV_WIDTH], w_in1[:, kv0 + KV_WIDTH:]

    def dup_heads(wm):
        return jnp.concatenate([wm[:, :HEAD_DIM], wm[:, :HEAD_DIM], wm[:, HEAD_DIM:], wm[:, HEAD_DIM:]], axis=1)

    w_kv_dup = jnp.concatenate([dup_heads(wk), dup_heads(wv)], axis=1)
    w_lat1 = jnp.concatenate([w_in1[:, :kv0], w_kv_dup], axis=1).astype(BF16)
    w_out1 = odd_w_out[0].astype(BF16)

    cos_t, sin_t = _rope_tables(n)
    q_scale = HEAD_DIM ** -0.5
    h1b, up, q, kd, vd = _inproj_call(h1, norm_mix_g[1], m[0], m[1], w_lat1, mode="odd", tm=tm_lat,
                                      moe=moe0, moe_row0=0, gate=gate_lat0,
                                      extra=(cos_t * q_scale, sin_t * q_scale, cos_t, sin_t))
    (kvx,) = _inproj_call(hc1, norm_mix_g[1], mc[0], mc[1], w_kv_dup.astype(BF16), mode="plain", tm=tm_ctx,
                          moe=moe0, moe_row0=b * n, gate=gate_ctx0)
    h2, fx2, bk2 = _odd_out_call(odd_sink[0], up, q, kd, vd, kvx, odd_pool_w[0].astype(BF16),
                                 odd_pool_scale[0], w_out1, h1b, m[2], norm_ffn_g[1], m[3], m[4], rwt, rb, tq=tq)
    moe1 = _moe_layer([fx2.reshape(b * n, ROW_WIDTH)], [bk2], moe_w_gate[1].astype(BF16),
                      moe_w_up[1].astype(BF16), moe_w_down[1].astype(BF16), tm=tm_moe)
    return _final_call(h2, moe1, m[5], final_g, tm=tm_lat)


def kernel(x, c, ctx, c_ctx, ada_w, ada_b, norm_mix_g, norm_ffn_g, even_w_in, even_conv_w, even_w_out,
           odd_w_in, odd_pool_w, odd_pool_scale, odd_sink, odd_w_out, router_w, router_b,
           moe_w_gate, moe_w_up, moe_w_down, final_g):
    return _forward(x, c, ctx, c_ctx, ada_w, ada_b, norm_mix_g, norm_ffn_g, even_w_in, even_conv_w,
                    even_w_out, odd_w_in, odd_pool_w, odd_pool_scale, odd_sink, odd_w_out, router_w,
                    router_b, moe_w_gate, moe_w_up, moe_w_down, final_g,
                    tm_lat=512, tm_ctx=256, tq=256, tm_dft=512, tm_moe=512)
```

```python
import functools
import math

import numpy as np
import jax
import jax.numpy as jnp
from jax import lax
from jax.experimental import pallas as pl
from jax.experimental.pallas import tpu as pltpu

F32 = jnp.float32
BF16 = jnp.bfloat16

D_MODEL = 1024
GRID_W = 64
EPS = 1e-6
N_MOD = 6
FOURIER_HEADS = 4
FOURIER_HEAD_DIM = 128
FOURIER_WIDTH = 512
CONV_WIDTH = 512
CONV_K = 3
POOL_WINDOWS = (2, 4, 8, 16)
POOL_GROUP_DIM = 128
POOL_WIDTH = 512
POOL_HALO = 8
HEAD_DIM = 64
N_Q_HEADS = 8
N_KV_HEADS = 2
ATTN_WIDTH = 512
KV_WIDTH = 128
ATTN_BLOCK = 128
ROPE_THETA = 10000.0
N_EXPERTS = 16
N_GROUPS = 4
EXPERTS_PER_GROUP = 4
D_EXPERT = 512
N_PAIRS = 6
N_BUCKETS = N_GROUPS * N_PAIRS

_PAIRS = [(a, b) for a in range(EXPERTS_PER_GROUP) for b in range(a + 1, EXPERTS_PER_GROUP)]
_BUCKET_LO = [(k // N_PAIRS) * EXPERTS_PER_GROUP + _PAIRS[k % N_PAIRS][0] for k in range(N_BUCKETS)]
_BUCKET_HI = [(k // N_PAIRS) * EXPERTS_PER_GROUP + _PAIRS[k % N_PAIRS][1] for k in range(N_BUCKETS)]

LANES = 128
ROW_WIDTH = D_MODEL + LANES
INFO_BUCKET, INFO_W_LO, INFO_W_HI = 0, 1, 2
VMEM_LIMIT_BYTES = 48 * 1024 * 1024

HIGHEST = lax.Precision.HIGHEST


def _cparams(sem):
    return pltpu.CompilerParams(dimension_semantics=sem, vmem_limit_bytes=VMEM_LIMIT_BYTES)


def _rms_mod(x, g, shift, scale):
    y = x * lax.rsqrt(jnp.mean(x * x, axis=-1, keepdims=True) + EPS) * g
    return y * (1.0 + scale) + shift


def _dot(a, b):
    return jnp.dot(a, b, preferred_element_type=F32)


def _dot_nt(a, b):
    return lax.dot_general(a, b, (((1,), (1,)), ((), ())), preferred_element_type=F32)


def _ada_kernel(s_ref, w_ref, b_ref, o_ref):
    s = s_ref[...]
    s = s * jax.nn.sigmoid(s)
    o_ref[0] = jnp.dot(s, w_ref[0], preferred_element_type=F32, precision=HIGHEST) + b_ref[0]


def _ada_call(s_rows, ada_w, ada_b):
    depth, d, n6 = ada_w.shape
    r = s_rows.shape[0]
    tn = 1536
    return pl.pallas_call(
        _ada_kernel,
        grid=(depth, n6 // tn),
        in_specs=[
            pl.BlockSpec((r, d), lambda l, j: (0, 0)),
            pl.BlockSpec((1, d, tn), lambda l, j: (l, 0, j)),
            pl.BlockSpec((1, 1, tn), lambda l, j: (l, 0, j)),
        ],
        out_specs=pl.BlockSpec((1, r, tn), lambda l, j: (l, 0, j)),
        out_shape=jax.ShapeDtypeStruct((depth, r, n6), F32),
        compiler_params=_cparams(("arbitrary", "arbitrary")),
        name="ada_mod",
    )(s_rows, ada_w, ada_b.reshape(depth, 1, n6))


def _rope_group(x, cos, sin_signed):
    lane = lax.broadcasted_iota(jnp.int32, x.shape, 1)
    first_half = (lane % 32) < 16
    partner = jnp.where(first_half, pltpu.roll(x, LANES - 16, 1), pltpu.roll(x, 16, 1))
    return x * cos + partner * sin_signed


def _inproj_kernel(*refs, mode, add_moe):
    it = iter(refs)
    h_ref = next(it)
    if add_moe:
        moe_ref, gate_ref = next(it), next(it)
    g_ref, sh_ref, sc_ref, w_ref = next(it), next(it), next(it), next(it)
    x = h_ref[0]
    if add_moe:
        x = x + gate_ref[0] * moe_ref[...]
    a = _rms_mod(x, g_ref[...], sh_ref[0], sc_ref[0])
    proj = _dot(a.astype(BF16), w_ref[...])
    if mode == "even":
        cs_ref = next(it)
        uc_ref, us_ref, gb_ref, gu_ref = next(it), next(it), next(it), next(it)
        uf = proj[:, :FOURIER_WIDTH].astype(BF16)
        cs = cs_ref[...].astype(BF16)
        for hh in range(FOURIER_HEADS):
            cols = slice(hh * LANES, (hh + 1) * LANES)
            r = _dot(uf[:, cols], cs)
            uc_ref[0, :, cols] = r[:, :LANES].astype(BF16)
            us_ref[0, :, cols] = r[:, LANES:].astype(BF16)
        c0 = FOURIER_WIDTH
        gb_ref[0] = proj[:, c0:c0 + CONV_WIDTH]
        gu_ref[0] = proj[:, c0 + CONV_WIDTH:c0 + 2 * CONV_WIDTH] * proj[:, c0 + 2 * CONV_WIDTH:]
    elif mode == "odd":
        cq_ref, sq_ref, ck_ref, sk_ref = next(it), next(it), next(it), next(it)
        hn_ref, up_ref, q_ref, kd_ref, vd_ref = next(it), next(it), next(it), next(it), next(it)
        hn_ref[0] = x
        up_ref[0] = proj[:, :POOL_WIDTH]
        c0 = POOL_WIDTH
        for gi in range(ATTN_WIDTH // LANES):
            cols = slice(c0 + gi * LANES, c0 + (gi + 1) * LANES)
            q_ref[0, :, gi * LANES:(gi + 1) * LANES] = _rope_group(
                proj[:, cols], cq_ref[...], sq_ref[...]).astype(BF16)
        c0 += ATTN_WIDTH
        for gi in range(2 * KV_WIDTH // LANES):
            cols = slice(c0 + gi * LANES, c0 + (gi + 1) * LANES)
            kd_ref[0, :, gi * LANES:(gi + 1) * LANES] = _rope_group(
                proj[:, cols], ck_ref[...], sk_ref[...]).astype(BF16)
        c0 += 2 * KV_WIDTH
        vd_ref[0] = proj[:, c0:].astype(BF16)
    else:
        kv_ref = next(it)
        kv_ref[0] = proj.astype(BF16)


def _mod_spec(arr):
    if arr.shape[0] > 1:
        return pl.BlockSpec((1, 1, arr.shape[2]), lambda b, i: (b, 0, 0))
    return pl.BlockSpec((1, 1, arr.shape[2]), lambda b, i: (0, 0, 0))


def _flat_rows_spec(tm, d, nt, row0):
    tile0 = row0 // tm
    return pl.BlockSpec((tm, d), lambda bb, i: (tile0 + bb * nt + i, 0))


def _inproj_call(h, g, shift, scale, w, *, mode, tm, moe=None, moe_row0=0, gate=None, extra=()):
    b, n, d = h.shape
    nout = w.shape[1]
    add_moe = moe is not None
    row = lambda bb, i: (bb, i, 0)
    full2 = lambda bb, i: (0, 0)
    args = [h]
    in_specs = [pl.BlockSpec((1, tm, d), row)]
    if add_moe:
        args += [moe, gate]
        in_specs += [_flat_rows_spec(tm, d, n // tm, moe_row0), _mod_spec(gate)]
    args += [g.reshape(1, d), shift, scale, w]
    in_specs += [pl.BlockSpec((1, d), full2), _mod_spec(shift), _mod_spec(scale),
                 pl.BlockSpec((d, nout), full2)]
    if mode == "even":
        args += list(extra)
        in_specs += [pl.BlockSpec(extra[0].shape, full2)]
        out_shape = [jax.ShapeDtypeStruct((b, n, FOURIER_WIDTH), BF16),
                     jax.ShapeDtypeStruct((b, n, FOURIER_WIDTH), BF16),
                     jax.ShapeDtypeStruct((b, n, CONV_WIDTH), F32),
                     jax.ShapeDtypeStruct((b, n, CONV_WIDTH), F32)]
        out_specs = [pl.BlockSpec((1, tm, 512), row)] * 4
    elif mode == "odd":
        args += list(extra)
        in_specs += [pl.BlockSpec((tm, LANES), lambda bb, i: (i, 0))] * 4
        out_shape = [jax.ShapeDtypeStruct((b, n, d), F32),
                     jax.ShapeDtypeStruct((b, n, POOL_WIDTH), F32),
                     jax.ShapeDtypeStruct((b, n, ATTN_WIDTH), BF16),
                     jax.ShapeDtypeStruct((b, n, 2 * KV_WIDTH), BF16),
                     jax.ShapeDtypeStruct((b, n, 2 * KV_WIDTH), BF16)]
        out_specs = [pl.BlockSpec((1, tm, d), row), pl.BlockSpec((1, tm, POOL_WIDTH), row),
                     pl.BlockSpec((1, tm, ATTN_WIDTH), row),
                     pl.BlockSpec((1, tm, 2 * KV_WIDTH), row), pl.BlockSpec((1, tm, 2 * KV_WIDTH), row)]
    else:
        out_shape = [jax.ShapeDtypeStruct((b, n, nout), BF16)]
        out_specs = [pl.BlockSpec((1, tm, nout), row)]
    return pl.pallas_call(
        functools.partial(_inproj_kernel, mode=mode, add_moe=add_moe),
        grid=(b, n // tm),
        in_specs=in_specs,
        out_specs=out_specs,
        out_shape=out_shape,
        compiler_params=_cparams(("arbitrary", "arbitrary")),
        name="inproj_" + mode,
    )(*args)


DFT_RADIX = 64


def _dft_kernel(ca_ref, sa_ref, cb_ref, sb_ref, uc_ref, us_ref, o_ref, c_scr, s_scr, *, norm):
    @pl.when(pl.program_id(1) == 0)
    def _():
        cb, sb = cb_ref[...], sb_ref[...]
        for r in range(ca_ref.shape[0]):
            ca, sa = ca_ref[r:r + 1, :], sa_ref[r:r + 1, :]
            rows = slice(r * DFT_RADIX, (r + 1) * DFT_RADIX)
            c_scr[rows, :] = (ca * cb - sa * sb).astype(BF16)
            s_scr[rows, :] = (-(sa * cb + ca * sb)).astype(BF16)

    acc = _dot(c_scr[...], uc_ref[0]) + _dot(s_scr[...], us_ref[0])
    o_ref[0] = (acc * norm).astype(BF16)


def _dft_call(tabs, uc, us, *, tm):
    b, n, wdt = uc.shape
    r_tile = tm // DFT_RADIX
    norm = 1.0 / math.sqrt(n * FOURIER_HEAD_DIM)
    a_spec = pl.BlockSpec((r_tile, n), lambda i, bb: (i, 0))
    b_spec = pl.BlockSpec((DFT_RADIX, n), lambda i, bb: (0, 0))
    u_spec = pl.BlockSpec((1, n, wdt), lambda i, bb: (bb, 0, 0))
    return pl.pallas_call(
        functools.partial(_dft_kernel, norm=norm),
        grid=(n // tm, b),
        in_specs=[a_spec, a_spec, b_spec, b_spec, u_spec, u_spec],
        out_specs=pl.BlockSpec((1, tm, wdt), lambda i, bb: (bb, i, 0)),
        out_shape=jax.ShapeDtypeStruct((b, n, wdt), BF16),
        scratch_shapes=[pltpu.VMEM((tm, n), BF16), pltpu.VMEM((tm, n), BF16)],
        compiler_params=_cparams(("arbitrary", "arbitrary")),
        name="dft_rows",
    )(*tabs, uc, us)


def _tail(y, h_ref, gate_ref, g2_ref, sh_ref, sc_ref, rwt_ref, rb_ref, hout_ref, fx_ref, bk_ref):
    hn = h_ref[0] + gate_ref[0] * y
    hout_ref[0] = hn
    f = _rms_mod(hn, g2_ref[...], sh_ref[0], sc_ref[0])
    tm = f.shape[0]
    f_hi = f.astype(BF16)
    f_lo = (f - f_hi.astype(F32)).astype(BF16)
    both = _dot_nt(rwt_ref[...], f_hi)
    logits = (both[:N_EXPERTS] + both[N_EXPERTS:]) + _dot_nt(rwt_ref[:N_EXPERTS, :], f_lo)
    aff = jax.nn.sigmoid(logits)
    sel = aff + rb_ref[...]
    best = bucket = a_lo = a_hi = None
    for bkt in range(N_BUCKETS):
        lo, hi = _BUCKET_LO[bkt], _BUCKET_HI[bkt]
        pair_sum = sel[lo:lo + 1, :] + sel[hi:hi + 1, :]
        if bkt == 0:
            best, bucket = pair_sum, jnp.zeros_like(pair_sum)
            a_lo, a_hi = aff[lo:lo + 1, :], aff[hi:hi + 1, :]
        else:
            upd = pair_sum > best
            best = jnp.where(upd, pair_sum, best)
            bucket = jnp.where(upd, float(bkt), bucket)
            a_lo = jnp.where(upd, aff[lo:lo + 1, :], a_lo)
            a_hi = jnp.where(upd, aff[hi:hi + 1, :], a_hi)
    den = a_lo + a_hi
    info = jnp.concatenate([bucket, a_lo / den, a_hi / den, jnp.zeros((LANES - 3, tm), F32)], axis=0)
    fx_ref[0, :, :D_MODEL] = f
    fx_ref[0, :, D_MODEL:] = info.T
    bk_ref[...] = jnp.concatenate([bucket, jnp.zeros((7, tm), F32)], axis=0)


def _tail_specs(h, gate, shift, scale, tm):
    b, n, d = h.shape
    nt = n // tm
    row = lambda bb, i: (bb, i, 0)
    in_specs = [pl.BlockSpec((1, tm, d), row), _mod_spec(gate),
                pl.BlockSpec((1, d), lambda bb, i: (0, 0)), _mod_spec(shift), _mod_spec(scale),
                pl.BlockSpec((2 * N_EXPERTS, d), lambda bb, i: (0, 0)),
                pl.BlockSpec((N_EXPERTS, 1), lambda bb, i: (0, 0))]
    out_shape = [jax.ShapeDtypeStruct((b, n, d), F32), jax.ShapeDtypeStruct((b, n, ROW_WIDTH), F32),
                 jax.ShapeDtypeStruct((8, b * n), F32)]
    out_specs = [pl.BlockSpec((1, tm, d), row), pl.BlockSpec((1, tm, ROW_WIDTH), row),
                 pl.BlockSpec((8, tm), lambda bb, i: (0, bb * nt + i))]
    return in_specs, out_shape, out_specs


def _even_out_kernel(yf_ref, gb_ref, gu_ref, gp_ref, gn_ref, cw_ref, wo_ref,
                     h_ref, gate_ref, g2_ref, sh_ref, sc_ref, rwt_ref, rb_ref,
                     hout_ref, fx_ref, bk_ref):
    i = pl.program_id(1)
    last = pl.num_programs(1) - 1
    gu = gu_ref[0]
    tm = gu.shape[0]
    prev = jnp.where(i > 0, gp_ref[0, 7:8, :], 0.0)
    nxt = jnp.where(i < last, gn_ref[0, 0:1, :], 0.0)
    row = lax.broadcasted_iota(jnp.int32, gu.shape, 0)
    up = jnp.where(row == 0, prev, pltpu.roll(gu, 1, 0))
    dn = jnp.where(row == tm - 1, nxt, pltpu.roll(gu, tm - 1, 0))
    conv = up * cw_ref[0:1, :] + gu * cw_ref[1:2, :] + dn * cw_ref[2:3, :]
    yc = (gb_ref[0] * conv).astype(BF16)
    y = _dot(yf_ref[0], wo_ref[:FOURIER_WIDTH, :]) + _dot(yc, wo_ref[FOURIER_WIDTH:, :])
    _tail(y, h_ref, gate_ref, g2_ref, sh_ref, sc_ref, rwt_ref, rb_ref, hout_ref, fx_ref, bk_ref)


def _even_out_call(yf, gb, gu, conv_w, w_out, h, gate, g2, shift, scale, rwt, rb, *, tm):
    b, n, d = h.shape
    row = lambda bb, i: (bb, i, 0)
    nb8 = n // 8
    t8 = tm // 8
    tin, out_shape, out_specs = _tail_specs(h, gate, shift, scale, tm)
    in_specs = [pl.BlockSpec((1, tm, 512), row), pl.BlockSpec((1, tm, 512), row),
                pl.BlockSpec((1, tm, 512), row),
                pl.BlockSpec((1, 8, 512), lambda bb, i: (bb, jnp.maximum(i * t8 - 1, 0), 0)),
                pl.BlockSpec((1, 8, 512), lambda bb, i: (bb, jnp.minimum((i + 1) * t8, nb8 - 1), 0)),
                pl.BlockSpec((CONV_K, CONV_WIDTH), lambda bb, i: (0, 0)),
                pl.BlockSpec((d, d), lambda bb, i: (0, 0))] + tin
    return pl.pallas_call(
        _even_out_kernel,
        grid=(b, n // tm),
        in_specs=in_specs,
        out_specs=out_specs,
        out_shape=out_shape,
        compiler_params=_cparams(("arbitrary", "arbitrary")),
        name="even_out",
    )(yf, gb, gu, gu, gu, conv_w, w_out, h, gate, g2.reshape(1, d), shift, scale, rwt, rb)


def _odd_out_kernel(sink_ref, up_ref, upp_ref, upn_ref, q_ref, kc_ref, kp_ref, kn_ref,
                    vc_ref, vp_ref, vn_ref, kvx_ref, pw_ref, ps_ref, wo_ref,
                    h_ref, gate_ref, g2_ref, sh_ref, sc_ref, rwt_ref, rb_ref,
                    hout_ref, fx_ref, bk_ref, ext_ref, mix_ref, *, n_total):
    i = pl.program_id(1)
    last = pl.num_programs(1) - 1
    tq = q_ref.shape[1]
    nsub = tq // ATTN_BLOCK

    u = up_ref[0]
    ext_ref[0:POOL_HALO, :] = jnp.where(i > 0, upp_ref[0], 0.0)
    ext_ref[POOL_HALO:POOL_HALO + tq, :] = u
    ext_ref[POOL_HALO + tq:, :] = jnp.where(i < last, upn_ref[0], 0.0)
    t = i * tq + lax.broadcasted_iota(jnp.int32, (tq, LANES), 0)
    for gi, win in enumerate(POOL_WINDOWS):
        r = win // 2
        cols = slice(gi * LANES, (gi + 1) * LANES)
        acc = ext_ref[POOL_HALO - r:POOL_HALO - r + tq, cols]
        for dlt in range(-r + 1, r + 1):
            acc = acc + ext_ref[POOL_HALO + dlt:POOL_HALO + dlt + tq, cols]
        cnt = (jnp.minimum(t + r + 1, n_total) - jnp.maximum(t - r, 0)).astype(F32)
        p = acc / cnt - u[:, cols]
        y = _dot(p.astype(BF16), pw_ref[gi]) * ps_ref[:, cols]
        mix_ref[:, cols] = y.astype(BF16)

    kwin = jnp.concatenate([kp_ref[0], kc_ref[0], kn_ref[0]], axis=0)
    vwin = jnp.concatenate([vp_ref[0], vc_ref[0], vn_ref[0]], axis=0)
    kvx = kvx_ref[0]
    kx, vx = kvx[:, :2 * KV_WIDTH], kvx[:, 2 * KV_WIDTH:]
    low = (lax.broadcasted_iota(jnp.int32, (1, 2 * KV_WIDTH), 1) % LANES) < HEAD_DIM
    zero = jnp.zeros((), BF16)
    k_half = (jnp.where(low, kwin, zero), jnp.where(low, zero, kwin))
    v_half = (jnp.where(low, vwin, zero), jnp.where(low, zero, vwin))
    kx_half = (jnp.where(low, kx, zero), jnp.where(low, zero, kx))
    vx_half = (jnp.where(low, vx, zero), jnp.where(low, zero, vx))

    span = 3 * ATTN_BLOCK
    rows2 = 2 * ATTN_BLOCK
    qi = lax.broadcasted_iota(jnp.int32, (rows2, span), 0) % ATTN_BLOCK
    kj = lax.broadcasted_iota(jnp.int32, (rows2, span), 1)
    in_prev = kj < ATTN_BLOCK
    in_next = kj >= 2 * ATTN_BLOCK
    neg_inf = jnp.float32(-jnp.inf)
    band_bias = (jnp.where(in_prev & (kj < qi), neg_inf, 0.0)
                 + jnp.where(in_next & (kj - 2 * ATTN_BLOCK > qi), neg_inf, 0.0))
    top_rows = lax.broadcasted_iota(jnp.int32, (rows2, 1), 0) < ATTN_BLOCK

    for j in range(nsub):
        blk = i * nsub + j
        prev_bias = jnp.where(blk > 0, 0.0, neg_inf)
        next_bias = jnp.where(blk < (n_total // ATTN_BLOCK) - 1, 0.0, neg_inf)
        bias = band_bias + jnp.where(in_prev, prev_bias, 0.0) + jnp.where(in_next, next_bias, 0.0)
        r0 = j * ATTN_BLOCK
        for kh in range(N_KV_HEADS):
            kcols = slice(kh * LANES, (kh + 1) * LANES)
            xq = jnp.concatenate(
                [q_ref[0, r0:r0 + ATTN_BLOCK, (2 * kh + pr) * LANES:(2 * kh + pr + 1) * LANES]
                 for pr in range(2)], axis=0)
            o_pair = None
            for half in range(2):
                ks = k_half[half][r0:r0 + span, kcols]
                vs = v_half[half][r0:r0 + span, kcols]
                s1 = _dot_nt(xq, ks) + bias
                s2 = _dot_nt(xq, kx_half[half][:, kcols])
                head0 = kh * 4 + half
                snk = jnp.where(top_rows, sink_ref[head0], sink_ref[head0 + 2])
                m = jnp.maximum(jnp.maximum(jnp.max(s1, axis=-1, keepdims=True),
                                            jnp.max(s2, axis=-1, keepdims=True)), snk)
                e1 = jnp.exp(s1 - m)
                e2 = jnp.exp(s2 - m)
                den = (jnp.sum(e1, axis=-1, keepdims=True) + jnp.sum(e2, axis=-1, keepdims=True)
                       + jnp.exp(snk - m))
                o = _dot(e1.astype(BF16), vs) + _dot(e2.astype(BF16), vx_half[half][:, kcols])
                o = o / den
                o_pair = o if o_pair is None else o_pair + o
            for pr in range(2):
                c0 = POOL_WIDTH + (2 * kh + pr) * LANES
                mix_ref[r0:r0 + ATTN_BLOCK, c0:c0 + LANES] = (
                    o_pair[pr * ATTN_BLOCK:(pr + 1) * ATTN_BLOCK].astype(BF16))

    y = _dot(mix_ref[...], wo_ref[...])
    _tail(y, h_ref, gate_ref, g2_ref, sh_ref, sc_ref, rwt_ref, rb_ref, hout_ref, fx_ref, bk_ref)


def _odd_out_call(sink, up, q, kd, vd, kvx, pool_w, pool_scale, w_out,
                  h, gate, g2, shift, scale, rwt, rb, *, tq):
    b, n, d = h.shape
    row = lambda bb, i: (bb, i, 0)
    nb8, t8 = n // POOL_HALO, tq // POOL_HALO
    nbk, tk = n // ATTN_BLOCK, tq // ATTN_BLOCK
    prev8 = lambda bb, i: (bb, jnp.maximum(i * t8 - 1, 0), 0)
    next8 = lambda bb, i: (bb, jnp.minimum((i + 1) * t8, nb8 - 1), 0)
    prevk = lambda bb, i: (bb, jnp.maximum(i * tk - 1, 0), 0)
    nextk = lambda bb, i: (bb, jnp.minimum((i + 1) * tk, nbk - 1), 0)
    kvw = 2 * KV_WIDTH
    tin, out_shape, out_specs = _tail_specs(h, gate, shift, scale, tq)
    in_specs = [pl.BlockSpec(memory_space=pltpu.SMEM),
                pl.BlockSpec((1, tq, POOL_WIDTH), row),
                pl.BlockSpec((1, POOL_HALO, POOL_WIDTH), prev8),
                pl.BlockSpec((1, POOL_HALO, POOL_WIDTH), next8),
                pl.BlockSpec((1, tq, ATTN_WIDTH), row),
                pl.BlockSpec((1, tq, kvw), row),
                pl.BlockSpec((1, ATTN_BLOCK, kvw), prevk),
                pl.BlockSpec((1, ATTN_BLOCK, kvw), nextk),
                pl.BlockSpec((1, tq, kvw), row),
                pl.BlockSpec((1, ATTN_BLOCK, kvw), prevk),
                pl.BlockSpec((1, ATTN_BLOCK, kvw), nextk),
                pl.BlockSpec((1, kvx.shape[1], 2 * kvw), lambda bb, i: (bb, 0, 0)),
                pl.BlockSpec(pool_w.shape, lambda bb, i: (0, 0, 0)),
                pl.BlockSpec((1, POOL_WIDTH), lambda bb, i: (0, 0)),
                pl.BlockSpec((d, d), lambda bb, i: (0, 0))] + tin
    return pl.pallas_call(
        functools.partial(_odd_out_kernel, n_total=n),
        grid=(b, n // tq),
        in_specs=in_specs,
        out_specs=out_specs,
        out_shape=out_shape,
        scratch_shapes=[pltpu.VMEM((tq + 2 * POOL_HALO, POOL_WIDTH), F32),
                        pltpu.VMEM((tq, d), BF16)],
        compiler_params=_cparams(("arbitrary", "arbitrary")),
        name="odd_out",
    )(sink, up, up, up, q, kd, kd, kd, vd, vd, vd, kvx, pool_w, pool_scale.reshape(1, POOL_WIDTH),
      w_out, h, gate, g2.reshape(1, d), shift, scale, rwt, rb)


def _moe_kernel(e_lo_ref, e_hi_ref, valid_ref, x_ref,
                g1_ref, u1_ref, d1_ref, g2_ref, u2_ref, d2_ref, o_ref):
    j = pl.program_id(0)

    @pl.when(valid_ref[j] != 0)
    def _():
        x = x_ref[:, :D_MODEL].astype(BF16)
        w_lo = x_ref[:, D_MODEL + INFO_W_LO:D_MODEL + INFO_W_LO + 1]
        w_hi = x_ref[:, D_MODEL + INFO_W_HI:D_MODEL + INFO_W_HI + 1]

        def expert(g_ref, u_ref, d_ref):
            gate = _dot(x, g_ref[0])
            hid = gate * jax.nn.sigmoid(gate) * _dot(x, u_ref[0])
            return _dot(hid.astype(BF16), d_ref[0])

        o_lo = expert(g1_ref, u1_ref, d1_ref)
        o_hi = expert(g2_ref, u2_ref, d2_ref)
        o_ref[...] = w_lo * o_lo + w_hi * o_hi

    @pl.when(valid_ref[j] == 0)
    def _():
        o_ref[...] = jnp.zeros(o_ref.shape, o_ref.dtype)


def _moe_call(tile_lo, tile_hi, tile_valid, xs, wg, wu, wd, *, tm):
    p = xs.shape[0]
    d = D_MODEL
    ntiles = p // tm
    lo3 = lambda j, lo, hi, v: (lo[j], 0, 0)
    hi3 = lambda j, lo, hi, v: (hi[j], 0, 0)
    rowm = lambda j, lo, hi, v: (j, 0)
    row_in = lambda j, lo, hi, v: (jnp.where(v[j] != 0, j, 0), 0)
    grid_spec = pltpu.PrefetchScalarGridSpec(
        num_scalar_prefetch=3,
        grid=(ntiles,),
        in_specs=[pl.BlockSpec((tm, ROW_WIDTH), row_in),
                  pl.BlockSpec((1, d, D_EXPERT), lo3), pl.BlockSpec((1, d, D_EXPERT), lo3),
                  pl.BlockSpec((1, D_EXPERT, d), lo3),
                  pl.BlockSpec((1, d, D_EXPERT), hi3), pl.BlockSpec((1, d, D_EXPERT), hi3),
                  pl.BlockSpec((1, D_EXPERT, d), hi3)],
        out_specs=pl.BlockSpec((tm, d), rowm),
    )
    return pl.pallas_call(
        _moe_kernel,
        grid_spec=grid_spec,
        out_shape=jax.ShapeDtypeStruct((p, d), F32),
        compiler_params=_cparams(("arbitrary",)),
        name="moe_pairs",
    )(tile_lo, tile_hi, tile_valid, xs, wg, wu, wd, wg, wu, wd)


RANK_ROWS = 32


def _rank_kernel(bk_ref, tri_ref, rk_ref, cnt_ref, carry_ref):
    @pl.when(pl.program_id(0) == 0)
    def _():
        carry_ref[...] = jnp.zeros(carry_ref.shape, F32)

    tr = bk_ref.shape[1]
    bucket = bk_ref[0:1, :]
    rows = lax.broadcasted_iota(jnp.int32, (RANK_ROWS, tr), 0).astype(F32)
    onehot = jnp.where(rows == bucket, 1.0, 0.0)
    before = _dot(onehot.astype(BF16), tri_ref[...]) + carry_ref[:, 0:1]
    rank = jnp.sum(onehot * before, axis=0, keepdims=True)
    rk_ref[...] = jnp.broadcast_to(rank, rk_ref.shape)
    carry_ref[...] = carry_ref[...] + jnp.sum(onehot, axis=1, keepdims=True)
    cnt_ref[...] = carry_ref[...]


def _rank_call(bk, *, tr):
    t = bk.shape[1]
    tri = jnp.asarray(np.triu(np.ones((tr, tr), np.float32), 1), BF16)
    return pl.pallas_call(
        _rank_kernel,
        grid=(t // tr,),
        in_specs=[pl.BlockSpec((8, tr), lambda j: (0, j)), pl.BlockSpec((tr, tr), lambda j: (0, 0))],
        out_specs=[pl.BlockSpec((8, tr), lambda j: (0, j)), pl.BlockSpec((RANK_ROWS, LANES), lambda j: (0, 0))],
        out_shape=[jax.ShapeDtypeStruct((8, t), F32), jax.ShapeDtypeStruct((RANK_ROWS, LANES), F32)],
        scratch_shapes=[pltpu.VMEM((RANK_ROWS, LANES), F32)],
        compiler_params=_cparams(("arbitrary",)),
        name="bucket_rank",
    )(bk, tri)


ROW_TILE = 512


def _scatter_rows_kernel(tile_end_ref, pos_ref, *refs, tile_starts, tm):
    nsrc = len(tile_starts)
    srcs, out_hbm, zero_ref, sem = refs[:nsrc], refs[nsrc], refs[nsrc + 1], refs[nsrc + 2]
    j = pl.program_id(0)

    @pl.when(j == 0)
    def _():
        zero_ref[...] = jnp.zeros(zero_ref.shape, F32)

        def fill(tile, start):
            copy = pltpu.make_async_copy(zero_ref, out_hbm.at[pl.ds(tile * tm, tm)], sem)
            copy.start() if start else copy.wait()

        n_slot_tiles = out_hbm.shape[0] // tm
        used = tile_end_ref[N_BUCKETS - 1]
        for start in (True, False):
            for b in range(N_BUCKETS):
                first_tile = tile_end_ref[b - 1] if b else 0
                pl.when(tile_end_ref[b] > first_tile)(functools.partial(fill, tile_end_ref[b] - 1, start))
                pl.when(n_slot_tiles - 1 - b >= used)(functools.partial(fill, n_slot_tiles - 1 - b, start))

    def move(src_vmem):
        def body(r, carry):
            pltpu.make_async_copy(src_vmem.at[pl.ds(r, 1)],
                                  out_hbm.at[pl.ds(pos_ref[0, 0, r], 1)], sem).start()
            return carry
        lax.fori_loop(0, ROW_TILE, body, 0, unroll=8)
        pltpu.make_async_copy(src_vmem, out_hbm.at[pl.ds(0, ROW_TILE)], sem).wait()

    for s in range(nsrc):
        first = tile_starts[s]
        if nsrc == 1:
            move(srcs[s])
        else:
            in_range = (j >= first) if s == nsrc - 1 else ((j >= first) & (j < tile_starts[s + 1]))
            pl.when(in_range)(functools.partial(move, srcs[s]))


def _scatter_rows_call(pos, tile_end, sources, nslots, tm):
    w = sources[0].shape[1]
    t = pos.shape[0]
    ntiles = t // ROW_TILE
    tile_starts, src_specs, acc = [], [], 0
    for s in sources:
        first, count = acc, s.shape[0] // ROW_TILE
        tile_starts.append(first)
        src_specs.append(pl.BlockSpec(
            (ROW_TILE, w), lambda j, te, first=first, count=count: (jnp.clip(j - first, 0, count - 1), 0)))
        acc += count
    grid_spec = pltpu.PrefetchScalarGridSpec(
        num_scalar_prefetch=1,
        grid=(ntiles,),
        in_specs=[pl.BlockSpec((1, 1, ROW_TILE), lambda j, te: (j, 0, 0), memory_space=pltpu.SMEM)] + src_specs,
        out_specs=pl.BlockSpec(memory_space=pl.ANY),
        scratch_shapes=[pltpu.VMEM((tm, w), F32), pltpu.SemaphoreType.DMA(())],
    )
    return pl.pallas_call(
        functools.partial(_scatter_rows_kernel, tile_starts=tuple(tile_starts), tm=tm),
        grid_spec=grid_spec,
        out_shape=jax.ShapeDtypeStruct((nslots, w), F32),
        compiler_params=_cparams(("arbitrary",)),
        name="scatter_rows",
    )(tile_end, pos.reshape(ntiles, 1, ROW_TILE), *sources)


def _gather_rows_kernel(pos_ref, src_hbm, out_ref, sem):
    def body(r, carry):
        pltpu.make_async_copy(src_hbm.at[pl.ds(pos_ref[0, 0, r], 1)], out_ref.at[pl.ds(r, 1)], sem).start()
        return carry
    lax.fori_loop(0, ROW_TILE, body, 0, unroll=8)
    pltpu.make_async_copy(src_hbm.at[pl.ds(0, ROW_TILE)], out_ref, sem).wait()


def _gather_rows_call(pos, src):
    t = pos.shape[0]
    w = src.shape[1]
    ntiles = t // ROW_TILE
    return pl.pallas_call(
        _gather_rows_kernel,
        grid=(ntiles,),
        in_specs=[pl.BlockSpec((1, 1, ROW_TILE), lambda j: (j, 0, 0), memory_space=pltpu.SMEM),
                  pl.BlockSpec(memory_space=pl.ANY)],
        out_specs=pl.BlockSpec((ROW_TILE, w), lambda j: (j, 0)),
        out_shape=jax.ShapeDtypeStruct((t, w), F32),
        scratch_shapes=[pltpu.SemaphoreType.DMA(())],
        compiler_params=_cparams(("arbitrary",)),
        name="gather_rows",
    )(pos.reshape(ntiles, 1, ROW_TILE), src)


def _final_kernel(h_ref, moe_ref, gate_ref, g_ref, o_ref):
    x = h_ref[0] + gate_ref[0] * moe_ref[...]
    o_ref[0] = x * lax.rsqrt(jnp.mean(x * x, axis=-1, keepdims=True) + EPS) * g_ref[...]


def _final_call(h, moe, gate, g, *, tm):
    b, n, d = h.shape
    row = lambda bb, i: (bb, i, 0)
    return pl.pallas_call(
        _final_kernel,
        grid=(b, n // tm),
        in_specs=[pl.BlockSpec((1, tm, d), row), _flat_rows_spec(tm, d, n // tm, 0), _mod_spec(gate),
                  pl.BlockSpec((1, d), lambda bb, i: (0, 0))],
        out_specs=pl.BlockSpec((1, tm, d), row),
        out_shape=jax.ShapeDtypeStruct((b, n, d), F32),
        compiler_params=_cparams(("arbitrary", "arbitrary")),
        name="final_norm",
    )(h, moe, gate, g.reshape(1, d))


def _channel_dft_table():
    c = np.arange(FOURIER_HEAD_DIM)
    ang = 2.0 * np.pi * ((c[:, None] * c[None, :]) % FOURIER_HEAD_DIM) / FOURIER_HEAD_DIM
    return jnp.asarray(np.concatenate([np.cos(ang), np.sin(ang)], axis=1), F32)


def _position_dft_tables(n):
    n1 = n // DFT_RADIX
    t = np.arange(n)
    a = 2.0 * np.pi * ((np.arange(n1)[:, None] * t[None, :]) % n1) / n1
    bb = 2.0 * np.pi * ((np.arange(DFT_RADIX)[:, None] * t[None, :]) % n) / n
    return tuple(jnp.asarray(v, F32) for v in (np.cos(a), np.sin(a), np.cos(bb), np.sin(bb)))


def _rope_tables(n):
    quarter = HEAD_DIM // 4
    inv = ROPE_THETA ** (-jnp.arange(quarter, dtype=F32) / quarter)
    t = jnp.arange(n)
    ang_r = (t // GRID_W).astype(F32)[:, None] * inv
    ang_c = (t % GRID_W).astype(F32)[:, None] * inv
    cos = jnp.concatenate([jnp.cos(ang_r)] * 2 + [jnp.cos(ang_c)] * 2, axis=1)
    sin = jnp.concatenate([-jnp.sin(ang_r), jnp.sin(ang_r), -jnp.sin(ang_c), jnp.sin(ang_c)], axis=1)
    return jnp.tile(cos, (1, 2)), jnp.tile(sin, (1, 2))


def _dispatch_plan(bucket, rank, counts, tm):
    t = bucket.shape[0]
    ntiles = t // tm + N_BUCKETS
    tiles_per = (counts + tm - 1) // tm
    tile_end = jnp.cumsum(tiles_per)
    tile_start = tile_end - tiles_per
    onehot = bucket[:, None] == jnp.arange(N_BUCKETS, dtype=jnp.int32)[None, :]
    pos = jnp.sum(jnp.where(onehot, (tile_start * tm)[None, :], 0), axis=-1) + rank
    tile_ids = jnp.arange(ntiles, dtype=jnp.int32)
    used = tile_end[-1]
    tile_bucket = jnp.sum((tile_ids[:, None] >= tile_end[None, :]).astype(jnp.int32), axis=1)
    last_bucket = jnp.sum((jnp.maximum(used - 1, 0) >= tile_end).astype(jnp.int32))
    tile_valid = (tile_ids < used).astype(jnp.int32)
    tile_bucket = jnp.where(tile_valid == 1, tile_bucket, last_bucket)
    tile_lo = jnp.asarray(np.asarray(_BUCKET_LO, np.int32))[tile_bucket]
    tile_hi = jnp.asarray(np.asarray(_BUCKET_HI, np.int32))[tile_bucket]
    return pos.astype(jnp.int32), tile_end.astype(jnp.int32), tile_lo, tile_hi, tile_valid, ntiles * tm


def _moe_layer(fx_list, bk_list, wg, wu, wd, *, tm):
    bk = bk_list[0] if len(bk_list) == 1 else jnp.concatenate(bk_list, axis=1)
    rk, cnt = _rank_call(bk, tr=ROW_TILE)
    pos, tile_end, tile_lo, tile_hi, tile_valid, nslots = _dispatch_plan(
        bk[0].astype(jnp.int32), rk[0].astype(jnp.int32), cnt[:N_BUCKETS, 0].astype(jnp.int32), tm)
    xs = _scatter_rows_call(pos, tile_end, fx_list, nslots, tm)
    out_sorted = _moe_call(tile_lo, tile_hi, tile_valid, xs, wg, wu, wd, tm=tm)
    return _gather_rows_call(pos, out_sorted)


def _forward(x, c, ctx, c_ctx, ada_w, ada_b, norm_mix_g, norm_ffn_g, even_w_in, even_conv_w, even_w_out,
             odd_w_in, odd_pool_w, odd_pool_scale, odd_sink, odd_w_out, router_w, router_b,
             moe_w_gate, moe_w_up, moe_w_down, final_g, *, tm_lat, tm_ctx, tq, tm_dft, tm_moe):
    b, n, d = x.shape
    l = ctx.shape[1]

    rows = ((b + 1 + 7) // 8) * 8
    s_rows = jnp.zeros((rows, d), F32).at[:b].set(c).at[b].set(c_ctx)
    mods = _ada_call(s_rows, ada_w, ada_b)

    def mod_vecs(layer):
        m = mods[layer, :b].reshape(b, N_MOD, 1, d)
        mc = mods[layer, b].reshape(N_MOD, 1, 1, d)
        return [m[:, k] for k in range(N_MOD)], [mc[k] for k in range(N_MOD)]

    rw_t = router_w.T
    rw_hi = rw_t.astype(BF16)
    rw_lo = (rw_t - rw_hi.astype(F32)).astype(BF16)
    rwt = jnp.concatenate([rw_hi, rw_lo], axis=0)
    rb = router_b.astype(F32).reshape(N_EXPERTS, 1)
    cs_tab = _channel_dft_table()

    m, mc = mod_vecs(0)
    w_in0 = even_w_in[0].astype(BF16)
    w_out0 = even_w_out[0].astype(BF16)

    def even_stream(h, mv, tm):
        nn = h.shape[1]
        uc, us, gb, gu = _inproj_call(h, norm_mix_g[0], mv[0], mv[1], w_in0, mode="even", tm=tm,
                                      extra=(cs_tab,))
        yf = _dft_call(_position_dft_tables(nn), uc, us, tm=min(tm_dft, nn))
        return _even_out_call(yf, gb, gu, even_conv_w[0], w_out0, h, mv[2], norm_ffn_g[0],
                              mv[3], mv[4], rwt, rb, tm=tm)

    h1, fx_lat, bk_lat = even_stream(x, m, tm_lat)
    hc1, fx_ctx, bk_ctx = even_stream(ctx, mc, tm_ctx)

    moe0 = _moe_layer([fx_lat.reshape(b * n, ROW_WIDTH), fx_ctx.reshape(b * l, ROW_WIDTH)], [bk_lat, bk_ctx],
                      moe_w_gate[0].astype(BF16), moe_w_up[0].astype(BF16), moe_w_down[0].astype(BF16),
                      tm=tm_moe)
    gate_lat0, gate_ctx0 = m[5], mc[5]

    m, mc = mod_vecs(1)
    w_in1 = odd_w_in[0]
    kv0 = POOL_WIDTH + ATTN_WIDTH
    wk, wv = w_in1[:, kv0:kv0 + KV_WIDTH], w_in1[:, kv0 + KV_WIDTH:]

    def dup_heads(wm):
        return jnp.concatenate([wm[:, :HEAD_DIM], wm[:, :HEAD_DIM], wm[:, HEAD_DIM:], wm[:, HEAD_DIM:]], axis=1)

    w_kv_dup = jnp.concatenate([dup_heads(wk), dup_heads(wv)], axis=1)
    w_lat1 = jnp.concatenate([w_in1[:, :kv0], w_kv_dup], axis=1).astype(BF16)
    w_out1 = odd_w_out[0].astype(BF16)

    cos_t, sin_t = _rope_tables(n)
    q_scale = HEAD_DIM ** -0.5
    h1b, up, q, kd, vd = _inproj_call(h1, norm_mix_g[1], m[0], m[1], w_lat1, mode="odd", tm=tm_lat,
                                      moe=moe0, moe_row0=0, gate=gate_lat0,
                                      extra=(cos_t * q_scale, sin_t * q_scale, cos_t, sin_t))
    (kvx,) = _inproj_call(hc1, norm_mix_g[1], mc[0], mc[1], w_kv_dup.astype(BF16), mode="plain", tm=tm_ctx,
                          moe=moe0, moe_row0=b * n, gate=gate_ctx0)
    h2, fx2, bk2 = _odd_out_call(odd_sink[0], up, q, kd, vd, kvx, odd_pool_w[0].astype(BF16),
                                 odd_pool_scale[0], w_out1, h1b, m[2], norm_ffn_g[1], m[3], m[4], rwt, rb, tq=tq)
    moe1 = _moe_layer([fx2.reshape(b * n, ROW_WIDTH)], [bk2], moe_w_gate[1].astype(BF16),
                      moe_w_up[1].astype(BF16), moe_w_down[1].astype(BF16), tm=tm_moe)
    return _final_call(h2, moe1, m[5], final_g, tm=tm_lat)


def kernel(x, c, ctx, c_ctx, ada_w, ada_b, norm_mix_g, norm_ffn_g, even_w_in, even_conv_w, even_w_out,
           odd_w_in, odd_pool_w, odd_pool_scale, odd_sink, odd_w_out, router_w, router_b,
           moe_w_gate, moe_w_up, moe_w_down, final_g):
    return _forward(x, c, ctx, c_ctx, ada_w, ada_b, norm_mix_g, norm_ffn_g, even_w_in, even_conv_w,
                    even_w_out, odd_w_in, odd_pool_w, odd_pool_scale, odd_sink, odd_w_out, router_w,
                    router_b, moe_w_gate, moe_w_up, moe_w_down, final_g,
                    tm_lat=512, tm_ctx=256, tq=256, tm_dft=512, tm_moe=512)
```

```python
import functools
import math

import numpy as np
import jax
import jax.numpy as jnp
from jax import lax
from jax.experimental import pallas as pl
from jax.experimental.pallas import tpu as pltpu

F32 = jnp.float32
BF16 = jnp.bfloat16

D_MODEL = 1024
GRID_W = 64
EPS = 1e-6
N_MOD = 6
FOURIER_HEADS = 4
FOURIER_HEAD_DIM = 128
FOURIER_WIDTH = 512
CONV_WIDTH = 512
CONV_K = 3
POOL_WINDOWS = (2, 4, 8, 16)
POOL_GROUP_DIM = 128
POOL_WIDTH = 512
POOL_HALO = 8
HEAD_DIM = 64
N_Q_HEADS = 8
N_KV_HEADS = 2
ATTN_WIDTH = 512
KV_WIDTH = 128
ATTN_BLOCK = 128
ROPE_THETA = 10000.0
N_EXPERTS = 16
N_GROUPS = 4
EXPERTS_PER_GROUP = 4
D_EXPERT = 512
N_PAIRS = 6
N_BUCKETS = N_GROUPS * N_PAIRS

_PAIRS = [(a, b) for a in range(EXPERTS_PER_GROUP) for b in range(a + 1, EXPERTS_PER_GROUP)]
_BUCKET_LO = [(k // N_PAIRS) * EXPERTS_PER_GROUP + _PAIRS[k % N_PAIRS][0] for k in range(N_BUCKETS)]
_BUCKET_HI = [(k // N_PAIRS) * EXPERTS_PER_GROUP + _PAIRS[k % N_PAIRS][1] for k in range(N_BUCKETS)]

LANES = 128
ROW_WIDTH = D_MODEL + LANES
INFO_BUCKET, INFO_W_LO, INFO_W_HI = 0, 1, 2
VMEM_LIMIT_BYTES = 48 * 1024 * 1024

HIGHEST = lax.Precision.HIGHEST


def _cparams(sem):
    return pltpu.CompilerParams(dimension_semantics=sem, vmem_limit_bytes=VMEM_LIMIT_BYTES)


def _rms_mod(x, g, shift, scale):
    y = x * lax.rsqrt(jnp.mean(x * x, axis=-1, keepdims=True) + EPS) * g
    return y * (1.0 + scale) + shift


def _dot(a, b):
    return jnp.dot(a, b, preferred_element_type=F32)


def _dot_nt(a, b):
    return lax.dot_general(a, b, (((1,), (1,)), ((), ())), preferred_element_type=F32)


def _ada_kernel(s_ref, w_ref, b_ref, o_ref):
    s = s_ref[...]
    s = s * jax.nn.sigmoid(s)
    o_ref[0] = jnp.dot(s, w_ref[0], preferred_element_type=F32, precision=HIGHEST) + b_ref[0]


def _ada_call(s_rows, ada_w, ada_b):
    depth, d, n6 = ada_w.shape
    r = s_rows.shape[0]
    tn = 1536
    return pl.pallas_call(
        _ada_kernel,
        grid=(depth, n6 // tn),
        in_specs=[
            pl.BlockSpec((r, d), lambda l, j: (0, 0)),
            pl.BlockSpec((1, d, tn), lambda l, j: (l, 0, j)),
            pl.BlockSpec((1, 1, tn), lambda l, j: (l, 0, j)),
        ],
        out_specs=pl.BlockSpec((1, r, tn), lambda l, j: (l, 0, j)),
        out_shape=jax.ShapeDtypeStruct((depth, r, n6), F32),
        compiler_params=_cparams(("arbitrary", "arbitrary")),
        name="ada_mod",
    )(s_rows, ada_w, ada_b.reshape(depth, 1, n6))


def _rope_group(x, cos, sin_signed):
    lane = lax.broadcasted_iota(jnp.int32, x.shape, 1)
    first_half = (lane % 32) < 16
    partner = jnp.where(first_half, pltpu.roll(x, LANES - 16, 1), pltpu.roll(x, 16, 1))
    return x * cos + partner * sin_signed


def _inproj_kernel(*refs, mode, add_moe):
    it = iter(refs)
    h_ref = next(it)
    if add_moe:
        moe_ref, gate_ref = next(it), next(it)
    g_ref, sh_ref, sc_ref, w_ref = next(it), next(it), next(it), next(it)
    x = h_ref[0]
    if add_moe:
        x = x + gate_ref[0] * moe_ref[...]
    a = _rms_mod(x, g_ref[...], sh_ref[0], sc_ref[0])
    proj = _dot(a.astype(BF16), w_ref[...])
    if mode == "even":
        cs_ref = next(it)
        uc_ref, us_ref, gb_ref, gu_ref = next(it), next(it), next(it), next(it)
        uf = proj[:, :FOURIER_WIDTH].astype(BF16)
        cs = cs_ref[...].astype(BF16)
        for hh in range(FOURIER_HEADS):
            cols = slice(hh * LANES, (hh + 1) * LANES)
            r = _dot(uf[:, cols], cs)
            uc_ref[0, :, cols] = r[:, :LANES].astype(BF16)
            us_ref[0, :, cols] = r[:, LANES:].astype(BF16)
        c0 = FOURIER_WIDTH
        gb_ref[0] = proj[:, c0:c0 + CONV_WIDTH]
        gu_ref[0] = proj[:, c0 + CONV_WIDTH:c0 + 2 * CONV_WIDTH] * proj[:, c0 + 2 * CONV_WIDTH:]
    elif mode == "odd":
        cq_ref, sq_ref, ck_ref, sk_ref = next(it), next(it), next(it), next(it)
        hn_ref, up_ref, q_ref, kd_ref, vd_ref = next(it), next(it), next(it), next(it), next(it)
        hn_ref[0] = x
        up_ref[0] = proj[:, :POOL_WIDTH]
        c0 = POOL_WIDTH
        for gi in range(ATTN_WIDTH // LANES):
            cols = slice(c0 + gi * LANES, c0 + (gi + 1) * LANES)
            q_ref[0, :, gi * LANES:(gi + 1) * LANES] = _rope_group(
                proj[:, cols], cq_ref[...], sq_ref[...]).astype(BF16)
        c0 += ATTN_WIDTH
        for gi in range(2 * KV_WIDTH // LANES):
            cols = slice(c0 + gi * LANES, c0 + (gi + 1) * LANES)
            kd_ref[0, :, gi * LANES:(gi + 1) * LANES] = _rope_group(
                proj[:, cols], ck_ref[...], sk_ref[...]).astype(BF16)
        c0 += 2 * KV_WIDTH
        vd_ref[0] = proj[:, c0:].astype(BF16)
    else:
        kv_ref = next(it)
        kv_ref[0] = proj.astype(BF16)


def _mod_spec(arr):
    if arr.shape[0] > 1:
        return pl.BlockSpec((1, 1, arr.shape[2]), lambda b, i: (b, 0, 0))
    return pl.BlockSpec((1, 1, arr.shape[2]), lambda b, i: (0, 0, 0))


def _flat_rows_spec(tm, d, nt, row0):
    tile0 = row0 // tm
    return pl.BlockSpec((tm, d), lambda bb, i: (tile0 + bb * nt + i, 0))


def _inproj_call(h, g, shift, scale, w, *, mode, tm, moe=None, moe_row0=0, gate=None, extra=()):
    b, n, d = h.shape
    nout = w.shape[1]
    add_moe = moe is not None
    row = lambda bb, i: (bb, i, 0)
    full2 = lambda bb, i: (0, 0)
    args = [h]
    in_specs = [pl.BlockSpec((1, tm, d), row)]
    if add_moe:
        args += [moe, gate]
        in_specs += [_flat_rows_spec(tm, d, n // tm, moe_row0), _mod_spec(gate)]
    args += [g.reshape(1, d), shift, scale, w]
    in_specs += [pl.BlockSpec((1, d), full2), _mod_spec(shift), _mod_spec(scale),
                 pl.BlockSpec((d, nout), full2)]
    if mode == "even":
        args += list(extra)
        in_specs += [pl.BlockSpec(extra[0].shape, full2)]
        out_shape = [jax.ShapeDtypeStruct((b, n, FOURIER_WIDTH), BF16),
                     jax.ShapeDtypeStruct((b, n, FOURIER_WIDTH), BF16),
                     jax.ShapeDtypeStruct((b, n, CONV_WIDTH), F32),
                     jax.ShapeDtypeStruct((b, n, CONV_WIDTH), F32)]
        out_specs = [pl.BlockSpec((1, tm, 512), row)] * 4
    elif mode == "odd":
        args += list(extra)
        in_specs += [pl.BlockSpec((tm, LANES), lambda bb, i: (i, 0))] * 4
        out_shape = [jax.ShapeDtypeStruct((b, n, d), F32),
                     jax.ShapeDtypeStruct((b, n, POOL_WIDTH), F32),
                     jax.ShapeDtypeStruct((b, n, ATTN_WIDTH), BF16),
                     jax.ShapeDtypeStruct((b, n, 2 * KV_WIDTH), BF16),
                     jax.ShapeDtypeStruct((b, n, 2 * KV_WIDTH), BF16)]
        out_specs = [pl.BlockSpec((1, tm, d), row), pl.BlockSpec((1, tm, POOL_WIDTH), row),
                     pl.BlockSpec((1, tm, ATTN_WIDTH), row),
                     pl.BlockSpec((1, tm, 2 * KV_WIDTH), row), pl.BlockSpec((1, tm, 2 * KV_WIDTH), row)]
    else:
        out_shape = [jax.ShapeDtypeStruct((b, n, nout), BF16)]
        out_specs = [pl.BlockSpec((1, tm, nout), row)]
    return pl.pallas_call(
        functools.partial(_inproj_kernel, mode=mode, add_moe=add_moe),
        grid=(b, n // tm),
        in_specs=in_specs,
        out_specs=out_specs,
        out_shape=out_shape,
        compiler_params=_cparams(("arbitrary", "arbitrary")),
        name="inproj_" + mode,
    )(*args)


DFT_RADIX = 64


def _dft_kernel(ca_ref, sa_ref, cb_ref, sb_ref, uc_ref, us_ref, o_ref, c_scr, s_scr, *, norm):
    @pl.when(pl.program_id(1) == 0)
    def _():
        cb, sb = cb_ref[...], sb_ref[...]
        for r in range(ca_ref.shape[0]):
            ca, sa = ca_ref[r:r + 1, :], sa_ref[r:r + 1, :]
            rows = slice(r * DFT_RADIX, (r + 1) * DFT_RADIX)
            c_scr[rows, :] = (ca * cb - sa * sb).astype(BF16)
            s_scr[rows, :] = (-(sa * cb + ca * sb)).astype(BF16)

    acc = _dot(c_scr[...], uc_ref[0]) + _dot(s_scr[...], us_ref[0])
    o_ref[0] = (acc * norm).astype(BF16)


def _dft_call(tabs, uc, us, *, tm):
    b, n, wdt = uc.shape
    r_tile = tm // DFT_RADIX
    norm = 1.0 / math.sqrt(n * FOURIER_HEAD_DIM)
    a_spec = pl.BlockSpec((r_tile, n), lambda i, bb: (i, 0))
    b_spec = pl.BlockSpec((DFT_RADIX, n), lambda i, bb: (0, 0))
    u_spec = pl.BlockSpec((1, n, wdt), lambda i, bb: (bb, 0, 0))
    return pl.pallas_call(
        functools.partial(_dft_kernel, norm=norm),
        grid=(n // tm, b),
        in_specs=[a_spec, a_spec, b_spec, b_spec, u_spec, u_spec],
        out_specs=pl.BlockSpec((1, tm, wdt), lambda i, bb: (bb, i, 0)),
        out_shape=jax.ShapeDtypeStruct((b, n, wdt), BF16),
        scratch_shapes=[pltpu.VMEM((tm, n), BF16), pltpu.VMEM((tm, n), BF16)],
        compiler_params=_cparams(("arbitrary", "arbitrary")),
        name="dft_rows",
    )(*tabs, uc, us)


def _tail(y, h_ref, gate_ref, g2_ref, sh_ref, sc_ref, rwt_ref, rb_ref, hout_ref, fx_ref, bk_ref):
    hn = h_ref[0] + gate_ref[0] * y
    hout_ref[0] = hn
    f = _rms_mod(hn, g2_ref[...], sh_ref[0], sc_ref[0])
    tm = f.shape[0]
    f_hi = f.astype(BF16)
    f_lo = (f - f_hi.astype(F32)).astype(BF16)
    both = _dot_nt(rwt_ref[...], f_hi)
    logits = (both[:N_EXPERTS] + both[N_EXPERTS:]) + _dot_nt(rwt_ref[:N_EXPERTS, :], f_lo)
    aff = jax.nn.sigmoid(logits)
    sel = aff + rb_ref[...]
    cands = []
    for bkt in range(N_BUCKETS):
        lo, hi = _BUCKET_LO[bkt], _BUCKET_HI[bkt]
        cands.append((sel[lo:lo + 1, :] + sel[hi:hi + 1, :], jnp.full((1, tm), float(bkt), F32),
                      aff[lo:lo + 1, :], aff[hi:hi + 1, :]))
    while len(cands) > 1:
        merged = []
        for k in range(0, len(cands) - 1, 2):
            left, right = cands[k], cands[k + 1]
            take_right = right[0] > left[0]
            merged.append(tuple(jnp.where(take_right, r, l) for l, r in zip(left, right)))
        if len(cands) % 2:
            merged.append(cands[-1])
        cands = merged
    _, bucket, a_lo, a_hi = cands[0]
    den = a_lo + a_hi
    info = jnp.concatenate([bucket, a_lo / den, a_hi / den, jnp.zeros((LANES - 3, tm), F32)], axis=0)
    fx_ref[0, :, :D_MODEL] = f
    fx_ref[0, :, D_MODEL:] = info.T
    bk_ref[...] = jnp.concatenate([bucket, jnp.zeros((7, tm), F32)], axis=0)


def _tail_specs(h, gate, shift, scale, tm):
    b, n, d = h.shape
    nt = n // tm
    row = lambda bb, i: (bb, i, 0)
    in_specs = [pl.BlockSpec((1, tm, d), row), _mod_spec(gate),
                pl.BlockSpec((1, d), lambda bb, i: (0, 0)), _mod_spec(shift), _mod_spec(scale),
                pl.BlockSpec((2 * N_EXPERTS, d), lambda bb, i: (0, 0)),
                pl.BlockSpec((N_EXPERTS, 1), lambda bb, i: (0, 0))]
    out_shape = [jax.ShapeDtypeStruct((b, n, d), F32), jax.ShapeDtypeStruct((b, n, ROW_WIDTH), F32),
                 jax.ShapeDtypeStruct((8, b * n), F32)]
    out_specs = [pl.BlockSpec((1, tm, d), row), pl.BlockSpec((1, tm, ROW_WIDTH), row),
                 pl.BlockSpec((8, tm), lambda bb, i: (0, bb * nt + i))]
    return in_specs, out_shape, out_specs


def _even_out_kernel(yf_ref, gb_ref, gu_ref, gp_ref, gn_ref, cw_ref, wo_ref,
                     h_ref, gate_ref, g2_ref, sh_ref, sc_ref, rwt_ref, rb_ref,
                     hout_ref, fx_ref, bk_ref):
    i = pl.program_id(1)
    last = pl.num_programs(1) - 1
    gu = gu_ref[0]
    tm = gu.shape[0]
    prev = jnp.where(i > 0, gp_ref[0, 7:8, :], 0.0)
    nxt = jnp.where(i < last, gn_ref[0, 0:1, :], 0.0)
    row = lax.broadcasted_iota(jnp.int32, gu.shape, 0)
    up = jnp.where(row == 0, prev, pltpu.roll(gu, 1, 0))
    dn = jnp.where(row == tm - 1, nxt, pltpu.roll(gu, tm - 1, 0))
    conv = up * cw_ref[0:1, :] + gu * cw_ref[1:2, :] + dn * cw_ref[2:3, :]
    yc = (gb_ref[0] * conv).astype(BF16)
    y = _dot(yf_ref[0], wo_ref[:FOURIER_WIDTH, :]) + _dot(yc, wo_ref[FOURIER_WIDTH:, :])
    _tail(y, h_ref, gate_ref, g2_ref, sh_ref, sc_ref, rwt_ref, rb_ref, hout_ref, fx_ref, bk_ref)


def _even_out_call(yf, gb, gu, conv_w, w_out, h, gate, g2, shift, scale, rwt, rb, *, tm):
    b, n, d = h.shape
    row = lambda bb, i: (bb, i, 0)
    nb8 = n // 8
    t8 = tm // 8
    tin, out_shape, out_specs = _tail_specs(h, gate, shift, scale, tm)
    in_specs = [pl.BlockSpec((1, tm, 512), row), pl.BlockSpec((1, tm, 512), row),
                pl.BlockSpec((1, tm, 512), row),
                pl.BlockSpec((1, 8, 512), lambda bb, i: (bb, jnp.maximum(i * t8 - 1, 0), 0)),
                pl.BlockSpec((1, 8, 512), lambda bb, i: (bb, jnp.minimum((i + 1) * t8, nb8 - 1), 0)),
                pl.BlockSpec((CONV_K, CONV_WIDTH), lambda bb, i: (0, 0)),
                pl.BlockSpec((d, d), lambda bb, i: (0, 0))] + tin
    return pl.pallas_call(
        _even_out_kernel,
        grid=(b, n // tm),
        in_specs=in_specs,
        out_specs=out_specs,
        out_shape=out_shape,
        compiler_params=_cparams(("arbitrary", "arbitrary")),
        name="even_out",
    )(yf, gb, gu, gu, gu, conv_w, w_out, h, gate, g2.reshape(1, d), shift, scale, rwt, rb)


def _odd_out_kernel(sink_ref, up_ref, upp_ref, upn_ref, q_ref, kc_ref, kp_ref, kn_ref,
                    vc_ref, vp_ref, vn_ref, kvx_ref, pw_ref, ps_ref, wo_ref,
                    h_ref, gate_ref, g2_ref, sh_ref, sc_ref, rwt_ref, rb_ref,
                    hout_ref, fx_ref, bk_ref, ext_ref, mix_ref, *, n_total):
    i = pl.program_id(1)
    last = pl.num_programs(1) - 1
    tq = q_ref.shape[1]
    nsub = tq // ATTN_BLOCK

    u = up_ref[0]
    ext_ref[0:POOL_HALO, :] = jnp.where(i > 0, upp_ref[0], 0.0)
    ext_ref[POOL_HALO:POOL_HALO + tq, :] = u
    ext_ref[POOL_HALO + tq:, :] = jnp.where(i < last, upn_ref[0], 0.0)
    t = i * tq + lax.broadcasted_iota(jnp.int32, (tq, LANES), 0)
    for gi, win in enumerate(POOL_WINDOWS):
        r = win // 2
        cols = slice(gi * LANES, (gi + 1) * LANES)
        acc = ext_ref[POOL_HALO - r:POOL_HALO - r + tq, cols]
        for dlt in range(-r + 1, r + 1):
            acc = acc + ext_ref[POOL_HALO + dlt:POOL_HALO + dlt + tq, cols]
        cnt = (jnp.minimum(t + r + 1, n_total) - jnp.maximum(t - r, 0)).astype(F32)
        p = acc / cnt - u[:, cols]
        y = _dot(p.astype(BF16), pw_ref[gi]) * ps_ref[:, cols]
        mix_ref[:, cols] = y.astype(BF16)

    kwin = jnp.concatenate([kp_ref[0], kc_ref[0], kn_ref[0]], axis=0)
    vwin = jnp.concatenate([vp_ref[0], vc_ref[0], vn_ref[0]], axis=0)
    kvx = kvx_ref[0]
    kx, vx = kvx[:, :2 * KV_WIDTH], kvx[:, 2 * KV_WIDTH:]
    low = (lax.broadcasted_iota(jnp.int32, (1, 2 * KV_WIDTH), 1) % LANES) < HEAD_DIM
    zero = jnp.zeros((), BF16)
    k_half = (jnp.where(low, kwin, zero), jnp.where(low, zero, kwin))
    v_half = (jnp.where(low, vwin, zero), jnp.where(low, zero, vwin))
    kx_half = (jnp.where(low, kx, zero), jnp.where(low, zero, kx))
    vx_half = (jnp.where(low, vx, zero), jnp.where(low, zero, vx))

    span = 3 * ATTN_BLOCK
    rows2 = 2 * ATTN_BLOCK
    qi = lax.broadcasted_iota(jnp.int32, (rows2, span), 0) % ATTN_BLOCK
    kj = lax.broadcasted_iota(jnp.int32, (rows2, span), 1)
    in_prev = kj < ATTN_BLOCK
    in_next = kj >= 2 * ATTN_BLOCK
    neg_inf = jnp.float32(-jnp.inf)
    band_bias = (jnp.where(in_prev & (kj < qi), neg_inf, 0.0)
                 + jnp.where(in_next & (kj - 2 * ATTN_BLOCK > qi), neg_inf, 0.0))
    top_rows = lax.broadcasted_iota(jnp.int32, (rows2, 1), 0) < ATTN_BLOCK

    for j in range(nsub):
        blk = i * nsub + j
        prev_bias = jnp.where(blk > 0, 0.0, neg_inf)
        next_bias = jnp.where(blk < (n_total // ATTN_BLOCK) - 1, 0.0, neg_inf)
        bias = band_bias + jnp.where(in_prev, prev_bias, 0.0) + jnp.where(in_next, next_bias, 0.0)
        r0 = j * ATTN_BLOCK
        for kh in range(N_KV_HEADS):
            kcols = slice(kh * LANES, (kh + 1) * LANES)
            xq = jnp.concatenate(
                [q_ref[0, r0:r0 + ATTN_BLOCK, (2 * kh + pr) * LANES:(2 * kh + pr + 1) * LANES]
                 for pr in range(2)], axis=0)
            o_pair = None
            for half in range(2):
                ks = k_half[half][r0:r0 + span, kcols]
                vs = v_half[half][r0:r0 + span, kcols]
                s1 = _dot_nt(xq, ks) + bias
                s2 = _dot_nt(xq, kx_half[half][:, kcols])
                head0 = kh * 4 + half
                snk = jnp.where(top_rows, sink_ref[head0], sink_ref[head0 + 2])
                m = jnp.maximum(jnp.maximum(jnp.max(s1, axis=-1, keepdims=True),
                                            jnp.max(s2, axis=-1, keepdims=True)), snk)
                e1 = jnp.exp(s1 - m)
                e2 = jnp.exp(s2 - m)
                den = (jnp.sum(e1, axis=-1, keepdims=True) + jnp.sum(e2, axis=-1, keepdims=True)
                       + jnp.exp(snk - m))
                o = _dot(e1.astype(BF16), vs) + _dot(e2.astype(BF16), vx_half[half][:, kcols])
                o = o / den
                o_pair = o if o_pair is None else o_pair + o
            for pr in range(2):
                c0 = POOL_WIDTH + (2 * kh + pr) * LANES
                mix_ref[r0:r0 + ATTN_BLOCK, c0:c0 + LANES] = (
                    o_pair[pr * ATTN_BLOCK:(pr + 1) * ATTN_BLOCK].astype(BF16))

    y = _dot(mix_ref[...], wo_ref[...])
    _tail(y, h_ref, gate_ref, g2_ref, sh_ref, sc_ref, rwt_ref, rb_ref, hout_ref, fx_ref, bk_ref)


def _odd_out_call(sink, up, q, kd, vd, kvx, pool_w, pool_scale, w_out,
                  h, gate, g2, shift, scale, rwt, rb, *, tq):
    b, n, d = h.shape
    row = lambda bb, i: (bb, i, 0)
    nb8, t8 = n // POOL_HALO, tq // POOL_HALO
    nbk, tk = n // ATTN_BLOCK, tq // ATTN_BLOCK
    prev8 = lambda bb, i: (bb, jnp.maximum(i * t8 - 1, 0), 0)
    next8 = lambda bb, i: (bb, jnp.minimum((i + 1) * t8, nb8 - 1), 0)
    prevk = lambda bb, i: (bb, jnp.maximum(i * tk - 1, 0), 0)
    nextk = lambda bb, i: (bb, jnp.minimum((i + 1) * tk, nbk - 1), 0)
    kvw = 2 * KV_WIDTH
    tin, out_shape, out_specs = _tail_specs(h, gate, shift, scale, tq)
    in_specs = [pl.BlockSpec(memory_space=pltpu.SMEM),
                pl.BlockSpec((1, tq, POOL_WIDTH), row),
                pl.BlockSpec((1, POOL_HALO, POOL_WIDTH), prev8),
                pl.BlockSpec((1, POOL_HALO, POOL_WIDTH), next8),
                pl.BlockSpec((1, tq, ATTN_WIDTH), row),
                pl.BlockSpec((1, tq, kvw), row),
                pl.BlockSpec((1, ATTN_BLOCK, kvw), prevk),
                pl.BlockSpec((1, ATTN_BLOCK, kvw), nextk),
                pl.BlockSpec((1, tq, kvw), row),
                pl.BlockSpec((1, ATTN_BLOCK, kvw), prevk),
                pl.BlockSpec((1, ATTN_BLOCK, kvw), nextk),
                pl.BlockSpec((1, kvx.shape[1], 2 * kvw), lambda bb, i: (bb, 0, 0)),
                pl.BlockSpec(pool_w.shape, lambda bb, i: (0, 0, 0)),
                pl.BlockSpec((1, POOL_WIDTH), lambda bb, i: (0, 0)),
                pl.BlockSpec((d, d), lambda bb, i: (0, 0))] + tin
    return pl.pallas_call(
        functools.partial(_odd_out_kernel, n_total=n),
        grid=(b, n // tq),
        in_specs=in_specs,
        out_specs=out_specs,
        out_shape=out_shape,
        scratch_shapes=[pltpu.VMEM((tq + 2 * POOL_HALO, POOL_WIDTH), F32),
                        pltpu.VMEM((tq, d), BF16)],
        compiler_params=_cparams(("arbitrary", "arbitrary")),
        name="odd_out",
    )(sink, up, up, up, q, kd, kd, kd, vd, vd, vd, kvx, pool_w, pool_scale.reshape(1, POOL_WIDTH),
      w_out, h, gate, g2.reshape(1, d), shift, scale, rwt, rb)


def _moe_kernel(e_lo_ref, e_hi_ref, valid_ref, x_ref,
                g1_ref, u1_ref, d1_ref, g2_ref, u2_ref, d2_ref, o_ref):
    j = pl.program_id(0)

    @pl.when(valid_ref[j] != 0)
    def _():
        x = x_ref[:, :D_MODEL].astype(BF16)
        w_lo = x_ref[:, D_MODEL + INFO_W_LO:D_MODEL + INFO_W_LO + 1]
        w_hi = x_ref[:, D_MODEL + INFO_W_HI:D_MODEL + INFO_W_HI + 1]

        def expert(g_ref, u_ref, d_ref):
            gate = _dot(x, g_ref[0])
            hid = gate * jax.nn.sigmoid(gate) * _dot(x, u_ref[0])
            return _dot(hid.astype(BF16), d_ref[0])

        o_lo = expert(g1_ref, u1_ref, d1_ref)
        o_hi = expert(g2_ref, u2_ref, d2_ref)
        o_ref[...] = w_lo * o_lo + w_hi * o_hi

    @pl.when(valid_ref[j] == 0)
    def _():
        o_ref[...] = jnp.zeros(o_ref.shape, o_ref.dtype)


def _moe_call(tile_lo, tile_hi, tile_valid, xs, wg, wu, wd, *, tm):
    p = xs.shape[0]
    d = D_MODEL
    ntiles = p // tm
    lo3 = lambda j, lo, hi, v: (lo[j], 0, 0)
    hi3 = lambda j, lo, hi, v: (hi[j], 0, 0)
    rowm = lambda j, lo, hi, v: (j, 0)
    row_in = lambda j, lo, hi, v: (jnp.where(v[j] != 0, j, 0), 0)
    grid_spec = pltpu.PrefetchScalarGridSpec(
        num_scalar_prefetch=3,
        grid=(ntiles,),
        in_specs=[pl.BlockSpec((tm, ROW_WIDTH), row_in),
                  pl.BlockSpec((1, d, D_EXPERT), lo3), pl.BlockSpec((1, d, D_EXPERT), lo3),
                  pl.BlockSpec((1, D_EXPERT, d), lo3),
                  pl.BlockSpec((1, d, D_EXPERT), hi3), pl.BlockSpec((1, d, D_EXPERT), hi3),
                  pl.BlockSpec((1, D_EXPERT, d), hi3)],
        out_specs=pl.BlockSpec((tm, d), rowm),
    )
    return pl.pallas_call(
        _moe_kernel,
        grid_spec=grid_spec,
        out_shape=jax.ShapeDtypeStruct((p, d), F32),
        compiler_params=_cparams(("arbitrary",)),
        name="moe_pairs",
    )(tile_lo, tile_hi, tile_valid, xs, wg, wu, wd, wg, wu, wd)


RANK_ROWS = 32


def _rank_kernel(bk_ref, tri_ref, rk_ref, cnt_ref, carry_ref):
    @pl.when(pl.program_id(0) == 0)
    def _():
        carry_ref[...] = jnp.zeros(carry_ref.shape, F32)

    tr = bk_ref.shape[1]
    bucket = bk_ref[0:1, :]
    rows = lax.broadcasted_iota(jnp.int32, (RANK_ROWS, tr), 0).astype(F32)
    onehot = jnp.where(rows == bucket, 1.0, 0.0)
    before = _dot(onehot.astype(BF16), tri_ref[...]) + carry_ref[:, 0:1]
    rank = jnp.sum(onehot * before, axis=0, keepdims=True)
    rk_ref[...] = jnp.broadcast_to(rank, rk_ref.shape)
    carry_ref[...] = carry_ref[...] + jnp.sum(onehot, axis=1, keepdims=True)
    cnt_ref[...] = carry_ref[...]


def _rank_call(bk, *, tr):
    t = bk.shape[1]
    tri = jnp.asarray(np.triu(np.ones((tr, tr), np.float32), 1), BF16)
    return pl.pallas_call(
        _rank_kernel,
        grid=(t // tr,),
        in_specs=[pl.BlockSpec((8, tr), lambda j: (0, j)), pl.BlockSpec((tr, tr), lambda j: (0, 0))],
        out_specs=[pl.BlockSpec((8, tr), lambda j: (0, j)), pl.BlockSpec((RANK_ROWS, LANES), lambda j: (0, 0))],
        out_shape=[jax.ShapeDtypeStruct((8, t), F32), jax.ShapeDtypeStruct((RANK_ROWS, LANES), F32)],
        scratch_shapes=[pltpu.VMEM((RANK_ROWS, LANES), F32)],
        compiler_params=_cparams(("arbitrary",)),
        name="bucket_rank",
    )(bk, tri)


ROW_TILE = 512


def _scatter_rows_kernel(tile_end_ref, pos_ref, *refs, tile_starts, tm):
    nsrc = len(tile_starts)
    srcs, out_hbm, zero_ref, sem = refs[:nsrc], refs[nsrc], refs[nsrc + 1], refs[nsrc + 2]
    j = pl.program_id(0)

    @pl.when(j == 0)
    def _():
        zero_ref[...] = jnp.zeros(zero_ref.shape, F32)

        def fill(tile, start):
            copy = pltpu.make_async_copy(zero_ref, out_hbm.at[pl.ds(tile * tm, tm)], sem)
            copy.start() if start else copy.wait()

        n_slot_tiles = out_hbm.shape[0] // tm
        used = tile_end_ref[N_BUCKETS - 1]
        for start in (True, False):
            for b in range(N_BUCKETS):
                first_tile = tile_end_ref[b - 1] if b else 0
                pl.when(tile_end_ref[b] > first_tile)(functools.partial(fill, tile_end_ref[b] - 1, start))
                pl.when(n_slot_tiles - 1 - b >= used)(functools.partial(fill, n_slot_tiles - 1 - b, start))

    def move(src_vmem):
        def body(r, carry):
            pltpu.make_async_copy(src_vmem.at[pl.ds(r, 1)],
                                  out_hbm.at[pl.ds(pos_ref[0, 0, r], 1)], sem).start()
            return carry
        lax.fori_loop(0, ROW_TILE, body, 0, unroll=8)
        pltpu.make_async_copy(src_vmem, out_hbm.at[pl.ds(0, ROW_TILE)], sem).wait()

    for s in range(nsrc):
        first = tile_starts[s]
        if nsrc == 1:
            move(srcs[s])
        else:
            in_range = (j >= first) if s == nsrc - 1 else ((j >= first) & (j < tile_starts[s + 1]))
            pl.when(in_range)(functools.partial(move, srcs[s]))


def _scatter_rows_call(pos, tile_end, sources, nslots, tm):
    w = sources[0].shape[1]
    t = pos.shape[0]
    ntiles = t // ROW_TILE
    tile_starts, src_specs, acc = [], [], 0
    for s in sources:
        first, count = acc, s.shape[0] // ROW_TILE
        tile_starts.append(first)
        src_specs.append(pl.BlockSpec(
            (ROW_TILE, w), lambda j, te, first=first, count=count: (jnp.clip(j - first, 0, count - 1), 0)))
        acc += count
    grid_spec = pltpu.PrefetchScalarGridSpec(
        num_scalar_prefetch=1,
        grid=(ntiles,),
        in_specs=[pl.BlockSpec((1, 1, ROW_TILE), lambda j, te: (j, 0, 0), memory_space=pltpu.SMEM)] + src_specs,
        out_specs=pl.BlockSpec(memory_space=pl.ANY),
        scratch_shapes=[pltpu.VMEM((tm, w), F32), pltpu.SemaphoreType.DMA(())],
    )
    return pl.pallas_call(
        functools.partial(_scatter_rows_kernel, tile_starts=tuple(tile_starts), tm=tm),
        grid_spec=grid_spec,
        out_shape=jax.ShapeDtypeStruct((nslots, w), F32),
        compiler_params=_cparams(("arbitrary",)),
        name="scatter_rows",
    )(tile_end, pos.reshape(ntiles, 1, ROW_TILE), *sources)


def _gather_rows_kernel(pos_ref, src_hbm, out_ref, sem):
    def body(r, carry):
        pltpu.make_async_copy(src_hbm.at[pl.ds(pos_ref[0, 0, r], 1)], out_ref.at[pl.ds(r, 1)], sem).start()
        return carry
    lax.fori_loop(0, ROW_TILE, body, 0, unroll=8)
    pltpu.make_async_copy(src_hbm.at[pl.ds(0, ROW_TILE)], out_ref, sem).wait()


def _gather_rows_call(pos, src):
    t = pos.shape[0]
    w = src.shape[1]
    ntiles = t // ROW_TILE
    return pl.pallas_call(
        _gather_rows_kernel,
        grid=(ntiles,),
        in_specs=[pl.BlockSpec((1, 1, ROW_TILE), lambda j: (j, 0, 0), memory_space=pltpu.SMEM),
                  pl.BlockSpec(memory_space=pl.ANY)],
        out_specs=pl.BlockSpec((ROW_TILE, w), lambda j: (j, 0)),
        out_shape=jax.ShapeDtypeStruct((t, w), F32),
        scratch_shapes=[pltpu.SemaphoreType.DMA(())],
        compiler_params=_cparams(("arbitrary",)),
        name="gather_rows",
    )(pos.reshape(ntiles, 1, ROW_TILE), src)


def _final_kernel(h_ref, moe_ref, gate_ref, g_ref, o_ref):
    x = h_ref[0] + gate_ref[0] * moe_ref[...]
    o_ref[0] = x * lax.rsqrt(jnp.mean(x * x, axis=-1, keepdims=True) + EPS) * g_ref[...]


def _final_call(h, moe, gate, g, *, tm):
    b, n, d = h.shape
    row = lambda bb, i: (bb, i, 0)
    return pl.pallas_call(
        _final_kernel,
        grid=(b, n // tm),
        in_specs=[pl.BlockSpec((1, tm, d), row), _flat_rows_spec(tm, d, n // tm, 0), _mod_spec(gate),
                  pl.BlockSpec((1, d), lambda bb, i: (0, 0))],
        out_specs=pl.BlockSpec((1, tm, d), row),
        out_shape=jax.ShapeDtypeStruct((b, n, d), F32),
        compiler_params=_cparams(("arbitrary", "arbitrary")),
        name="final_norm",
    )(h, moe, gate, g.reshape(1, d))


def _channel_dft_table():
    c = np.arange(FOURIER_HEAD_DIM)
    ang = 2.0 * np.pi * ((c[:, None] * c[None, :]) % FOURIER_HEAD_DIM) / FOURIER_HEAD_DIM
    return jnp.asarray(np.concatenate([np.cos(ang), np.sin(ang)], axis=1), F32)


def _position_dft_tables(n):
    n1 = n // DFT_RADIX
    t = np.arange(n)
    a = 2.0 * np.pi * ((np.arange(n1)[:, None] * t[None, :]) % n1) / n1
    bb = 2.0 * np.pi * ((np.arange(DFT_RADIX)[:, None] * t[None, :]) % n) / n
    return tuple(jnp.asarray(v, F32) for v in (np.cos(a), np.sin(a), np.cos(bb), np.sin(bb)))


def _rope_tables(n):
    quarter = HEAD_DIM // 4
    inv = ROPE_THETA ** (-jnp.arange(quarter, dtype=F32) / quarter)
    t = jnp.arange(n)
    ang_r = (t // GRID_W).astype(F32)[:, None] * inv
    ang_c = (t % GRID_W).astype(F32)[:, None] * inv
    cos = jnp.concatenate([jnp.cos(ang_r)] * 2 + [jnp.cos(ang_c)] * 2, axis=1)
    sin = jnp.concatenate([-jnp.sin(ang_r), jnp.sin(ang_r), -jnp.sin(ang_c), jnp.sin(ang_c)], axis=1)
    return jnp.tile(cos, (1, 2)), jnp.tile(sin, (1, 2))


def _dispatch_plan(bucket, rank, counts, tm):
    t = bucket.shape[0]
    ntiles = t // tm + N_BUCKETS
    tiles_per = (counts + tm - 1) // tm
    tile_end = jnp.cumsum(tiles_per)
    tile_start = tile_end - tiles_per
    onehot = bucket[:, None] == jnp.arange(N_BUCKETS, dtype=jnp.int32)[None, :]
    pos = jnp.sum(jnp.where(onehot, (tile_start * tm)[None, :], 0), axis=-1) + rank
    tile_ids = jnp.arange(ntiles, dtype=jnp.int32)
    used = tile_end[-1]
    tile_bucket = jnp.sum((tile_ids[:, None] >= tile_end[None, :]).astype(jnp.int32), axis=1)
    last_bucket = jnp.sum((jnp.maximum(used - 1, 0) >= tile_end).astype(jnp.int32))
    tile_valid = (tile_ids < used).astype(jnp.int32)
    tile_bucket = jnp.where(tile_valid == 1, tile_bucket, last_bucket)
    tile_lo = jnp.asarray(np.asarray(_BUCKET_LO, np.int32))[tile_bucket]
    tile_hi = jnp.asarray(np.asarray(_BUCKET_HI, np.int32))[tile_bucket]
    return pos.astype(jnp.int32), tile_end.astype(jnp.int32), tile_lo, tile_hi, tile_valid, ntiles * tm


def _moe_layer(fx_list, bk_list, wg, wu, wd, *, tm):
    bk = bk_list[0] if len(bk_list) == 1 else jnp.concatenate(bk_list, axis=1)
    rk, cnt = _rank_call(bk, tr=ROW_TILE)
    pos, tile_end, tile_lo, tile_hi, tile_valid, nslots = _dispatch_plan(
        bk[0].astype(jnp.int32), rk[0].astype(jnp.int32), cnt[:N_BUCKETS, 0].astype(jnp.int32), tm)
    xs = _scatter_rows_call(pos, tile_end, fx_list, nslots, tm)
    out_sorted = _moe_call(tile_lo, tile_hi, tile_valid, xs, wg, wu, wd, tm=tm)
    return _gather_rows_call(pos, out_sorted)


def _forward(x, c, ctx, c_ctx, ada_w, ada_b, norm_mix_g, norm_ffn_g, even_w_in, even_conv_w, even_w_out,
             odd_w_in, odd_pool_w, odd_pool_scale, odd_sink, odd_w_out, router_w, router_b,
             moe_w_gate, moe_w_up, moe_w_down, final_g, *, tm_lat, tm_ctx, tq, tm_dft, tm_moe):
    b, n, d = x.shape
    l = ctx.shape[1]

    rows = ((b + 1 + 7) // 8) * 8
    s_rows = jnp.zeros((rows, d), F32).at[:b].set(c).at[b].set(c_ctx)
    mods = _ada_call(s_rows, ada_w, ada_b)

    def mod_vecs(layer):
        m = mods[layer, :b].reshape(b, N_MOD, 1, d)
        mc = mods[layer, b].reshape(N_MOD, 1, 1, d)
        return [m[:, k] for k in range(N_MOD)], [mc[k] for k in range(N_MOD)]

    rw_t = router_w.T
    rw_hi = rw_t.astype(BF16)
    rw_lo = (rw_t - rw_hi.astype(F32)).astype(BF16)
    rwt = jnp.concatenate([rw_hi, rw_lo], axis=0)
    rb = router_b.astype(F32).reshape(N_EXPERTS, 1)
    cs_tab = _channel_dft_table()

    m, mc = mod_vecs(0)
    w_in0 = even_w_in[0].astype(BF16)
    w_out0 = even_w_out[0].astype(BF16)

    def even_stream(h, mv, tm):
        nn = h.shape[1]
        uc, us, gb, gu = _inproj_call(h, norm_mix_g[0], mv[0], mv[1], w_in0, mode="even", tm=tm,
                                      extra=(cs_tab,))
        yf = _dft_call(_position_dft_tables(nn), uc, us, tm=min(tm_dft, nn))
        return _even_out_call(yf, gb, gu, even_conv_w[0], w_out0, h, mv[2], norm_ffn_g[0],
                              mv[3], mv[4], rwt, rb, tm=tm)

    h1, fx_lat, bk_lat = even_stream(x, m, tm_lat)
    hc1, fx_ctx, bk_ctx = even_stream(ctx, mc, tm_ctx)

    moe0 = _moe_layer([fx_lat.reshape(b * n, ROW_WIDTH), fx_ctx.reshape(b * l, ROW_WIDTH)], [bk_lat, bk_ctx],
                      moe_w_gate[0].astype(BF16), moe_w_up[0].astype(BF16), moe_w_down[0].astype(BF16),
                      tm=tm_moe)
    gate_lat0, gate_ctx0 = m[5], mc[5]

    m, mc = mod_vecs(1)
    w_in1 = odd_w_in[0]
    kv0 = POOL_WIDTH + ATTN_WIDTH
    wk, wv = w_in1[:, kv0:kv0 + KV_WIDTH], w_in1[:, kv0 + KV_WIDTH:]

    def dup_heads(wm):
        return jnp.concatenate([wm[:, :HEAD_DIM], wm[:, :HEAD_DIM], wm[:, HEAD_DIM:], wm[:, HEAD_DIM:]], axis=1)

    w_kv_dup = jnp.concatenate([dup_heads(wk), dup_heads(wv)], axis=1)
    w_lat1 = jnp.concatenate([w_in1[:, :kv0], w_kv_dup], axis=1).astype(BF16)
    w_out1 = odd_w_out[0].astype(BF16)

    cos_t, sin_t = _rope_tables(n)
    q_scale = HEAD_DIM ** -0.5
    h1b, up, q, kd, vd = _inproj_call(h1, norm_mix_g[1], m[0], m[1], w_lat1, mode="odd", tm=tm_lat,
                                      moe=moe0, moe_row0=0, gate=gate_lat0,
                                      extra=(cos_t * q_scale, sin_t * q_scale, cos_t, sin_t))
    (kvx,) = _inproj_call(hc1, norm_mix_g[1], mc[0], mc[1], w_kv_dup.astype(BF16), mode="plain", tm=tm_ctx,
                          moe=moe0, moe_row0=b * n, gate=gate_ctx0)
    h2, fx2, bk2 = _odd_out_call(odd_sink[0], up, q, kd, vd, kvx, odd_pool_w[0].astype(BF16),
                                 odd_pool_scale[0], w_out1, h1b, m[2], norm_ffn_g[1], m[3], m[4], rwt, rb, tq=tq)
    moe1 = _moe_layer([fx2.reshape(b * n, ROW_WIDTH)], [bk2], moe_w_gate[1].astype(BF16),
                      moe_w_up[1].astype(BF16), moe_w_down[1].astype(BF16), tm=tm_moe)
    return _final_call(h2, moe1, m[5], final_g, tm=tm_lat)


def kernel(x, c, ctx, c_ctx, ada_w, ada_b, norm_mix_g, norm_ffn_g, even_w_in, even_conv_w, even_w_out,
           odd_w_in, odd_pool_w, odd_pool_scale, odd_sink, odd_w_out, router_w, router_b,
           moe_w_gate, moe_w_up, moe_w_down, final_g):
    return _forward(x, c, ctx, c_ctx, ada_w, ada_b, norm_mix_g, norm_ffn_g, even_w_in, even_conv_w,
                    even_w_out, odd_w_in, odd_pool_w, odd_pool_scale, odd_sink, odd_w_out, router_w,
                    router_b, moe_w_gate, moe_w_up, moe_w_down, final_g,
                    tm_lat=512, tm_ctx=256, tq=512, tm_dft=1024, tm_moe=512)
```

```python
import functools
import math

import numpy as np
import jax
import jax.numpy as jnp
from jax import lax
from jax.experimental import pallas as pl
from jax.experimental.pallas import tpu as pltpu

F32 = jnp.float32
BF16 = jnp.bfloat16

D_MODEL = 1024
GRID_W = 64
EPS = 1e-6
N_MOD = 6
FOURIER_HEADS = 4
FOURIER_HEAD_DIM = 128
FOURIER_WIDTH = 512
CONV_WIDTH = 512
CONV_K = 3
POOL_WINDOWS = (2, 4, 8, 16)
POOL_GROUP_DIM = 128
POOL_WIDTH = 512
POOL_HALO = 8
HEAD_DIM = 64
N_Q_HEADS = 8
N_KV_HEADS = 2
ATTN_WIDTH = 512
KV_WIDTH = 128
ATTN_BLOCK = 128
ROPE_THETA = 10000.0
N_EXPERTS = 16
N_GROUPS = 4
EXPERTS_PER_GROUP = 4
D_EXPERT = 512
N_PAIRS = 6
N_BUCKETS = N_GROUPS * N_PAIRS

_PAIRS = [(a, b) for a in range(EXPERTS_PER_GROUP) for b in range(a + 1, EXPERTS_PER_GROUP)]
_BUCKET_LO = [(k // N_PAIRS) * EXPERTS_PER_GROUP + _PAIRS[k % N_PAIRS][0] for k in range(N_BUCKETS)]
_BUCKET_HI = [(k // N_PAIRS) * EXPERTS_PER_GROUP + _PAIRS[k % N_PAIRS][1] for k in range(N_BUCKETS)]

LANES = 128
ROW_WIDTH = D_MODEL + LANES
INFO_BUCKET, INFO_W_LO, INFO_W_HI = 0, 1, 2
VMEM_LIMIT_BYTES = 48 * 1024 * 1024

HIGHEST = lax.Precision.HIGHEST


def _cparams(sem):
    return pltpu.CompilerParams(dimension_semantics=sem, vmem_limit_bytes=VMEM_LIMIT_BYTES)


def _rms_mod(x, g, shift, scale):
    y = x * lax.rsqrt(jnp.mean(x * x, axis=-1, keepdims=True) + EPS) * g
    return y * (1.0 + scale) + shift


def _dot(a, b):
    return jnp.dot(a, b, preferred_element_type=F32)


def _dot_nt(a, b):
    return lax.dot_general(a, b, (((1,), (1,)), ((), ())), preferred_element_type=F32)


def _ada_kernel(s_ref, w_ref, b_ref, o_ref):
    s = s_ref[...]
    s = s * jax.nn.sigmoid(s)
    o_ref[0] = jnp.dot(s, w_ref[0], preferred_element_type=F32, precision=HIGHEST) + b_ref[0]


def _ada_call(s_rows, ada_w, ada_b):
    depth, d, n6 = ada_w.shape
    r = s_rows.shape[0]
    tn = 1536
    return pl.pallas_call(
        _ada_kernel,
        grid=(depth, n6 // tn),
        in_specs=[
            pl.BlockSpec((r, d), lambda l, j: (0, 0)),
            pl.BlockSpec((1, d, tn), lambda l, j: (l, 0, j)),
            pl.BlockSpec((1, 1, tn), lambda l, j: (l, 0, j)),
        ],
        out_specs=pl.BlockSpec((1, r, tn), lambda l, j: (l, 0, j)),
        out_shape=jax.ShapeDtypeStruct((depth, r, n6), F32),
        compiler_params=_cparams(("arbitrary", "arbitrary")),
        name="ada_mod",
    )(s_rows, ada_w, ada_b.reshape(depth, 1, n6))


def _rope_group(x, cos, sin_signed):
    lane = lax.broadcasted_iota(jnp.int32, x.shape, 1)
    first_half = (lane % 32) < 16
    partner = jnp.where(first_half, pltpu.roll(x, LANES - 16, 1), pltpu.roll(x, 16, 1))
    return x * cos + partner * sin_signed


def _inproj_kernel(*refs, mode, add_moe):
    it = iter(refs)
    h_ref = next(it)
    if add_moe:
        pos_ref, moe_hbm, gate_ref = next(it), next(it), next(it)
        moe_buf, moe_sem = refs[-2], refs[-1]
        _gather_rows_into(pos_ref, moe_hbm, moe_buf, moe_sem)
    g_ref, sh_ref, sc_ref, w_ref = next(it), next(it), next(it), next(it)
    x = h_ref[0]
    if add_moe:
        x = x + gate_ref[0] * moe_buf[...]
    a = _rms_mod(x, g_ref[...], sh_ref[0], sc_ref[0])
    proj = _dot(a.astype(BF16), w_ref[...])
    if mode == "even":
        cs_ref = next(it)
        uc_ref, us_ref, gb_ref, gu_ref = next(it), next(it), next(it), next(it)
        uf = proj[:, :FOURIER_WIDTH].astype(BF16)
        cs = cs_ref[...].astype(BF16)
        for hh in range(FOURIER_HEADS):
            cols = slice(hh * LANES, (hh + 1) * LANES)
            r = _dot(uf[:, cols], cs)
            uc_ref[0, :, cols] = r[:, :LANES].astype(BF16)
            us_ref[0, :, cols] = r[:, LANES:].astype(BF16)
        c0 = FOURIER_WIDTH
        gb_ref[0] = proj[:, c0:c0 + CONV_WIDTH]
        gu_ref[0] = proj[:, c0 + CONV_WIDTH:c0 + 2 * CONV_WIDTH] * proj[:, c0 + 2 * CONV_WIDTH:]
    elif mode == "odd":
        cq_ref, sq_ref, ck_ref, sk_ref = next(it), next(it), next(it), next(it)
        hn_ref, up_ref, q_ref, kd_ref, vd_ref = next(it), next(it), next(it), next(it), next(it)
        hn_ref[0] = x
        up_ref[0] = proj[:, :POOL_WIDTH]
        c0 = POOL_WIDTH
        for gi in range(ATTN_WIDTH // LANES):
            cols = slice(c0 + gi * LANES, c0 + (gi + 1) * LANES)
            q_ref[0, :, gi * LANES:(gi + 1) * LANES] = _rope_group(
                proj[:, cols], cq_ref[...], sq_ref[...]).astype(BF16)
        c0 += ATTN_WIDTH
        for gi in range(2 * KV_WIDTH // LANES):
            cols = slice(c0 + gi * LANES, c0 + (gi + 1) * LANES)
            kd_ref[0, :, gi * LANES:(gi + 1) * LANES] = _rope_group(
                proj[:, cols], ck_ref[...], sk_ref[...]).astype(BF16)
        c0 += 2 * KV_WIDTH
        vd_ref[0] = proj[:, c0:].astype(BF16)
    else:
        kv_ref = next(it)
        kv_ref[0] = proj.astype(BF16)


def _mod_spec(arr):
    if arr.shape[0] > 1:
        return pl.BlockSpec((1, 1, arr.shape[2]), lambda b, i: (b, 0, 0))
    return pl.BlockSpec((1, 1, arr.shape[2]), lambda b, i: (0, 0, 0))


def _gather_rows_into(pos_ref, src_hbm, buf, sem):
    nrows = buf.shape[0]

    def body(r, carry):
        pltpu.make_async_copy(src_hbm.at[pl.ds(pos_ref[0, 0, r], 1)], buf.at[pl.ds(r, 1)], sem).start()
        return carry
    lax.fori_loop(0, nrows, body, 0, unroll=8)
    pltpu.make_async_copy(src_hbm.at[pl.ds(0, nrows)], buf, sem).wait()


def _pos_tiles(pos, tm, nt, row0):
    tile0 = row0 // tm
    spec = pl.BlockSpec((1, 1, tm), lambda bb, i: (tile0 + bb * nt + i, 0, 0), memory_space=pltpu.SMEM)
    return pos.reshape(pos.shape[0] // tm, 1, tm), spec


def _inproj_call(h, g, shift, scale, w, *, mode, tm, moe=None, pos=None, moe_row0=0, gate=None, extra=()):
    b, n, d = h.shape
    nout = w.shape[1]
    add_moe = moe is not None
    row = lambda bb, i: (bb, i, 0)
    full2 = lambda bb, i: (0, 0)
    args = [h]
    in_specs = [pl.BlockSpec((1, tm, d), row)]
    scratch = []
    if add_moe:
        pos_tiles, pos_spec = _pos_tiles(pos, tm, n // tm, moe_row0)
        args += [pos_tiles, moe, gate]
        in_specs += [pos_spec, pl.BlockSpec(memory_space=pl.ANY), _mod_spec(gate)]
        scratch = [pltpu.VMEM((tm, d), F32), pltpu.SemaphoreType.DMA(())]
    args += [g.reshape(1, d), shift, scale, w]
    in_specs += [pl.BlockSpec((1, d), full2), _mod_spec(shift), _mod_spec(scale),
                 pl.BlockSpec((d, nout), full2)]
    if mode == "even":
        args += list(extra)
        in_specs += [pl.BlockSpec(extra[0].shape, full2)]
        out_shape = [jax.ShapeDtypeStruct((b, n, FOURIER_WIDTH), BF16),
                     jax.ShapeDtypeStruct((b, n, FOURIER_WIDTH), BF16),
                     jax.ShapeDtypeStruct((b, n, CONV_WIDTH), F32),
                     jax.ShapeDtypeStruct((b, n, CONV_WIDTH), F32)]
        out_specs = [pl.BlockSpec((1, tm, 512), row)] * 4
    elif mode == "odd":
        args += list(extra)
        in_specs += [pl.BlockSpec((tm, LANES), lambda bb, i: (i, 0))] * 4
        out_shape = [jax.ShapeDtypeStruct((b, n, d), F32),
                     jax.ShapeDtypeStruct((b, n, POOL_WIDTH), F32),
                     jax.ShapeDtypeStruct((b, n, ATTN_WIDTH), BF16),
                     jax.ShapeDtypeStruct((b, n, 2 * KV_WIDTH), BF16),
                     jax.ShapeDtypeStruct((b, n, 2 * KV_WIDTH), BF16)]
        out_specs = [pl.BlockSpec((1, tm, d), row), pl.BlockSpec((1, tm, POOL_WIDTH), row),
                     pl.BlockSpec((1, tm, ATTN_WIDTH), row),
                     pl.BlockSpec((1, tm, 2 * KV_WIDTH), row), pl.BlockSpec((1, tm, 2 * KV_WIDTH), row)]
    else:
        out_shape = [jax.ShapeDtypeStruct((b, n, nout), BF16)]
        out_specs = [pl.BlockSpec((1, tm, nout), row)]
    return pl.pallas_call(
        functools.partial(_inproj_kernel, mode=mode, add_moe=add_moe),
        grid=(b, n // tm),
        in_specs=in_specs,
        out_specs=out_specs,
        out_shape=out_shape,
        scratch_shapes=scratch,
        compiler_params=_cparams(("arbitrary", "arbitrary")),
        name="inproj_" + mode,
    )(*args)


DFT_RADIX = 64


def _dft_kernel(ca_ref, sa_ref, cb_ref, sb_ref, uc_ref, us_ref, o_ref, c_scr, s_scr, *, norm):
    @pl.when(pl.program_id(1) == 0)
    def _():
        cb, sb = cb_ref[...], sb_ref[...]
        for r in range(ca_ref.shape[0]):
            ca, sa = ca_ref[r:r + 1, :], sa_ref[r:r + 1, :]
            rows = slice(r * DFT_RADIX, (r + 1) * DFT_RADIX)
            c_scr[rows, :] = (ca * cb - sa * sb).astype(BF16)
            s_scr[rows, :] = (-(sa * cb + ca * sb)).astype(BF16)

    acc = _dot(c_scr[...], uc_ref[0]) + _dot(s_scr[...], us_ref[0])
    o_ref[0] = (acc * norm).astype(BF16)


def _dft_call(tabs, uc, us, *, tm):
    b, n, wdt = uc.shape
    r_tile = tm // DFT_RADIX
    norm = 1.0 / math.sqrt(n * FOURIER_HEAD_DIM)
    a_spec = pl.BlockSpec((r_tile, n), lambda i, bb: (i, 0))
    b_spec = pl.BlockSpec((DFT_RADIX, n), lambda i, bb: (0, 0))
    u_spec = pl.BlockSpec((1, n, wdt), lambda i, bb: (bb, 0, 0))
    return pl.pallas_call(
        functools.partial(_dft_kernel, norm=norm),
        grid=(n // tm, b),
        in_specs=[a_spec, a_spec, b_spec, b_spec, u_spec, u_spec],
        out_specs=pl.BlockSpec((1, tm, wdt), lambda i, bb: (bb, i, 0)),
        out_shape=jax.ShapeDtypeStruct((b, n, wdt), BF16),
        scratch_shapes=[pltpu.VMEM((tm, n), BF16), pltpu.VMEM((tm, n), BF16)],
        compiler_params=_cparams(("arbitrary", "arbitrary")),
        name="dft_rows",
    )(*tabs, uc, us)


def _tail(y, h_ref, gate_ref, g2_ref, sh_ref, sc_ref, rwt_ref, rb_ref, hout_ref, fx_ref, bk_ref):
    hn = h_ref[0] + gate_ref[0] * y
    hout_ref[0] = hn
    f = _rms_mod(hn, g2_ref[...], sh_ref[0], sc_ref[0])
    tm = f.shape[0]
    f_hi = f.astype(BF16)
    f_lo = (f - f_hi.astype(F32)).astype(BF16)
    both = _dot_nt(rwt_ref[...], f_hi)
    logits = (both[:N_EXPERTS] + both[N_EXPERTS:]) + _dot_nt(rwt_ref[:N_EXPERTS, :], f_lo)
    aff = jax.nn.sigmoid(logits)
    sel = aff + rb_ref[...]
    cands = []
    for bkt in range(N_BUCKETS):
        lo, hi = _BUCKET_LO[bkt], _BUCKET_HI[bkt]
        cands.append((sel[lo:lo + 1, :] + sel[hi:hi + 1, :], jnp.full((1, tm), float(bkt), F32),
                      aff[lo:lo + 1, :], aff[hi:hi + 1, :]))
    while len(cands) > 1:
        merged = []
        for k in range(0, len(cands) - 1, 2):
            left, right = cands[k], cands[k + 1]
            take_right = right[0] > left[0]
            merged.append(tuple(jnp.where(take_right, r, l) for l, r in zip(left, right)))
        if len(cands) % 2:
            merged.append(cands[-1])
        cands = merged
    _, bucket, a_lo, a_hi = cands[0]
    den = a_lo + a_hi
    info = jnp.concatenate([bucket, a_lo / den, a_hi / den, jnp.zeros((LANES - 3, tm), F32)], axis=0)
    fx_ref[0, :, :D_MODEL] = f
    fx_ref[0, :, D_MODEL:] = info.T
    bk_ref[...] = jnp.concatenate([bucket, jnp.zeros((7, tm), F32)], axis=0)


def _tail_specs(h, gate, shift, scale, tm):
    b, n, d = h.shape
    nt = n // tm
    row = lambda bb, i: (bb, i, 0)
    in_specs = [pl.BlockSpec((1, tm, d), row), _mod_spec(gate),
                pl.BlockSpec((1, d), lambda bb, i: (0, 0)), _mod_spec(shift), _mod_spec(scale),
                pl.BlockSpec((2 * N_EXPERTS, d), lambda bb, i: (0, 0)),
                pl.BlockSpec((N_EXPERTS, 1), lambda bb, i: (0, 0))]
    out_shape = [jax.ShapeDtypeStruct((b, n, d), F32), jax.ShapeDtypeStruct((b, n, ROW_WIDTH), F32),
                 jax.ShapeDtypeStruct((8, b * n), F32)]
    out_specs = [pl.BlockSpec((1, tm, d), row), pl.BlockSpec((1, tm, ROW_WIDTH), row),
                 pl.BlockSpec((8, tm), lambda bb, i: (0, bb * nt + i))]
    return in_specs, out_shape, out_specs


def _even_out_kernel(yf_ref, gb_ref, gu_ref, gp_ref, gn_ref, cw_ref, wo_ref,
                     h_ref, gate_ref, g2_ref, sh_ref, sc_ref, rwt_ref, rb_ref,
                     hout_ref, fx_ref, bk_ref):
    i = pl.program_id(1)
    last = pl.num_programs(1) - 1
    gu = gu_ref[0]
    tm = gu.shape[0]
    prev = jnp.where(i > 0, gp_ref[0, 7:8, :], 0.0)
    nxt = jnp.where(i < last, gn_ref[0, 0:1, :], 0.0)
    row = lax.broadcasted_iota(jnp.int32, gu.shape, 0)
    up = jnp.where(row == 0, prev, pltpu.roll(gu, 1, 0))
    dn = jnp.where(row == tm - 1, nxt, pltpu.roll(gu, tm - 1, 0))
    conv = up * cw_ref[0:1, :] + gu * cw_ref[1:2, :] + dn * cw_ref[2:3, :]
    yc = (gb_ref[0] * conv).astype(BF16)
    y = _dot(yf_ref[0], wo_ref[:FOURIER_WIDTH, :]) + _dot(yc, wo_ref[FOURIER_WIDTH:, :])
    _tail(y, h_ref, gate_ref, g2_ref, sh_ref, sc_ref, rwt_ref, rb_ref, hout_ref, fx_ref, bk_ref)


def _even_out_call(yf, gb, gu, conv_w, w_out, h, gate, g2, shift, scale, rwt, rb, *, tm):
    b, n, d = h.shape
    row = lambda bb, i: (bb, i, 0)
    nb8 = n // 8
    t8 = tm // 8
    tin, out_shape, out_specs = _tail_specs(h, gate, shift, scale, tm)
    in_specs = [pl.BlockSpec((1, tm, 512), row), pl.BlockSpec((1, tm, 512), row),
                pl.BlockSpec((1, tm, 512), row),
                pl.BlockSpec((1, 8, 512), lambda bb, i: (bb, jnp.maximum(i * t8 - 1, 0), 0)),
                pl.BlockSpec((1, 8, 512), lambda bb, i: (bb, jnp.minimum((i + 1) * t8, nb8 - 1), 0)),
                pl.BlockSpec((CONV_K, CONV_WIDTH), lambda bb, i: (0, 0)),
                pl.BlockSpec((d, d), lambda bb, i: (0, 0))] + tin
    return pl.pallas_call(
        _even_out_kernel,
        grid=(b, n // tm),
        in_specs=in_specs,
        out_specs=out_specs,
        out_shape=out_shape,
        compiler_params=_cparams(("arbitrary", "arbitrary")),
        name="even_out",
    )(yf, gb, gu, gu, gu, conv_w, w_out, h, gate, g2.reshape(1, d), shift, scale, rwt, rb)


def _odd_out_kernel(sink_ref, up_ref, upp_ref, upn_ref, q_ref, kc_ref, kp_ref, kn_ref,
                    vc_ref, vp_ref, vn_ref, kvx_ref, pw_ref, ps_ref, wo_ref,
                    h_ref, gate_ref, g2_ref, sh_ref, sc_ref, rwt_ref, rb_ref,
                    hout_ref, fx_ref, bk_ref, ext_ref, mix_ref, *, n_total):
    i = pl.program_id(1)
    last = pl.num_programs(1) - 1
    tq = q_ref.shape[1]
    nsub = tq // ATTN_BLOCK

    u = up_ref[0]
    ext_ref[0:POOL_HALO, :] = jnp.where(i > 0, upp_ref[0], 0.0)
    ext_ref[POOL_HALO:POOL_HALO + tq, :] = u
    ext_ref[POOL_HALO + tq:, :] = jnp.where(i < last, upn_ref[0], 0.0)
    t = i * tq + lax.broadcasted_iota(jnp.int32, (tq, LANES), 0)
    for gi, win in enumerate(POOL_WINDOWS):
        r = win // 2
        cols = slice(gi * LANES, (gi + 1) * LANES)
        acc = ext_ref[POOL_HALO - r:POOL_HALO - r + tq, cols]
        for dlt in range(-r + 1, r + 1):
            acc = acc + ext_ref[POOL_HALO + dlt:POOL_HALO + dlt + tq, cols]
        cnt = (jnp.minimum(t + r + 1, n_total) - jnp.maximum(t - r, 0)).astype(F32)
        p = acc / cnt - u[:, cols]
        y = _dot(p.astype(BF16), pw_ref[gi]) * ps_ref[:, cols]
        mix_ref[:, cols] = y.astype(BF16)

    kwin = jnp.concatenate([kp_ref[0], kc_ref[0], kn_ref[0]], axis=0)
    vwin = jnp.concatenate([vp_ref[0], vc_ref[0], vn_ref[0]], axis=0)
    kvx = kvx_ref[0]
    kx, vx = kvx[:, :2 * KV_WIDTH], kvx[:, 2 * KV_WIDTH:]
    low = (lax.broadcasted_iota(jnp.int32, (1, 2 * KV_WIDTH), 1) % LANES) < HEAD_DIM
    zero = jnp.zeros((), BF16)
    k_half = (jnp.where(low, kwin, zero), jnp.where(low, zero, kwin))
    v_half = (jnp.where(low, vwin, zero), jnp.where(low, zero, vwin))
    kx_half = (jnp.where(low, kx, zero), jnp.where(low, zero, kx))
    vx_half = (jnp.where(low, vx, zero), jnp.where(low, zero, vx))

    span = 3 * ATTN_BLOCK
    rows2 = 2 * ATTN_BLOCK
    qi = lax.broadcasted_iota(jnp.int32, (rows2, span), 0) % ATTN_BLOCK
    kj = lax.broadcasted_iota(jnp.int32, (rows2, span), 1)
    in_prev = kj < ATTN_BLOCK
    in_next = kj >= 2 * ATTN_BLOCK
    neg_inf = jnp.float32(-jnp.inf)
    band_bias = (jnp.where(in_prev & (kj < qi), neg_inf, 0.0)
                 + jnp.where(in_next & (kj - 2 * ATTN_BLOCK > qi), neg_inf, 0.0))
    top_rows = lax.broadcasted_iota(jnp.int32, (rows2, 1), 0) < ATTN_BLOCK

    for j in range(nsub):
        blk = i * nsub + j
        prev_bias = jnp.where(blk > 0, 0.0, neg_inf)
        next_bias = jnp.where(blk < (n_total // ATTN_BLOCK) - 1, 0.0, neg_inf)
        bias = band_bias + jnp.where(in_prev, prev_bias, 0.0) + jnp.where(in_next, next_bias, 0.0)
        r0 = j * ATTN_BLOCK
        for kh in range(N_KV_HEADS):
            kcols = slice(kh * LANES, (kh + 1) * LANES)
            xq = jnp.concatenate(
                [q_ref[0, r0:r0 + ATTN_BLOCK, (2 * kh + pr) * LANES:(2 * kh + pr + 1) * LANES]
                 for pr in range(2)], axis=0)
            o_pair = None
            for half in range(2):
                ks = k_half[half][r0:r0 + span, kcols]
                vs = v_half[half][r0:r0 + span, kcols]
                s1 = _dot_nt(xq, ks) + bias
                s2 = _dot_nt(xq, kx_half[half][:, kcols])
                head0 = kh * 4 + half
                snk = jnp.where(top_rows, sink_ref[head0], sink_ref[head0 + 2])
                m = jnp.maximum(jnp.maximum(jnp.max(s1, axis=-1, keepdims=True),
                                            jnp.max(s2, axis=-1, keepdims=True)), snk)
                e1 = jnp.exp(s1 - m)
                e2 = jnp.exp(s2 - m)
                den = (jnp.sum(e1, axis=-1, keepdims=True) + jnp.sum(e2, axis=-1, keepdims=True)
                       + jnp.exp(snk - m))
                o = _dot(e1.astype(BF16), vs) + _dot(e2.astype(BF16), vx_half[half][:, kcols])
                o = o / den
                o_pair = o if o_pair is None else o_pair + o
            for pr in range(2):
                c0 = POOL_WIDTH + (2 * kh + pr) * LANES
                mix_ref[r0:r0 + ATTN_BLOCK, c0:c0 + LANES] = (
                    o_pair[pr * ATTN_BLOCK:(pr + 1) * ATTN_BLOCK].astype(BF16))

    y = _dot(mix_ref[...], wo_ref[...])
    _tail(y, h_ref, gate_ref, g2_ref, sh_ref, sc_ref, rwt_ref, rb_ref, hout_ref, fx_ref, bk_ref)


def _odd_out_call(sink, up, q, kd, vd, kvx, pool_w, pool_scale, w_out,
                  h, gate, g2, shift, scale, rwt, rb, *, tq):
    b, n, d = h.shape
    row = lambda bb, i: (bb, i, 0)
    nb8, t8 = n // POOL_HALO, tq // POOL_HALO
    nbk, tk = n // ATTN_BLOCK, tq // ATTN_BLOCK
    prev8 = lambda bb, i: (bb, jnp.maximum(i * t8 - 1, 0), 0)
    next8 = lambda bb, i: (bb, jnp.minimum((i + 1) * t8, nb8 - 1), 0)
    prevk = lambda bb, i: (bb, jnp.maximum(i * tk - 1, 0), 0)
    nextk = lambda bb, i: (bb, jnp.minimum((i + 1) * tk, nbk - 1), 0)
    kvw = 2 * KV_WIDTH
    tin, out_shape, out_specs = _tail_specs(h, gate, shift, scale, tq)
    in_specs = [pl.BlockSpec(memory_space=pltpu.SMEM),
                pl.BlockSpec((1, tq, POOL_WIDTH), row),
                pl.BlockSpec((1, POOL_HALO, POOL_WIDTH), prev8),
                pl.BlockSpec((1, POOL_HALO, POOL_WIDTH), next8),
                pl.BlockSpec((1, tq, ATTN_WIDTH), row),
                pl.BlockSpec((1, tq, kvw), row),
                pl.BlockSpec((1, ATTN_BLOCK, kvw), prevk),
                pl.BlockSpec((1, ATTN_BLOCK, kvw), nextk),
                pl.BlockSpec((1, tq, kvw), row),
                pl.BlockSpec((1, ATTN_BLOCK, kvw), prevk),
                pl.BlockSpec((1, ATTN_BLOCK, kvw), nextk),
                pl.BlockSpec((1, kvx.shape[1], 2 * kvw), lambda bb, i: (bb, 0, 0)),
                pl.BlockSpec(pool_w.shape, lambda bb, i: (0, 0, 0)),
                pl.BlockSpec((1, POOL_WIDTH), lambda bb, i: (0, 0)),
                pl.BlockSpec((d, d), lambda bb, i: (0, 0))] + tin
    return pl.pallas_call(
        functools.partial(_odd_out_kernel, n_total=n),
        grid=(b, n // tq),
        in_specs=in_specs,
        out_specs=out_specs,
        out_shape=out_shape,
        scratch_shapes=[pltpu.VMEM((tq + 2 * POOL_HALO, POOL_WIDTH), F32),
                        pltpu.VMEM((tq, d), BF16)],
        compiler_params=_cparams(("arbitrary", "arbitrary")),
        name="odd_out",
    )(sink, up, up, up, q, kd, kd, kd, vd, vd, vd, kvx, pool_w, pool_scale.reshape(1, POOL_WIDTH),
      w_out, h, gate, g2.reshape(1, d), shift, scale, rwt, rb)


def _moe_kernel(e_lo_ref, e_hi_ref, valid_ref, x_ref,
                g1_ref, u1_ref, d1_ref, g2_ref, u2_ref, d2_ref, o_ref):
    j = pl.program_id(0)

    @pl.when(valid_ref[j] != 0)
    def _():
        x = x_ref[:, :D_MODEL].astype(BF16)
        w_lo = x_ref[:, D_MODEL + INFO_W_LO:D_MODEL + INFO_W_LO + 1]
        w_hi = x_ref[:, D_MODEL + INFO_W_HI:D_MODEL + INFO_W_HI + 1]

        def expert(g_ref, u_ref, d_ref):
            gate = _dot(x, g_ref[0, 0].astype(BF16))
            hid = gate * jax.nn.sigmoid(gate) * _dot(x, u_ref[0, 0].astype(BF16))
            return _dot(hid.astype(BF16), d_ref[0, 0].astype(BF16))

        o_lo = expert(g1_ref, u1_ref, d1_ref)
        o_hi = expert(g2_ref, u2_ref, d2_ref)
        o_ref[...] = w_lo * o_lo + w_hi * o_hi

    @pl.when(valid_ref[j] == 0)
    def _():
        o_ref[...] = jnp.zeros(o_ref.shape, o_ref.dtype)


def _moe_call(tile_lo, tile_hi, tile_valid, xs, wg, wu, wd, *, layer, tm):
    p = xs.shape[0]
    d = D_MODEL
    ntiles = p // tm
    lo4 = lambda j, lo, hi, v: (layer, lo[j], 0, 0)
    hi4 = lambda j, lo, hi, v: (layer, hi[j], 0, 0)
    rowm = lambda j, lo, hi, v: (j, 0)
    row_in = lambda j, lo, hi, v: (jnp.where(v[j] != 0, j, 0), 0)
    grid_spec = pltpu.PrefetchScalarGridSpec(
        num_scalar_prefetch=3,
        grid=(ntiles,),
        in_specs=[pl.BlockSpec((tm, ROW_WIDTH), row_in),
                  pl.BlockSpec((1, 1, d, D_EXPERT), lo4), pl.BlockSpec((1, 1, d, D_EXPERT), lo4),
                  pl.BlockSpec((1, 1, D_EXPERT, d), lo4),
                  pl.BlockSpec((1, 1, d, D_EXPERT), hi4), pl.BlockSpec((1, 1, d, D_EXPERT), hi4),
                  pl.BlockSpec((1, 1, D_EXPERT, d), hi4)],
        out_specs=pl.BlockSpec((tm, d), rowm),
    )
    return pl.pallas_call(
        _moe_kernel,
        grid_spec=grid_spec,
        out_shape=jax.ShapeDtypeStruct((p, d), F32),
        compiler_params=_cparams(("arbitrary",)),
        name="moe_pairs",
    )(tile_lo, tile_hi, tile_valid, xs, wg, wu, wd, wg, wu, wd)


RANK_ROWS = 32


def _rank_kernel(bk_ref, tri_ref, rk_ref, cnt_ref, carry_ref):
    @pl.when(pl.program_id(0) == 0)
    def _():
        carry_ref[...] = jnp.zeros(carry_ref.shape, F32)

    tr = bk_ref.shape[1]
    bucket = bk_ref[0:1, :]
    rows = lax.broadcasted_iota(jnp.int32, (RANK_ROWS, tr), 0).astype(F32)
    onehot = jnp.where(rows == bucket, 1.0, 0.0)
    before = _dot(onehot.astype(BF16), tri_ref[...]) + carry_ref[:, 0:1]
    rank = jnp.sum(onehot * before, axis=0, keepdims=True)
    rk_ref[...] = jnp.broadcast_to(rank, rk_ref.shape)
    carry_ref[...] = carry_ref[...] + jnp.sum(onehot, axis=1, keepdims=True)
    cnt_ref[...] = carry_ref[...]


def _rank_call(bk, *, tr):
    t = bk.shape[1]
    tri = jnp.asarray(np.triu(np.ones((tr, tr), np.float32), 1), BF16)
    return pl.pallas_call(
        _rank_kernel,
        grid=(t // tr,),
        in_specs=[pl.BlockSpec((8, tr), lambda j: (0, j)), pl.BlockSpec((tr, tr), lambda j: (0, 0))],
        out_specs=[pl.BlockSpec((8, tr), lambda j: (0, j)), pl.BlockSpec((RANK_ROWS, LANES), lambda j: (0, 0))],
        out_shape=[jax.ShapeDtypeStruct((8, t), F32), jax.ShapeDtypeStruct((RANK_ROWS, LANES), F32)],
        scratch_shapes=[pltpu.VMEM((RANK_ROWS, LANES), F32)],
        compiler_params=_cparams(("arbitrary",)),
        name="bucket_rank",
    )(bk, tri)


ROW_TILE = 512


def _scatter_rows_kernel(tile_end_ref, pos_ref, *refs, tile_starts, tm):
    nsrc = len(tile_starts)
    srcs, out_hbm, zero_ref, sem = refs[:nsrc], refs[nsrc], refs[nsrc + 1], refs[nsrc + 2]
    j = pl.program_id(0)

    @pl.when(j == 0)
    def _():
        zero_ref[...] = jnp.zeros(zero_ref.shape, F32)

        def fill(tile, start):
            copy = pltpu.make_async_copy(zero_ref, out_hbm.at[pl.ds(tile * tm, tm)], sem)
            copy.start() if start else copy.wait()

        n_slot_tiles = out_hbm.shape[0] // tm
        used = tile_end_ref[N_BUCKETS - 1]
        for start in (True, False):
            for b in range(N_BUCKETS):
                first_tile = tile_end_ref[b - 1] if b else 0
                pl.when(tile_end_ref[b] > first_tile)(functools.partial(fill, tile_end_ref[b] - 1, start))
                pl.when(n_slot_tiles - 1 - b >= used)(functools.partial(fill, n_slot_tiles - 1 - b, start))

    def move(src_vmem):
        def body(r, carry):
            pltpu.make_async_copy(src_vmem.at[pl.ds(r, 1)],
                                  out_hbm.at[pl.ds(pos_ref[0, 0, r], 1)], sem).start()
            return carry
        lax.fori_loop(0, ROW_TILE, body, 0, unroll=8)
        pltpu.make_async_copy(src_vmem, out_hbm.at[pl.ds(0, ROW_TILE)], sem).wait()

    for s in range(nsrc):
        first = tile_starts[s]
        if nsrc == 1:
            move(srcs[s])
        else:
            in_range = (j >= first) if s == nsrc - 1 else ((j >= first) & (j < tile_starts[s + 1]))
            pl.when(in_range)(functools.partial(move, srcs[s]))


def _scatter_rows_call(pos, tile_end, sources, nslots, tm):
    w = sources[0].shape[1]
    t = pos.shape[0]
    ntiles = t // ROW_TILE
    tile_starts, src_specs, acc = [], [], 0
    for s in sources:
        first, count = acc, s.shape[0] // ROW_TILE
        tile_starts.append(first)
        src_specs.append(pl.BlockSpec(
            (ROW_TILE, w), lambda j, te, first=first, count=count: (jnp.clip(j - first, 0, count - 1), 0)))
        acc += count
    grid_spec = pltpu.PrefetchScalarGridSpec(
        num_scalar_prefetch=1,
        grid=(ntiles,),
        in_specs=[pl.BlockSpec((1, 1, ROW_TILE), lambda j, te: (j, 0, 0), memory_space=pltpu.SMEM)] + src_specs,
        out_specs=pl.BlockSpec(memory_space=pl.ANY),
        scratch_shapes=[pltpu.VMEM((tm, w), F32), pltpu.SemaphoreType.DMA(())],
    )
    return pl.pallas_call(
        functools.partial(_scatter_rows_kernel, tile_starts=tuple(tile_starts), tm=tm),
        grid_spec=grid_spec,
        out_shape=jax.ShapeDtypeStruct((nslots, w), F32),
        compiler_params=_cparams(("arbitrary",)),
        name="scatter_rows",
    )(tile_end, pos.reshape(ntiles, 1, ROW_TILE), *sources)


def _final_kernel(h_ref, pos_ref, moe_hbm, gate_ref, g_ref, o_ref, moe_buf, moe_sem):
    _gather_rows_into(pos_ref, moe_hbm, moe_buf, moe_sem)
    x = h_ref[0] + gate_ref[0] * moe_buf[...]
    o_ref[0] = x * lax.rsqrt(jnp.mean(x * x, axis=-1, keepdims=True) + EPS) * g_ref[...]


def _final_call(h, moe, pos, gate, g, *, tm):
    b, n, d = h.shape
    row = lambda bb, i: (bb, i, 0)
    pos_tiles, pos_spec = _pos_tiles(pos, tm, n // tm, 0)
    return pl.pallas_call(
        _final_kernel,
        grid=(b, n // tm),
        in_specs=[pl.BlockSpec((1, tm, d), row), pos_spec, pl.BlockSpec(memory_space=pl.ANY), _mod_spec(gate),
                  pl.BlockSpec((1, d), lambda bb, i: (0, 0))],
        out_specs=pl.BlockSpec((1, tm, d), row),
        out_shape=jax.ShapeDtypeStruct((b, n, d), F32),
        scratch_shapes=[pltpu.VMEM((tm, d), F32), pltpu.SemaphoreType.DMA(())],
        compiler_params=_cparams(("arbitrary", "arbitrary")),
        name="final_norm",
    )(h, pos_tiles, moe, gate, g.reshape(1, d))


def _channel_dft_table():
    c = np.arange(FOURIER_HEAD_DIM)
    ang = 2.0 * np.pi * ((c[:, None] * c[None, :]) % FOURIER_HEAD_DIM) / FOURIER_HEAD_DIM
    return jnp.asarray(np.concatenate([np.cos(ang), np.sin(ang)], axis=1), F32)


def _position_dft_tables(n):
    n1 = n // DFT_RADIX
    t = np.arange(n)
    a = 2.0 * np.pi * ((np.arange(n1)[:, None] * t[None, :]) % n1) / n1
    bb = 2.0 * np.pi * ((np.arange(DFT_RADIX)[:, None] * t[None, :]) % n) / n
    return tuple(jnp.asarray(v, F32) for v in (np.cos(a), np.sin(a), np.cos(bb), np.sin(bb)))


def _rope_tables(n):
    quarter = HEAD_DIM // 4
    inv = ROPE_THETA ** (-jnp.arange(quarter, dtype=F32) / quarter)
    t = jnp.arange(n)
    ang_r = (t // GRID_W).astype(F32)[:, None] * inv
    ang_c = (t % GRID_W).astype(F32)[:, None] * inv
    cos = jnp.concatenate([jnp.cos(ang_r)] * 2 + [jnp.cos(ang_c)] * 2, axis=1)
    sin = jnp.concatenate([-jnp.sin(ang_r), jnp.sin(ang_r), -jnp.sin(ang_c), jnp.sin(ang_c)], axis=1)
    return jnp.tile(cos, (1, 2)), jnp.tile(sin, (1, 2))


def _dispatch_plan(bucket, rank, counts, tm):
    t = bucket.shape[0]
    ntiles = t // tm + N_BUCKETS
    tiles_per = (counts + tm - 1) // tm
    tile_end = jnp.cumsum(tiles_per)
    tile_start = tile_end - tiles_per
    onehot = bucket[:, None] == jnp.arange(N_BUCKETS, dtype=jnp.int32)[None, :]
    pos = jnp.sum(jnp.where(onehot, (tile_start * tm)[None, :], 0), axis=-1) + rank
    tile_ids = jnp.arange(ntiles, dtype=jnp.int32)
    used = tile_end[-1]
    tile_bucket = jnp.sum((tile_ids[:, None] >= tile_end[None, :]).astype(jnp.int32), axis=1)
    last_bucket = jnp.sum((jnp.maximum(used - 1, 0) >= tile_end).astype(jnp.int32))
    tile_valid = (tile_ids < used).astype(jnp.int32)
    tile_bucket = jnp.where(tile_valid == 1, tile_bucket, last_bucket)
    tile_lo = jnp.asarray(np.asarray(_BUCKET_LO, np.int32))[tile_bucket]
    tile_hi = jnp.asarray(np.asarray(_BUCKET_HI, np.int32))[tile_bucket]
    return pos.astype(jnp.int32), tile_end.astype(jnp.int32), tile_lo, tile_hi, tile_valid, ntiles * tm


def _moe_layer(fx_list, bk_list, wg, wu, wd, *, layer, tm):
    bk = bk_list[0] if len(bk_list) == 1 else jnp.concatenate(bk_list, axis=1)
    rk, cnt = _rank_call(bk, tr=ROW_TILE)
    pos, tile_end, tile_lo, tile_hi, tile_valid, nslots = _dispatch_plan(
        bk[0].astype(jnp.int32), rk[0].astype(jnp.int32), cnt[:N_BUCKETS, 0].astype(jnp.int32), tm)
    xs = _scatter_rows_call(pos, tile_end, fx_list, nslots, tm)
    return _moe_call(tile_lo, tile_hi, tile_valid, xs, wg, wu, wd, layer=layer, tm=tm), pos


def _forward(x, c, ctx, c_ctx, ada_w, ada_b, norm_mix_g, norm_ffn_g, even_w_in, even_conv_w, even_w_out,
             odd_w_in, odd_pool_w, odd_pool_scale, odd_sink, odd_w_out, router_w, router_b,
             moe_w_gate, moe_w_up, moe_w_down, final_g, *, tm_lat, tm_ctx, tq, tm_dft, tm_moe):
    b, n, d = x.shape
    l = ctx.shape[1]

    rows = ((b + 1 + 7) // 8) * 8
    s_rows = jnp.zeros((rows, d), F32).at[:b].set(c).at[b].set(c_ctx)
    mods = _ada_call(s_rows, ada_w, ada_b)

    def mod_vecs(layer):
        m = mods[layer, :b].reshape(b, N_MOD, 1, d)
        mc = mods[layer, b].reshape(N_MOD, 1, 1, d)
        return [m[:, k] for k in range(N_MOD)], [mc[k] for k in range(N_MOD)]

    rw_t = router_w.T
    rw_hi = rw_t.astype(BF16)
    rw_lo = (rw_t - rw_hi.astype(F32)).astype(BF16)
    rwt = jnp.concatenate([rw_hi, rw_lo], axis=0)
    rb = router_b.astype(F32).reshape(N_EXPERTS, 1)
    cs_tab = _channel_dft_table()

    m, mc = mod_vecs(0)
    w_in0 = even_w_in[0].astype(BF16)
    w_out0 = even_w_out[0].astype(BF16)

    def even_stream(h, mv, tm):
        nn = h.shape[1]
        uc, us, gb, gu = _inproj_call(h, norm_mix_g[0], mv[0], mv[1], w_in0, mode="even", tm=tm,
                                      extra=(cs_tab,))
        yf = _dft_call(_position_dft_tables(nn), uc, us, tm=min(tm_dft, nn))
        return _even_out_call(yf, gb, gu, even_conv_w[0], w_out0, h, mv[2], norm_ffn_g[0],
                              mv[3], mv[4], rwt, rb, tm=tm)

    h1, fx_lat, bk_lat = even_stream(x, m, tm_lat)
    hc1, fx_ctx, bk_ctx = even_stream(ctx, mc, tm_ctx)

    moe0, pos0 = _moe_layer([fx_lat.reshape(b * n, ROW_WIDTH), fx_ctx.reshape(b * l, ROW_WIDTH)],
                            [bk_lat, bk_ctx], moe_w_gate, moe_w_up, moe_w_down, layer=0, tm=tm_moe)
    gate_lat0, gate_ctx0 = m[5], mc[5]

    m, mc = mod_vecs(1)
    w_in1 = odd_w_in[0]
    kv0 = POOL_WIDTH + ATTN_WIDTH
    wk, wv = w_in1[:, kv0:kv0 + KV_WIDTH], w_in1[:, kv0 + KV_WIDTH:]

    def dup_heads(wm):
        return jnp.concatenate([wm[:, :HEAD_DIM], wm[:, :HEAD_DIM], wm[:, HEAD_DIM:], wm[:, HEAD_DIM:]], axis=1)

    w_kv_dup = jnp.concatenate([dup_heads(wk), dup_heads(wv)], axis=1)
    w_lat1 = jnp.concatenate([w_in1[:, :kv0], w_kv_dup], axis=1).astype(BF16)
    w_out1 = odd_w_out[0].astype(BF16)

    cos_t, sin_t = _rope_tables(n)
    q_scale = HEAD_DIM ** -0.5
    h1b, up, q, kd, vd = _inproj_call(h1, norm_mix_g[1], m[0], m[1], w_lat1, mode="odd", tm=tm_lat,
                                      moe=moe0, pos=pos0, moe_row0=0, gate=gate_lat0,
                                      extra=(cos_t * q_scale, sin_t * q_scale, cos_t, sin_t))
    (kvx,) = _inproj_call(hc1, norm_mix_g[1], mc[0], mc[1], w_kv_dup.astype(BF16), mode="plain", tm=tm_ctx,
                          moe=moe0, pos=pos0, moe_row0=b * n, gate=gate_ctx0)
    h2, fx2, bk2 = _odd_out_call(odd_sink[0], up, q, kd, vd, kvx, odd_pool_w[0].astype(BF16),
                                 odd_pool_scale[0], w_out1, h1b, m[2], norm_ffn_g[1], m[3], m[4], rwt, rb, tq=tq)
    moe1, pos1 = _moe_layer([fx2.reshape(b * n, ROW_WIDTH)], [bk2], moe_w_gate, moe_w_up, moe_w_down,
                            layer=1, tm=tm_moe)
    return _final_call(h2, moe1, pos1, m[5], final_g, tm=tm_lat)


def kernel(x, c, ctx, c_ctx, ada_w, ada_b, norm_mix_g, norm_ffn_g, even_w_in, even_conv_w, even_w_out,
           odd_w_in, odd_pool_w, odd_pool_scale, odd_sink, odd_w_out, router_w, router_b,
           moe_w_gate, moe_w_up, moe_w_down, final_g):
    return _forward(x, c, ctx, c_ctx, ada_w, ada_b, norm_mix_g, norm_ffn_g, even_w_in, even_conv_w,
                    even_w_out, odd_w_in, odd_pool_w, odd_pool_scale, odd_sink, odd_w_out, router_w,
                    router_b, moe_w_gate, moe_w_up, moe_w_down, final_g,
                    tm_lat=512, tm_ctx=256, tq=512, tm_dft=1024, tm_moe=512)
```

```python
import functools
import math

import numpy as np
import jax
import jax.numpy as jnp
from jax import lax
from jax.experimental import pallas as pl
from jax.experimental.pallas import tpu as pltpu

F32 = jnp.float32
BF16 = jnp.bfloat16

D_MODEL = 1024
GRID_W = 64
EPS = 1e-6
N_MOD = 6
FOURIER_HEADS = 4
FOURIER_HEAD_DIM = 128
FOURIER_WIDTH = 512
CONV_WIDTH = 512
CONV_K = 3
POOL_WINDOWS = (2, 4, 8, 16)
POOL_GROUP_DIM = 128
POOL_WIDTH = 512
POOL_HALO = 8
HEAD_DIM = 64
N_Q_HEADS = 8
N_KV_HEADS = 2
ATTN_WIDTH = 512
KV_WIDTH = 128
ATTN_BLOCK = 128
ROPE_THETA = 10000.0
N_EXPERTS = 16
N_GROUPS = 4
EXPERTS_PER_GROUP = 4
D_EXPERT = 512
N_PAIRS = 6
N_BUCKETS = N_GROUPS * N_PAIRS

_PAIRS = [(a, b) for a in range(EXPERTS_PER_GROUP) for b in range(a + 1, EXPERTS_PER_GROUP)]
_BUCKET_LO = [(k // N_PAIRS) * EXPERTS_PER_GROUP + _PAIRS[k % N_PAIRS][0] for k in range(N_BUCKETS)]
_BUCKET_HI = [(k // N_PAIRS) * EXPERTS_PER_GROUP + _PAIRS[k % N_PAIRS][1] for k in range(N_BUCKETS)]

LANES = 128
ROW_WIDTH = D_MODEL + LANES
INFO_BUCKET, INFO_W_LO, INFO_W_HI = 0, 1, 2
VMEM_LIMIT_BYTES = 48 * 1024 * 1024

HIGHEST = lax.Precision.HIGHEST


def _cparams(sem):
    return pltpu.CompilerParams(dimension_semantics=sem, vmem_limit_bytes=VMEM_LIMIT_BYTES)


def _rms_mod(x, g, shift, scale):
    y = x * lax.rsqrt(jnp.mean(x * x, axis=-1, keepdims=True) + EPS) * g
    return y * (1.0 + scale) + shift


def _dot(a, b):
    return jnp.dot(a, b, preferred_element_type=F32)


def _dot_nt(a, b):
    return lax.dot_general(a, b, (((1,), (1,)), ((), ())), preferred_element_type=F32)


def _ada_kernel(s_ref, w_ref, b_ref, o_ref):
    s = s_ref[...]
    s = s * jax.nn.sigmoid(s)
    o_ref[0] = jnp.dot(s, w_ref[0], preferred_element_type=F32, precision=HIGHEST) + b_ref[0]


def _ada_call(s_rows, ada_w, ada_b):
    depth, d, n6 = ada_w.shape
    r = s_rows.shape[0]
    tn = 1536
    return pl.pallas_call(
        _ada_kernel,
        grid=(depth, n6 // tn),
        in_specs=[
            pl.BlockSpec((r, d), lambda l, j: (0, 0)),
            pl.BlockSpec((1, d, tn), lambda l, j: (l, 0, j)),
            pl.BlockSpec((1, 1, tn), lambda l, j: (l, 0, j)),
        ],
        out_specs=pl.BlockSpec((1, r, tn), lambda l, j: (l, 0, j)),
        out_shape=jax.ShapeDtypeStruct((depth, r, n6), F32),
        compiler_params=_cparams(("arbitrary", "arbitrary")),
        name="ada_mod",
    )(s_rows, ada_w, ada_b.reshape(depth, 1, n6))


def _rope_group(x, cos, sin_signed):
    lane = lax.broadcasted_iota(jnp.int32, x.shape, 1)
    first_half = (lane % 32) < 16
    partner = jnp.where(first_half, pltpu.roll(x, LANES - 16, 1), pltpu.roll(x, 16, 1))
    return x * cos + partner * sin_signed


def _inproj_kernel(*refs, mode, add_moe):
    it = iter(refs)
    h_ref = next(it)
    if add_moe:
        pos_ref, next_pos_ref, moe_hbm, gate_ref = next(it), next(it), next(it), next(it)
        moe_bufs, moe_sems = refs[-2], refs[-1]
        step = pl.program_id(0) * pl.num_programs(1) + pl.program_id(1)
        nsteps = pl.num_programs(0) * pl.num_programs(1)
        moe_rows = _gathered_rows(step, pos_ref, moe_hbm, moe_bufs, moe_sems)
    g_ref, sh_ref, sc_ref, w_ref = next(it), next(it), next(it), next(it)
    x = h_ref[0]
    if add_moe:
        x = x + gate_ref[0] * moe_rows
    a = _rms_mod(x, g_ref[...], sh_ref[0], sc_ref[0])
    proj = _dot(a.astype(BF16), w_ref[...])
    if mode == "even":
        cs_ref = next(it)
        uc_ref, us_ref, gb_ref, gu_ref = next(it), next(it), next(it), next(it)
        uf = proj[:, :FOURIER_WIDTH].astype(BF16)
        cs = cs_ref[...].astype(BF16)
        for hh in range(FOURIER_HEADS):
            cols = slice(hh * LANES, (hh + 1) * LANES)
            r = _dot(uf[:, cols], cs)
            uc_ref[0, :, cols] = r[:, :LANES].astype(BF16)
            us_ref[0, :, cols] = r[:, LANES:].astype(BF16)
        c0 = FOURIER_WIDTH
        gb_ref[0] = proj[:, c0:c0 + CONV_WIDTH]
        gu_ref[0] = proj[:, c0 + CONV_WIDTH:c0 + 2 * CONV_WIDTH] * proj[:, c0 + 2 * CONV_WIDTH:]
    elif mode == "odd":
        cq_ref, sq_ref, ck_ref, sk_ref = next(it), next(it), next(it), next(it)
        hn_ref, up_ref, q_ref, kd_ref, vd_ref = next(it), next(it), next(it), next(it), next(it)
        hn_ref[0] = x
        up_ref[0] = proj[:, :POOL_WIDTH]
        c0 = POOL_WIDTH
        for gi in range(ATTN_WIDTH // LANES):
            cols = slice(c0 + gi * LANES, c0 + (gi + 1) * LANES)
            q_ref[0, :, gi * LANES:(gi + 1) * LANES] = _rope_group(
                proj[:, cols], cq_ref[...], sq_ref[...]).astype(BF16)
        c0 += ATTN_WIDTH
        for gi in range(2 * KV_WIDTH // LANES):
            cols = slice(c0 + gi * LANES, c0 + (gi + 1) * LANES)
            kd_ref[0, :, gi * LANES:(gi + 1) * LANES] = _rope_group(
                proj[:, cols], ck_ref[...], sk_ref[...]).astype(BF16)
        c0 += 2 * KV_WIDTH
        vd_ref[0] = proj[:, c0:].astype(BF16)
    else:
        kv_ref = next(it)
        kv_ref[0] = proj.astype(BF16)
    if add_moe:
        _request_next_rows(step, nsteps, next_pos_ref, moe_hbm, moe_bufs, moe_sems)


def _mod_spec(arr):
    if arr.shape[0] > 1:
        return pl.BlockSpec((1, 1, arr.shape[2]), lambda b, i: (b, 0, 0))
    return pl.BlockSpec((1, 1, arr.shape[2]), lambda b, i: (0, 0, 0))


def _gathered_rows(step, pos_ref, src_hbm, bufs, sems):
    nrows = bufs.shape[1]
    slot = step % 2

    @pl.when(step == 0)
    def _():
        def body(r, carry):
            pltpu.make_async_copy(src_hbm.at[pl.ds(pos_ref[0, 0, r], 1)], bufs.at[0, pl.ds(r, 1)],
                                  sems.at[0]).start()
            return carry
        lax.fori_loop(0, nrows, body, 0, unroll=8)

    pltpu.make_async_copy(src_hbm.at[pl.ds(0, nrows)], bufs.at[slot], sems.at[slot]).wait()
    return bufs[slot]


def _request_next_rows(step, nsteps, next_pos_ref, src_hbm, bufs, sems):
    nrows = bufs.shape[1]
    slot = 1 - step % 2
    for r in range(nrows):
        pltpu.make_async_copy(src_hbm.at[pl.ds(next_pos_ref[0, 0, r], 1)], bufs.at[slot, pl.ds(r, 1)],
                              sems.at[slot]).start()

    @pl.when(step == nsteps - 1)
    def _():
        pltpu.make_async_copy(src_hbm.at[pl.ds(0, nrows)], bufs.at[slot], sems.at[slot]).wait()


def _pos_tiles(pos, tm, nt, row0):
    tile0 = row0 // tm
    ntiles = pos.shape[0] // tm
    cur = pl.BlockSpec((1, 1, tm), lambda bb, i: (tile0 + bb * nt + i, 0, 0), memory_space=pltpu.SMEM)
    nxt = pl.BlockSpec((1, 1, tm), lambda bb, i: (jnp.minimum(tile0 + bb * nt + i + 1, ntiles - 1), 0, 0),
                       memory_space=pltpu.SMEM)
    return pos.reshape(ntiles, 1, tm), cur, nxt


def _inproj_call(h, g, shift, scale, w, *, mode, tm, moe=None, pos=None, moe_row0=0, gate=None, extra=()):
    b, n, d = h.shape
    nout = w.shape[1]
    add_moe = moe is not None
    row = lambda bb, i: (bb, i, 0)
    full2 = lambda bb, i: (0, 0)
    args = [h]
    in_specs = [pl.BlockSpec((1, tm, d), row)]
    scratch = []
    if add_moe:
        pos_tiles, pos_spec, next_pos_spec = _pos_tiles(pos, tm, n // tm, moe_row0)
        args += [pos_tiles, pos_tiles, moe, gate]
        in_specs += [pos_spec, next_pos_spec, pl.BlockSpec(memory_space=pl.ANY), _mod_spec(gate)]
        scratch = [pltpu.VMEM((2, tm, d), F32), pltpu.SemaphoreType.DMA((2,))]
    args += [g.reshape(1, d), shift, scale, w]
    in_specs += [pl.BlockSpec((1, d), full2), _mod_spec(shift), _mod_spec(scale),
                 pl.BlockSpec((d, nout), full2)]
    if mode == "even":
        args += list(extra)
        in_specs += [pl.BlockSpec(extra[0].shape, full2)]
        out_shape = [jax.ShapeDtypeStruct((b, n, FOURIER_WIDTH), BF16),
                     jax.ShapeDtypeStruct((b, n, FOURIER_WIDTH), BF16),
                     jax.ShapeDtypeStruct((b, n, CONV_WIDTH), F32),
                     jax.ShapeDtypeStruct((b, n, CONV_WIDTH), F32)]
        out_specs = [pl.BlockSpec((1, tm, 512), row)] * 4
    elif mode == "odd":
        args += list(extra)
        in_specs += [pl.BlockSpec((tm, LANES), lambda bb, i: (i, 0))] * 4
        out_shape = [jax.ShapeDtypeStruct((b, n, d), F32),
                     jax.ShapeDtypeStruct((b, n, POOL_WIDTH), F32),
                     jax.ShapeDtypeStruct((b, n, ATTN_WIDTH), BF16),
                     jax.ShapeDtypeStruct((b, n, 2 * KV_WIDTH), BF16),
                     jax.ShapeDtypeStruct((b, n, 2 * KV_WIDTH), BF16)]
        out_specs = [pl.BlockSpec((1, tm, d), row), pl.BlockSpec((1, tm, POOL_WIDTH), row),
                     pl.BlockSpec((1, tm, ATTN_WIDTH), row),
                     pl.BlockSpec((1, tm, 2 * KV_WIDTH), row), pl.BlockSpec((1, tm, 2 * KV_WIDTH), row)]
    else:
        out_shape = [jax.ShapeDtypeStruct((b, n, nout), BF16)]
        out_specs = [pl.BlockSpec((1, tm, nout), row)]
    return pl.pallas_call(
        functools.partial(_inproj_kernel, mode=mode, add_moe=add_moe),
        grid=(b, n // tm),
        in_specs=in_specs,
        out_specs=out_specs,
        out_shape=out_shape,
        scratch_shapes=scratch,
        compiler_params=_cparams(("arbitrary", "arbitrary")),
        name="inproj_" + mode,
    )(*args)


DFT_RADIX = 64


def _dft_kernel(ca_ref, sa_ref, cb_ref, sb_ref, uc_ref, us_ref, o_ref, c_scr, s_scr, *, norm):
    @pl.when(pl.program_id(1) == 0)
    def _():
        cb, sb = cb_ref[...], sb_ref[...]
        for r in range(ca_ref.shape[0]):
            ca, sa = ca_ref[r:r + 1, :], sa_ref[r:r + 1, :]
            rows = slice(r * DFT_RADIX, (r + 1) * DFT_RADIX)
            c_scr[rows, :] = (ca * cb - sa * sb).astype(BF16)
            s_scr[rows, :] = (-(sa * cb + ca * sb)).astype(BF16)

    acc = _dot(c_scr[...], uc_ref[0]) + _dot(s_scr[...], us_ref[0])
    o_ref[0] = (acc * norm).astype(BF16)


def _dft_call(tabs, uc, us, *, tm):
    b, n, wdt = uc.shape
    r_tile = tm // DFT_RADIX
    norm = 1.0 / math.sqrt(n * FOURIER_HEAD_DIM)
    a_spec = pl.BlockSpec((r_tile, n), lambda i, bb: (i, 0))
    b_spec = pl.BlockSpec((DFT_RADIX, n), lambda i, bb: (0, 0))
    u_spec = pl.BlockSpec((1, n, wdt), lambda i, bb: (bb, 0, 0))
    return pl.pallas_call(
        functools.partial(_dft_kernel, norm=norm),
        grid=(n // tm, b),
        in_specs=[a_spec, a_spec, b_spec, b_spec, u_spec, u_spec],
        out_specs=pl.BlockSpec((1, tm, wdt), lambda i, bb: (bb, i, 0)),
        out_shape=jax.ShapeDtypeStruct((b, n, wdt), BF16),
        scratch_shapes=[pltpu.VMEM((tm, n), BF16), pltpu.VMEM((tm, n), BF16)],
        compiler_params=_cparams(("arbitrary", "arbitrary")),
        name="dft_rows",
    )(*tabs, uc, us)


def _tail(y, h_ref, gate_ref, g2_ref, sh_ref, sc_ref, rwt_ref, rb_ref, hout_ref, fx_ref, bk_ref):
    hn = h_ref[0] + gate_ref[0] * y
    hout_ref[0] = hn
    f = _rms_mod(hn, g2_ref[...], sh_ref[0], sc_ref[0])
    tm = f.shape[0]
    f_hi = f.astype(BF16)
    f_lo = (f - f_hi.astype(F32)).astype(BF16)
    both = _dot_nt(rwt_ref[...], f_hi)
    logits = (both[:N_EXPERTS] + both[N_EXPERTS:]) + _dot_nt(rwt_ref[:N_EXPERTS, :], f_lo)
    aff = jax.nn.sigmoid(logits)
    sel = aff + rb_ref[...]
    cands = []
    for bkt in range(N_BUCKETS):
        lo, hi = _BUCKET_LO[bkt], _BUCKET_HI[bkt]
        cands.append((sel[lo:lo + 1, :] + sel[hi:hi + 1, :], jnp.full((1, tm), float(bkt), F32),
                      aff[lo:lo + 1, :], aff[hi:hi + 1, :]))
    while len(cands) > 1:
        merged = []
        for k in range(0, len(cands) - 1, 2):
            left, right = cands[k], cands[k + 1]
            take_right = right[0] > left[0]
            merged.append(tuple(jnp.where(take_right, r, l) for l, r in zip(left, right)))
        if len(cands) % 2:
            merged.append(cands[-1])
        cands = merged
    _, bucket, a_lo, a_hi = cands[0]
    den = a_lo + a_hi
    info = jnp.concatenate([bucket, a_lo / den, a_hi / den, jnp.zeros((LANES - 3, tm), F32)], axis=0)
    fx_ref[0, :, :D_MODEL] = f
    fx_ref[0, :, D_MODEL:] = info.T
    bk_ref[...] = jnp.concatenate([bucket, jnp.zeros((7, tm), F32)], axis=0)


def _tail_specs(h, gate, shift, scale, tm):
    b, n, d = h.shape
    nt = n // tm
    row = lambda bb, i: (bb, i, 0)
    in_specs = [pl.BlockSpec((1, tm, d), row), _mod_spec(gate),
                pl.BlockSpec((1, d), lambda bb, i: (0, 0)), _mod_spec(shift), _mod_spec(scale),
                pl.BlockSpec((2 * N_EXPERTS, d), lambda bb, i: (0, 0)),
                pl.BlockSpec((N_EXPERTS, 1), lambda bb, i: (0, 0))]
    out_shape = [jax.ShapeDtypeStruct((b, n, d), F32), jax.ShapeDtypeStruct((b, n, ROW_WIDTH), F32),
                 jax.ShapeDtypeStruct((8, b * n), F32)]
    out_specs = [pl.BlockSpec((1, tm, d), row), pl.BlockSpec((1, tm, ROW_WIDTH), row),
                 pl.BlockSpec((8, tm), lambda bb, i: (0, bb * nt + i))]
    return in_specs, out_shape, out_specs


def _even_out_kernel(yf_ref, gb_ref, gu_ref, gp_ref, gn_ref, cw_ref, wo_ref,
                     h_ref, gate_ref, g2_ref, sh_ref, sc_ref, rwt_ref, rb_ref,
                     hout_ref, fx_ref, bk_ref):
    i = pl.program_id(1)
    last = pl.num_programs(1) - 1
    gu = gu_ref[0]
    tm = gu.shape[0]
    prev = jnp.where(i > 0, gp_ref[0, 7:8, :], 0.0)
    nxt = jnp.where(i < last, gn_ref[0, 0:1, :], 0.0)
    row = lax.broadcasted_iota(jnp.int32, gu.shape, 0)
    up = jnp.where(row == 0, prev, pltpu.roll(gu, 1, 0))
    dn = jnp.where(row == tm - 1, nxt, pltpu.roll(gu, tm - 1, 0))
    conv = up * cw_ref[0:1, :] + gu * cw_ref[1:2, :] + dn * cw_ref[2:3, :]
    yc = (gb_ref[0] * conv).astype(BF16)
    y = _dot(yf_ref[0], wo_ref[:FOURIER_WIDTH, :]) + _dot(yc, wo_ref[FOURIER_WIDTH:, :])
    _tail(y, h_ref, gate_ref, g2_ref, sh_ref, sc_ref, rwt_ref, rb_ref, hout_ref, fx_ref, bk_ref)


def _even_out_call(yf, gb, gu, conv_w, w_out, h, gate, g2, shift, scale, rwt, rb, *, tm):
    b, n, d = h.shape
    row = lambda bb, i: (bb, i, 0)
    nb8 = n // 8
    t8 = tm // 8
    tin, out_shape, out_specs = _tail_specs(h, gate, shift, scale, tm)
    in_specs = [pl.BlockSpec((1, tm, 512), row), pl.BlockSpec((1, tm, 512), row),
                pl.BlockSpec((1, tm, 512), row),
                pl.BlockSpec((1, 8, 512), lambda bb, i: (bb, jnp.maximum(i * t8 - 1, 0), 0)),
                pl.BlockSpec((1, 8, 512), lambda bb, i: (bb, jnp.minimum((i + 1) * t8, nb8 - 1), 0)),
                pl.BlockSpec((CONV_K, CONV_WIDTH), lambda bb, i: (0, 0)),
                pl.BlockSpec((d, d), lambda bb, i: (0, 0))] + tin
    return pl.pallas_call(
        _even_out_kernel,
        grid=(b, n // tm),
        in_specs=in_specs,
        out_specs=out_specs,
        out_shape=out_shape,
        compiler_params=_cparams(("arbitrary", "arbitrary")),
        name="even_out",
    )(yf, gb, gu, gu, gu, conv_w, w_out, h, gate, g2.reshape(1, d), shift, scale, rwt, rb)


def _odd_out_kernel(sink_ref, up_ref, upp_ref, upn_ref, q_ref, kc_ref, kp_ref, kn_ref,
                    vc_ref, vp_ref, vn_ref, kvx_ref, pw_ref, ps_ref, wo_ref,
                    h_ref, gate_ref, g2_ref, sh_ref, sc_ref, rwt_ref, rb_ref,
                    hout_ref, fx_ref, bk_ref, ext_ref, mix_ref, *, n_total):
    i = pl.program_id(1)
    last = pl.num_programs(1) - 1
    tq = q_ref.shape[1]
    nsub = tq // ATTN_BLOCK

    u = up_ref[0]
    ext_ref[0:POOL_HALO, :] = jnp.where(i > 0, upp_ref[0], 0.0)
    ext_ref[POOL_HALO:POOL_HALO + tq, :] = u
    ext_ref[POOL_HALO + tq:, :] = jnp.where(i < last, upn_ref[0], 0.0)
    t = i * tq + lax.broadcasted_iota(jnp.int32, (tq, LANES), 0)
    for gi, win in enumerate(POOL_WINDOWS):
        r = win // 2
        cols = slice(gi * LANES, (gi + 1) * LANES)
        acc = ext_ref[POOL_HALO - r:POOL_HALO - r + tq, cols]
        for dlt in range(-r + 1, r + 1):
            acc = acc + ext_ref[POOL_HALO + dlt:POOL_HALO + dlt + tq, cols]
        cnt = (jnp.minimum(t + r + 1, n_total) - jnp.maximum(t - r, 0)).astype(F32)
        p = acc / cnt - u[:, cols]
        y = _dot(p.astype(BF16), pw_ref[gi]) * ps_ref[:, cols]
        mix_ref[:, cols] = y.astype(BF16)

    kwin = jnp.concatenate([kp_ref[0], kc_ref[0], kn_ref[0]], axis=0)
    vwin = jnp.concatenate([vp_ref[0], vc_ref[0], vn_ref[0]], axis=0)
    kvx = kvx_ref[0]
    kx, vx = kvx[:, :2 * KV_WIDTH], kvx[:, 2 * KV_WIDTH:]
    low = (lax.broadcasted_iota(jnp.int32, (1, 2 * KV_WIDTH), 1) % LANES) < HEAD_DIM
    zero = jnp.zeros((), BF16)
    k_half = (jnp.where(low, kwin, zero), jnp.where(low, zero, kwin))
    v_half = (jnp.where(low, vwin, zero), jnp.where(low, zero, vwin))
    kx_half = (jnp.where(low, kx, zero), jnp.where(low, zero, kx))
    vx_half = (jnp.where(low, vx, zero), jnp.where(low, zero, vx))

    span = 3 * ATTN_BLOCK
    rows2 = 2 * ATTN_BLOCK
    qi = lax.broadcasted_iota(jnp.int32, (rows2, span), 0) % ATTN_BLOCK
    kj = lax.broadcasted_iota(jnp.int32, (rows2, span), 1)
    in_prev = kj < ATTN_BLOCK
    in_next = kj >= 2 * ATTN_BLOCK
    neg_inf = jnp.float32(-jnp.inf)
    band_bias = (jnp.where(in_prev & (kj < qi), neg_inf, 0.0)
                 + jnp.where(in_next & (kj - 2 * ATTN_BLOCK > qi), neg_inf, 0.0))
    top_rows = lax.broadcasted_iota(jnp.int32, (rows2, 1), 0) < ATTN_BLOCK

    for j in range(nsub):
        blk = i * nsub + j
        prev_bias = jnp.where(blk > 0, 0.0, neg_inf)
        next_bias = jnp.where(blk < (n_total // ATTN_BLOCK) - 1, 0.0, neg_inf)
        bias = band_bias + jnp.where(in_prev, prev_bias, 0.0) + jnp.where(in_next, next_bias, 0.0)
        r0 = j * ATTN_BLOCK
        for kh in range(N_KV_HEADS):
            kcols = slice(kh * LANES, (kh + 1) * LANES)
            xq = jnp.concatenate(
                [q_ref[0, r0:r0 + ATTN_BLOCK, (2 * kh + pr) * LANES:(2 * kh + pr + 1) * LANES]
                 for pr in range(2)], axis=0)
            o_pair = None
            for half in range(2):
                ks = k_half[half][r0:r0 + span, kcols]
                vs = v_half[half][r0:r0 + span, kcols]
                s1 = _dot_nt(xq, ks) + bias
                s2 = _dot_nt(xq, kx_half[half][:, kcols])
                head0 = kh * 4 + half
                snk = jnp.where(top_rows, sink_ref[head0], sink_ref[head0 + 2])
                m = jnp.maximum(jnp.maximum(jnp.max(s1, axis=-1, keepdims=True),
                                            jnp.max(s2, axis=-1, keepdims=True)), snk)
                e1 = jnp.exp(s1 - m)
                e2 = jnp.exp(s2 - m)
                den = (jnp.sum(e1, axis=-1, keepdims=True) + jnp.sum(e2, axis=-1, keepdims=True)
                       + jnp.exp(snk - m))
                o = _dot(e1.astype(BF16), vs) + _dot(e2.astype(BF16), vx_half[half][:, kcols])
                o = o / den
                o_pair = o if o_pair is None else o_pair + o
            for pr in range(2):
                c0 = POOL_WIDTH + (2 * kh + pr) * LANES
                mix_ref[r0:r0 + ATTN_BLOCK, c0:c0 + LANES] = (
                    o_pair[pr * ATTN_BLOCK:(pr + 1) * ATTN_BLOCK].astype(BF16))

    y = _dot(mix_ref[...], wo_ref[...])
    _tail(y, h_ref, gate_ref, g2_ref, sh_ref, sc_ref, rwt_ref, rb_ref, hout_ref, fx_ref, bk_ref)


def _odd_out_call(sink, up, q, kd, vd, kvx, pool_w, pool_scale, w_out,
                  h, gate, g2, shift, scale, rwt, rb, *, tq):
    b, n, d = h.shape
    row = lambda bb, i: (bb, i, 0)
    nb8, t8 = n // POOL_HALO, tq // POOL_HALO
    nbk, tk = n // ATTN_BLOCK, tq // ATTN_BLOCK
    prev8 = lambda bb, i: (bb, jnp.maximum(i * t8 - 1, 0), 0)
    next8 = lambda bb, i: (bb, jnp.minimum((i + 1) * t8, nb8 - 1), 0)
    prevk = lambda bb, i: (bb, jnp.maximum(i * tk - 1, 0), 0)
    nextk = lambda bb, i: (bb, jnp.minimum((i + 1) * tk, nbk - 1), 0)
    kvw = 2 * KV_WIDTH
    tin, out_shape, out_specs = _tail_specs(h, gate, shift, scale, tq)
    in_specs = [pl.BlockSpec(memory_space=pltpu.SMEM),
                pl.BlockSpec((1, tq, POOL_WIDTH), row),
                pl.BlockSpec((1, POOL_HALO, POOL_WIDTH), prev8),
                pl.BlockSpec((1, POOL_HALO, POOL_WIDTH), next8),
                pl.BlockSpec((1, tq, ATTN_WIDTH), row),
                pl.BlockSpec((1, tq, kvw), row),
                pl.BlockSpec((1, ATTN_BLOCK, kvw), prevk),
                pl.BlockSpec((1, ATTN_BLOCK, kvw), nextk),
                pl.BlockSpec((1, tq, kvw), row),
                pl.BlockSpec((1, ATTN_BLOCK, kvw), prevk),
                pl.BlockSpec((1, ATTN_BLOCK, kvw), nextk),
                pl.BlockSpec((1, kvx.shape[1], 2 * kvw), lambda bb, i: (bb, 0, 0)),
                pl.BlockSpec(pool_w.shape, lambda bb, i: (0, 0, 0)),
                pl.BlockSpec((1, POOL_WIDTH), lambda bb, i: (0, 0)),
                pl.BlockSpec((d, d), lambda bb, i: (0, 0))] + tin
    return pl.pallas_call(
        functools.partial(_odd_out_kernel, n_total=n),
        grid=(b, n // tq),
        in_specs=in_specs,
        out_specs=out_specs,
        out_shape=out_shape,
        scratch_shapes=[pltpu.VMEM((tq + 2 * POOL_HALO, POOL_WIDTH), F32),
                        pltpu.VMEM((tq, d), BF16)],
        compiler_params=_cparams(("arbitrary", "arbitrary")),
        name="odd_out",
    )(sink, up, up, up, q, kd, kd, kd, vd, vd, vd, kvx, pool_w, pool_scale.reshape(1, POOL_WIDTH),
      w_out, h, gate, g2.reshape(1, d), shift, scale, rwt, rb)


def _moe_kernel(e_lo_ref, e_hi_ref, valid_ref, x_ref,
                g1_ref, u1_ref, d1_ref, g2_ref, u2_ref, d2_ref, o_ref):
    j = pl.program_id(0)

    @pl.when(valid_ref[j] != 0)
    def _():
        x = x_ref[:, :D_MODEL].astype(BF16)
        w_lo = x_ref[:, D_MODEL + INFO_W_LO:D_MODEL + INFO_W_LO + 1]
        w_hi = x_ref[:, D_MODEL + INFO_W_HI:D_MODEL + INFO_W_HI + 1]

        def expert(g_ref, u_ref, d_ref):
            gate = _dot(x, g_ref[0, 0].astype(BF16))
            hid = gate * jax.nn.sigmoid(gate) * _dot(x, u_ref[0, 0].astype(BF16))
            return _dot(hid.astype(BF16), d_ref[0, 0].astype(BF16))

        o_lo = expert(g1_ref, u1_ref, d1_ref)
        o_hi = expert(g2_ref, u2_ref, d2_ref)
        o_ref[...] = w_lo * o_lo + w_hi * o_hi

    @pl.when(valid_ref[j] == 0)
    def _():
        o_ref[...] = jnp.zeros(o_ref.shape, o_ref.dtype)


def _moe_call(tile_lo, tile_hi, tile_valid, xs, wg, wu, wd, *, layer, tm):
    p = xs.shape[0]
    d = D_MODEL
    ntiles = p // tm
    lo4 = lambda j, lo, hi, v: (layer, lo[j], 0, 0)
    hi4 = lambda j, lo, hi, v: (layer, hi[j], 0, 0)
    rowm = lambda j, lo, hi, v: (j, 0)
    row_in = lambda j, lo, hi, v: (jnp.where(v[j] != 0, j, 0), 0)
    grid_spec = pltpu.PrefetchScalarGridSpec(
        num_scalar_prefetch=3,
        grid=(ntiles,),
        in_specs=[pl.BlockSpec((tm, ROW_WIDTH), row_in),
                  pl.BlockSpec((1, 1, d, D_EXPERT), lo4), pl.BlockSpec((1, 1, d, D_EXPERT), lo4),
                  pl.BlockSpec((1, 1, D_EXPERT, d), lo4),
                  pl.BlockSpec((1, 1, d, D_EXPERT), hi4), pl.BlockSpec((1, 1, d, D_EXPERT), hi4),
                  pl.BlockSpec((1, 1, D_EXPERT, d), hi4)],
        out_specs=pl.BlockSpec((tm, d), rowm),
    )
    return pl.pallas_call(
        _moe_kernel,
        grid_spec=grid_spec,
        out_shape=jax.ShapeDtypeStruct((p, d), F32),
        compiler_params=_cparams(("arbitrary",)),
        name="moe_pairs",
    )(tile_lo, tile_hi, tile_valid, xs, wg, wu, wd, wg, wu, wd)


RANK_ROWS = 32


def _rank_kernel(bk_ref, tri_ref, rk_ref, cnt_ref, carry_ref):
    @pl.when(pl.program_id(0) == 0)
    def _():
        carry_ref[...] = jnp.zeros(carry_ref.shape, F32)

    tr = bk_ref.shape[1]
    bucket = bk_ref[0:1, :]
    rows = lax.broadcasted_iota(jnp.int32, (RANK_ROWS, tr), 0).astype(F32)
    onehot = jnp.where(rows == bucket, 1.0, 0.0)
    before = _dot(onehot.astype(BF16), tri_ref[...]) + carry_ref[:, 0:1]
    rank = jnp.sum(onehot * before, axis=0, keepdims=True)
    rk_ref[...] = jnp.broadcast_to(rank, rk_ref.shape)
    carry_ref[...] = carry_ref[...] + jnp.sum(onehot, axis=1, keepdims=True)
    cnt_ref[...] = carry_ref[...]


def _rank_call(bk, *, tr):
    t = bk.shape[1]
    tri = jnp.asarray(np.triu(np.ones((tr, tr), np.float32), 1), BF16)
    return pl.pallas_call(
        _rank_kernel,
        grid=(t // tr,),
        in_specs=[pl.BlockSpec((8, tr), lambda j: (0, j)), pl.BlockSpec((tr, tr), lambda j: (0, 0))],
        out_specs=[pl.BlockSpec((8, tr), lambda j: (0, j)), pl.BlockSpec((RANK_ROWS, LANES), lambda j: (0, 0))],
        out_shape=[jax.ShapeDtypeStruct((8, t), F32), jax.ShapeDtypeStruct((RANK_ROWS, LANES), F32)],
        scratch_shapes=[pltpu.VMEM((RANK_ROWS, LANES), F32)],
        compiler_params=_cparams(("arbitrary",)),
        name="bucket_rank",
    )(bk, tri)


ROW_TILE = 512


def _scatter_rows_kernel(tile_end_ref, pos_ref, *refs, tile_starts, tm):
    nsrc = len(tile_starts)
    srcs, out_hbm, zero_ref, sem = refs[:nsrc], refs[nsrc], refs[nsrc + 1], refs[nsrc + 2]
    j = pl.program_id(0)

    @pl.when(j == 0)
    def _():
        zero_ref[...] = jnp.zeros(zero_ref.shape, F32)

        def fill(tile, start):
            copy = pltpu.make_async_copy(zero_ref, out_hbm.at[pl.ds(tile * tm, tm)], sem)
            copy.start() if start else copy.wait()

        n_slot_tiles = out_hbm.shape[0] // tm
        used = tile_end_ref[N_BUCKETS - 1]
        for start in (True, False):
            for b in range(N_BUCKETS):
                first_tile = tile_end_ref[b - 1] if b else 0
                pl.when(tile_end_ref[b] > first_tile)(functools.partial(fill, tile_end_ref[b] - 1, start))
                pl.when(n_slot_tiles - 1 - b >= used)(functools.partial(fill, n_slot_tiles - 1 - b, start))

    def move(src_vmem):
        for r in range(ROW_TILE):
            pltpu.make_async_copy(src_vmem.at[pl.ds(r, 1)],
                                  out_hbm.at[pl.ds(pos_ref[0, 0, r], 1)], sem).start()
        pltpu.make_async_copy(src_vmem, out_hbm.at[pl.ds(0, ROW_TILE)], sem).wait()

    for s in range(nsrc):
        first = tile_starts[s]
        if nsrc == 1:
            move(srcs[s])
        else:
            in_range = (j >= first) if s == nsrc - 1 else ((j >= first) & (j < tile_starts[s + 1]))
            pl.when(in_range)(functools.partial(move, srcs[s]))


def _scatter_rows_call(pos, tile_end, sources, nslots, tm):
    w = sources[0].shape[1]
    t = pos.shape[0]
    ntiles = t // ROW_TILE
    tile_starts, src_specs, acc = [], [], 0
    for s in sources:
        first, count = acc, s.shape[0] // ROW_TILE
        tile_starts.append(first)
        src_specs.append(pl.BlockSpec(
            (ROW_TILE, w), lambda j, te, first=first, count=count: (jnp.clip(j - first, 0, count - 1), 0)))
        acc += count
    grid_spec = pltpu.PrefetchScalarGridSpec(
        num_scalar_prefetch=1,
        grid=(ntiles,),
        in_specs=[pl.BlockSpec((1, 1, ROW_TILE), lambda j, te: (j, 0, 0), memory_space=pltpu.SMEM)] + src_specs,
        out_specs=pl.BlockSpec(memory_space=pl.ANY),
        scratch_shapes=[pltpu.VMEM((tm, w), F32), pltpu.SemaphoreType.DMA(())],
    )
    return pl.pallas_call(
        functools.partial(_scatter_rows_kernel, tile_starts=tuple(tile_starts), tm=tm),
        grid_spec=grid_spec,
        out_shape=jax.ShapeDtypeStruct((nslots, w), F32),
        compiler_params=_cparams(("arbitrary",)),
        name="scatter_rows",
    )(tile_end, pos.reshape(ntiles, 1, ROW_TILE), *sources)


def _final_kernel(h_ref, pos_ref, next_pos_ref, moe_hbm, gate_ref, g_ref, o_ref, moe_bufs, moe_sems):
    step = pl.program_id(0) * pl.num_programs(1) + pl.program_id(1)
    nsteps = pl.num_programs(0) * pl.num_programs(1)
    x = h_ref[0] + gate_ref[0] * _gathered_rows(step, pos_ref, moe_hbm, moe_bufs, moe_sems)
    o_ref[0] = x * lax.rsqrt(jnp.mean(x * x, axis=-1, keepdims=True) + EPS) * g_ref[...]
    _request_next_rows(step, nsteps, next_pos_ref, moe_hbm, moe_bufs, moe_sems)


def _final_call(h, moe, pos, gate, g, *, tm):
    b, n, d = h.shape
    row = lambda bb, i: (bb, i, 0)
    pos_tiles, pos_spec, next_pos_spec = _pos_tiles(pos, tm, n // tm, 0)
    return pl.pallas_call(
        _final_kernel,
        grid=(b, n // tm),
        in_specs=[pl.BlockSpec((1, tm, d), row), pos_spec, next_pos_spec, pl.BlockSpec(memory_space=pl.ANY),
                  _mod_spec(gate), pl.BlockSpec((1, d), lambda bb, i: (0, 0))],
        out_specs=pl.BlockSpec((1, tm, d), row),
        out_shape=jax.ShapeDtypeStruct((b, n, d), F32),
        scratch_shapes=[pltpu.VMEM((2, tm, d), F32), pltpu.SemaphoreType.DMA((2,))],
        compiler_params=_cparams(("arbitrary", "arbitrary")),
        name="final_norm",
    )(h, pos_tiles, pos_tiles, moe, gate, g.reshape(1, d))


def _channel_dft_table():
    c = np.arange(FOURIER_HEAD_DIM)
    ang = 2.0 * np.pi * ((c[:, None] * c[None, :]) % FOURIER_HEAD_DIM) / FOURIER_HEAD_DIM
    return jnp.asarray(np.concatenate([np.cos(ang), np.sin(ang)], axis=1), F32)


def _position_dft_tables(n):
    n1 = n // DFT_RADIX
    t = np.arange(n)
    a = 2.0 * np.pi * ((np.arange(n1)[:, None] * t[None, :]) % n1) / n1
    bb = 2.0 * np.pi * ((np.arange(DFT_RADIX)[:, None] * t[None, :]) % n) / n
    return tuple(jnp.asarray(v, F32) for v in (np.cos(a), np.sin(a), np.cos(bb), np.sin(bb)))


def _rope_tables(n):
    quarter = HEAD_DIM // 4
    inv = ROPE_THETA ** (-jnp.arange(quarter, dtype=F32) / quarter)
    t = jnp.arange(n)
    ang_r = (t // GRID_W).astype(F32)[:, None] * inv
    ang_c = (t % GRID_W).astype(F32)[:, None] * inv
    cos = jnp.concatenate([jnp.cos(ang_r)] * 2 + [jnp.cos(ang_c)] * 2, axis=1)
    sin = jnp.concatenate([-jnp.sin(ang_r), jnp.sin(ang_r), -jnp.sin(ang_c), jnp.sin(ang_c)], axis=1)
    return jnp.tile(cos, (1, 2)), jnp.tile(sin, (1, 2))


def _dispatch_plan(bucket, rank, counts, tm):
    t = bucket.shape[0]
    ntiles = t // tm + N_BUCKETS
    tiles_per = (counts + tm - 1) // tm
    tile_end = jnp.cumsum(tiles_per)
    tile_start = tile_end - tiles_per
    onehot = bucket[:, None] == jnp.arange(N_BUCKETS, dtype=jnp.int32)[None, :]
    pos = jnp.sum(jnp.where(onehot, (tile_start * tm)[None, :], 0), axis=-1) + rank
    tile_ids = jnp.arange(ntiles, dtype=jnp.int32)
    used = tile_end[-1]
    tile_bucket = jnp.sum((tile_ids[:, None] >= tile_end[None, :]).astype(jnp.int32), axis=1)
    last_bucket = jnp.sum((jnp.maximum(used - 1, 0) >= tile_end).astype(jnp.int32))
    tile_valid = (tile_ids < used).astype(jnp.int32)
    tile_bucket = jnp.where(tile_valid == 1, tile_bucket, last_bucket)
    tile_lo = jnp.asarray(np.asarray(_BUCKET_LO, np.int32))[tile_bucket]
    tile_hi = jnp.asarray(np.asarray(_BUCKET_HI, np.int32))[tile_bucket]
    return pos.astype(jnp.int32), tile_end.astype(jnp.int32), tile_lo, tile_hi, tile_valid, ntiles * tm


def _moe_layer(fx_list, bk_list, wg, wu, wd, *, layer, tm):
    bk = bk_list[0] if len(bk_list) == 1 else jnp.concatenate(bk_list, axis=1)
    rk, cnt = _rank_call(bk, tr=ROW_TILE)
    pos, tile_end, tile_lo, tile_hi, tile_valid, nslots = _dispatch_plan(
        bk[0].astype(jnp.int32), rk[0].astype(jnp.int32), cnt[:N_BUCKETS, 0].astype(jnp.int32), tm)
    xs = _scatter_rows_call(pos, tile_end, fx_list, nslots, tm)
    return _moe_call(tile_lo, tile_hi, tile_valid, xs, wg, wu, wd, layer=layer, tm=tm), pos


def _forward(x, c, ctx, c_ctx, ada_w, ada_b, norm_mix_g, norm_ffn_g, even_w_in, even_conv_w, even_w_out,
             odd_w_in, odd_pool_w, odd_pool_scale, odd_sink, odd_w_out, router_w, router_b,
             moe_w_gate, moe_w_up, moe_w_down, final_g, *, tm_lat, tm_ctx, tq, tm_dft, tm_moe):
    b, n, d = x.shape
    l = ctx.shape[1]

    rows = ((b + 1 + 7) // 8) * 8
    s_rows = jnp.zeros((rows, d), F32).at[:b].set(c).at[b].set(c_ctx)
    mods = _ada_call(s_rows, ada_w, ada_b)

    def mod_vecs(layer):
        m = mods[layer, :b].reshape(b, N_MOD, 1, d)
        mc = mods[layer, b].reshape(N_MOD, 1, 1, d)
        return [m[:, k] for k in range(N_MOD)], [mc[k] for k in range(N_MOD)]

    rw_t = router_w.T
    rw_hi = rw_t.astype(BF16)
    rw_lo = (rw_t - rw_hi.astype(F32)).astype(BF16)
    rwt = jnp.concatenate([rw_hi, rw_lo], axis=0)
    rb = router_b.astype(F32).reshape(N_EXPERTS, 1)
    cs_tab = _channel_dft_table()

    m, mc = mod_vecs(0)
    w_in0 = even_w_in[0].astype(BF16)
    w_out0 = even_w_out[0].astype(BF16)

    def even_stream(h, mv, tm):
        nn = h.shape[1]
        uc, us, gb, gu = _inproj_call(h, norm_mix_g[0], mv[0], mv[1], w_in0, mode="even", tm=tm,
                                      extra=(cs_tab,))
        yf = _dft_call(_position_dft_tables(nn), uc, us, tm=min(tm_dft, nn))
        return _even_out_call(yf, gb, gu, even_conv_w[0], w_out0, h, mv[2], norm_ffn_g[0],
                              mv[3], mv[4], rwt, rb, tm=tm)

    h1, fx_lat, bk_lat = even_stream(x, m, tm_lat)
    hc1, fx_ctx, bk_ctx = even_stream(ctx, mc, tm_ctx)

    moe0, pos0 = _moe_layer([fx_lat.reshape(b * n, ROW_WIDTH), fx_ctx.reshape(b * l, ROW_WIDTH)],
                            [bk_lat, bk_ctx], moe_w_gate, moe_w_up, moe_w_down, layer=0, tm=tm_moe)
    gate_lat0, gate_ctx0 = m[5], mc[5]

    m, mc = mod_vecs(1)
    w_in1 = odd_w_in[0]
    kv0 = POOL_WIDTH + ATTN_WIDTH
    wk, wv = w_in1[:, kv0:kv0 + KV_WIDTH], w_in1[:, kv0 + KV_WIDTH:]

    def dup_heads(wm):
        return jnp.concatenate([wm[:, :HEAD_DIM], wm[:, :HEAD_DIM], wm[:, HEAD_DIM:], wm[:, HEAD_DIM:]], axis=1)

    w_kv_dup = jnp.concatenate([dup_heads(wk), dup_heads(wv)], axis=1)
    w_lat1 = jnp.concatenate([w_in1[:, :kv0], w_kv_dup], axis=1).astype(BF16)
    w_out1 = odd_w_out[0].astype(BF16)

    cos_t, sin_t = _rope_tables(n)
    q_scale = HEAD_DIM ** -0.5
    h1b, up, q, kd, vd = _inproj_call(h1, norm_mix_g[1], m[0], m[1], w_lat1, mode="odd", tm=tm_lat,
                                      moe=moe0, pos=pos0, moe_row0=0, gate=gate_lat0,
                                      extra=(cos_t * q_scale, sin_t * q_scale, cos_t, sin_t))
    (kvx,) = _inproj_call(hc1, norm_mix_g[1], mc[0], mc[1], w_kv_dup.astype(BF16), mode="plain", tm=tm_ctx,
                          moe=moe0, pos=pos0, moe_row0=b * n, gate=gate_ctx0)
    h2, fx2, bk2 = _odd_out_call(odd_sink[0], up, q, kd, vd, kvx, odd_pool_w[0].astype(BF16),
                                 odd_pool_scale[0], w_out1, h1b, m[2], norm_ffn_g[1], m[3], m[4], rwt, rb, tq=tq)
    moe1, pos1 = _moe_layer([fx2.reshape(b * n, ROW_WIDTH)], [bk2], moe_w_gate, moe_w_up, moe_w_down,
                            layer=1, tm=tm_moe)
    return _final_call(h2, moe1, pos1, m[5], final_g, tm=tm_lat)


def kernel(x, c, ctx, c_ctx, ada_w, ada_b, norm_mix_g, norm_ffn_g, even_w_in, even_conv_w, even_w_out,
           odd_w_in, odd_pool_w, odd_pool_scale, odd_sink, odd_w_out, router_w, router_b,
           moe_w_gate, moe_w_up, moe_w_down, final_g):
    return _forward(x, c, ctx, c_ctx, ada_w, ada_b, norm_mix_g, norm_ffn_g, even_w_in, even_conv_w,
                    even_w_out, odd_w_in, odd_pool_w, odd_pool_scale, odd_sink, odd_w_out, router_w,
                    router_b, moe_w_gate, moe_w_up, moe_w_down, final_g,
                    tm_lat=512, tm_ctx=256, tq=512, tm_dft=1024, tm_moe=512)
```

```python
import functools
import math

import numpy as np
import jax
import jax.numpy as jnp
from jax import lax
from jax.experimental import pallas as pl
from jax.experimental.pallas import tpu as pltpu

F32 = jnp.float32
BF16 = jnp.bfloat16

D_MODEL = 1024
GRID_W = 64
EPS = 1e-6
N_MOD = 6
FOURIER_HEADS = 4
FOURIER_HEAD_DIM = 128
FOURIER_WIDTH = 512
CONV_WIDTH = 512
CONV_K = 3
POOL_WINDOWS = (2, 4, 8, 16)
POOL_GROUP_DIM = 128
POOL_WIDTH = 512
POOL_HALO = 8
HEAD_DIM = 64
N_Q_HEADS = 8
N_KV_HEADS = 2
ATTN_WIDTH = 512
KV_WIDTH = 128
ATTN_BLOCK = 128
ROPE_THETA = 10000.0
N_EXPERTS = 16
N_GROUPS = 4
EXPERTS_PER_GROUP = 4
D_EXPERT = 512
N_PAIRS = 6
N_BUCKETS = N_GROUPS * N_PAIRS

_PAIRS = [(a, b) for a in range(EXPERTS_PER_GROUP) for b in range(a + 1, EXPERTS_PER_GROUP)]
_BUCKET_LO = [(k // N_PAIRS) * EXPERTS_PER_GROUP + _PAIRS[k % N_PAIRS][0] for k in range(N_BUCKETS)]
_BUCKET_HI = [(k // N_PAIRS) * EXPERTS_PER_GROUP + _PAIRS[k % N_PAIRS][1] for k in range(N_BUCKETS)]

LANES = 128
ROW_WIDTH = D_MODEL + LANES
INFO_BUCKET, INFO_W_LO, INFO_W_HI = 0, 1, 2
VMEM_LIMIT_BYTES = 48 * 1024 * 1024

HIGHEST = lax.Precision.HIGHEST


def _cparams(sem):
    return pltpu.CompilerParams(dimension_semantics=sem, vmem_limit_bytes=VMEM_LIMIT_BYTES)


def _rms_mod(x, g, shift, scale):
    y = x * lax.rsqrt(jnp.mean(x * x, axis=-1, keepdims=True) + EPS) * g
    return y * (1.0 + scale) + shift


def _dot(a, b):
    return jnp.dot(a, b, preferred_element_type=F32)


def _dot_nt(a, b):
    return lax.dot_general(a, b, (((1,), (1,)), ((), ())), preferred_element_type=F32)


def _ada_kernel(s_ref, w_ref, b_ref, o_ref):
    s = s_ref[...]
    s = s * jax.nn.sigmoid(s)
    o_ref[0] = jnp.dot(s, w_ref[0], preferred_element_type=F32, precision=HIGHEST) + b_ref[0]


def _ada_call(s_rows, ada_w, ada_b):
    depth, d, n6 = ada_w.shape
    r = s_rows.shape[0]
    tn = 1536
    return pl.pallas_call(
        _ada_kernel,
        grid=(depth, n6 // tn),
        in_specs=[
            pl.BlockSpec((r, d), lambda l, j: (0, 0)),
            pl.BlockSpec((1, d, tn), lambda l, j: (l, 0, j)),
            pl.BlockSpec((1, 1, tn), lambda l, j: (l, 0, j)),
        ],
        out_specs=pl.BlockSpec((1, r, tn), lambda l, j: (l, 0, j)),
        out_shape=jax.ShapeDtypeStruct((depth, r, n6), F32),
        compiler_params=_cparams(("arbitrary", "arbitrary")),
        name="ada_mod",
    )(s_rows, ada_w, ada_b.reshape(depth, 1, n6))


def _rope_group(x, cos, sin_signed):
    lane = lax.broadcasted_iota(jnp.int32, x.shape, 1)
    first_half = (lane % 32) < 16
    partner = jnp.where(first_half, pltpu.roll(x, LANES - 16, 1), pltpu.roll(x, 16, 1))
    return x * cos + partner * sin_signed


def _inproj_kernel(*refs, mode, add_moe):
    it = iter(refs)
    h_ref = next(it)
    if add_moe:
        pos_ref, next_pos_ref, moe_hbm, gate_ref = next(it), next(it), next(it), next(it)
        moe_bufs, moe_sems = refs[-2], refs[-1]
        step = pl.program_id(0) * pl.num_programs(1) + pl.program_id(1)
        nsteps = pl.num_programs(0) * pl.num_programs(1)
        moe_rows = _gathered_rows(step, pos_ref, moe_hbm, moe_bufs, moe_sems)
    g_ref, sh_ref, sc_ref, w_ref = next(it), next(it), next(it), next(it)
    x = h_ref[0]
    if add_moe:
        x = x + gate_ref[0] * moe_rows
    a = _rms_mod(x, g_ref[...], sh_ref[0], sc_ref[0])
    proj = _dot(a.astype(BF16), w_ref[...])
    if mode == "even":
        cs_ref = next(it)
        uc_ref, us_ref, gb_ref, gu_ref = next(it), next(it), next(it), next(it)
        uf = proj[:, :FOURIER_WIDTH].astype(BF16)
        cs = cs_ref[...].astype(BF16)
        for hh in range(FOURIER_HEADS):
            cols = slice(hh * LANES, (hh + 1) * LANES)
            r = _dot(uf[:, cols], cs)
            uc_ref[0, :, cols] = r[:, :LANES].astype(BF16)
            us_ref[0, :, cols] = r[:, LANES:].astype(BF16)
        c0 = FOURIER_WIDTH
        gb_ref[0] = proj[:, c0:c0 + CONV_WIDTH]
        gu_ref[0] = proj[:, c0 + CONV_WIDTH:c0 + 2 * CONV_WIDTH] * proj[:, c0 + 2 * CONV_WIDTH:]
    elif mode == "odd":
        cq_ref, sq_ref, ck_ref, sk_ref = next(it), next(it), next(it), next(it)
        hn_ref, up_ref, q_ref, kd_ref, vd_ref = next(it), next(it), next(it), next(it), next(it)
        hn_ref[0] = x
        up_ref[0] = proj[:, :POOL_WIDTH]
        c0 = POOL_WIDTH
        for gi in range(ATTN_WIDTH // LANES):
            cols = slice(c0 + gi * LANES, c0 + (gi + 1) * LANES)
            q_ref[0, :, gi * LANES:(gi + 1) * LANES] = _rope_group(
                proj[:, cols], cq_ref[...], sq_ref[...]).astype(BF16)
        c0 += ATTN_WIDTH
        for gi in range(2 * KV_WIDTH // LANES):
            cols = slice(c0 + gi * LANES, c0 + (gi + 1) * LANES)
            kd_ref[0, :, gi * LANES:(gi + 1) * LANES] = _rope_group(
                proj[:, cols], ck_ref[...], sk_ref[...]).astype(BF16)
        c0 += 2 * KV_WIDTH
        vd_ref[0] = proj[:, c0:].astype(BF16)
    else:
        kv_ref = next(it)
        kv_ref[0] = proj.astype(BF16)
    if add_moe:
        _request_next_rows(step, nsteps, next_pos_ref, moe_hbm, moe_bufs, moe_sems)


def _mod_spec(arr):
    if arr.shape[0] > 1:
        return pl.BlockSpec((1, 1, arr.shape[2]), lambda b, i: (b, 0, 0))
    return pl.BlockSpec((1, 1, arr.shape[2]), lambda b, i: (0, 0, 0))


def _gathered_rows(step, pos_ref, src_hbm, bufs, sems):
    nrows = bufs.shape[1]
    slot = step % 2

    @pl.when(step == 0)
    def _():
        def body(r, carry):
            pltpu.make_async_copy(src_hbm.at[pl.ds(pos_ref[0, 0, r], 1)], bufs.at[0, pl.ds(r, 1)],
                                  sems.at[0]).start()
            return carry
        lax.fori_loop(0, nrows, body, 0, unroll=8)

    pltpu.make_async_copy(src_hbm.at[pl.ds(0, nrows)], bufs.at[slot], sems.at[slot]).wait()
    return bufs[slot]


def _request_next_rows(step, nsteps, next_pos_ref, src_hbm, bufs, sems):
    nrows = bufs.shape[1]
    slot = 1 - step % 2
    for r in range(nrows):
        pltpu.make_async_copy(src_hbm.at[pl.ds(next_pos_ref[0, 0, r], 1)], bufs.at[slot, pl.ds(r, 1)],
                              sems.at[slot]).start(priority=r % 2)

    @pl.when(step == nsteps - 1)
    def _():
        pltpu.make_async_copy(src_hbm.at[pl.ds(0, nrows)], bufs.at[slot], sems.at[slot]).wait()


def _pos_tiles(pos, tm, nt, row0):
    tile0 = row0 // tm
    ntiles = pos.shape[0] // tm
    cur = pl.BlockSpec((1, 1, tm), lambda bb, i: (tile0 + bb * nt + i, 0, 0), memory_space=pltpu.SMEM)
    nxt = pl.BlockSpec((1, 1, tm), lambda bb, i: (jnp.minimum(tile0 + bb * nt + i + 1, ntiles - 1), 0, 0),
                       memory_space=pltpu.SMEM)
    return pos.reshape(ntiles, 1, tm), cur, nxt


def _inproj_call(h, g, shift, scale, w, *, mode, tm, moe=None, pos=None, moe_row0=0, gate=None, extra=()):
    b, n, d = h.shape
    nout = w.shape[1]
    add_moe = moe is not None
    row = lambda bb, i: (bb, i, 0)
    full2 = lambda bb, i: (0, 0)
    args = [h]
    in_specs = [pl.BlockSpec((1, tm, d), row)]
    scratch = []
    if add_moe:
        pos_tiles, pos_spec, next_pos_spec = _pos_tiles(pos, tm, n // tm, moe_row0)
        args += [pos_tiles, pos_tiles, moe, gate]
        in_specs += [pos_spec, next_pos_spec, pl.BlockSpec(memory_space=pl.ANY), _mod_spec(gate)]
        scratch = [pltpu.VMEM((2, tm, d), F32), pltpu.SemaphoreType.DMA((2,))]
    args += [g.reshape(1, d), shift, scale, w]
    in_specs += [pl.BlockSpec((1, d), full2), _mod_spec(shift), _mod_spec(scale),
                 pl.BlockSpec((d, nout), full2)]
    if mode == "even":
        args += list(extra)
        in_specs += [pl.BlockSpec(extra[0].shape, full2)]
        out_shape = [jax.ShapeDtypeStruct((b, n, FOURIER_WIDTH), BF16),
                     jax.ShapeDtypeStruct((b, n, FOURIER_WIDTH), BF16),
                     jax.ShapeDtypeStruct((b, n, CONV_WIDTH), F32),
                     jax.ShapeDtypeStruct((b, n, CONV_WIDTH), F32)]
        out_specs = [pl.BlockSpec((1, tm, 512), row)] * 4
    elif mode == "odd":
        args += list(extra)
        in_specs += [pl.BlockSpec((tm, LANES), lambda bb, i: (i, 0))] * 4
        out_shape = [jax.ShapeDtypeStruct((b, n, d), F32),
                     jax.ShapeDtypeStruct((b, n, POOL_WIDTH), F32),
                     jax.ShapeDtypeStruct((b, n, ATTN_WIDTH), BF16),
                     jax.ShapeDtypeStruct((b, n, 2 * KV_WIDTH), BF16),
                     jax.ShapeDtypeStruct((b, n, 2 * KV_WIDTH), BF16)]
        out_specs = [pl.BlockSpec((1, tm, d), row), pl.BlockSpec((1, tm, POOL_WIDTH), row),
                     pl.BlockSpec((1, tm, ATTN_WIDTH), row),
                     pl.BlockSpec((1, tm, 2 * KV_WIDTH), row), pl.BlockSpec((1, tm, 2 * KV_WIDTH), row)]
    else:
        out_shape = [jax.ShapeDtypeStruct((b, n, nout), BF16)]
        out_specs = [pl.BlockSpec((1, tm, nout), row)]
    return pl.pallas_call(
        functools.partial(_inproj_kernel, mode=mode, add_moe=add_moe),
        grid=(b, n // tm),
        in_specs=in_specs,
        out_specs=out_specs,
        out_shape=out_shape,
        scratch_shapes=scratch,
        compiler_params=_cparams(("arbitrary", "arbitrary")),
        name="inproj_" + mode,
    )(*args)


DFT_RADIX = 64


def _dft_kernel(ca_ref, sa_ref, cb_ref, sb_ref, uc_ref, us_ref, o_ref, c_scr, s_scr, *, norm):
    @pl.when(pl.program_id(1) == 0)
    def _():
        cb, sb = cb_ref[...], sb_ref[...]
        for r in range(ca_ref.shape[0]):
            ca, sa = ca_ref[r:r + 1, :], sa_ref[r:r + 1, :]
            rows = slice(r * DFT_RADIX, (r + 1) * DFT_RADIX)
            c_scr[rows, :] = (ca * cb - sa * sb).astype(BF16)
            s_scr[rows, :] = (-(sa * cb + ca * sb)).astype(BF16)

    acc = _dot(c_scr[...], uc_ref[0]) + _dot(s_scr[...], us_ref[0])
    o_ref[0] = (acc * norm).astype(BF16)


def _dft_call(tabs, uc, us, *, tm):
    b, n, wdt = uc.shape
    r_tile = tm // DFT_RADIX
    norm = 1.0 / math.sqrt(n * FOURIER_HEAD_DIM)
    a_spec = pl.BlockSpec((r_tile, n), lambda i, bb: (i, 0))
    b_spec = pl.BlockSpec((DFT_RADIX, n), lambda i, bb: (0, 0))
    u_spec = pl.BlockSpec((1, n, wdt), lambda i, bb: (bb, 0, 0))
    return pl.pallas_call(
        functools.partial(_dft_kernel, norm=norm),
        grid=(n // tm, b),
        in_specs=[a_spec, a_spec, b_spec, b_spec, u_spec, u_spec],
        out_specs=pl.BlockSpec((1, tm, wdt), lambda i, bb: (bb, i, 0)),
        out_shape=jax.ShapeDtypeStruct((b, n, wdt), BF16),
        scratch_shapes=[pltpu.VMEM((tm, n), BF16), pltpu.VMEM((tm, n), BF16)],
        compiler_params=_cparams(("arbitrary", "arbitrary")),
        name="dft_rows",
    )(*tabs, uc, us)


def _tail(y, h_ref, gate_ref, g2_ref, sh_ref, sc_ref, rwt_ref, rb_ref, hout_ref, fx_ref, bk_ref):
    hn = h_ref[0] + gate_ref[0] * y
    hout_ref[0] = hn
    f = _rms_mod(hn, g2_ref[...], sh_ref[0], sc_ref[0])
    tm = f.shape[0]
    f_hi = f.astype(BF16)
    f_lo = (f - f_hi.astype(F32)).astype(BF16)
    both = _dot_nt(rwt_ref[...], f_hi)
    logits = (both[:N_EXPERTS] + both[N_EXPERTS:]) + _dot_nt(rwt_ref[:N_EXPERTS, :], f_lo)
    aff = jax.nn.sigmoid(logits)
    sel = aff + rb_ref[...]
    cands = []
    for bkt in range(N_BUCKETS):
        lo, hi = _BUCKET_LO[bkt], _BUCKET_HI[bkt]
        cands.append((sel[lo:lo + 1, :] + sel[hi:hi + 1, :], jnp.full((1, tm), float(bkt), F32),
                      aff[lo:lo + 1, :], aff[hi:hi + 1, :]))
    while len(cands) > 1:
        merged = []
        for k in range(0, len(cands) - 1, 2):
            left, right = cands[k], cands[k + 1]
            take_right = right[0] > left[0]
            merged.append(tuple(jnp.where(take_right, r, l) for l, r in zip(left, right)))
        if len(cands) % 2:
            merged.append(cands[-1])
        cands = merged
    _, bucket, a_lo, a_hi = cands[0]
    den = a_lo + a_hi
    info = jnp.concatenate([bucket, a_lo / den, a_hi / den, jnp.zeros((LANES - 3, tm), F32)], axis=0)
    fx_ref[0, :, :D_MODEL] = f
    fx_ref[0, :, D_MODEL:] = info.T
    bk_ref[...] = jnp.concatenate([bucket, jnp.zeros((7, tm), F32)], axis=0)


def _tail_specs(h, gate, shift, scale, tm):
    b, n, d = h.shape
    nt = n // tm
    row = lambda bb, i: (bb, i, 0)
    in_specs = [pl.BlockSpec((1, tm, d), row), _mod_spec(gate),
                pl.BlockSpec((1, d), lambda bb, i: (0, 0)), _mod_spec(shift), _mod_spec(scale),
                pl.BlockSpec((2 * N_EXPERTS, d), lambda bb, i: (0, 0)),
                pl.BlockSpec((N_EXPERTS, 1), lambda bb, i: (0, 0))]
    out_shape = [jax.ShapeDtypeStruct((b, n, d), F32), jax.ShapeDtypeStruct((b, n, ROW_WIDTH), F32),
                 jax.ShapeDtypeStruct((8, b * n), F32)]
    out_specs = [pl.BlockSpec((1, tm, d), row), pl.BlockSpec((1, tm, ROW_WIDTH), row),
                 pl.BlockSpec((8, tm), lambda bb, i: (0, bb * nt + i))]
    return in_specs, out_shape, out_specs


def _even_out_kernel(yf_ref, gb_ref, gu_ref, gp_ref, gn_ref, cw_ref, wo_ref,
                     h_ref, gate_ref, g2_ref, sh_ref, sc_ref, rwt_ref, rb_ref,
                     hout_ref, fx_ref, bk_ref):
    i = pl.program_id(1)
    last = pl.num_programs(1) - 1
    gu = gu_ref[0]
    tm = gu.shape[0]
    prev = jnp.where(i > 0, gp_ref[0, 7:8, :], 0.0)
    nxt = jnp.where(i < last, gn_ref[0, 0:1, :], 0.0)
    row = lax.broadcasted_iota(jnp.int32, gu.shape, 0)
    up = jnp.where(row == 0, prev, pltpu.roll(gu, 1, 0))
    dn = jnp.where(row == tm - 1, nxt, pltpu.roll(gu, tm - 1, 0))
    conv = up * cw_ref[0:1, :] + gu * cw_ref[1:2, :] + dn * cw_ref[2:3, :]
    yc = (gb_ref[0] * conv).astype(BF16)
    y = _dot(yf_ref[0], wo_ref[:FOURIER_WIDTH, :]) + _dot(yc, wo_ref[FOURIER_WIDTH:, :])
    _tail(y, h_ref, gate_ref, g2_ref, sh_ref, sc_ref, rwt_ref, rb_ref, hout_ref, fx_ref, bk_ref)


def _even_out_call(yf, gb, gu, conv_w, w_out, h, gate, g2, shift, scale, rwt, rb, *, tm):
    b, n, d = h.shape
    row = lambda bb, i: (bb, i, 0)
    nb8 = n // 8
    t8 = tm // 8
    tin, out_shape, out_specs = _tail_specs(h, gate, shift, scale, tm)
    in_specs = [pl.BlockSpec((1, tm, 512), row), pl.BlockSpec((1, tm, 512), row),
                pl.BlockSpec((1, tm, 512), row),
                pl.BlockSpec((1, 8, 512), lambda bb, i: (bb, jnp.maximum(i * t8 - 1, 0), 0)),
                pl.BlockSpec((1, 8, 512), lambda bb, i: (bb, jnp.minimum((i + 1) * t8, nb8 - 1), 0)),
                pl.BlockSpec((CONV_K, CONV_WIDTH), lambda bb, i: (0, 0)),
                pl.BlockSpec((d, d), lambda bb, i: (0, 0))] + tin
    return pl.pallas_call(
        _even_out_kernel,
        grid=(b, n // tm),
        in_specs=in_specs,
        out_specs=out_specs,
        out_shape=out_shape,
        compiler_params=_cparams(("arbitrary", "arbitrary")),
        name="even_out",
    )(yf, gb, gu, gu, gu, conv_w, w_out, h, gate, g2.reshape(1, d), shift, scale, rwt, rb)


def _odd_out_kernel(sink_ref, up_ref, upp_ref, upn_ref, q_ref, kc_ref, kp_ref, kn_ref,
                    vc_ref, vp_ref, vn_ref, kvx_ref, pw_ref, ps_ref, wo_ref,
                    h_ref, gate_ref, g2_ref, sh_ref, sc_ref, rwt_ref, rb_ref,
                    hout_ref, fx_ref, bk_ref, ext_ref, mix_ref, *, n_total):
    i = pl.program_id(1)
    last = pl.num_programs(1) - 1
    tq = q_ref.shape[1]
    nsub = tq // ATTN_BLOCK

    u = up_ref[0]
    ext_ref[0:POOL_HALO, :] = jnp.where(i > 0, upp_ref[0], 0.0)
    ext_ref[POOL_HALO:POOL_HALO + tq, :] = u
    ext_ref[POOL_HALO + tq:, :] = jnp.where(i < last, upn_ref[0], 0.0)
    t = i * tq + lax.broadcasted_iota(jnp.int32, (tq, LANES), 0)
    for gi, win in enumerate(POOL_WINDOWS):
        r = win // 2
        cols = slice(gi * LANES, (gi + 1) * LANES)
        acc = ext_ref[POOL_HALO - r:POOL_HALO - r + tq, cols]
        for dlt in range(-r + 1, r + 1):
            acc = acc + ext_ref[POOL_HALO + dlt:POOL_HALO + dlt + tq, cols]
        cnt = (jnp.minimum(t + r + 1, n_total) - jnp.maximum(t - r, 0)).astype(F32)
        p = acc / cnt - u[:, cols]
        y = _dot(p.astype(BF16), pw_ref[gi]) * ps_ref[:, cols]
        mix_ref[:, cols] = y.astype(BF16)

    kwin = jnp.concatenate([kp_ref[0], kc_ref[0], kn_ref[0]], axis=0)
    vwin = jnp.concatenate([vp_ref[0], vc_ref[0], vn_ref[0]], axis=0)
    kvx = kvx_ref[0]
    kx, vx = kvx[:, :2 * KV_WIDTH], kvx[:, 2 * KV_WIDTH:]
    low = (lax.broadcasted_iota(jnp.int32, (1, 2 * KV_WIDTH), 1) % LANES) < HEAD_DIM
    zero = jnp.zeros((), BF16)
    k_half = (jnp.where(low, kwin, zero), jnp.where(low, zero, kwin))
    v_half = (jnp.where(low, vwin, zero), jnp.where(low, zero, vwin))
    kx_half = (jnp.where(low, kx, zero), jnp.where(low, zero, kx))
    vx_half = (jnp.where(low, vx, zero), jnp.where(low, zero, vx))

    span = 3 * ATTN_BLOCK
    rows2 = 2 * ATTN_BLOCK
    qi = lax.broadcasted_iota(jnp.int32, (rows2, span), 0) % ATTN_BLOCK
    kj = lax.broadcasted_iota(jnp.int32, (rows2, span), 1)
    in_prev = kj < ATTN_BLOCK
    in_next = kj >= 2 * ATTN_BLOCK
    neg_inf = jnp.float32(-jnp.inf)
    band_bias = (jnp.where(in_prev & (kj < qi), neg_inf, 0.0)
                 + jnp.where(in_next & (kj - 2 * ATTN_BLOCK > qi), neg_inf, 0.0))
    top_rows = lax.broadcasted_iota(jnp.int32, (rows2, 1), 0) < ATTN_BLOCK

    for j in range(nsub):
        blk = i * nsub + j
        prev_bias = jnp.where(blk > 0, 0.0, neg_inf)
        next_bias = jnp.where(blk < (n_total // ATTN_BLOCK) - 1, 0.0, neg_inf)
        bias = band_bias + jnp.where(in_prev, prev_bias, 0.0) + jnp.where(in_next, next_bias, 0.0)
        r0 = j * ATTN_BLOCK
        for kh in range(N_KV_HEADS):
            kcols = slice(kh * LANES, (kh + 1) * LANES)
            xq = jnp.concatenate(
                [q_ref[0, r0:r0 + ATTN_BLOCK, (2 * kh + pr) * LANES:(2 * kh + pr + 1) * LANES]
                 for pr in range(2)], axis=0)
            o_pair = None
            for half in range(2):
                ks = k_half[half][r0:r0 + span, kcols]
                vs = v_half[half][r0:r0 + span, kcols]
                s1 = _dot_nt(xq, ks) + bias
                s2 = _dot_nt(xq, kx_half[half][:, kcols])
                head0 = kh * 4 + half
                snk = jnp.where(top_rows, sink_ref[head0], sink_ref[head0 + 2])
                m = jnp.maximum(jnp.maximum(jnp.max(s1, axis=-1, keepdims=True),
                                            jnp.max(s2, axis=-1, keepdims=True)), snk)
                e1 = jnp.exp(s1 - m)
                e2 = jnp.exp(s2 - m)
                den = (jnp.sum(e1, axis=-1, keepdims=True) + jnp.sum(e2, axis=-1, keepdims=True)
                       + jnp.exp(snk - m))
                o = _dot(e1.astype(BF16), vs) + _dot(e2.astype(BF16), vx_half[half][:, kcols])
                o = o / den
                o_pair = o if o_pair is None else o_pair + o
            for pr in range(2):
                c0 = POOL_WIDTH + (2 * kh + pr) * LANES
                mix_ref[r0:r0 + ATTN_BLOCK, c0:c0 + LANES] = (
                    o_pair[pr * ATTN_BLOCK:(pr + 1) * ATTN_BLOCK].astype(BF16))

    y = _dot(mix_ref[...], wo_ref[...])
    _tail(y, h_ref, gate_ref, g2_ref, sh_ref, sc_ref, rwt_ref, rb_ref, hout_ref, fx_ref, bk_ref)


def _odd_out_call(sink, up, q, kd, vd, kvx, pool_w, pool_scale, w_out,
                  h, gate, g2, shift, scale, rwt, rb, *, tq):
    b, n, d = h.shape
    row = lambda bb, i: (bb, i, 0)
    nb8, t8 = n // POOL_HALO, tq // POOL_HALO
    nbk, tk = n // ATTN_BLOCK, tq // ATTN_BLOCK
    prev8 = lambda bb, i: (bb, jnp.maximum(i * t8 - 1, 0), 0)
    next8 = lambda bb, i: (bb, jnp.minimum((i + 1) * t8, nb8 - 1), 0)
    prevk = lambda bb, i: (bb, jnp.maximum(i * tk - 1, 0), 0)
    nextk = lambda bb, i: (bb, jnp.minimum((i + 1) * tk, nbk - 1), 0)
    kvw = 2 * KV_WIDTH
    tin, out_shape, out_specs = _tail_specs(h, gate, shift, scale, tq)
    in_specs = [pl.BlockSpec(memory_space=pltpu.SMEM),
                pl.BlockSpec((1, tq, POOL_WIDTH), row),
                pl.BlockSpec((1, POOL_HALO, POOL_WIDTH), prev8),
                pl.BlockSpec((1, POOL_HALO, POOL_WIDTH), next8),
                pl.BlockSpec((1, tq, ATTN_WIDTH), row),
                pl.BlockSpec((1, tq, kvw), row),
                pl.BlockSpec((1, ATTN_BLOCK, kvw), prevk),
                pl.BlockSpec((1, ATTN_BLOCK, kvw), nextk),
                pl.BlockSpec((1, tq, kvw), row),
                pl.BlockSpec((1, ATTN_BLOCK, kvw), prevk),
                pl.BlockSpec((1, ATTN_BLOCK, kvw), nextk),
                pl.BlockSpec((1, kvx.shape[1], 2 * kvw), lambda bb, i: (bb, 0, 0)),
                pl.BlockSpec(pool_w.shape, lambda bb, i: (0, 0, 0)),
                pl.BlockSpec((1, POOL_WIDTH), lambda bb, i: (0, 0)),
                pl.BlockSpec((d, d), lambda bb, i: (0, 0))] + tin
    return pl.pallas_call(
        functools.partial(_odd_out_kernel, n_total=n),
        grid=(b, n // tq),
        in_specs=in_specs,
        out_specs=out_specs,
        out_shape=out_shape,
        scratch_shapes=[pltpu.VMEM((tq + 2 * POOL_HALO, POOL_WIDTH), F32),
                        pltpu.VMEM((tq, d), BF16)],
        compiler_params=_cparams(("arbitrary", "arbitrary")),
        name="odd_out",
    )(sink, up, up, up, q, kd, kd, kd, vd, vd, vd, kvx, pool_w, pool_scale.reshape(1, POOL_WIDTH),
      w_out, h, gate, g2.reshape(1, d), shift, scale, rwt, rb)


def _moe_kernel(e_lo_ref, e_hi_ref, valid_ref, x_ref,
                g1_ref, u1_ref, d1_ref, g2_ref, u2_ref, d2_ref, o_ref):
    j = pl.program_id(0)

    @pl.when(valid_ref[j] != 0)
    def _():
        x = x_ref[:, :D_MODEL].astype(BF16)
        w_lo = x_ref[:, D_MODEL + INFO_W_LO:D_MODEL + INFO_W_LO + 1]
        w_hi = x_ref[:, D_MODEL + INFO_W_HI:D_MODEL + INFO_W_HI + 1]

        def expert(g_ref, u_ref, d_ref):
            gate = _dot(x, g_ref[0, 0].astype(BF16))
            hid = gate * jax.nn.sigmoid(gate) * _dot(x, u_ref[0, 0].astype(BF16))
            return _dot(hid.astype(BF16), d_ref[0, 0].astype(BF16))

        o_lo = expert(g1_ref, u1_ref, d1_ref)
        o_hi = expert(g2_ref, u2_ref, d2_ref)
        o_ref[...] = w_lo * o_lo + w_hi * o_hi

    @pl.when(valid_ref[j] == 0)
    def _():
        o_ref[...] = jnp.zeros(o_ref.shape, o_ref.dtype)


def _moe_call(tile_lo, tile_hi, tile_valid, xs, wg, wu, wd, *, layer, tm):
    p = xs.shape[0]
    d = D_MODEL
    ntiles = p // tm
    lo4 = lambda j, lo, hi, v: (layer, lo[j], 0, 0)
    hi4 = lambda j, lo, hi, v: (layer, hi[j], 0, 0)
    rowm = lambda j, lo, hi, v: (j, 0)
    row_in = lambda j, lo, hi, v: (jnp.where(v[j] != 0, j, 0), 0)
    grid_spec = pltpu.PrefetchScalarGridSpec(
        num_scalar_prefetch=3,
        grid=(ntiles,),
        in_specs=[pl.BlockSpec((tm, ROW_WIDTH), row_in),
                  pl.BlockSpec((1, 1, d, D_EXPERT), lo4), pl.BlockSpec((1, 1, d, D_EXPERT), lo4),
                  pl.BlockSpec((1, 1, D_EXPERT, d), lo4),
                  pl.BlockSpec((1, 1, d, D_EXPERT), hi4), pl.BlockSpec((1, 1, d, D_EXPERT), hi4),
                  pl.BlockSpec((1, 1, D_EXPERT, d), hi4)],
        out_specs=pl.BlockSpec((tm, d), rowm),
    )
    return pl.pallas_call(
        _moe_kernel,
        grid_spec=grid_spec,
        out_shape=jax.ShapeDtypeStruct((p, d), F32),
        compiler_params=_cparams(("arbitrary",)),
        name="moe_pairs",
    )(tile_lo, tile_hi, tile_valid, xs, wg, wu, wd, wg, wu, wd)


RANK_ROWS = 32


def _rank_kernel(bk_ref, tri_ref, rk_ref, cnt_ref, carry_ref):
    @pl.when(pl.program_id(0) == 0)
    def _():
        carry_ref[...] = jnp.zeros(carry_ref.shape, F32)

    tr = bk_ref.shape[1]
    bucket = bk_ref[0:1, :]
    rows = lax.broadcasted_iota(jnp.int32, (RANK_ROWS, tr), 0).astype(F32)
    onehot = jnp.where(rows == bucket, 1.0, 0.0)
    before = _dot(onehot.astype(BF16), tri_ref[...]) + carry_ref[:, 0:1]
    rank = jnp.sum(onehot * before, axis=0, keepdims=True)
    rk_ref[...] = jnp.broadcast_to(rank, rk_ref.shape)
    carry_ref[...] = carry_ref[...] + jnp.sum(onehot, axis=1, keepdims=True)
    cnt_ref[...] = carry_ref[...]


def _rank_call(bk, *, tr):
    t = bk.shape[1]
    tri = jnp.asarray(np.triu(np.ones((tr, tr), np.float32), 1), BF16)
    return pl.pallas_call(
        _rank_kernel,
        grid=(t // tr,),
        in_specs=[pl.BlockSpec((8, tr), lambda j: (0, j)), pl.BlockSpec((tr, tr), lambda j: (0, 0))],
        out_specs=[pl.BlockSpec((8, tr), lambda j: (0, j)), pl.BlockSpec((RANK_ROWS, LANES), lambda j: (0, 0))],
        out_shape=[jax.ShapeDtypeStruct((8, t), F32), jax.ShapeDtypeStruct((RANK_ROWS, LANES), F32)],
        scratch_shapes=[pltpu.VMEM((RANK_ROWS, LANES), F32)],
        compiler_params=_cparams(("arbitrary",)),
        name="bucket_rank",
    )(bk, tri)


ROW_TILE = 512


def _scatter_rows_kernel(tile_end_ref, pos_ref, *refs, tile_starts, tm):
    nsrc = len(tile_starts)
    srcs, out_hbm, zero_ref, sem = refs[:nsrc], refs[nsrc], refs[nsrc + 1], refs[nsrc + 2]
    j = pl.program_id(0)

    @pl.when(j == 0)
    def _():
        zero_ref[...] = jnp.zeros(zero_ref.shape, F32)

        def fill(tile, start):
            copy = pltpu.make_async_copy(zero_ref, out_hbm.at[pl.ds(tile * tm, tm)], sem)
            copy.start() if start else copy.wait()

        n_slot_tiles = out_hbm.shape[0] // tm
        used = tile_end_ref[N_BUCKETS - 1]
        for start in (True, False):
            for b in range(N_BUCKETS):
                first_tile = tile_end_ref[b - 1] if b else 0
                pl.when(tile_end_ref[b] > first_tile)(functools.partial(fill, tile_end_ref[b] - 1, start))
                pl.when(n_slot_tiles - 1 - b >= used)(functools.partial(fill, n_slot_tiles - 1 - b, start))

    def move(src_vmem):
        for r in range(ROW_TILE):
            pltpu.make_async_copy(src_vmem.at[pl.ds(r, 1)],
                                  out_hbm.at[pl.ds(pos_ref[0, 0, r], 1)], sem).start(priority=r % 2)
        pltpu.make_async_copy(src_vmem, out_hbm.at[pl.ds(0, ROW_TILE)], sem).wait()

    for s in range(nsrc):
        first = tile_starts[s]
        if nsrc == 1:
            move(srcs[s])
        else:
            in_range = (j >= first) if s == nsrc - 1 else ((j >= first) & (j < tile_starts[s + 1]))
            pl.when(in_range)(functools.partial(move, srcs[s]))


def _scatter_rows_call(pos, tile_end, sources, nslots, tm):
    w = sources[0].shape[1]
    t = pos.shape[0]
    ntiles = t // ROW_TILE
    tile_starts, src_specs, acc = [], [], 0
    for s in sources:
        first, count = acc, s.shape[0] // ROW_TILE
        tile_starts.append(first)
        src_specs.append(pl.BlockSpec(
            (ROW_TILE, w), lambda j, te, first=first, count=count: (jnp.clip(j - first, 0, count - 1), 0)))
        acc += count
    grid_spec = pltpu.PrefetchScalarGridSpec(
        num_scalar_prefetch=1,
        grid=(ntiles,),
        in_specs=[pl.BlockSpec((1, 1, ROW_TILE), lambda j, te: (j, 0, 0), memory_space=pltpu.SMEM)] + src_specs,
        out_specs=pl.BlockSpec(memory_space=pl.ANY),
        scratch_shapes=[pltpu.VMEM((tm, w), F32), pltpu.SemaphoreType.DMA(())],
    )
    return pl.pallas_call(
        functools.partial(_scatter_rows_kernel, tile_starts=tuple(tile_starts), tm=tm),
        grid_spec=grid_spec,
        out_shape=jax.ShapeDtypeStruct((nslots, w), F32),
        compiler_params=_cparams(("arbitrary",)),
        name="scatter_rows",
    )(tile_end, pos.reshape(ntiles, 1, ROW_TILE), *sources)


def _final_kernel(h_ref, pos_ref, next_pos_ref, moe_hbm, gate_ref, g_ref, o_ref, moe_bufs, moe_sems):
    step = pl.program_id(0) * pl.num_programs(1) + pl.program_id(1)
    nsteps = pl.num_programs(0) * pl.num_programs(1)
    x = h_ref[0] + gate_ref[0] * _gathered_rows(step, pos_ref, moe_hbm, moe_bufs, moe_sems)
    o_ref[0] = x * lax.rsqrt(jnp.mean(x * x, axis=-1, keepdims=True) + EPS) * g_ref[...]
    _request_next_rows(step, nsteps, next_pos_ref, moe_hbm, moe_bufs, moe_sems)


def _final_call(h, moe, pos, gate, g, *, tm):
    b, n, d = h.shape
    row = lambda bb, i: (bb, i, 0)
    pos_tiles, pos_spec, next_pos_spec = _pos_tiles(pos, tm, n // tm, 0)
    return pl.pallas_call(
        _final_kernel,
        grid=(b, n // tm),
        in_specs=[pl.BlockSpec((1, tm, d), row), pos_spec, next_pos_spec, pl.BlockSpec(memory_space=pl.ANY),
                  _mod_spec(gate), pl.BlockSpec((1, d), lambda bb, i: (0, 0))],
        out_specs=pl.BlockSpec((1, tm, d), row),
        out_shape=jax.ShapeDtypeStruct((b, n, d), F32),
        scratch_shapes=[pltpu.VMEM((2, tm, d), F32), pltpu.SemaphoreType.DMA((2,))],
        compiler_params=_cparams(("arbitrary", "arbitrary")),
        name="final_norm",
    )(h, pos_tiles, pos_tiles, moe, gate, g.reshape(1, d))


def _channel_dft_table():
    c = np.arange(FOURIER_HEAD_DIM)
    ang = 2.0 * np.pi * ((c[:, None] * c[None, :]) % FOURIER_HEAD_DIM) / FOURIER_HEAD_DIM
    return jnp.asarray(np.concatenate([np.cos(ang), np.sin(ang)], axis=1), F32)


def _position_dft_tables(n):
    n1 = n // DFT_RADIX
    t = np.arange(n)
    a = 2.0 * np.pi * ((np.arange(n1)[:, None] * t[None, :]) % n1) / n1
    bb = 2.0 * np.pi * ((np.arange(DFT_RADIX)[:, None] * t[None, :]) % n) / n
    return tuple(jnp.asarray(v, F32) for v in (np.cos(a), np.sin(a), np.cos(bb), np.sin(bb)))


def _rope_tables(n):
    quarter = HEAD_DIM // 4
    inv = ROPE_THETA ** (-jnp.arange(quarter, dtype=F32) / quarter)
    t = jnp.arange(n)
    ang_r = (t // GRID_W).astype(F32)[:, None] * inv
    ang_c = (t % GRID_W).astype(F32)[:, None] * inv
    cos = jnp.concatenate([jnp.cos(ang_r)] * 2 + [jnp.cos(ang_c)] * 2, axis=1)
    sin = jnp.concatenate([-jnp.sin(ang_r), jnp.sin(ang_r), -jnp.sin(ang_c), jnp.sin(ang_c)], axis=1)
    return jnp.tile(cos, (1, 2)), jnp.tile(sin, (1, 2))


def _dispatch_plan(bucket, rank, counts, tm):
    t = bucket.shape[0]
    ntiles = t // tm + N_BUCKETS
    tiles_per = (counts + tm - 1) // tm
    tile_end = jnp.cumsum(tiles_per)
    tile_start = tile_end - tiles_per
    onehot = bucket[:, None] == jnp.arange(N_BUCKETS, dtype=jnp.int32)[None, :]
    pos = jnp.sum(jnp.where(onehot, (tile_start * tm)[None, :], 0), axis=-1) + rank
    tile_ids = jnp.arange(ntiles, dtype=jnp.int32)
    used = tile_end[-1]
    tile_bucket = jnp.sum((tile_ids[:, None] >= tile_end[None, :]).astype(jnp.int32), axis=1)
    last_bucket = jnp.sum((jnp.maximum(used - 1, 0) >= tile_end).astype(jnp.int32))
    tile_valid = (tile_ids < used).astype(jnp.int32)
    tile_bucket = jnp.where(tile_valid == 1, tile_bucket, last_bucket)
    tile_lo = jnp.asarray(np.asarray(_BUCKET_LO, np.int32))[tile_bucket]
    tile_hi = jnp.asarray(np.asarray(_BUCKET_HI, np.int32))[tile_bucket]
    return pos.astype(jnp.int32), tile_end.astype(jnp.int32), tile_lo, tile_hi, tile_valid, ntiles * tm


def _moe_layer(fx_list, bk_list, wg, wu, wd, *, layer, tm):
    bk = bk_list[0] if len(bk_list) == 1 else jnp.concatenate(bk_list, axis=1)
    rk, cnt = _rank_call(bk, tr=ROW_TILE)
    pos, tile_end, tile_lo, tile_hi, tile_valid, nslots = _dispatch_plan(
        bk[0].astype(jnp.int32), rk[0].astype(jnp.int32), cnt[:N_BUCKETS, 0].astype(jnp.int32), tm)
    xs = _scatter_rows_call(pos, tile_end, fx_list, nslots, tm)
    return _moe_call(tile_lo, tile_hi, tile_valid, xs, wg, wu, wd, layer=layer, tm=tm), pos


def _forward(x, c, ctx, c_ctx, ada_w, ada_b, norm_mix_g, norm_ffn_g, even_w_in, even_conv_w, even_w_out,
             odd_w_in, odd_pool_w, odd_pool_scale, odd_sink, odd_w_out, router_w, router_b,
             moe_w_gate, moe_w_up, moe_w_down, final_g, *, tm_lat, tm_ctx, tq, tm_dft, tm_moe):
    b, n, d = x.shape
    l = ctx.shape[1]

    rows = ((b + 1 + 7) // 8) * 8
    s_rows = jnp.zeros((rows, d), F32).at[:b].set(c).at[b].set(c_ctx)
    mods = _ada_call(s_rows, ada_w, ada_b)

    def mod_vecs(layer):
        m = mods[layer, :b].reshape(b, N_MOD, 1, d)
        mc = mods[layer, b].reshape(N_MOD, 1, 1, d)
        return [m[:, k] for k in range(N_MOD)], [mc[k] for k in range(N_MOD)]

    rw_t = router_w.T
    rw_hi = rw_t.astype(BF16)
    rw_lo = (rw_t - rw_hi.astype(F32)).astype(BF16)
    rwt = jnp.concatenate([rw_hi, rw_lo], axis=0)
    rb = router_b.astype(F32).reshape(N_EXPERTS, 1)
    cs_tab = _channel_dft_table()

    m, mc = mod_vecs(0)
    w_in0 = even_w_in[0].astype(BF16)
    w_out0 = even_w_out[0].astype(BF16)

    def even_stream(h, mv, tm):
        nn = h.shape[1]
        uc, us, gb, gu = _inproj_call(h, norm_mix_g[0], mv[0], mv[1], w_in0, mode="even", tm=tm,
                                      extra=(cs_tab,))
        yf = _dft_call(_position_dft_tables(nn), uc, us, tm=min(tm_dft, nn))
        return _even_out_call(yf, gb, gu, even_conv_w[0], w_out0, h, mv[2], norm_ffn_g[0],
                              mv[3], mv[4], rwt, rb, tm=tm)

    h1, fx_lat, bk_lat = even_stream(x, m, tm_lat)
    hc1, fx_ctx, bk_ctx = even_stream(ctx, mc, tm_ctx)

    moe0, pos0 = _moe_layer([fx_lat.reshape(b * n, ROW_WIDTH), fx_ctx.reshape(b * l, ROW_WIDTH)],
                            [bk_lat, bk_ctx], moe_w_gate, moe_w_up, moe_w_down, layer=0, tm=tm_moe)
    gate_lat0, gate_ctx0 = m[5], mc[5]

    m, mc = mod_vecs(1)
    w_in1 = odd_w_in[0]
    kv0 = POOL_WIDTH + ATTN_WIDTH
    wk, wv = w_in1[:, kv0:kv0 + KV_WIDTH], w_in1[:, kv0 + KV_WIDTH:]

    def dup_heads(wm):
        return jnp.concatenate([wm[:, :HEAD_DIM], wm[:, :HEAD_DIM], wm[:, HEAD_DIM:], wm[:, HEAD_DIM:]], axis=1)

    w_kv_dup = jnp.concatenate([dup_heads(wk), dup_heads(wv)], axis=1)
    w_lat1 = jnp.concatenate([w_in1[:, :kv0], w_kv_dup], axis=1).astype(BF16)
    w_out1 = odd_w_out[0].astype(BF16)

    cos_t, sin_t = _rope_tables(n)
    q_scale = HEAD_DIM ** -0.5
    h1b, up, q, kd, vd = _inproj_call(h1, norm_mix_g[1], m[0], m[1], w_lat1, mode="odd", tm=tm_lat,
                                      moe=moe0, pos=pos0, moe_row0=0, gate=gate_lat0,
                                      extra=(cos_t * q_scale, sin_t * q_scale, cos_t, sin_t))
    (kvx,) = _inproj_call(hc1, norm_mix_g[1], mc[0], mc[1], w_kv_dup.astype(BF16), mode="plain", tm=tm_ctx,
                          moe=moe0, pos=pos0, moe_row0=b * n, gate=gate_ctx0)
    h2, fx2, bk2 = _odd_out_call(odd_sink[0], up, q, kd, vd, kvx, odd_pool_w[0].astype(BF16),
                                 odd_pool_scale[0], w_out1, h1b, m[2], norm_ffn_g[1], m[3], m[4], rwt, rb, tq=tq)
    moe1, pos1 = _moe_layer([fx2.reshape(b * n, ROW_WIDTH)], [bk2], moe_w_gate, moe_w_up, moe_w_down,
                            layer=1, tm=tm_moe)
    return _final_call(h2, moe1, pos1, m[5], final_g, tm=tm_lat)


def kernel(x, c, ctx, c_ctx, ada_w, ada_b, norm_mix_g, norm_ffn_g, even_w_in, even_conv_w, even_w_out,
           odd_w_in, odd_pool_w, odd_pool_scale, odd_sink, odd_w_out, router_w, router_b,
           moe_w_gate, moe_w_up, moe_w_down, final_g):
    return _forward(x, c, ctx, c_ctx, ada_w, ada_b, norm_mix_g, norm_ffn_g, even_w_in, even_conv_w,
                    even_w_out, odd_w_in, odd_pool_w, odd_pool_scale, odd_sink, odd_w_out, router_w,
                    router_b, moe_w_gate, moe_w_up, moe_w_down, final_g,
                    tm_lat=512, tm_ctx=256, tq=512, tm_dft=1024, tm_moe=512)
```

```python
import functools
import math

import numpy as np
import jax
import jax.numpy as jnp
from jax import lax
from jax.experimental import pallas as pl
from jax.experimental.pallas import tpu as pltpu

F32 = jnp.float32
BF16 = jnp.bfloat16

D_MODEL = 1024
GRID_W = 64
EPS = 1e-6
N_MOD = 6
FOURIER_HEADS = 4
FOURIER_HEAD_DIM = 128
FOURIER_WIDTH = 512
CONV_WIDTH = 512
CONV_K = 3
POOL_WINDOWS = (2, 4, 8, 16)
POOL_GROUP_DIM = 128
POOL_WIDTH = 512
POOL_HALO = 8
HEAD_DIM = 64
N_Q_HEADS = 8
N_KV_HEADS = 2
ATTN_WIDTH = 512
KV_WIDTH = 128
ATTN_BLOCK = 128
ROPE_THETA = 10000.0
N_EXPERTS = 16
N_GROUPS = 4
EXPERTS_PER_GROUP = 4
D_EXPERT = 512
N_PAIRS = 6
N_BUCKETS = N_GROUPS * N_PAIRS

_PAIRS = [(a, b) for a in range(EXPERTS_PER_GROUP) for b in range(a + 1, EXPERTS_PER_GROUP)]
_BUCKET_LO = [(k // N_PAIRS) * EXPERTS_PER_GROUP + _PAIRS[k % N_PAIRS][0] for k in range(N_BUCKETS)]
_BUCKET_HI = [(k // N_PAIRS) * EXPERTS_PER_GROUP + _PAIRS[k % N_PAIRS][1] for k in range(N_BUCKETS)]

LANES = 128
ROW_WIDTH = D_MODEL + LANES
INFO_BUCKET, INFO_W_LO, INFO_W_HI = 0, 1, 2
VMEM_LIMIT_BYTES = 48 * 1024 * 1024

HIGHEST = lax.Precision.HIGHEST
LOG2_E = math.log2(math.e)


def _cparams(sem):
    return pltpu.CompilerParams(dimension_semantics=sem, vmem_limit_bytes=VMEM_LIMIT_BYTES)


def _rms_mod(x, g, shift, scale):
    y = x * lax.rsqrt(jnp.mean(x * x, axis=-1, keepdims=True) + EPS) * g
    return y * (1.0 + scale) + shift


def _dot(a, b):
    return jnp.dot(a, b, preferred_element_type=F32)


def _dot_nt(a, b):
    return lax.dot_general(a, b, (((1,), (1,)), ((), ())), preferred_element_type=F32)


def _ada_kernel(s_ref, w_ref, b_ref, o_ref):
    s = s_ref[...]
    s = s * jax.nn.sigmoid(s)
    o_ref[0] = jnp.dot(s, w_ref[0], preferred_element_type=F32, precision=HIGHEST) + b_ref[0]


def _ada_call(s_rows, ada_w, ada_b):
    depth, d, n6 = ada_w.shape
    r = s_rows.shape[0]
    tn = 1536
    return pl.pallas_call(
        _ada_kernel,
        grid=(depth, n6 // tn),
        in_specs=[
            pl.BlockSpec((r, d), lambda l, j: (0, 0)),
            pl.BlockSpec((1, d, tn), lambda l, j: (l, 0, j)),
            pl.BlockSpec((1, 1, tn), lambda l, j: (l, 0, j)),
        ],
        out_specs=pl.BlockSpec((1, r, tn), lambda l, j: (l, 0, j)),
        out_shape=jax.ShapeDtypeStruct((depth, r, n6), F32),
        compiler_params=_cparams(("arbitrary", "arbitrary")),
        name="ada_mod",
    )(s_rows, ada_w, ada_b.reshape(depth, 1, n6))


def _rope_group(x, cos, sin_signed):
    lane = lax.broadcasted_iota(jnp.int32, x.shape, 1)
    first_half = (lane % 32) < 16
    partner = jnp.where(first_half, pltpu.roll(x, LANES - 16, 1), pltpu.roll(x, 16, 1))
    return x * cos + partner * sin_signed


def _inproj_kernel(*refs, mode, add_moe):
    it = iter(refs)
    h_ref = next(it)
    if add_moe:
        pos_ref, next_pos_ref, moe_hbm, gate_ref = next(it), next(it), next(it), next(it)
        moe_bufs, moe_sems = refs[-2], refs[-1]
        step = pl.program_id(0) * pl.num_programs(1) + pl.program_id(1)
        nsteps = pl.num_programs(0) * pl.num_programs(1)
        moe_rows = _gathered_rows(step, pos_ref, moe_hbm, moe_bufs, moe_sems)
    g_ref, sh_ref, sc_ref, w_ref = next(it), next(it), next(it), next(it)
    x = h_ref[0]
    if add_moe:
        x = x + gate_ref[0] * moe_rows
    a = _rms_mod(x, g_ref[...], sh_ref[0], sc_ref[0])
    proj = _dot(a.astype(BF16), w_ref[...])
    if mode == "even":
        cs_ref = next(it)
        uc_ref, us_ref, gb_ref, gu_ref = next(it), next(it), next(it), next(it)
        uf = proj[:, :FOURIER_WIDTH].astype(BF16)
        cs = cs_ref[...].astype(BF16)
        for hh in range(FOURIER_HEADS):
            cols = slice(hh * LANES, (hh + 1) * LANES)
            r = _dot(uf[:, cols], cs)
            uc_ref[0, :, cols] = r[:, :LANES].astype(BF16)
            us_ref[0, :, cols] = r[:, LANES:].astype(BF16)
        c0 = FOURIER_WIDTH
        gb_ref[0] = proj[:, c0:c0 + CONV_WIDTH]
        gu_ref[0] = proj[:, c0 + CONV_WIDTH:c0 + 2 * CONV_WIDTH] * proj[:, c0 + 2 * CONV_WIDTH:]
    elif mode == "odd":
        cq_ref, sq_ref, ck_ref, sk_ref = next(it), next(it), next(it), next(it)
        hn_ref, up_ref, q_ref, kd_ref, vd_ref = next(it), next(it), next(it), next(it), next(it)
        hn_ref[0] = x
        up_ref[0] = proj[:, :POOL_WIDTH]
        c0 = POOL_WIDTH
        for gi in range(ATTN_WIDTH // LANES):
            cols = slice(c0 + gi * LANES, c0 + (gi + 1) * LANES)
            q_ref[0, :, gi * LANES:(gi + 1) * LANES] = _rope_group(
                proj[:, cols], cq_ref[...], sq_ref[...]).astype(BF16)
        c0 += ATTN_WIDTH
        for gi in range(2 * KV_WIDTH // LANES):
            cols = slice(c0 + gi * LANES, c0 + (gi + 1) * LANES)
            kd_ref[0, :, gi * LANES:(gi + 1) * LANES] = _rope_group(
                proj[:, cols], ck_ref[...], sk_ref[...]).astype(BF16)
        c0 += 2 * KV_WIDTH
        vd_ref[0] = proj[:, c0:].astype(BF16)
    else:
        kv_ref = next(it)
        kv_ref[0] = proj.astype(BF16)
    if add_moe:
        _request_next_rows(step, nsteps, next_pos_ref, moe_hbm, moe_bufs, moe_sems)


def _mod_spec(arr):
    if arr.shape[0] > 1:
        return pl.BlockSpec((1, 1, arr.shape[2]), lambda b, i: (b, 0, 0))
    return pl.BlockSpec((1, 1, arr.shape[2]), lambda b, i: (0, 0, 0))


def _gathered_rows(step, pos_ref, src_hbm, bufs, sems):
    nrows = bufs.shape[1]
    slot = step % 2

    @pl.when(step == 0)
    def _():
        def body(r, carry):
            pltpu.make_async_copy(src_hbm.at[pl.ds(pos_ref[0, 0, r], 1)], bufs.at[0, pl.ds(r, 1)],
                                  sems.at[0]).start()
            return carry
        lax.fori_loop(0, nrows, body, 0, unroll=8)

    pltpu.make_async_copy(src_hbm.at[pl.ds(0, nrows)], bufs.at[slot], sems.at[slot]).wait()
    return bufs[slot]


def _request_next_rows(step, nsteps, next_pos_ref, src_hbm, bufs, sems):
    nrows = bufs.shape[1]
    slot = 1 - step % 2
    for r in range(nrows):
        pltpu.make_async_copy(src_hbm.at[pl.ds(next_pos_ref[0, 0, r], 1)], bufs.at[slot, pl.ds(r, 1)],
                              sems.at[slot]).start(priority=r % 2)

    @pl.when(step == nsteps - 1)
    def _():
        pltpu.make_async_copy(src_hbm.at[pl.ds(0, nrows)], bufs.at[slot], sems.at[slot]).wait()


def _pos_tiles(pos, tm, nt, row0):
    tile0 = row0 // tm
    ntiles = pos.shape[0] // tm
    cur = pl.BlockSpec((1, 1, tm), lambda bb, i: (tile0 + bb * nt + i, 0, 0), memory_space=pltpu.SMEM)
    nxt = pl.BlockSpec((1, 1, tm), lambda bb, i: (jnp.minimum(tile0 + bb * nt + i + 1, ntiles - 1), 0, 0),
                       memory_space=pltpu.SMEM)
    return pos.reshape(ntiles, 1, tm), cur, nxt


def _inproj_call(h, g, shift, scale, w, *, mode, tm, moe=None, pos=None, moe_row0=0, gate=None, extra=()):
    b, n, d = h.shape
    nout = w.shape[1]
    add_moe = moe is not None
    row = lambda bb, i: (bb, i, 0)
    full2 = lambda bb, i: (0, 0)
    args = [h]
    in_specs = [pl.BlockSpec((1, tm, d), row)]
    scratch = []
    if add_moe:
        pos_tiles, pos_spec, next_pos_spec = _pos_tiles(pos, tm, n // tm, moe_row0)
        args += [pos_tiles, pos_tiles, moe, gate]
        in_specs += [pos_spec, next_pos_spec, pl.BlockSpec(memory_space=pl.ANY), _mod_spec(gate)]
        scratch = [pltpu.VMEM((2, tm, d), F32), pltpu.SemaphoreType.DMA((2,))]
    args += [g.reshape(1, d), shift, scale, w]
    in_specs += [pl.BlockSpec((1, d), full2), _mod_spec(shift), _mod_spec(scale),
                 pl.BlockSpec((d, nout), full2)]
    if mode == "even":
        args += list(extra)
        in_specs += [pl.BlockSpec(extra[0].shape, full2)]
        out_shape = [jax.ShapeDtypeStruct((b, n, FOURIER_WIDTH), BF16),
                     jax.ShapeDtypeStruct((b, n, FOURIER_WIDTH), BF16),
                     jax.ShapeDtypeStruct((b, n, CONV_WIDTH), F32),
                     jax.ShapeDtypeStruct((b, n, CONV_WIDTH), F32)]
        out_specs = [pl.BlockSpec((1, tm, 512), row)] * 4
    elif mode == "odd":
        args += list(extra)
        in_specs += [pl.BlockSpec((tm, LANES), lambda bb, i: (i, 0))] * 4
        out_shape = [jax.ShapeDtypeStruct((b, n, d), F32),
                     jax.ShapeDtypeStruct((b, n, POOL_WIDTH), F32),
                     jax.ShapeDtypeStruct((b, n, ATTN_WIDTH), BF16),
                     jax.ShapeDtypeStruct((b, n, 2 * KV_WIDTH), BF16),
                     jax.ShapeDtypeStruct((b, n, 2 * KV_WIDTH), BF16)]
        out_specs = [pl.BlockSpec((1, tm, d), row), pl.BlockSpec((1, tm, POOL_WIDTH), row),
                     pl.BlockSpec((1, tm, ATTN_WIDTH), row),
                     pl.BlockSpec((1, tm, 2 * KV_WIDTH), row), pl.BlockSpec((1, tm, 2 * KV_WIDTH), row)]
    else:
        out_shape = [jax.ShapeDtypeStruct((b, n, nout), BF16)]
        out_specs = [pl.BlockSpec((1, tm, nout), row)]
    return pl.pallas_call(
        functools.partial(_inproj_kernel, mode=mode, add_moe=add_moe),
        grid=(b, n // tm),
        in_specs=in_specs,
        out_specs=out_specs,
        out_shape=out_shape,
        scratch_shapes=scratch,
        compiler_params=_cparams(("arbitrary", "arbitrary")),
        name="inproj_" + mode,
    )(*args)


DFT_RADIX = 64


def _dft_kernel(ca_ref, sa_ref, cb_ref, sb_ref, uc_ref, us_ref, o_ref, c_scr, s_scr, *, norm):
    @pl.when(pl.program_id(1) == 0)
    def _():
        cb, sb = cb_ref[...], sb_ref[...]
        for r in range(ca_ref.shape[0]):
            ca, sa = ca_ref[r:r + 1, :], sa_ref[r:r + 1, :]
            rows = slice(r * DFT_RADIX, (r + 1) * DFT_RADIX)
            c_scr[rows, :] = (ca * cb - sa * sb).astype(BF16)
            s_scr[rows, :] = (-(sa * cb + ca * sb)).astype(BF16)

    acc = _dot(c_scr[...], uc_ref[0]) + _dot(s_scr[...], us_ref[0])
    o_ref[0] = (acc * norm).astype(BF16)


def _dft_call(tabs, uc, us, *, tm):
    b, n, wdt = uc.shape
    r_tile = tm // DFT_RADIX
    norm = 1.0 / math.sqrt(n * FOURIER_HEAD_DIM)
    a_spec = pl.BlockSpec((r_tile, n), lambda i, bb: (i, 0))
    b_spec = pl.BlockSpec((DFT_RADIX, n), lambda i, bb: (0, 0))
    u_spec = pl.BlockSpec((1, n, wdt), lambda i, bb: (bb, 0, 0))
    return pl.pallas_call(
        functools.partial(_dft_kernel, norm=norm),
        grid=(n // tm, b),
        in_specs=[a_spec, a_spec, b_spec, b_spec, u_spec, u_spec],
        out_specs=pl.BlockSpec((1, tm, wdt), lambda i, bb: (bb, i, 0)),
        out_shape=jax.ShapeDtypeStruct((b, n, wdt), BF16),
        scratch_shapes=[pltpu.VMEM((tm, n), BF16), pltpu.VMEM((tm, n), BF16)],
        compiler_params=_cparams(("arbitrary", "arbitrary")),
        name="dft_rows",
    )(*tabs, uc, us)


def _tail(y, h_ref, gate_ref, g2_ref, sh_ref, sc_ref, rwt_ref, rb_ref, hout_ref, fx_ref, bk_ref):
    hn = h_ref[0] + gate_ref[0] * y
    hout_ref[0] = hn
    f = _rms_mod(hn, g2_ref[...], sh_ref[0], sc_ref[0])
    tm = f.shape[0]
    f_hi = f.astype(BF16)
    f_lo = (f - f_hi.astype(F32)).astype(BF16)
    both = _dot_nt(rwt_ref[...], f_hi)
    logits = (both[:N_EXPERTS] + both[N_EXPERTS:]) + _dot_nt(rwt_ref[:N_EXPERTS, :], f_lo)
    aff = jax.nn.sigmoid(logits)
    sel = aff + rb_ref[...]
    cands = []
    for bkt in range(N_BUCKETS):
        lo, hi = _BUCKET_LO[bkt], _BUCKET_HI[bkt]
        cands.append((sel[lo:lo + 1, :] + sel[hi:hi + 1, :], jnp.full((1, tm), float(bkt), F32),
                      aff[lo:lo + 1, :], aff[hi:hi + 1, :]))
    while len(cands) > 1:
        merged = []
        for k in range(0, len(cands) - 1, 2):
            left, right = cands[k], cands[k + 1]
            take_right = right[0] > left[0]
            merged.append(tuple(jnp.where(take_right, r, l) for l, r in zip(left, right)))
        if len(cands) % 2:
            merged.append(cands[-1])
        cands = merged
    _, bucket, a_lo, a_hi = cands[0]
    den = a_lo + a_hi
    info = jnp.concatenate([bucket, a_lo / den, a_hi / den, jnp.zeros((LANES - 3, tm), F32)], axis=0)
    fx_ref[0, :, :D_MODEL] = f
    fx_ref[0, :, D_MODEL:] = info.T
    bk_ref[...] = jnp.concatenate([bucket, jnp.zeros((7, tm), F32)], axis=0)


def _tail_specs(h, gate, shift, scale, tm):
    b, n, d = h.shape
    nt = n // tm
    row = lambda bb, i: (bb, i, 0)
    in_specs = [pl.BlockSpec((1, tm, d), row), _mod_spec(gate),
                pl.BlockSpec((1, d), lambda bb, i: (0, 0)), _mod_spec(shift), _mod_spec(scale),
                pl.BlockSpec((2 * N_EXPERTS, d), lambda bb, i: (0, 0)),
                pl.BlockSpec((N_EXPERTS, 1), lambda bb, i: (0, 0))]
    out_shape = [jax.ShapeDtypeStruct((b, n, d), F32), jax.ShapeDtypeStruct((b, n, ROW_WIDTH), F32),
                 jax.ShapeDtypeStruct((8, b * n), F32)]
    out_specs = [pl.BlockSpec((1, tm, d), row), pl.BlockSpec((1, tm, ROW_WIDTH), row),
                 pl.BlockSpec((8, tm), lambda bb, i: (0, bb * nt + i))]
    return in_specs, out_shape, out_specs


def _even_out_kernel(yf_ref, gb_ref, gu_ref, gp_ref, gn_ref, cw_ref, wo_ref,
                     h_ref, gate_ref, g2_ref, sh_ref, sc_ref, rwt_ref, rb_ref,
                     hout_ref, fx_ref, bk_ref):
    i = pl.program_id(1)
    last = pl.num_programs(1) - 1
    gu = gu_ref[0]
    tm = gu.shape[0]
    prev = jnp.where(i > 0, gp_ref[0, 7:8, :], 0.0)
    nxt = jnp.where(i < last, gn_ref[0, 0:1, :], 0.0)
    row = lax.broadcasted_iota(jnp.int32, gu.shape, 0)
    up = jnp.where(row == 0, prev, pltpu.roll(gu, 1, 0))
    dn = jnp.where(row == tm - 1, nxt, pltpu.roll(gu, tm - 1, 0))
    conv = up * cw_ref[0:1, :] + gu * cw_ref[1:2, :] + dn * cw_ref[2:3, :]
    yc = (gb_ref[0] * conv).astype(BF16)
    y = _dot(yf_ref[0], wo_ref[:FOURIER_WIDTH, :]) + _dot(yc, wo_ref[FOURIER_WIDTH:, :])
    _tail(y, h_ref, gate_ref, g2_ref, sh_ref, sc_ref, rwt_ref, rb_ref, hout_ref, fx_ref, bk_ref)


def _even_out_call(yf, gb, gu, conv_w, w_out, h, gate, g2, shift, scale, rwt, rb, *, tm):
    b, n, d = h.shape
    row = lambda bb, i: (bb, i, 0)
    nb8 = n // 8
    t8 = tm // 8
    tin, out_shape, out_specs = _tail_specs(h, gate, shift, scale, tm)
    in_specs = [pl.BlockSpec((1, tm, 512), row), pl.BlockSpec((1, tm, 512), row),
                pl.BlockSpec((1, tm, 512), row),
                pl.BlockSpec((1, 8, 512), lambda bb, i: (bb, jnp.maximum(i * t8 - 1, 0), 0)),
                pl.BlockSpec((1, 8, 512), lambda bb, i: (bb, jnp.minimum((i + 1) * t8, nb8 - 1), 0)),
                pl.BlockSpec((CONV_K, CONV_WIDTH), lambda bb, i: (0, 0)),
                pl.BlockSpec((d, d), lambda bb, i: (0, 0))] + tin
    return pl.pallas_call(
        _even_out_kernel,
        grid=(b, n // tm),
        in_specs=in_specs,
        out_specs=out_specs,
        out_shape=out_shape,
        compiler_params=_cparams(("arbitrary", "arbitrary")),
        name="even_out",
    )(yf, gb, gu, gu, gu, conv_w, w_out, h, gate, g2.reshape(1, d), shift, scale, rwt, rb)


def _odd_out_kernel(sink_ref, up_ref, upp_ref, upn_ref, q_ref, kc_ref, kp_ref, kn_ref,
                    vc_ref, vp_ref, vn_ref, kvx_ref, pw_ref, ps_ref, wo_ref,
                    h_ref, gate_ref, g2_ref, sh_ref, sc_ref, rwt_ref, rb_ref,
                    hout_ref, fx_ref, bk_ref, ext_ref, mix_ref, *, n_total):
    i = pl.program_id(1)
    last = pl.num_programs(1) - 1
    tq = q_ref.shape[1]
    nsub = tq // ATTN_BLOCK

    u = up_ref[0]
    ext_ref[0:POOL_HALO, :] = jnp.where(i > 0, upp_ref[0], 0.0)
    ext_ref[POOL_HALO:POOL_HALO + tq, :] = u
    ext_ref[POOL_HALO + tq:, :] = jnp.where(i < last, upn_ref[0], 0.0)
    t = i * tq + lax.broadcasted_iota(jnp.int32, (tq, LANES), 0)
    for gi, win in enumerate(POOL_WINDOWS):
        r = win // 2
        cols = slice(gi * LANES, (gi + 1) * LANES)
        acc = ext_ref[POOL_HALO - r:POOL_HALO - r + tq, cols]
        for dlt in range(-r + 1, r + 1):
            acc = acc + ext_ref[POOL_HALO + dlt:POOL_HALO + dlt + tq, cols]
        cnt = (jnp.minimum(t + r + 1, n_total) - jnp.maximum(t - r, 0)).astype(F32)
        p = acc / cnt - u[:, cols]
        y = _dot(p.astype(BF16), pw_ref[gi]) * ps_ref[:, cols]
        mix_ref[:, cols] = y.astype(BF16)

    kwin = jnp.concatenate([kp_ref[0], kc_ref[0], kn_ref[0]], axis=0)
    vwin = jnp.concatenate([vp_ref[0], vc_ref[0], vn_ref[0]], axis=0)
    kvx = kvx_ref[0]
    kx, vx = kvx[:, :2 * KV_WIDTH], kvx[:, 2 * KV_WIDTH:]
    low = lax.broadcasted_iota(jnp.int32, (1, LANES), 1) < HEAD_DIM
    zero = jnp.zeros((), BF16)

    span = 3 * ATTN_BLOCK
    rows4 = 4 * ATTN_BLOCK
    qi = lax.broadcasted_iota(jnp.int32, (rows4, ATTN_BLOCK), 0) % ATTN_BLOCK
    kj = lax.broadcasted_iota(jnp.int32, (rows4, ATTN_BLOCK), 1)
    neg_inf = jnp.float32(-jnp.inf)
    prev_band = jnp.where(kj < qi, neg_inf, 0.0)
    next_band = jnp.where(kj > qi, neg_inf, 0.0)
    row_block = jnp.right_shift(lax.broadcasted_iota(jnp.int32, (rows4, 1), 0), int(math.log2(ATTN_BLOCK)))

    for j in range(nsub):
        blk = i * nsub + j
        prev_bias = prev_band + jnp.where(blk > 0, 0.0, neg_inf)
        next_bias = next_band + jnp.where(blk < (n_total // ATTN_BLOCK) - 1, 0.0, neg_inf)
        r0 = j * ATTN_BLOCK
        for kh in range(N_KV_HEADS):
            kcols = slice(kh * LANES, (kh + 1) * LANES)
            qa = q_ref[0, r0:r0 + ATTN_BLOCK, (2 * kh) * LANES:(2 * kh + 1) * LANES]
            qb = q_ref[0, r0:r0 + ATTN_BLOCK, (2 * kh + 1) * LANES:(2 * kh + 2) * LANES]
            xq = jnp.concatenate([jnp.where(low, qa, zero), jnp.where(low, qb, zero),
                                  jnp.where(low, zero, qa), jnp.where(low, zero, qb)], axis=0)
            h0 = 4 * kh
            snk = LOG2_E * jnp.where(row_block == 0, sink_ref[h0],
                                     jnp.where(row_block == 1, sink_ref[h0 + 2],
                                               jnp.where(row_block == 2, sink_ref[h0 + 1], sink_ref[h0 + 3])))
            s1 = _dot_nt(xq, kwin[r0:r0 + span, kcols])
            s1 = jnp.concatenate([s1[:, :ATTN_BLOCK] + prev_bias, s1[:, ATTN_BLOCK:2 * ATTN_BLOCK],
                                  s1[:, 2 * ATTN_BLOCK:] + next_bias], axis=1)
            s2 = _dot_nt(xq, kx[:, kcols])
            m = jnp.maximum(jnp.maximum(jnp.max(s1, axis=-1, keepdims=True),
                                        jnp.max(s2, axis=-1, keepdims=True)), snk)
            e1 = jnp.exp2(s1 - m)
            e2 = jnp.exp2(s2 - m)
            den = (jnp.sum(e1, axis=-1, keepdims=True) + jnp.sum(e2, axis=-1, keepdims=True)
                   + jnp.exp2(snk - m))
            o = _dot(e1.astype(BF16), vwin[r0:r0 + span, kcols]) + _dot(e2.astype(BF16), vx[:, kcols])
            o = o / den
            for pr in range(2):
                o_low = o[pr * ATTN_BLOCK:(pr + 1) * ATTN_BLOCK]
                o_high = o[(2 + pr) * ATTN_BLOCK:(3 + pr) * ATTN_BLOCK]
                c0 = POOL_WIDTH + (2 * kh + pr) * LANES
                mix_ref[r0:r0 + ATTN_BLOCK, c0:c0 + LANES] = jnp.where(low, o_low, o_high).astype(BF16)

    y = _dot(mix_ref[...], wo_ref[...])
    _tail(y, h_ref, gate_ref, g2_ref, sh_ref, sc_ref, rwt_ref, rb_ref, hout_ref, fx_ref, bk_ref)


def _odd_out_call(sink, up, q, kd, vd, kvx, pool_w, pool_scale, w_out,
                  h, gate, g2, shift, scale, rwt, rb, *, tq):
    b, n, d = h.shape
    row = lambda bb, i: (bb, i, 0)
    nb8, t8 = n // POOL_HALO, tq // POOL_HALO
    nbk, tk = n // ATTN_BLOCK, tq // ATTN_BLOCK
    prev8 = lambda bb, i: (bb, jnp.maximum(i * t8 - 1, 0), 0)
    next8 = lambda bb, i: (bb, jnp.minimum((i + 1) * t8, nb8 - 1), 0)
    prevk = lambda bb, i: (bb, jnp.maximum(i * tk - 1, 0), 0)
    nextk = lambda bb, i: (bb, jnp.minimum((i + 1) * tk, nbk - 1), 0)
    kvw = 2 * KV_WIDTH
    tin, out_shape, out_specs = _tail_specs(h, gate, shift, scale, tq)
    in_specs = [pl.BlockSpec(memory_space=pltpu.SMEM),
                pl.BlockSpec((1, tq, POOL_WIDTH), row),
                pl.BlockSpec((1, POOL_HALO, POOL_WIDTH), prev8),
                pl.BlockSpec((1, POOL_HALO, POOL_WIDTH), next8),
                pl.BlockSpec((1, tq, ATTN_WIDTH), row),
                pl.BlockSpec((1, tq, kvw), row),
                pl.BlockSpec((1, ATTN_BLOCK, kvw), prevk),
                pl.BlockSpec((1, ATTN_BLOCK, kvw), nextk),
                pl.BlockSpec((1, tq, kvw), row),
                pl.BlockSpec((1, ATTN_BLOCK, kvw), prevk),
                pl.BlockSpec((1, ATTN_BLOCK, kvw), nextk),
                pl.BlockSpec((1, kvx.shape[1], 2 * kvw), lambda bb, i: (bb, 0, 0)),
                pl.BlockSpec(pool_w.shape, lambda bb, i: (0, 0, 0)),
                pl.BlockSpec((1, POOL_WIDTH), lambda bb, i: (0, 0)),
                pl.BlockSpec((d, d), lambda bb, i: (0, 0))] + tin
    return pl.pallas_call(
        functools.partial(_odd_out_kernel, n_total=n),
        grid=(b, n // tq),
        in_specs=in_specs,
        out_specs=out_specs,
        out_shape=out_shape,
        scratch_shapes=[pltpu.VMEM((tq + 2 * POOL_HALO, POOL_WIDTH), F32),
                        pltpu.VMEM((tq, d), BF16)],
        compiler_params=_cparams(("arbitrary", "arbitrary")),
        name="odd_out",
    )(sink, up, up, up, q, kd, kd, kd, vd, vd, vd, kvx, pool_w, pool_scale.reshape(1, POOL_WIDTH),
      w_out, h, gate, g2.reshape(1, d), shift, scale, rwt, rb)


def _moe_kernel(e_lo_ref, e_hi_ref, valid_ref, x_ref,
                g1_ref, u1_ref, d1_ref, g2_ref, u2_ref, d2_ref, o_ref):
    j = pl.program_id(0)

    @pl.when(valid_ref[j] != 0)
    def _():
        x = x_ref[:, :D_MODEL].astype(BF16)
        w_lo = x_ref[:, D_MODEL + INFO_W_LO:D_MODEL + INFO_W_LO + 1]
        w_hi = x_ref[:, D_MODEL + INFO_W_HI:D_MODEL + INFO_W_HI + 1]

        def expert(g_ref, u_ref, d_ref):
            gate = _dot(x, g_ref[0, 0].astype(BF16))
            hid = gate * jax.nn.sigmoid(gate) * _dot(x, u_ref[0, 0].astype(BF16))
            return _dot(hid.astype(BF16), d_ref[0, 0].astype(BF16))

        o_lo = expert(g1_ref, u1_ref, d1_ref)
        o_hi = expert(g2_ref, u2_ref, d2_ref)
        o_ref[...] = w_lo * o_lo + w_hi * o_hi

    @pl.when(valid_ref[j] == 0)
    def _():
        o_ref[...] = jnp.zeros(o_ref.shape, o_ref.dtype)


def _moe_call(tile_lo, tile_hi, tile_valid, xs, wg, wu, wd, *, layer, tm):
    p = xs.shape[0]
    d = D_MODEL
    ntiles = p // tm
    lo4 = lambda j, lo, hi, v: (layer, lo[j], 0, 0)
    hi4 = lambda j, lo, hi, v: (layer, hi[j], 0, 0)
    rowm = lambda j, lo, hi, v: (j, 0)
    row_in = lambda j, lo, hi, v: (jnp.where(v[j] != 0, j, 0), 0)
    grid_spec = pltpu.PrefetchScalarGridSpec(
        num_scalar_prefetch=3,
        grid=(ntiles,),
        in_specs=[pl.BlockSpec((tm, ROW_WIDTH), row_in),
                  pl.BlockSpec((1, 1, d, D_EXPERT), lo4), pl.BlockSpec((1, 1, d, D_EXPERT), lo4),
                  pl.BlockSpec((1, 1, D_EXPERT, d), lo4),
                  pl.BlockSpec((1, 1, d, D_EXPERT), hi4), pl.BlockSpec((1, 1, d, D_EXPERT), hi4),
                  pl.BlockSpec((1, 1, D_EXPERT, d), hi4)],
        out_specs=pl.BlockSpec((tm, d), rowm),
    )
    return pl.pallas_call(
        _moe_kernel,
        grid_spec=grid_spec,
        out_shape=jax.ShapeDtypeStruct((p, d), F32),
        compiler_params=_cparams(("arbitrary",)),
        name="moe_pairs",
    )(tile_lo, tile_hi, tile_valid, xs, wg, wu, wd, wg, wu, wd)


RANK_ROWS = 32


def _rank_kernel(bk_ref, tri_ref, rk_ref, cnt_ref, carry_ref):
    @pl.when(pl.program_id(0) == 0)
    def _():
        carry_ref[...] = jnp.zeros(carry_ref.shape, F32)

    tr = bk_ref.shape[1]
    bucket = bk_ref[0:1, :]
    rows = lax.broadcasted_iota(jnp.int32, (RANK_ROWS, tr), 0).astype(F32)
    onehot = jnp.where(rows == bucket, 1.0, 0.0)
    before = _dot(onehot.astype(BF16), tri_ref[...]) + carry_ref[:, 0:1]
    rank = jnp.sum(onehot * before, axis=0, keepdims=True)
    rk_ref[...] = jnp.broadcast_to(rank, rk_ref.shape)
    carry_ref[...] = carry_ref[...] + jnp.sum(onehot, axis=1, keepdims=True)
    cnt_ref[...] = carry_ref[...]


def _rank_call(bk, *, tr):
    t = bk.shape[1]
    tri = jnp.asarray(np.triu(np.ones((tr, tr), np.float32), 1), BF16)
    return pl.pallas_call(
        _rank_kernel,
        grid=(t // tr,),
        in_specs=[pl.BlockSpec((8, tr), lambda j: (0, j)), pl.BlockSpec((tr, tr), lambda j: (0, 0))],
        out_specs=[pl.BlockSpec((8, tr), lambda j: (0, j)), pl.BlockSpec((RANK_ROWS, LANES), lambda j: (0, 0))],
        out_shape=[jax.ShapeDtypeStruct((8, t), F32), jax.ShapeDtypeStruct((RANK_ROWS, LANES), F32)],
        scratch_shapes=[pltpu.VMEM((RANK_ROWS, LANES), F32)],
        compiler_params=_cparams(("arbitrary",)),
        name="bucket_rank",
    )(bk, tri)


ROW_TILE = 512


def _scatter_rows_kernel(tile_end_ref, pos_ref, *refs, tile_starts, tm):
    nsrc = len(tile_starts)
    srcs, out_hbm, zero_ref, sem = refs[:nsrc], refs[nsrc], refs[nsrc + 1], refs[nsrc + 2]
    j = pl.program_id(0)

    @pl.when(j == 0)
    def _():
        zero_ref[...] = jnp.zeros(zero_ref.shape, F32)

        def fill(tile, start):
            copy = pltpu.make_async_copy(zero_ref, out_hbm.at[pl.ds(tile * tm, tm)], sem)
            copy.start() if start else copy.wait()

        n_slot_tiles = out_hbm.shape[0] // tm
        used = tile_end_ref[N_BUCKETS - 1]
        for start in (True, False):
            for b in range(N_BUCKETS):
                first_tile = tile_end_ref[b - 1] if b else 0
                pl.when(tile_end_ref[b] > first_tile)(functools.partial(fill, tile_end_ref[b] - 1, start))
                pl.when(n_slot_tiles - 1 - b >= used)(functools.partial(fill, n_slot_tiles - 1 - b, start))

    def move(src_vmem):
        for r in range(ROW_TILE):
            pltpu.make_async_copy(src_vmem.at[pl.ds(r, 1)],
                                  out_hbm.at[pl.ds(pos_ref[0, 0, r], 1)], sem).start(priority=r % 2)
        pltpu.make_async_copy(src_vmem, out_hbm.at[pl.ds(0, ROW_TILE)], sem).wait()

    for s in range(nsrc):
        first = tile_starts[s]
        if nsrc == 1:
            move(srcs[s])
        else:
            in_range = (j >= first) if s == nsrc - 1 else ((j >= first) & (j < tile_starts[s + 1]))
            pl.when(in_range)(functools.partial(move, srcs[s]))


def _scatter_rows_call(pos, tile_end, sources, nslots, tm):
    w = sources[0].shape[1]
    t = pos.shape[0]
    ntiles = t // ROW_TILE
    tile_starts, src_specs, acc = [], [], 0
    for s in sources:
        first, count = acc, s.shape[0] // ROW_TILE
        tile_starts.append(first)
        src_specs.append(pl.BlockSpec(
            (ROW_TILE, w), lambda j, te, first=first, count=count: (jnp.clip(j - first, 0, count - 1), 0)))
        acc += count
    grid_spec = pltpu.PrefetchScalarGridSpec(
        num_scalar_prefetch=1,
        grid=(ntiles,),
        in_specs=[pl.BlockSpec((1, 1, ROW_TILE), lambda j, te: (j, 0, 0), memory_space=pltpu.SMEM)] + src_specs,
        out_specs=pl.BlockSpec(memory_space=pl.ANY),
        scratch_shapes=[pltpu.VMEM((tm, w), F32), pltpu.SemaphoreType.DMA(())],
    )
    return pl.pallas_call(
        functools.partial(_scatter_rows_kernel, tile_starts=tuple(tile_starts), tm=tm),
        grid_spec=grid_spec,
        out_shape=jax.ShapeDtypeStruct((nslots, w), F32),
        compiler_params=_cparams(("arbitrary",)),
        name="scatter_rows",
    )(tile_end, pos.reshape(ntiles, 1, ROW_TILE), *sources)


def _final_kernel(h_ref, pos_ref, next_pos_ref, moe_hbm, gate_ref, g_ref, o_ref, moe_bufs, moe_sems):
    step = pl.program_id(0) * pl.num_programs(1) + pl.program_id(1)
    nsteps = pl.num_programs(0) * pl.num_programs(1)
    x = h_ref[0] + gate_ref[0] * _gathered_rows(step, pos_ref, moe_hbm, moe_bufs, moe_sems)
    o_ref[0] = x * lax.rsqrt(jnp.mean(x * x, axis=-1, keepdims=True) + EPS) * g_ref[...]
    _request_next_rows(step, nsteps, next_pos_ref, moe_hbm, moe_bufs, moe_sems)


def _final_call(h, moe, pos, gate, g, *, tm):
    b, n, d = h.shape
    row = lambda bb, i: (bb, i, 0)
    pos_tiles, pos_spec, next_pos_spec = _pos_tiles(pos, tm, n // tm, 0)
    return pl.pallas_call(
        _final_kernel,
        grid=(b, n // tm),
        in_specs=[pl.BlockSpec((1, tm, d), row), pos_spec, next_pos_spec, pl.BlockSpec(memory_space=pl.ANY),
                  _mod_spec(gate), pl.BlockSpec((1, d), lambda bb, i: (0, 0))],
        out_specs=pl.BlockSpec((1, tm, d), row),
        out_shape=jax.ShapeDtypeStruct((b, n, d), F32),
        scratch_shapes=[pltpu.VMEM((2, tm, d), F32), pltpu.SemaphoreType.DMA((2,))],
        compiler_params=_cparams(("arbitrary", "arbitrary")),
        name="final_norm",
    )(h, pos_tiles, pos_tiles, moe, gate, g.reshape(1, d))


def _channel_dft_table():
    c = np.arange(FOURIER_HEAD_DIM)
    ang = 2.0 * np.pi * ((c[:, None] * c[None, :]) % FOURIER_HEAD_DIM) / FOURIER_HEAD_DIM
    return jnp.asarray(np.concatenate([np.cos(ang), np.sin(ang)], axis=1), F32)


def _position_dft_tables(n):
    n1 = n // DFT_RADIX
    t = np.arange(n)
    a = 2.0 * np.pi * ((np.arange(n1)[:, None] * t[None, :]) % n1) / n1
    bb = 2.0 * np.pi * ((np.arange(DFT_RADIX)[:, None] * t[None, :]) % n) / n
    return tuple(jnp.asarray(v, F32) for v in (np.cos(a), np.sin(a), np.cos(bb), np.sin(bb)))


def _rope_tables(n):
    quarter = HEAD_DIM // 4
    inv = ROPE_THETA ** (-jnp.arange(quarter, dtype=F32) / quarter)
    t = jnp.arange(n)
    ang_r = (t // GRID_W).astype(F32)[:, None] * inv
    ang_c = (t % GRID_W).astype(F32)[:, None] * inv
    cos = jnp.concatenate([jnp.cos(ang_r)] * 2 + [jnp.cos(ang_c)] * 2, axis=1)
    sin = jnp.concatenate([-jnp.sin(ang_r), jnp.sin(ang_r), -jnp.sin(ang_c), jnp.sin(ang_c)], axis=1)
    return jnp.tile(cos, (1, 2)), jnp.tile(sin, (1, 2))


def _dispatch_plan(bucket, rank, counts, tm):
    t = bucket.shape[0]
    ntiles = t // tm + N_BUCKETS
    tiles_per = (counts + tm - 1) // tm
    tile_end = jnp.cumsum(tiles_per)
    tile_start = tile_end - tiles_per
    onehot = bucket[:, None] == jnp.arange(N_BUCKETS, dtype=jnp.int32)[None, :]
    pos = jnp.sum(jnp.where(onehot, (tile_start * tm)[None, :], 0), axis=-1) + rank
    tile_ids = jnp.arange(ntiles, dtype=jnp.int32)
    used = tile_end[-1]
    tile_bucket = jnp.sum((tile_ids[:, None] >= tile_end[None, :]).astype(jnp.int32), axis=1)
    last_bucket = jnp.sum((jnp.maximum(used - 1, 0) >= tile_end).astype(jnp.int32))
    tile_valid = (tile_ids < used).astype(jnp.int32)
    tile_bucket = jnp.where(tile_valid == 1, tile_bucket, last_bucket)
    tile_lo = jnp.asarray(np.asarray(_BUCKET_LO, np.int32))[tile_bucket]
    tile_hi = jnp.asarray(np.asarray(_BUCKET_HI, np.int32))[tile_bucket]
    return pos.astype(jnp.int32), tile_end.astype(jnp.int32), tile_lo, tile_hi, tile_valid, ntiles * tm


def _moe_layer(fx_list, bk_list, wg, wu, wd, *, layer, tm):
    bk = bk_list[0] if len(bk_list) == 1 else jnp.concatenate(bk_list, axis=1)
    rk, cnt = _rank_call(bk, tr=ROW_TILE)
    pos, tile_end, tile_lo, tile_hi, tile_valid, nslots = _dispatch_plan(
        bk[0].astype(jnp.int32), rk[0].astype(jnp.int32), cnt[:N_BUCKETS, 0].astype(jnp.int32), tm)
    xs = _scatter_rows_call(pos, tile_end, fx_list, nslots, tm)
    return _moe_call(tile_lo, tile_hi, tile_valid, xs, wg, wu, wd, layer=layer, tm=tm), pos


def _forward(x, c, ctx, c_ctx, ada_w, ada_b, norm_mix_g, norm_ffn_g, even_w_in, even_conv_w, even_w_out,
             odd_w_in, odd_pool_w, odd_pool_scale, odd_sink, odd_w_out, router_w, router_b,
             moe_w_gate, moe_w_up, moe_w_down, final_g, *, tm_lat, tm_ctx, tq, tm_dft, tm_moe):
    b, n, d = x.shape
    l = ctx.shape[1]

    rows = ((b + 1 + 7) // 8) * 8
    s_rows = jnp.zeros((rows, d), F32).at[:b].set(c).at[b].set(c_ctx)
    mods = _ada_call(s_rows, ada_w, ada_b)

    def mod_vecs(layer):
        m = mods[layer, :b].reshape(b, N_MOD, 1, d)
        mc = mods[layer, b].reshape(N_MOD, 1, 1, d)
        return [m[:, k] for k in range(N_MOD)], [mc[k] for k in range(N_MOD)]

    rw_t = router_w.T
    rw_hi = rw_t.astype(BF16)
    rw_lo = (rw_t - rw_hi.astype(F32)).astype(BF16)
    rwt = jnp.concatenate([rw_hi, rw_lo], axis=0)
    rb = router_b.astype(F32).reshape(N_EXPERTS, 1)
    cs_tab = _channel_dft_table()

    m, mc = mod_vecs(0)
    w_in0 = even_w_in[0].astype(BF16)
    w_out0 = even_w_out[0].astype(BF16)

    def even_stream(h, mv, tm):
        nn = h.shape[1]
        uc, us, gb, gu = _inproj_call(h, norm_mix_g[0], mv[0], mv[1], w_in0, mode="even", tm=tm,
                                      extra=(cs_tab,))
        yf = _dft_call(_position_dft_tables(nn), uc, us, tm=min(tm_dft, nn))
        return _even_out_call(yf, gb, gu, even_conv_w[0], w_out0, h, mv[2], norm_ffn_g[0],
                              mv[3], mv[4], rwt, rb, tm=tm)

    h1, fx_lat, bk_lat = even_stream(x, m, tm_lat)
    hc1, fx_ctx, bk_ctx = even_stream(ctx, mc, tm_ctx)

    moe0, pos0 = _moe_layer([fx_lat.reshape(b * n, ROW_WIDTH), fx_ctx.reshape(b * l, ROW_WIDTH)],
                            [bk_lat, bk_ctx], moe_w_gate, moe_w_up, moe_w_down, layer=0, tm=tm_moe)
    gate_lat0, gate_ctx0 = m[5], mc[5]

    m, mc = mod_vecs(1)
    w_in1 = odd_w_in[0]
    kv0 = POOL_WIDTH + ATTN_WIDTH
    wk, wv = w_in1[:, kv0:kv0 + KV_WIDTH], w_in1[:, kv0 + KV_WIDTH:]

    def dup_heads(wm):
        return jnp.concatenate([wm[:, :HEAD_DIM], wm[:, :HEAD_DIM], wm[:, HEAD_DIM:], wm[:, HEAD_DIM:]], axis=1)

    w_kv_dup = jnp.concatenate([dup_heads(wk), dup_heads(wv)], axis=1)
    w_lat1 = jnp.concatenate([w_in1[:, :kv0], w_kv_dup], axis=1).astype(BF16)
    w_out1 = odd_w_out[0].astype(BF16)

    cos_t, sin_t = _rope_tables(n)
    q_scale = HEAD_DIM ** -0.5 * LOG2_E
    h1b, up, q, kd, vd = _inproj_call(h1, norm_mix_g[1], m[0], m[1], w_lat1, mode="odd", tm=tm_lat,
                                      moe=moe0, pos=pos0, moe_row0=0, gate=gate_lat0,
                                      extra=(cos_t * q_scale, sin_t * q_scale, cos_t, sin_t))
    (kvx,) = _inproj_call(hc1, norm_mix_g[1], mc[0], mc[1], w_kv_dup.astype(BF16), mode="plain", tm=tm_ctx,
                          moe=moe0, pos=pos0, moe_row0=b * n, gate=gate_ctx0)
    h2, fx2, bk2 = _odd_out_call(odd_sink[0], up, q, kd, vd, kvx, odd_pool_w[0].astype(BF16),
                                 odd_pool_scale[0], w_out1, h1b, m[2], norm_ffn_g[1], m[3], m[4], rwt, rb, tq=tq)
    moe1, pos1 = _moe_layer([fx2.reshape(b * n, ROW_WIDTH)], [bk2], moe_w_gate, moe_w_up, moe_w_down,
                            layer=1, tm=tm_moe)
    return _final_call(h2, moe1, pos1, m[5], final_g, tm=tm_lat)


def kernel(x, c, ctx, c_ctx, ada_w, ada_b, norm_mix_g, norm_ffn_g, even_w_in, even_conv_w, even_w_out,
           odd_w_in, odd_pool_w, odd_pool_scale, odd_sink, odd_w_out, router_w, router_b,
           moe_w_gate, moe_w_up, moe_w_down, final_g):
    return _forward(x, c, ctx, c_ctx, ada_w, ada_b, norm_mix_g, norm_ffn_g, even_w_in, even_conv_w,
                    even_w_out, odd_w_in, odd_pool_w, odd_pool_scale, odd_sink, odd_w_out, router_w,
                    router_b, moe_w_gate, moe_w_up, moe_w_down, final_g,
                    tm_lat=512, tm_ctx=256, tq=512, tm_dft=1024, tm_moe=512)
```

```python
import functools
import math

import numpy as np
import jax
import jax.numpy as jnp
from jax import lax
from jax.experimental import pallas as pl
from jax.experimental.pallas import tpu as pltpu

F32 = jnp.float32
BF16 = jnp.bfloat16

D_MODEL = 1024
GRID_W = 64
EPS = 1e-6
N_MOD = 6
FOURIER_HEADS = 4
FOURIER_HEAD_DIM = 128
FOURIER_WIDTH = 512
CONV_WIDTH = 512
CONV_K = 3
POOL_WINDOWS = (2, 4, 8, 16)
POOL_GROUP_DIM = 128
POOL_WIDTH = 512
POOL_HALO = 8
HEAD_DIM = 64
N_Q_HEADS = 8
N_KV_HEADS = 2
ATTN_WIDTH = 512
KV_WIDTH = 128
ATTN_BLOCK = 128
ROPE_THETA = 10000.0
N_EXPERTS = 16
N_GROUPS = 4
EXPERTS_PER_GROUP = 4
D_EXPERT = 512
N_PAIRS = 6
N_BUCKETS = N_GROUPS * N_PAIRS

_PAIRS = [(a, b) for a in range(EXPERTS_PER_GROUP) for b in range(a + 1, EXPERTS_PER_GROUP)]
_BUCKET_LO = [(k // N_PAIRS) * EXPERTS_PER_GROUP + _PAIRS[k % N_PAIRS][0] for k in range(N_BUCKETS)]
_BUCKET_HI = [(k // N_PAIRS) * EXPERTS_PER_GROUP + _PAIRS[k % N_PAIRS][1] for k in range(N_BUCKETS)]

LANES = 128
ROW_WIDTH = D_MODEL + LANES
INFO_BUCKET, INFO_W_LO, INFO_W_HI = 0, 1, 2
VMEM_LIMIT_BYTES = 48 * 1024 * 1024

HIGHEST = lax.Precision.HIGHEST
LOG2_E = math.log2(math.e)


def _cparams(sem):
    return pltpu.CompilerParams(dimension_semantics=sem, vmem_limit_bytes=VMEM_LIMIT_BYTES)


def _rms_mod(x, g, shift, scale):
    y = x * lax.rsqrt(jnp.mean(x * x, axis=-1, keepdims=True) + EPS) * g
    return y * (1.0 + scale) + shift


def _dot(a, b):
    return jnp.dot(a, b, preferred_element_type=F32)


def _dot_nt(a, b):
    return lax.dot_general(a, b, (((1,), (1,)), ((), ())), preferred_element_type=F32)


def _ada_kernel(s_ref, w_ref, b_ref, o_ref):
    s = s_ref[...]
    s = s * jax.nn.sigmoid(s)
    o_ref[0] = jnp.dot(s, w_ref[0], preferred_element_type=F32, precision=HIGHEST) + b_ref[0]


def _ada_call(s_rows, ada_w, ada_b):
    depth, d, n6 = ada_w.shape
    r = s_rows.shape[0]
    tn = 1536
    return pl.pallas_call(
        _ada_kernel,
        grid=(depth, n6 // tn),
        in_specs=[
            pl.BlockSpec((r, d), lambda l, j: (0, 0)),
            pl.BlockSpec((1, d, tn), lambda l, j: (l, 0, j)),
            pl.BlockSpec((1, 1, tn), lambda l, j: (l, 0, j)),
        ],
        out_specs=pl.BlockSpec((1, r, tn), lambda l, j: (l, 0, j)),
        out_shape=jax.ShapeDtypeStruct((depth, r, n6), F32),
        compiler_params=_cparams(("arbitrary", "arbitrary")),
        name="ada_mod",
    )(s_rows, ada_w, ada_b.reshape(depth, 1, n6))


def _rope_group(x, cos, sin_signed):
    lane = lax.broadcasted_iota(jnp.int32, x.shape, 1)
    first_half = (lane % 32) < 16
    partner = jnp.where(first_half, pltpu.roll(x, LANES - 16, 1), pltpu.roll(x, 16, 1))
    return x * cos + partner * sin_signed


def _inproj_kernel(*refs, mode, add_moe):
    it = iter(refs)
    h_ref = next(it)
    if add_moe:
        pos_ref, next_pos_ref, moe_hbm, gate_ref = next(it), next(it), next(it), next(it)
        moe_bufs, moe_sems = refs[-2], refs[-1]
        step = pl.program_id(0) * pl.num_programs(1) + pl.program_id(1)
        nsteps = pl.num_programs(0) * pl.num_programs(1)
        moe_rows = _gathered_rows(step, pos_ref, moe_hbm, moe_bufs, moe_sems)
    g_ref, sh_ref, sc_ref, w_ref = next(it), next(it), next(it), next(it)
    x = h_ref[0]
    if add_moe:
        x = x + gate_ref[0] * moe_rows
    a = _rms_mod(x, g_ref[...], sh_ref[0], sc_ref[0])
    proj = _dot(a.astype(BF16), w_ref[...])
    if mode == "even":
        cs_ref = next(it)
        uc_ref, us_ref, gb_ref, gu_ref = next(it), next(it), next(it), next(it)
        uf = proj[:, :FOURIER_WIDTH].astype(BF16)
        cs = cs_ref[...].astype(BF16)
        for hh in range(FOURIER_HEADS):
            cols = slice(hh * LANES, (hh + 1) * LANES)
            r = _dot(uf[:, cols], cs)
            uc_ref[0, :, cols] = r[:, :LANES].astype(BF16)
            us_ref[0, :, cols] = r[:, LANES:].astype(BF16)
        c0 = FOURIER_WIDTH
        gb_ref[0] = proj[:, c0:c0 + CONV_WIDTH]
        gu_ref[0] = proj[:, c0 + CONV_WIDTH:c0 + 2 * CONV_WIDTH] * proj[:, c0 + 2 * CONV_WIDTH:]
    elif mode == "odd":
        cq_ref, sq_ref, ck_ref, sk_ref = next(it), next(it), next(it), next(it)
        hn_ref, up_ref, q_ref, kd_ref, vd_ref = next(it), next(it), next(it), next(it), next(it)
        hn_ref[0] = x
        up_ref[0] = proj[:, :POOL_WIDTH]
        c0 = POOL_WIDTH
        for gi in range(ATTN_WIDTH // LANES):
            cols = slice(c0 + gi * LANES, c0 + (gi + 1) * LANES)
            q_ref[0, :, gi * LANES:(gi + 1) * LANES] = _rope_group(
                proj[:, cols], cq_ref[...], sq_ref[...]).astype(BF16)
        c0 += ATTN_WIDTH
        for gi in range(2 * KV_WIDTH // LANES):
            cols = slice(c0 + gi * LANES, c0 + (gi + 1) * LANES)
            kd_ref[0, :, gi * LANES:(gi + 1) * LANES] = _rope_group(
                proj[:, cols], ck_ref[...], sk_ref[...]).astype(BF16)
        c0 += 2 * KV_WIDTH
        vd_ref[0] = proj[:, c0:].astype(BF16)
    else:
        kv_ref = next(it)
        kv_ref[0] = proj.astype(BF16)
    if add_moe:
        _request_next_rows(step, nsteps, next_pos_ref, moe_hbm, moe_bufs, moe_sems)


def _mod_spec(arr):
    if arr.shape[0] > 1:
        return pl.BlockSpec((1, 1, arr.shape[2]), lambda b, i: (b, 0, 0))
    return pl.BlockSpec((1, 1, arr.shape[2]), lambda b, i: (0, 0, 0))


def _gathered_rows(step, pos_ref, src_hbm, bufs, sems):
    nrows = bufs.shape[1]
    slot = step % 2

    @pl.when(step == 0)
    def _():
        def body(r, carry):
            pltpu.make_async_copy(src_hbm.at[pl.ds(pos_ref[0, 0, r], 1)], bufs.at[0, pl.ds(r, 1)],
                                  sems.at[0]).start()
            return carry
        lax.fori_loop(0, nrows, body, 0, unroll=8)

    pltpu.make_async_copy(src_hbm.at[pl.ds(0, nrows)], bufs.at[slot], sems.at[slot]).wait()
    return bufs[slot]


def _request_next_rows(step, nsteps, next_pos_ref, src_hbm, bufs, sems):
    nrows = bufs.shape[1]
    slot = 1 - step % 2
    for r in range(nrows):
        pltpu.make_async_copy(src_hbm.at[pl.ds(next_pos_ref[0, 0, r], 1)], bufs.at[slot, pl.ds(r, 1)],
                              sems.at[slot]).start(priority=r % 2)

    @pl.when(step == nsteps - 1)
    def _():
        pltpu.make_async_copy(src_hbm.at[pl.ds(0, nrows)], bufs.at[slot], sems.at[slot]).wait()


def _pos_tiles(pos, tm, nt, row0):
    tile0 = row0 // tm
    ntiles = pos.shape[0] // tm
    cur = pl.BlockSpec((1, 1, tm), lambda bb, i: (tile0 + bb * nt + i, 0, 0), memory_space=pltpu.SMEM)
    nxt = pl.BlockSpec((1, 1, tm), lambda bb, i: (jnp.minimum(tile0 + bb * nt + i + 1, ntiles - 1), 0, 0),
                       memory_space=pltpu.SMEM)
    return pos.reshape(ntiles, 1, tm), cur, nxt


def _inproj_call(h, g, shift, scale, w, *, mode, tm, moe=None, pos=None, moe_row0=0, gate=None, extra=()):
    b, n, d = h.shape
    nout = w.shape[1]
    add_moe = moe is not None
    row = lambda bb, i: (bb, i, 0)
    full2 = lambda bb, i: (0, 0)
    args = [h]
    in_specs = [pl.BlockSpec((1, tm, d), row)]
    scratch = []
    if add_moe:
        pos_tiles, pos_spec, next_pos_spec = _pos_tiles(pos, tm, n // tm, moe_row0)
        args += [pos_tiles, pos_tiles, moe, gate]
        in_specs += [pos_spec, next_pos_spec, pl.BlockSpec(memory_space=pl.ANY), _mod_spec(gate)]
        scratch = [pltpu.VMEM((2, tm, d), F32), pltpu.SemaphoreType.DMA((2,))]
    args += [g.reshape(1, d), shift, scale, w]
    in_specs += [pl.BlockSpec((1, d), full2), _mod_spec(shift), _mod_spec(scale),
                 pl.BlockSpec((d, nout), full2)]
    if mode == "even":
        args += list(extra)
        in_specs += [pl.BlockSpec(extra[0].shape, full2)]
        out_shape = [jax.ShapeDtypeStruct((b, n, FOURIER_WIDTH), BF16),
                     jax.ShapeDtypeStruct((b, n, FOURIER_WIDTH), BF16),
                     jax.ShapeDtypeStruct((b, n, CONV_WIDTH), F32),
                     jax.ShapeDtypeStruct((b, n, CONV_WIDTH), F32)]
        out_specs = [pl.BlockSpec((1, tm, 512), row)] * 4
    elif mode == "odd":
        args += list(extra)
        in_specs += [pl.BlockSpec((tm, LANES), lambda bb, i: (i, 0))] * 4
        out_shape = [jax.ShapeDtypeStruct((b, n, d), F32),
                     jax.ShapeDtypeStruct((b, n, POOL_WIDTH), F32),
                     jax.ShapeDtypeStruct((b, n, ATTN_WIDTH), BF16),
                     jax.ShapeDtypeStruct((b, n, 2 * KV_WIDTH), BF16),
                     jax.ShapeDtypeStruct((b, n, 2 * KV_WIDTH), BF16)]
        out_specs = [pl.BlockSpec((1, tm, d), row), pl.BlockSpec((1, tm, POOL_WIDTH), row),
                     pl.BlockSpec((1, tm, ATTN_WIDTH), row),
                     pl.BlockSpec((1, tm, 2 * KV_WIDTH), row), pl.BlockSpec((1, tm, 2 * KV_WIDTH), row)]
    else:
        out_shape = [jax.ShapeDtypeStruct((b, n, nout), BF16)]
        out_specs = [pl.BlockSpec((1, tm, nout), row)]
    return pl.pallas_call(
        functools.partial(_inproj_kernel, mode=mode, add_moe=add_moe),
        grid=(b, n // tm),
        in_specs=in_specs,
        out_specs=out_specs,
        out_shape=out_shape,
        scratch_shapes=scratch,
        compiler_params=_cparams(("arbitrary", "arbitrary")),
        name="inproj_" + mode,
    )(*args)


DFT_RADIX = 64


BF16_SUBLANES = 16


def _fold_kernel(rev_ref, uc_ref, ucm_ref, ucx_ref, us_ref, usm_ref, usx_ref, ue_ref, uo_ref):
    j = pl.program_id(1)
    tf = uc_ref.shape[1]
    first = lax.broadcasted_iota(jnp.int32, (tf, 1), 0) == 0

    def mirrored(m_ref, x_ref):
        return jnp.where(first, x_ref[0, 0:1, :].astype(F32), _dot(rev_ref[...], m_ref[0]))

    weight = jnp.where(first, jnp.where(j == 0, 0.5, 1.0), 1.0)
    ue_ref[0] = (weight * (uc_ref[0].astype(F32) + mirrored(ucm_ref, ucx_ref))).astype(BF16)
    uo_ref[0] = (us_ref[0].astype(F32) - mirrored(usm_ref, usx_ref)).astype(BF16)


def _fold_call(uc, us):
    b, n, wdt = uc.shape
    half = n // 2
    tf = min(256, half)
    nblk = n // tf
    rev = np.zeros((tf, tf), np.float32)
    rev[np.arange(1, tf), tf - np.arange(1, tf)] = 1.0
    direct = pl.BlockSpec((1, tf, wdt), lambda bb, j: (bb, j, 0))
    mirror = pl.BlockSpec((1, tf, wdt), lambda bb, j: (bb, nblk - 1 - j, 0))
    extra = pl.BlockSpec((1, BF16_SUBLANES, wdt),
                         lambda bb, j: (bb, jnp.where(j == 0, 0, (n - j * tf) // BF16_SUBLANES), 0))
    out = pl.BlockSpec((1, tf, wdt), lambda bb, j: (bb, j, 0))
    return pl.pallas_call(
        _fold_kernel,
        grid=(b, half // tf),
        in_specs=[pl.BlockSpec((tf, tf), lambda bb, j: (0, 0)), direct, mirror, extra, direct, mirror, extra],
        out_specs=[out, out],
        out_shape=[jax.ShapeDtypeStruct((b, half, wdt), BF16)] * 2,
        compiler_params=_cparams(("arbitrary", "arbitrary")),
        name="dft_fold",
    )(jnp.asarray(rev, BF16), uc, uc, uc, us, us, us)


def _dft_kernel(ca_ref, sa_ref, cb_ref, sb_ref, ue_ref, uo_ref, mid_ref, o_ref, c_scr, s_scr, *, norm):
    @pl.when(pl.program_id(1) == 0)
    def _():
        cb, sb = cb_ref[...], sb_ref[...]
        for r in range(ca_ref.shape[0]):
            ca, sa = ca_ref[r:r + 1, :], sa_ref[r:r + 1, :]
            rows = slice(r * DFT_RADIX, (r + 1) * DFT_RADIX)
            c_scr[rows, :] = (ca * cb - sa * sb).astype(BF16)
            s_scr[rows, :] = (-(sa * cb + ca * sb)).astype(BF16)

    tm = o_ref.shape[1]
    acc = _dot(c_scr[...], ue_ref[0]) + _dot(s_scr[...], uo_ref[0])
    k = pl.program_id(0) * tm + lax.broadcasted_iota(jnp.int32, (tm, 1), 0)
    sign = jnp.where((k & 1) == 0, 1.0, -1.0)
    acc = acc + sign * mid_ref[0, 0:1, :].astype(F32)
    o_ref[0] = (acc * norm).astype(BF16)


def _dft_call(tabs, uc, us, *, tm):
    b, n, wdt = uc.shape
    half = n // 2
    ue, uo = _fold_call(uc, us)
    r_tile = tm // DFT_RADIX
    norm = 1.0 / math.sqrt(n * FOURIER_HEAD_DIM)
    a_spec = pl.BlockSpec((r_tile, half), lambda i, bb: (i, 0))
    b_spec = pl.BlockSpec((DFT_RADIX, half), lambda i, bb: (0, 0))
    u_spec = pl.BlockSpec((1, half, wdt), lambda i, bb: (bb, 0, 0))
    mid_spec = pl.BlockSpec((1, BF16_SUBLANES, wdt), lambda i, bb: (bb, half // BF16_SUBLANES, 0))
    return pl.pallas_call(
        functools.partial(_dft_kernel, norm=norm),
        grid=(n // tm, b),
        in_specs=[a_spec, a_spec, b_spec, b_spec, u_spec, u_spec, mid_spec],
        out_specs=pl.BlockSpec((1, tm, wdt), lambda i, bb: (bb, i, 0)),
        out_shape=jax.ShapeDtypeStruct((b, n, wdt), BF16),
        scratch_shapes=[pltpu.VMEM((tm, half), BF16), pltpu.VMEM((tm, half), BF16)],
        compiler_params=_cparams(("arbitrary", "arbitrary")),
        name="dft_rows",
    )(*tabs, ue, uo, uc)


def _tail(y, h_ref, gate_ref, g2_ref, sh_ref, sc_ref, rwt_ref, rb_ref, hout_ref, fx_ref, bk_ref):
    hn = h_ref[0] + gate_ref[0] * y
    hout_ref[0] = hn
    f = _rms_mod(hn, g2_ref[...], sh_ref[0], sc_ref[0])
    tm = f.shape[0]
    f_hi = f.astype(BF16)
    f_lo = (f - f_hi.astype(F32)).astype(BF16)
    both = _dot_nt(rwt_ref[...], f_hi)
    logits = (both[:N_EXPERTS] + both[N_EXPERTS:]) + _dot_nt(rwt_ref[:N_EXPERTS, :], f_lo)
    aff = jax.nn.sigmoid(logits)
    sel = aff + rb_ref[...]
    cands = []
    for bkt in range(N_BUCKETS):
        lo, hi = _BUCKET_LO[bkt], _BUCKET_HI[bkt]
        cands.append((sel[lo:lo + 1, :] + sel[hi:hi + 1, :], jnp.full((1, tm), float(bkt), F32),
                      aff[lo:lo + 1, :], aff[hi:hi + 1, :]))
    while len(cands) > 1:
        merged = []
        for k in range(0, len(cands) - 1, 2):
            left, right = cands[k], cands[k + 1]
            take_right = right[0] > left[0]
            merged.append(tuple(jnp.where(take_right, r, l) for l, r in zip(left, right)))
        if len(cands) % 2:
            merged.append(cands[-1])
        cands = merged
    _, bucket, a_lo, a_hi = cands[0]
    den = a_lo + a_hi
    info = jnp.concatenate([bucket, a_lo / den, a_hi / den, jnp.zeros((LANES - 3, tm), F32)], axis=0)
    fx_ref[0, :, :D_MODEL] = f
    fx_ref[0, :, D_MODEL:] = info.T
    bk_ref[...] = jnp.concatenate([bucket, jnp.zeros((7, tm), F32)], axis=0)


def _tail_specs(h, gate, shift, scale, tm):
    b, n, d = h.shape
    nt = n // tm
    row = lambda bb, i: (bb, i, 0)
    in_specs = [pl.BlockSpec((1, tm, d), row), _mod_spec(gate),
                pl.BlockSpec((1, d), lambda bb, i: (0, 0)), _mod_spec(shift), _mod_spec(scale),
                pl.BlockSpec((2 * N_EXPERTS, d), lambda bb, i: (0, 0)),
                pl.BlockSpec((N_EXPERTS, 1), lambda bb, i: (0, 0))]
    out_shape = [jax.ShapeDtypeStruct((b, n, d), F32), jax.ShapeDtypeStruct((b, n, ROW_WIDTH), F32),
                 jax.ShapeDtypeStruct((8, b * n), F32)]
    out_specs = [pl.BlockSpec((1, tm, d), row), pl.BlockSpec((1, tm, ROW_WIDTH), row),
                 pl.BlockSpec((8, tm), lambda bb, i: (0, bb * nt + i))]
    return in_specs, out_shape, out_specs


def _even_out_kernel(yf_ref, gb_ref, gu_ref, gp_ref, gn_ref, cw_ref, wo_ref,
                     h_ref, gate_ref, g2_ref, sh_ref, sc_ref, rwt_ref, rb_ref,
                     hout_ref, fx_ref, bk_ref):
    i = pl.program_id(1)
    last = pl.num_programs(1) - 1
    gu = gu_ref[0]
    tm = gu.shape[0]
    prev = jnp.where(i > 0, gp_ref[0, 7:8, :], 0.0)
    nxt = jnp.where(i < last, gn_ref[0, 0:1, :], 0.0)
    row = lax.broadcasted_iota(jnp.int32, gu.shape, 0)
    up = jnp.where(row == 0, prev, pltpu.roll(gu, 1, 0))
    dn = jnp.where(row == tm - 1, nxt, pltpu.roll(gu, tm - 1, 0))
    conv = up * cw_ref[0:1, :] + gu * cw_ref[1:2, :] + dn * cw_ref[2:3, :]
    yc = (gb_ref[0] * conv).astype(BF16)
    y = _dot(yf_ref[0], wo_ref[:FOURIER_WIDTH, :]) + _dot(yc, wo_ref[FOURIER_WIDTH:, :])
    _tail(y, h_ref, gate_ref, g2_ref, sh_ref, sc_ref, rwt_ref, rb_ref, hout_ref, fx_ref, bk_ref)


def _even_out_call(yf, gb, gu, conv_w, w_out, h, gate, g2, shift, scale, rwt, rb, *, tm):
    b, n, d = h.shape
    row = lambda bb, i: (bb, i, 0)
    nb8 = n // 8
    t8 = tm // 8
    tin, out_shape, out_specs = _tail_specs(h, gate, shift, scale, tm)
    in_specs = [pl.BlockSpec((1, tm, 512), row), pl.BlockSpec((1, tm, 512), row),
                pl.BlockSpec((1, tm, 512), row),
                pl.BlockSpec((1, 8, 512), lambda bb, i: (bb, jnp.maximum(i * t8 - 1, 0), 0)),
                pl.BlockSpec((1, 8, 512), lambda bb, i: (bb, jnp.minimum((i + 1) * t8, nb8 - 1), 0)),
                pl.BlockSpec((CONV_K, CONV_WIDTH), lambda bb, i: (0, 0)),
                pl.BlockSpec((d, d), lambda bb, i: (0, 0))] + tin
    return pl.pallas_call(
        _even_out_kernel,
        grid=(b, n // tm),
        in_specs=in_specs,
        out_specs=out_specs,
        out_shape=out_shape,
        compiler_params=_cparams(("arbitrary", "arbitrary")),
        name="even_out",
    )(yf, gb, gu, gu, gu, conv_w, w_out, h, gate, g2.reshape(1, d), shift, scale, rwt, rb)


def _odd_out_kernel(sink_ref, up_ref, upp_ref, upn_ref, q_ref, kc_ref, kp_ref, kn_ref,
                    vc_ref, vp_ref, vn_ref, kvx_ref, pw_ref, ps_ref, wo_ref,
                    h_ref, gate_ref, g2_ref, sh_ref, sc_ref, rwt_ref, rb_ref,
                    hout_ref, fx_ref, bk_ref, ext_ref, mix_ref, *, n_total):
    i = pl.program_id(1)
    last = pl.num_programs(1) - 1
    tq = q_ref.shape[1]
    nsub = tq // ATTN_BLOCK

    u = up_ref[0]
    ext_ref[0:POOL_HALO, :] = jnp.where(i > 0, upp_ref[0], 0.0)
    ext_ref[POOL_HALO:POOL_HALO + tq, :] = u
    ext_ref[POOL_HALO + tq:, :] = jnp.where(i < last, upn_ref[0], 0.0)
    t = i * tq + lax.broadcasted_iota(jnp.int32, (tq, LANES), 0)
    for gi, win in enumerate(POOL_WINDOWS):
        r = win // 2
        cols = slice(gi * LANES, (gi + 1) * LANES)
        acc = ext_ref[POOL_HALO - r:POOL_HALO - r + tq, cols]
        for dlt in range(-r + 1, r + 1):
            acc = acc + ext_ref[POOL_HALO + dlt:POOL_HALO + dlt + tq, cols]
        cnt = (jnp.minimum(t + r + 1, n_total) - jnp.maximum(t - r, 0)).astype(F32)
        p = acc / cnt - u[:, cols]
        y = _dot(p.astype(BF16), pw_ref[gi]) * ps_ref[:, cols]
        mix_ref[:, cols] = y.astype(BF16)

    kwin = jnp.concatenate([kp_ref[0], kc_ref[0], kn_ref[0]], axis=0)
    vwin = jnp.concatenate([vp_ref[0], vc_ref[0], vn_ref[0]], axis=0)
    kvx = kvx_ref[0]
    kx, vx = kvx[:, :2 * KV_WIDTH], kvx[:, 2 * KV_WIDTH:]
    low = lax.broadcasted_iota(jnp.int32, (1, LANES), 1) < HEAD_DIM
    zero = jnp.zeros((), BF16)

    span = 3 * ATTN_BLOCK
    rows4 = 4 * ATTN_BLOCK
    qi = lax.broadcasted_iota(jnp.int32, (rows4, ATTN_BLOCK), 0) % ATTN_BLOCK
    kj = lax.broadcasted_iota(jnp.int32, (rows4, ATTN_BLOCK), 1)
    neg_inf = jnp.float32(-jnp.inf)
    prev_band = jnp.where(kj < qi, neg_inf, 0.0)
    next_band = jnp.where(kj > qi, neg_inf, 0.0)
    row_block = jnp.right_shift(lax.broadcasted_iota(jnp.int32, (rows4, 1), 0), int(math.log2(ATTN_BLOCK)))

    for j in range(nsub):
        blk = i * nsub + j
        prev_bias = prev_band + jnp.where(blk > 0, 0.0, neg_inf)
        next_bias = next_band + jnp.where(blk < (n_total // ATTN_BLOCK) - 1, 0.0, neg_inf)
        r0 = j * ATTN_BLOCK
        for kh in range(N_KV_HEADS):
            kcols = slice(kh * LANES, (kh + 1) * LANES)
            qa = q_ref[0, r0:r0 + ATTN_BLOCK, (2 * kh) * LANES:(2 * kh + 1) * LANES]
            qb = q_ref[0, r0:r0 + ATTN_BLOCK, (2 * kh + 1) * LANES:(2 * kh + 2) * LANES]
            xq = jnp.concatenate([jnp.where(low, qa, zero), jnp.where(low, qb, zero),
                                  jnp.where(low, zero, qa), jnp.where(low, zero, qb)], axis=0)
            h0 = 4 * kh
            snk = LOG2_E * jnp.where(row_block == 0, sink_ref[h0],
                                     jnp.where(row_block == 1, sink_ref[h0 + 2],
                                               jnp.where(row_block == 2, sink_ref[h0 + 1], sink_ref[h0 + 3])))
            s1 = _dot_nt(xq, kwin[r0:r0 + span, kcols])
            s1 = jnp.concatenate([s1[:, :ATTN_BLOCK] + prev_bias, s1[:, ATTN_BLOCK:2 * ATTN_BLOCK],
                                  s1[:, 2 * ATTN_BLOCK:] + next_bias], axis=1)
            s2 = _dot_nt(xq, kx[:, kcols])
            m = jnp.maximum(jnp.maximum(jnp.max(s1, axis=-1, keepdims=True),
                                        jnp.max(s2, axis=-1, keepdims=True)), snk)
            e1 = jnp.exp2(s1 - m)
            e2 = jnp.exp2(s2 - m)
            den = (jnp.sum(e1, axis=-1, keepdims=True) + jnp.sum(e2, axis=-1, keepdims=True)
                   + jnp.exp2(snk - m))
            o = _dot(e1.astype(BF16), vwin[r0:r0 + span, kcols]) + _dot(e2.astype(BF16), vx[:, kcols])
            o = o / den
            for pr in range(2):
                o_low = o[pr * ATTN_BLOCK:(pr + 1) * ATTN_BLOCK]
                o_high = o[(2 + pr) * ATTN_BLOCK:(3 + pr) * ATTN_BLOCK]
                c0 = POOL_WIDTH + (2 * kh + pr) * LANES
                mix_ref[r0:r0 + ATTN_BLOCK, c0:c0 + LANES] = jnp.where(low, o_low, o_high).astype(BF16)

    y = _dot(mix_ref[...], wo_ref[...])
    _tail(y, h_ref, gate_ref, g2_ref, sh_ref, sc_ref, rwt_ref, rb_ref, hout_ref, fx_ref, bk_ref)


def _odd_out_call(sink, up, q, kd, vd, kvx, pool_w, pool_scale, w_out,
                  h, gate, g2, shift, scale, rwt, rb, *, tq):
    b, n, d = h.shape
    row = lambda bb, i: (bb, i, 0)
    nb8, t8 = n // POOL_HALO, tq // POOL_HALO
    nbk, tk = n // ATTN_BLOCK, tq // ATTN_BLOCK
    prev8 = lambda bb, i: (bb, jnp.maximum(i * t8 - 1, 0), 0)
    next8 = lambda bb, i: (bb, jnp.minimum((i + 1) * t8, nb8 - 1), 0)
    prevk = lambda bb, i: (bb, jnp.maximum(i * tk - 1, 0), 0)
    nextk = lambda bb, i: (bb, jnp.minimum((i + 1) * tk, nbk - 1), 0)
    kvw = 2 * KV_WIDTH
    tin, out_shape, out_specs = _tail_specs(h, gate, shift, scale, tq)
    in_specs = [pl.BlockSpec(memory_space=pltpu.SMEM),
                pl.BlockSpec((1, tq, POOL_WIDTH), row),
                pl.BlockSpec((1, POOL_HALO, POOL_WIDTH), prev8),
                pl.BlockSpec((1, POOL_HALO, POOL_WIDTH), next8),
                pl.BlockSpec((1, tq, ATTN_WIDTH), row),
                pl.BlockSpec((1, tq, kvw), row),
                pl.BlockSpec((1, ATTN_BLOCK, kvw), prevk),
                pl.BlockSpec((1, ATTN_BLOCK, kvw), nextk),
                pl.BlockSpec((1, tq, kvw), row),
                pl.BlockSpec((1, ATTN_BLOCK, kvw), prevk),
                pl.BlockSpec((1, ATTN_BLOCK, kvw), nextk),
                pl.BlockSpec((1, kvx.shape[1], 2 * kvw), lambda bb, i: (bb, 0, 0)),
                pl.BlockSpec(pool_w.shape, lambda bb, i: (0, 0, 0)),
                pl.BlockSpec((1, POOL_WIDTH), lambda bb, i: (0, 0)),
                pl.BlockSpec((d, d), lambda bb, i: (0, 0))] + tin
    return pl.pallas_call(
        functools.partial(_odd_out_kernel, n_total=n),
        grid=(b, n // tq),
        in_specs=in_specs,
        out_specs=out_specs,
        out_shape=out_shape,
        scratch_shapes=[pltpu.VMEM((tq + 2 * POOL_HALO, POOL_WIDTH), F32),
                        pltpu.VMEM((tq, d), BF16)],
        compiler_params=_cparams(("arbitrary", "arbitrary")),
        name="odd_out",
    )(sink, up, up, up, q, kd, kd, kd, vd, vd, vd, kvx, pool_w, pool_scale.reshape(1, POOL_WIDTH),
      w_out, h, gate, g2.reshape(1, d), shift, scale, rwt, rb)


def _moe_kernel(e_lo_ref, e_hi_ref, valid_ref, x_ref,
                g1_ref, u1_ref, d1_ref, g2_ref, u2_ref, d2_ref, o_ref):
    j = pl.program_id(0)

    @pl.when(valid_ref[j] != 0)
    def _():
        x = x_ref[:, :D_MODEL].astype(BF16)
        w_lo = x_ref[:, D_MODEL + INFO_W_LO:D_MODEL + INFO_W_LO + 1]
        w_hi = x_ref[:, D_MODEL + INFO_W_HI:D_MODEL + INFO_W_HI + 1]

        def expert(g_ref, u_ref, d_ref):
            gate = _dot(x, g_ref[0, 0].astype(BF16))
            hid = gate * jax.nn.sigmoid(gate) * _dot(x, u_ref[0, 0].astype(BF16))
            return _dot(hid.astype(BF16), d_ref[0, 0].astype(BF16))

        o_lo = expert(g1_ref, u1_ref, d1_ref)
        o_hi = expert(g2_ref, u2_ref, d2_ref)
        o_ref[...] = w_lo * o_lo + w_hi * o_hi

    @pl.when(valid_ref[j] == 0)
    def _():
        o_ref[...] = jnp.zeros(o_ref.shape, o_ref.dtype)


def _moe_call(tile_lo, tile_hi, tile_valid, xs, wg, wu, wd, *, layer, tm):
    p = xs.shape[0]
    d = D_MODEL
    ntiles = p // tm
    lo4 = lambda j, lo, hi, v: (layer, lo[j], 0, 0)
    hi4 = lambda j, lo, hi, v: (layer, hi[j], 0, 0)
    rowm = lambda j, lo, hi, v: (j, 0)
    row_in = lambda j, lo, hi, v: (jnp.where(v[j] != 0, j, 0), 0)
    grid_spec = pltpu.PrefetchScalarGridSpec(
        num_scalar_prefetch=3,
        grid=(ntiles,),
        in_specs=[pl.BlockSpec((tm, ROW_WIDTH), row_in),
                  pl.BlockSpec((1, 1, d, D_EXPERT), lo4), pl.BlockSpec((1, 1, d, D_EXPERT), lo4),
                  pl.BlockSpec((1, 1, D_EXPERT, d), lo4),
                  pl.BlockSpec((1, 1, d, D_EXPERT), hi4), pl.BlockSpec((1, 1, d, D_EXPERT), hi4),
                  pl.BlockSpec((1, 1, D_EXPERT, d), hi4)],
        out_specs=pl.BlockSpec((tm, d), rowm),
    )
    return pl.pallas_call(
        _moe_kernel,
        grid_spec=grid_spec,
        out_shape=jax.ShapeDtypeStruct((p, d), F32),
        compiler_params=_cparams(("arbitrary",)),
        name="moe_pairs",
    )(tile_lo, tile_hi, tile_valid, xs, wg, wu, wd, wg, wu, wd)


RANK_ROWS = 32


def _rank_kernel(bk_ref, tri_ref, rk_ref, cnt_ref, carry_ref):
    @pl.when(pl.program_id(0) == 0)
    def _():
        carry_ref[...] = jnp.zeros(carry_ref.shape, F32)

    tr = bk_ref.shape[1]
    bucket = bk_ref[0:1, :]
    rows = lax.broadcasted_iota(jnp.int32, (RANK_ROWS, tr), 0).astype(F32)
    onehot = jnp.where(rows == bucket, 1.0, 0.0)
    before = _dot(onehot.astype(BF16), tri_ref[...]) + carry_ref[:, 0:1]
    rank = jnp.sum(onehot * before, axis=0, keepdims=True)
    rk_ref[...] = jnp.broadcast_to(rank, rk_ref.shape)
    carry_ref[...] = carry_ref[...] + jnp.sum(onehot, axis=1, keepdims=True)
    cnt_ref[...] = carry_ref[...]


def _rank_call(bk, *, tr):
    t = bk.shape[1]
    tri = jnp.asarray(np.triu(np.ones((tr, tr), np.float32), 1), BF16)
    return pl.pallas_call(
        _rank_kernel,
        grid=(t // tr,),
        in_specs=[pl.BlockSpec((8, tr), lambda j: (0, j)), pl.BlockSpec((tr, tr), lambda j: (0, 0))],
        out_specs=[pl.BlockSpec((8, tr), lambda j: (0, j)), pl.BlockSpec((RANK_ROWS, LANES), lambda j: (0, 0))],
        out_shape=[jax.ShapeDtypeStruct((8, t), F32), jax.ShapeDtypeStruct((RANK_ROWS, LANES), F32)],
        scratch_shapes=[pltpu.VMEM((RANK_ROWS, LANES), F32)],
        compiler_params=_cparams(("arbitrary",)),
        name="bucket_rank",
    )(bk, tri)


ROW_TILE = 512


def _scatter_rows_kernel(tile_end_ref, pos_ref, *refs, tile_starts, tm):
    nsrc = len(tile_starts)
    srcs, out_hbm, zero_ref, sem = refs[:nsrc], refs[nsrc], refs[nsrc + 1], refs[nsrc + 2]
    j = pl.program_id(0)

    @pl.when(j == 0)
    def _():
        zero_ref[...] = jnp.zeros(zero_ref.shape, F32)

        def fill(tile, start):
            copy = pltpu.make_async_copy(zero_ref, out_hbm.at[pl.ds(tile * tm, tm)], sem)
            copy.start() if start else copy.wait()

        n_slot_tiles = out_hbm.shape[0] // tm
        used = tile_end_ref[N_BUCKETS - 1]
        for start in (True, False):
            for b in range(N_BUCKETS):
                first_tile = tile_end_ref[b - 1] if b else 0
                pl.when(tile_end_ref[b] > first_tile)(functools.partial(fill, tile_end_ref[b] - 1, start))
                pl.when(n_slot_tiles - 1 - b >= used)(functools.partial(fill, n_slot_tiles - 1 - b, start))

    def move(src_vmem):
        for r in range(ROW_TILE):
            pltpu.make_async_copy(src_vmem.at[pl.ds(r, 1)],
                                  out_hbm.at[pl.ds(pos_ref[0, 0, r], 1)], sem).start(priority=r % 2)
        pltpu.make_async_copy(src_vmem, out_hbm.at[pl.ds(0, ROW_TILE)], sem).wait()

    for s in range(nsrc):
        first = tile_starts[s]
        if nsrc == 1:
            move(srcs[s])
        else:
            in_range = (j >= first) if s == nsrc - 1 else ((j >= first) & (j < tile_starts[s + 1]))
            pl.when(in_range)(functools.partial(move, srcs[s]))


def _scatter_rows_call(pos, tile_end, sources, nslots, tm):
    w = sources[0].shape[1]
    t = pos.shape[0]
    ntiles = t // ROW_TILE
    tile_starts, src_specs, acc = [], [], 0
    for s in sources:
        first, count = acc, s.shape[0] // ROW_TILE
        tile_starts.append(first)
        src_specs.append(pl.BlockSpec(
            (ROW_TILE, w), lambda j, te, first=first, count=count: (jnp.clip(j - first, 0, count - 1), 0)))
        acc += count
    grid_spec = pltpu.PrefetchScalarGridSpec(
        num_scalar_prefetch=1,
        grid=(ntiles,),
        in_specs=[pl.BlockSpec((1, 1, ROW_TILE), lambda j, te: (j, 0, 0), memory_space=pltpu.SMEM)] + src_specs,
        out_specs=pl.BlockSpec(memory_space=pl.ANY),
        scratch_shapes=[pltpu.VMEM((tm, w), F32), pltpu.SemaphoreType.DMA(())],
    )
    return pl.pallas_call(
        functools.partial(_scatter_rows_kernel, tile_starts=tuple(tile_starts), tm=tm),
        grid_spec=grid_spec,
        out_shape=jax.ShapeDtypeStruct((nslots, w), F32),
        compiler_params=_cparams(("arbitrary",)),
        name="scatter_rows",
    )(tile_end, pos.reshape(ntiles, 1, ROW_TILE), *sources)


def _final_kernel(h_ref, pos_ref, next_pos_ref, moe_hbm, gate_ref, g_ref, o_ref, moe_bufs, moe_sems):
    step = pl.program_id(0) * pl.num_programs(1) + pl.program_id(1)
    nsteps = pl.num_programs(0) * pl.num_programs(1)
    x = h_ref[0] + gate_ref[0] * _gathered_rows(step, pos_ref, moe_hbm, moe_bufs, moe_sems)
    o_ref[0] = x * lax.rsqrt(jnp.mean(x * x, axis=-1, keepdims=True) + EPS) * g_ref[...]
    _request_next_rows(step, nsteps, next_pos_ref, moe_hbm, moe_bufs, moe_sems)


def _final_call(h, moe, pos, gate, g, *, tm):
    b, n, d = h.shape
    row = lambda bb, i: (bb, i, 0)
    pos_tiles, pos_spec, next_pos_spec = _pos_tiles(pos, tm, n // tm, 0)
    return pl.pallas_call(
        _final_kernel,
        grid=(b, n // tm),
        in_specs=[pl.BlockSpec((1, tm, d), row), pos_spec, next_pos_spec, pl.BlockSpec(memory_space=pl.ANY),
                  _mod_spec(gate), pl.BlockSpec((1, d), lambda bb, i: (0, 0))],
        out_specs=pl.BlockSpec((1, tm, d), row),
        out_shape=jax.ShapeDtypeStruct((b, n, d), F32),
        scratch_shapes=[pltpu.VMEM((2, tm, d), F32), pltpu.SemaphoreType.DMA((2,))],
        compiler_params=_cparams(("arbitrary", "arbitrary")),
        name="final_norm",
    )(h, pos_tiles, pos_tiles, moe, gate, g.reshape(1, d))


def _channel_dft_table():
    c = np.arange(FOURIER_HEAD_DIM)
    ang = 2.0 * np.pi * ((c[:, None] * c[None, :]) % FOURIER_HEAD_DIM) / FOURIER_HEAD_DIM
    return jnp.asarray(np.concatenate([np.cos(ang), np.sin(ang)], axis=1), F32)


def _position_dft_tables(n):
    n1 = n // DFT_RADIX
    t = np.arange(n // 2)
    a = 2.0 * np.pi * ((np.arange(n1)[:, None] * t[None, :]) % n1) / n1
    bb = 2.0 * np.pi * ((np.arange(DFT_RADIX)[:, None] * t[None, :]) % n) / n
    return tuple(jnp.asarray(v, F32) for v in (np.cos(a), np.sin(a), np.cos(bb), np.sin(bb)))


def _rope_tables(n):
    quarter = HEAD_DIM // 4
    inv = ROPE_THETA ** (-jnp.arange(quarter, dtype=F32) / quarter)
    t = jnp.arange(n)
    ang_r = (t // GRID_W).astype(F32)[:, None] * inv
    ang_c = (t % GRID_W).astype(F32)[:, None] * inv
    cos = jnp.concatenate([jnp.cos(ang_r)] * 2 + [jnp.cos(ang_c)] * 2, axis=1)
    sin = jnp.concatenate([-jnp.sin(ang_r), jnp.sin(ang_r), -jnp.sin(ang_c), jnp.sin(ang_c)], axis=1)
    return jnp.tile(cos, (1, 2)), jnp.tile(sin, (1, 2))


def _dispatch_plan(bucket, rank, counts, tm):
    t = bucket.shape[0]
    ntiles = t // tm + N_BUCKETS
    tiles_per = (counts + tm - 1) // tm
    tile_end = jnp.cumsum(tiles_per)
    tile_start = tile_end - tiles_per
    onehot = bucket[:, None] == jnp.arange(N_BUCKETS, dtype=jnp.int32)[None, :]
    pos = jnp.sum(jnp.where(onehot, (tile_start * tm)[None, :], 0), axis=-1) + rank
    tile_ids = jnp.arange(ntiles, dtype=jnp.int32)
    used = tile_end[-1]
    tile_bucket = jnp.sum((tile_ids[:, None] >= tile_end[None, :]).astype(jnp.int32), axis=1)
    last_bucket = jnp.sum((jnp.maximum(used - 1, 0) >= tile_end).astype(jnp.int32))
    tile_valid = (tile_ids < used).astype(jnp.int32)
    tile_bucket = jnp.where(tile_valid == 1, tile_bucket, last_bucket)
    tile_lo = jnp.asarray(np.asarray(_BUCKET_LO, np.int32))[tile_bucket]
    tile_hi = jnp.asarray(np.asarray(_BUCKET_HI, np.int32))[tile_bucket]
    return pos.astype(jnp.int32), tile_end.astype(jnp.int32), tile_lo, tile_hi, tile_valid, ntiles * tm


def _moe_layer(fx_list, bk_list, wg, wu, wd, *, layer, tm):
    bk = bk_list[0] if len(bk_list) == 1 else jnp.concatenate(bk_list, axis=1)
    rk, cnt = _rank_call(bk, tr=ROW_TILE)
    pos, tile_end, tile_lo, tile_hi, tile_valid, nslots = _dispatch_plan(
        bk[0].astype(jnp.int32), rk[0].astype(jnp.int32), cnt[:N_BUCKETS, 0].astype(jnp.int32), tm)
    xs = _scatter_rows_call(pos, tile_end, fx_list, nslots, tm)
    return _moe_call(tile_lo, tile_hi, tile_valid, xs, wg, wu, wd, layer=layer, tm=tm), pos


def _forward(x, c, ctx, c_ctx, ada_w, ada_b, norm_mix_g, norm_ffn_g, even_w_in, even_conv_w, even_w_out,
             odd_w_in, odd_pool_w, odd_pool_scale, odd_sink, odd_w_out, router_w, router_b,
             moe_w_gate, moe_w_up, moe_w_down, final_g, *, tm_lat, tm_ctx, tq, tm_dft, tm_moe):
    b, n, d = x.shape
    l = ctx.shape[1]

    rows = ((b + 1 + 7) // 8) * 8
    s_rows = jnp.zeros((rows, d), F32).at[:b].set(c).at[b].set(c_ctx)
    mods = _ada_call(s_rows, ada_w, ada_b)

    def mod_vecs(layer):
        m = mods[layer, :b].reshape(b, N_MOD, 1, d)
        mc = mods[layer, b].reshape(N_MOD, 1, 1, d)
        return [m[:, k] for k in range(N_MOD)], [mc[k] for k in range(N_MOD)]

    rw_t = router_w.T
    rw_hi = rw_t.astype(BF16)
    rw_lo = (rw_t - rw_hi.astype(F32)).astype(BF16)
    rwt = jnp.concatenate([rw_hi, rw_lo], axis=0)
    rb = router_b.astype(F32).reshape(N_EXPERTS, 1)
    cs_tab = _channel_dft_table()

    m, mc = mod_vecs(0)
    w_in0 = even_w_in[0].astype(BF16)
    w_out0 = even_w_out[0].astype(BF16)

    def even_stream(h, mv, tm):
        nn = h.shape[1]
        uc, us, gb, gu = _inproj_call(h, norm_mix_g[0], mv[0], mv[1], w_in0, mode="even", tm=tm,
                                      extra=(cs_tab,))
        yf = _dft_call(_position_dft_tables(nn), uc, us, tm=min(tm_dft, nn))
        return _even_out_call(yf, gb, gu, even_conv_w[0], w_out0, h, mv[2], norm_ffn_g[0],
                              mv[3], mv[4], rwt, rb, tm=tm)

    h1, fx_lat, bk_lat = even_stream(x, m, tm_lat)
    hc1, fx_ctx, bk_ctx = even_stream(ctx, mc, tm_ctx)

    moe0, pos0 = _moe_layer([fx_lat.reshape(b * n, ROW_WIDTH), fx_ctx.reshape(b * l, ROW_WIDTH)],
                            [bk_lat, bk_ctx], moe_w_gate, moe_w_up, moe_w_down, layer=0, tm=tm_moe)
    gate_lat0, gate_ctx0 = m[5], mc[5]

    m, mc = mod_vecs(1)
    w_in1 = odd_w_in[0]
    kv0 = POOL_WIDTH + ATTN_WIDTH
    wk, wv = w_in1[:, kv0:kv0 + KV_WIDTH], w_in1[:, kv0 + KV_WIDTH:]

    def dup_heads(wm):
        return jnp.concatenate([wm[:, :HEAD_DIM], wm[:, :HEAD_DIM], wm[:, HEAD_DIM:], wm[:, HEAD_DIM:]], axis=1)

    w_kv_dup = jnp.concatenate([dup_heads(wk), dup_heads(wv)], axis=1)
    w_lat1 = jnp.concatenate([w_in1[:, :kv0], w_kv_dup], axis=1).astype(BF16)
    w_out1 = odd_w_out[0].astype(BF16)

    cos_t, sin_t = _rope_tables(n)
    q_scale = HEAD_DIM ** -0.5 * LOG2_E
    h1b, up, q, kd, vd = _inproj_call(h1, norm_mix_g[1], m[0], m[1], w_lat1, mode="odd", tm=tm_lat,
                                      moe=moe0, pos=pos0, moe_row0=0, gate=gate_lat0,
                                      extra=(cos_t * q_scale, sin_t * q_scale, cos_t, sin_t))
    (kvx,) = _inproj_call(hc1, norm_mix_g[1], mc[0], mc[1], w_kv_dup.astype(BF16), mode="plain", tm=tm_ctx,
                          moe=moe0, pos=pos0, moe_row0=b * n, gate=gate_ctx0)
    h2, fx2, bk2 = _odd_out_call(odd_sink[0], up, q, kd, vd, kvx, odd_pool_w[0].astype(BF16),
                                 odd_pool_scale[0], w_out1, h1b, m[2], norm_ffn_g[1], m[3], m[4], rwt, rb, tq=tq)
    moe1, pos1 = _moe_layer([fx2.reshape(b * n, ROW_WIDTH)], [bk2], moe_w_gate, moe_w_up, moe_w_down,
                            layer=1, tm=tm_moe)
    return _final_call(h2, moe1, pos1, m[5], final_g, tm=tm_lat)


def kernel(x, c, ctx, c_ctx, ada_w, ada_b, norm_mix_g, norm_ffn_g, even_w_in, even_conv_w, even_w_out,
           odd_w_in, odd_pool_w, odd_pool_scale, odd_sink, odd_w_out, router_w, router_b,
           moe_w_gate, moe_w_up, moe_w_down, final_g):
    return _forward(x, c, ctx, c_ctx, ada_w, ada_b, norm_mix_g, norm_ffn_g, even_w_in, even_conv_w,
                    even_w_out, odd_w_in, odd_pool_w, odd_pool_scale, odd_sink, odd_w_out, router_w,
                    router_b, moe_w_gate, moe_w_up, moe_w_down, final_g,
                    tm_lat=512, tm_ctx=256, tq=512, tm_dft=1024, tm_moe=512)
```

```python
import functools
import math

import numpy as np
import jax
import jax.numpy as jnp
from jax import lax
from jax.experimental import pallas as pl
from jax.experimental.pallas import tpu as pltpu

F32 = jnp.float32
BF16 = jnp.bfloat16

D_MODEL = 1024
GRID_W = 64
EPS = 1e-6
N_MOD = 6
FOURIER_HEADS = 4
FOURIER_HEAD_DIM = 128
FOURIER_WIDTH = 512
CONV_WIDTH = 512
CONV_K = 3
POOL_WINDOWS = (2, 4, 8, 16)
POOL_GROUP_DIM = 128
POOL_WIDTH = 512
POOL_HALO = 8
HEAD_DIM = 64
N_Q_HEADS = 8
N_KV_HEADS = 2
ATTN_WIDTH = 512
KV_WIDTH = 128
ATTN_BLOCK = 128
ROPE_THETA = 10000.0
N_EXPERTS = 16
N_GROUPS = 4
EXPERTS_PER_GROUP = 4
D_EXPERT = 512
N_PAIRS = 6
N_BUCKETS = N_GROUPS * N_PAIRS

_PAIRS = [(a, b) for a in range(EXPERTS_PER_GROUP) for b in range(a + 1, EXPERTS_PER_GROUP)]
_BUCKET_LO = [(k // N_PAIRS) * EXPERTS_PER_GROUP + _PAIRS[k % N_PAIRS][0] for k in range(N_BUCKETS)]
_BUCKET_HI = [(k // N_PAIRS) * EXPERTS_PER_GROUP + _PAIRS[k % N_PAIRS][1] for k in range(N_BUCKETS)]

LANES = 128
ROW_WIDTH = D_MODEL + LANES
INFO_BUCKET, INFO_W_LO, INFO_W_HI = 0, 1, 2
VMEM_LIMIT_BYTES = 48 * 1024 * 1024

HIGHEST = lax.Precision.HIGHEST
LOG2_E = math.log2(math.e)


def _cparams(sem):
    return pltpu.CompilerParams(dimension_semantics=sem, vmem_limit_bytes=VMEM_LIMIT_BYTES)


def _rms_mod(x, g, shift, scale):
    y = x * lax.rsqrt(jnp.mean(x * x, axis=-1, keepdims=True) + EPS) * g
    return y * (1.0 + scale) + shift


def _dot(a, b):
    return jnp.dot(a, b, preferred_element_type=F32)


def _dot_nt(a, b):
    return lax.dot_general(a, b, (((1,), (1,)), ((), ())), preferred_element_type=F32)


def _ada_kernel(s_ref, w_ref, b_ref, o_ref):
    s = s_ref[...]
    s = s * jax.nn.sigmoid(s)
    o_ref[0] = jnp.dot(s, w_ref[0], preferred_element_type=F32, precision=HIGHEST) + b_ref[0]


def _ada_call(s_rows, ada_w, ada_b):
    depth, d, n6 = ada_w.shape
    r = s_rows.shape[0]
    tn = 1536
    return pl.pallas_call(
        _ada_kernel,
        grid=(depth, n6 // tn),
        in_specs=[
            pl.BlockSpec((r, d), lambda l, j: (0, 0)),
            pl.BlockSpec((1, d, tn), lambda l, j: (l, 0, j)),
            pl.BlockSpec((1, 1, tn), lambda l, j: (l, 0, j)),
        ],
        out_specs=pl.BlockSpec((1, r, tn), lambda l, j: (l, 0, j)),
        out_shape=jax.ShapeDtypeStruct((depth, r, n6), F32),
        compiler_params=_cparams(("arbitrary", "arbitrary")),
        name="ada_mod",
    )(s_rows, ada_w, ada_b.reshape(depth, 1, n6))


def _rope_group(x, cos, sin_signed):
    lane = lax.broadcasted_iota(jnp.int32, x.shape, 1)
    first_half = (lane % 32) < 16
    partner = jnp.where(first_half, pltpu.roll(x, LANES - 16, 1), pltpu.roll(x, 16, 1))
    return x * cos + partner * sin_signed


def _inproj_kernel(*refs, mode, add_moe):
    it = iter(refs)
    h_ref = next(it)
    if add_moe:
        pos_refs = [next(it) for _ in range(ROW_BUFFERS)]
        moe_hbm, gate_ref = next(it), next(it)
        moe_bufs, moe_sems = refs[-2], refs[-1]
        step = pl.program_id(0) * pl.num_programs(1) + pl.program_id(1)
        nsteps = pl.num_programs(0) * pl.num_programs(1)
        moe_rows = _gathered_rows(step, pos_refs, moe_hbm, moe_bufs, moe_sems)
    g_ref, sh_ref, sc_ref, w_ref = next(it), next(it), next(it), next(it)
    x = h_ref[0]
    if add_moe:
        x = x + gate_ref[0] * moe_rows
    a = _rms_mod(x, g_ref[...], sh_ref[0], sc_ref[0])
    proj = _dot(a.astype(BF16), w_ref[...])
    if mode == "even":
        cs_ref = next(it)
        uc_ref, us_ref, gb_ref, gu_ref = next(it), next(it), next(it), next(it)
        uf = proj[:, :FOURIER_WIDTH].astype(BF16)
        cs = cs_ref[...].astype(BF16)
        for hh in range(FOURIER_HEADS):
            cols = slice(hh * LANES, (hh + 1) * LANES)
            r = _dot(uf[:, cols], cs)
            uc_ref[0, :, cols] = r[:, :LANES].astype(BF16)
            us_ref[0, :, cols] = r[:, LANES:].astype(BF16)
        c0 = FOURIER_WIDTH
        gb_ref[0] = proj[:, c0:c0 + CONV_WIDTH]
        gu_ref[0] = proj[:, c0 + CONV_WIDTH:c0 + 2 * CONV_WIDTH] * proj[:, c0 + 2 * CONV_WIDTH:]
    elif mode == "odd":
        cq_ref, sq_ref, ck_ref, sk_ref = next(it), next(it), next(it), next(it)
        hn_ref, up_ref, q_ref, kd_ref, vd_ref = next(it), next(it), next(it), next(it), next(it)
        hn_ref[0] = x
        up_ref[0] = proj[:, :POOL_WIDTH]
        c0 = POOL_WIDTH
        for gi in range(ATTN_WIDTH // LANES):
            cols = slice(c0 + gi * LANES, c0 + (gi + 1) * LANES)
            q_ref[0, :, gi * LANES:(gi + 1) * LANES] = _rope_group(
                proj[:, cols], cq_ref[...], sq_ref[...]).astype(BF16)
        c0 += ATTN_WIDTH
        for gi in range(2 * KV_WIDTH // LANES):
            cols = slice(c0 + gi * LANES, c0 + (gi + 1) * LANES)
            kd_ref[0, :, gi * LANES:(gi + 1) * LANES] = _rope_group(
                proj[:, cols], ck_ref[...], sk_ref[...]).astype(BF16)
        c0 += 2 * KV_WIDTH
        vd_ref[0] = proj[:, c0:].astype(BF16)
    else:
        kv_ref = next(it)
        kv_ref[0] = proj.astype(BF16)
    if add_moe:
        _request_rows_ahead(step, nsteps, pos_refs, moe_hbm, moe_bufs, moe_sems)


def _mod_spec(arr):
    if arr.shape[0] > 1:
        return pl.BlockSpec((1, 1, arr.shape[2]), lambda b, i: (b, 0, 0))
    return pl.BlockSpec((1, 1, arr.shape[2]), lambda b, i: (0, 0, 0))


ROW_LOOKAHEAD = 2
ROW_BUFFERS = ROW_LOOKAHEAD + 1


def _gathered_rows(step, pos_refs, src_hbm, bufs, sems):
    nrows = bufs.shape[1]

    @pl.when(step == 0)
    def _():
        for ahead in range(ROW_LOOKAHEAD):
            def body(r, carry, ahead=ahead):
                pltpu.make_async_copy(src_hbm.at[pl.ds(pos_refs[ahead][0, 0, r], 1)],
                                      bufs.at[ahead, pl.ds(r, 1)], sems.at[ahead]).start()
                return carry
            lax.fori_loop(0, nrows, body, 0, unroll=8)

    slot = lax.rem(step, ROW_BUFFERS)
    pltpu.make_async_copy(src_hbm.at[pl.ds(0, nrows)], bufs.at[slot], sems.at[slot]).wait()
    return bufs[slot]


def _request_rows_ahead(step, nsteps, pos_refs, src_hbm, bufs, sems):
    nrows = bufs.shape[1]
    slot = lax.rem(step + ROW_LOOKAHEAD, ROW_BUFFERS)
    ahead_pos = pos_refs[ROW_LOOKAHEAD]
    for r in range(nrows):
        pltpu.make_async_copy(src_hbm.at[pl.ds(ahead_pos[0, 0, r], 1)], bufs.at[slot, pl.ds(r, 1)],
                              sems.at[slot]).start(priority=r % 2)

    @pl.when(step == nsteps - 1)
    def _():
        for ahead in range(1, ROW_BUFFERS):
            pending = lax.rem(step + ahead, ROW_BUFFERS)
            pltpu.make_async_copy(src_hbm.at[pl.ds(0, nrows)], bufs.at[pending], sems.at[pending]).wait()


def _pos_tiles(pos, tm, nt, row0):
    tile0 = row0 // tm
    ntiles = pos.shape[0] // tm
    specs = [pl.BlockSpec((1, 1, tm),
                          lambda bb, i, ahead=ahead: (jnp.minimum(tile0 + bb * nt + i + ahead, ntiles - 1), 0, 0),
                          memory_space=pltpu.SMEM) for ahead in range(ROW_BUFFERS)]
    return pos.reshape(ntiles, 1, tm), specs


def _inproj_call(h, g, shift, scale, w, *, mode, tm, moe=None, pos=None, moe_row0=0, gate=None, extra=()):
    b, n, d = h.shape
    nout = w.shape[1]
    add_moe = moe is not None
    row = lambda bb, i: (bb, i, 0)
    full2 = lambda bb, i: (0, 0)
    args = [h]
    in_specs = [pl.BlockSpec((1, tm, d), row)]
    scratch = []
    if add_moe:
        pos_tiles, pos_specs = _pos_tiles(pos, tm, n // tm, moe_row0)
        args += [pos_tiles] * ROW_BUFFERS + [moe, gate]
        in_specs += pos_specs + [pl.BlockSpec(memory_space=pl.ANY), _mod_spec(gate)]
        scratch = [pltpu.VMEM((ROW_BUFFERS, tm, d), F32), pltpu.SemaphoreType.DMA((ROW_BUFFERS,))]
    args += [g.reshape(1, d), shift, scale, w]
    in_specs += [pl.BlockSpec((1, d), full2), _mod_spec(shift), _mod_spec(scale),
                 pl.BlockSpec((d, nout), full2)]
    if mode == "even":
        args += list(extra)
        in_specs += [pl.BlockSpec(extra[0].shape, full2)]
        out_shape = [jax.ShapeDtypeStruct((b, n, FOURIER_WIDTH), BF16),
                     jax.ShapeDtypeStruct((b, n, FOURIER_WIDTH), BF16),
                     jax.ShapeDtypeStruct((b, n, CONV_WIDTH), F32),
                     jax.ShapeDtypeStruct((b, n, CONV_WIDTH), F32)]
        out_specs = [pl.BlockSpec((1, tm, 512), row)] * 4
    elif mode == "odd":
        args += list(extra)
        in_specs += [pl.BlockSpec((tm, LANES), lambda bb, i: (i, 0))] * 4
        out_shape = [jax.ShapeDtypeStruct((b, n, d), F32),
                     jax.ShapeDtypeStruct((b, n, POOL_WIDTH), F32),
                     jax.ShapeDtypeStruct((b, n, ATTN_WIDTH), BF16),
                     jax.ShapeDtypeStruct((b, n, 2 * KV_WIDTH), BF16),
                     jax.ShapeDtypeStruct((b, n, 2 * KV_WIDTH), BF16)]
        out_specs = [pl.BlockSpec((1, tm, d), row), pl.BlockSpec((1, tm, POOL_WIDTH), row),
                     pl.BlockSpec((1, tm, ATTN_WIDTH), row),
                     pl.BlockSpec((1, tm, 2 * KV_WIDTH), row), pl.BlockSpec((1, tm, 2 * KV_WIDTH), row)]
    else:
        out_shape = [jax.ShapeDtypeStruct((b, n, nout), BF16)]
        out_specs = [pl.BlockSpec((1, tm, nout), row)]
    return pl.pallas_call(
        functools.partial(_inproj_kernel, mode=mode, add_moe=add_moe),
        grid=(b, n // tm),
        in_specs=in_specs,
        out_specs=out_specs,
        out_shape=out_shape,
        scratch_shapes=scratch,
        compiler_params=_cparams(("arbitrary", "arbitrary")),
        name="inproj_" + mode,
    )(*args)


DFT_RADIX = 64


BF16_SUBLANES = 16


def _fold_kernel(rev_ref, uc_ref, ucm_ref, ucx_ref, us_ref, usm_ref, usx_ref, ue_ref, uo_ref):
    j = pl.program_id(1)
    tf = uc_ref.shape[1]
    first = lax.broadcasted_iota(jnp.int32, (tf, 1), 0) == 0

    def mirrored(m_ref, x_ref):
        return jnp.where(first, x_ref[0, 0:1, :].astype(F32), _dot(rev_ref[...], m_ref[0]))

    weight = jnp.where(first, jnp.where(j == 0, 0.5, 1.0), 1.0)
    ue_ref[0] = (weight * (uc_ref[0].astype(F32) + mirrored(ucm_ref, ucx_ref))).astype(BF16)
    uo_ref[0] = (us_ref[0].astype(F32) - mirrored(usm_ref, usx_ref)).astype(BF16)


def _fold_call(uc, us):
    b, n, wdt = uc.shape
    half = n // 2
    tf = min(512, half)
    nblk = n // tf
    rev = np.zeros((tf, tf), np.float32)
    rev[np.arange(1, tf), tf - np.arange(1, tf)] = 1.0
    direct = pl.BlockSpec((1, tf, wdt), lambda bb, j: (bb, j, 0))
    mirror = pl.BlockSpec((1, tf, wdt), lambda bb, j: (bb, nblk - 1 - j, 0))
    extra = pl.BlockSpec((1, BF16_SUBLANES, wdt),
                         lambda bb, j: (bb, jnp.where(j == 0, 0, (n - j * tf) // BF16_SUBLANES), 0))
    out = pl.BlockSpec((1, tf, wdt), lambda bb, j: (bb, j, 0))
    return pl.pallas_call(
        _fold_kernel,
        grid=(b, half // tf),
        in_specs=[pl.BlockSpec((tf, tf), lambda bb, j: (0, 0)), direct, mirror, extra, direct, mirror, extra],
        out_specs=[out, out],
        out_shape=[jax.ShapeDtypeStruct((b, half, wdt), BF16)] * 2,
        compiler_params=_cparams(("arbitrary", "arbitrary")),
        name="dft_fold",
    )(jnp.asarray(rev, BF16), uc, uc, uc, us, us, us)


def _dft_kernel(ca_ref, sa_ref, cb_ref, sb_ref, ue_ref, uo_ref, mid_ref, o_ref, c_scr, s_scr, *, norm):
    @pl.when(pl.program_id(1) == 0)
    def _():
        cb, sb = cb_ref[...], sb_ref[...]
        for r in range(ca_ref.shape[0]):
            ca, sa = ca_ref[r:r + 1, :], sa_ref[r:r + 1, :]
            rows = slice(r * DFT_RADIX, (r + 1) * DFT_RADIX)
            c_scr[rows, :] = (ca * cb - sa * sb).astype(BF16)
            s_scr[rows, :] = (-(sa * cb + ca * sb)).astype(BF16)

    tm = o_ref.shape[1]
    acc = _dot(c_scr[...], ue_ref[0]) + _dot(s_scr[...], uo_ref[0])
    k = pl.program_id(0) * tm + lax.broadcasted_iota(jnp.int32, (tm, 1), 0)
    sign = jnp.where((k & 1) == 0, 1.0, -1.0)
    acc = acc + sign * mid_ref[0, 0:1, :].astype(F32)
    o_ref[0] = (acc * norm).astype(BF16)


def _dft_call(tabs, uc, us, *, tm):
    b, n, wdt = uc.shape
    half = n // 2
    ue, uo = _fold_call(uc, us)
    r_tile = tm // DFT_RADIX
    norm = 1.0 / math.sqrt(n * FOURIER_HEAD_DIM)
    a_spec = pl.BlockSpec((r_tile, half), lambda i, bb: (i, 0))
    b_spec = pl.BlockSpec((DFT_RADIX, half), lambda i, bb: (0, 0))
    u_spec = pl.BlockSpec((1, half, wdt), lambda i, bb: (bb, 0, 0))
    mid_spec = pl.BlockSpec((1, BF16_SUBLANES, wdt), lambda i, bb: (bb, half // BF16_SUBLANES, 0))
    return pl.pallas_call(
        functools.partial(_dft_kernel, norm=norm),
        grid=(n // tm, b),
        in_specs=[a_spec, a_spec, b_spec, b_spec, u_spec, u_spec, mid_spec],
        out_specs=pl.BlockSpec((1, tm, wdt), lambda i, bb: (bb, i, 0)),
        out_shape=jax.ShapeDtypeStruct((b, n, wdt), BF16),
        scratch_shapes=[pltpu.VMEM((tm, half), BF16), pltpu.VMEM((tm, half), BF16)],
        compiler_params=_cparams(("arbitrary", "arbitrary")),
        name="dft_rows",
    )(*tabs, ue, uo, uc)


def _tail(y, h_ref, gate_ref, g2_ref, sh_ref, sc_ref, rwt_ref, rb_ref, hout_ref, fx_ref, bk_ref):
    hn = h_ref[0] + gate_ref[0] * y
    hout_ref[0] = hn
    f = _rms_mod(hn, g2_ref[...], sh_ref[0], sc_ref[0])
    tm = f.shape[0]
    f_hi = f.astype(BF16)
    f_lo = (f - f_hi.astype(F32)).astype(BF16)
    both = _dot_nt(rwt_ref[...], f_hi)
    logits = (both[:N_EXPERTS] + both[N_EXPERTS:]) + _dot_nt(rwt_ref[:N_EXPERTS, :], f_lo)
    aff = jax.nn.sigmoid(logits)
    sel = aff + rb_ref[...]
    cands = []
    for bkt in range(N_BUCKETS):
        lo, hi = _BUCKET_LO[bkt], _BUCKET_HI[bkt]
        cands.append((sel[lo:lo + 1, :] + sel[hi:hi + 1, :], jnp.full((1, tm), float(bkt), F32),
                      aff[lo:lo + 1, :], aff[hi:hi + 1, :]))
    while len(cands) > 1:
        merged = []
        for k in range(0, len(cands) - 1, 2):
            left, right = cands[k], cands[k + 1]
            take_right = right[0] > left[0]
            merged.append(tuple(jnp.where(take_right, r, l) for l, r in zip(left, right)))
        if len(cands) % 2:
            merged.append(cands[-1])
        cands = merged
    _, bucket, a_lo, a_hi = cands[0]
    den = a_lo + a_hi
    info = jnp.concatenate([bucket, a_lo / den, a_hi / den, jnp.zeros((LANES - 3, tm), F32)], axis=0)
    fx_ref[0, :, :D_MODEL] = f
    fx_ref[0, :, D_MODEL:] = info.T
    bk_ref[...] = jnp.concatenate([bucket, jnp.zeros((7, tm), F32)], axis=0)


def _tail_specs(h, gate, shift, scale, tm):
    b, n, d = h.shape
    nt = n // tm
    row = lambda bb, i: (bb, i, 0)
    in_specs = [pl.BlockSpec((1, tm, d), row), _mod_spec(gate),
                pl.BlockSpec((1, d), lambda bb, i: (0, 0)), _mod_spec(shift), _mod_spec(scale),
                pl.BlockSpec((2 * N_EXPERTS, d), lambda bb, i: (0, 0)),
                pl.BlockSpec((N_EXPERTS, 1), lambda bb, i: (0, 0))]
    out_shape = [jax.ShapeDtypeStruct((b, n, d), F32), jax.ShapeDtypeStruct((b, n, ROW_WIDTH), F32),
                 jax.ShapeDtypeStruct((8, b * n), F32)]
    out_specs = [pl.BlockSpec((1, tm, d), row), pl.BlockSpec((1, tm, ROW_WIDTH), row),
                 pl.BlockSpec((8, tm), lambda bb, i: (0, bb * nt + i))]
    return in_specs, out_shape, out_specs


def _even_out_kernel(yf_ref, gb_ref, gu_ref, gp_ref, gn_ref, cw_ref, wo_ref,
                     h_ref, gate_ref, g2_ref, sh_ref, sc_ref, rwt_ref, rb_ref,
                     hout_ref, fx_ref, bk_ref):
    i = pl.program_id(1)
    last = pl.num_programs(1) - 1
    gu = gu_ref[0]
    tm = gu.shape[0]
    prev = jnp.where(i > 0, gp_ref[0, 7:8, :], 0.0)
    nxt = jnp.where(i < last, gn_ref[0, 0:1, :], 0.0)
    row = lax.broadcasted_iota(jnp.int32, gu.shape, 0)
    up = jnp.where(row == 0, prev, pltpu.roll(gu, 1, 0))
    dn = jnp.where(row == tm - 1, nxt, pltpu.roll(gu, tm - 1, 0))
    conv = up * cw_ref[0:1, :] + gu * cw_ref[1:2, :] + dn * cw_ref[2:3, :]
    yc = (gb_ref[0] * conv).astype(BF16)
    y = _dot(yf_ref[0], wo_ref[:FOURIER_WIDTH, :]) + _dot(yc, wo_ref[FOURIER_WIDTH:, :])
    _tail(y, h_ref, gate_ref, g2_ref, sh_ref, sc_ref, rwt_ref, rb_ref, hout_ref, fx_ref, bk_ref)


def _even_out_call(yf, gb, gu, conv_w, w_out, h, gate, g2, shift, scale, rwt, rb, *, tm):
    b, n, d = h.shape
    row = lambda bb, i: (bb, i, 0)
    nb8 = n // 8
    t8 = tm // 8
    tin, out_shape, out_specs = _tail_specs(h, gate, shift, scale, tm)
    in_specs = [pl.BlockSpec((1, tm, 512), row), pl.BlockSpec((1, tm, 512), row),
                pl.BlockSpec((1, tm, 512), row),
                pl.BlockSpec((1, 8, 512), lambda bb, i: (bb, jnp.maximum(i * t8 - 1, 0), 0)),
                pl.BlockSpec((1, 8, 512), lambda bb, i: (bb, jnp.minimum((i + 1) * t8, nb8 - 1), 0)),
                pl.BlockSpec((CONV_K, CONV_WIDTH), lambda bb, i: (0, 0)),
                pl.BlockSpec((d, d), lambda bb, i: (0, 0))] + tin
    return pl.pallas_call(
        _even_out_kernel,
        grid=(b, n // tm),
        in_specs=in_specs,
        out_specs=out_specs,
        out_shape=out_shape,
        compiler_params=_cparams(("arbitrary", "arbitrary")),
        name="even_out",
    )(yf, gb, gu, gu, gu, conv_w, w_out, h, gate, g2.reshape(1, d), shift, scale, rwt, rb)


def _odd_out_kernel(sink_ref, up_ref, upp_ref, upn_ref, q_ref, kc_ref, kp_ref, kn_ref,
                    vc_ref, vp_ref, vn_ref, kvx_ref, pw_ref, ps_ref, wo_ref,
                    h_ref, gate_ref, g2_ref, sh_ref, sc_ref, rwt_ref, rb_ref,
                    hout_ref, fx_ref, bk_ref, ext_ref, mix_ref, *, n_total):
    i = pl.program_id(1)
    last = pl.num_programs(1) - 1
    tq = q_ref.shape[1]
    nsub = tq // ATTN_BLOCK

    u = up_ref[0]
    ext_ref[0:POOL_HALO, :] = jnp.where(i > 0, upp_ref[0], 0.0)
    ext_ref[POOL_HALO:POOL_HALO + tq, :] = u
    ext_ref[POOL_HALO + tq:, :] = jnp.where(i < last, upn_ref[0], 0.0)
    t = i * tq + lax.broadcasted_iota(jnp.int32, (tq, LANES), 0)
    for gi, win in enumerate(POOL_WINDOWS):
        r = win // 2
        cols = slice(gi * LANES, (gi + 1) * LANES)
        acc = ext_ref[POOL_HALO - r:POOL_HALO - r + tq, cols]
        for dlt in range(-r + 1, r + 1):
            acc = acc + ext_ref[POOL_HALO + dlt:POOL_HALO + dlt + tq, cols]
        cnt = (jnp.minimum(t + r + 1, n_total) - jnp.maximum(t - r, 0)).astype(F32)
        p = acc / cnt - u[:, cols]
        y = _dot(p.astype(BF16), pw_ref[gi]) * ps_ref[:, cols]
        mix_ref[:, cols] = y.astype(BF16)

    kwin = jnp.concatenate([kp_ref[0], kc_ref[0], kn_ref[0]], axis=0)
    vwin = jnp.concatenate([vp_ref[0], vc_ref[0], vn_ref[0]], axis=0)
    kvx = kvx_ref[0]
    kx, vx = kvx[:, :2 * KV_WIDTH], kvx[:, 2 * KV_WIDTH:]
    low = lax.broadcasted_iota(jnp.int32, (1, LANES), 1) < HEAD_DIM
    zero = jnp.zeros((), BF16)

    span = 3 * ATTN_BLOCK
    rows4 = 4 * ATTN_BLOCK
    qi = lax.broadcasted_iota(jnp.int32, (rows4, ATTN_BLOCK), 0) % ATTN_BLOCK
    kj = lax.broadcasted_iota(jnp.int32, (rows4, ATTN_BLOCK), 1)
    neg_inf = jnp.float32(-jnp.inf)
    prev_band = jnp.where(kj < qi, neg_inf, 0.0)
    next_band = jnp.where(kj > qi, neg_inf, 0.0)
    row_block = jnp.right_shift(lax.broadcasted_iota(jnp.int32, (rows4, 1), 0), int(math.log2(ATTN_BLOCK)))

    for j in range(nsub):
        blk = i * nsub + j
        prev_bias = prev_band + jnp.where(blk > 0, 0.0, neg_inf)
        next_bias = next_band + jnp.where(blk < (n_total // ATTN_BLOCK) - 1, 0.0, neg_inf)
        r0 = j * ATTN_BLOCK
        for kh in range(N_KV_HEADS):
            kcols = slice(kh * LANES, (kh + 1) * LANES)
            qa = q_ref[0, r0:r0 + ATTN_BLOCK, (2 * kh) * LANES:(2 * kh + 1) * LANES]
            qb = q_ref[0, r0:r0 + ATTN_BLOCK, (2 * kh + 1) * LANES:(2 * kh + 2) * LANES]
            xq = jnp.concatenate([jnp.where(low, qa, zero), jnp.where(low, qb, zero),
                                  jnp.where(low, zero, qa), jnp.where(low, zero, qb)], axis=0)
            h0 = 4 * kh
            snk = LOG2_E * jnp.where(row_block == 0, sink_ref[h0],
                                     jnp.where(row_block == 1, sink_ref[h0 + 2],
                                               jnp.where(row_block == 2, sink_ref[h0 + 1], sink_ref[h0 + 3])))
            s1 = _dot_nt(xq, kwin[r0:r0 + span, kcols])
            s1 = jnp.concatenate([s1[:, :ATTN_BLOCK] + prev_bias, s1[:, ATTN_BLOCK:2 * ATTN_BLOCK],
                                  s1[:, 2 * ATTN_BLOCK:] + next_bias], axis=1)
            s2 = _dot_nt(xq, kx[:, kcols])
            m = jnp.maximum(jnp.maximum(jnp.max(s1, axis=-1, keepdims=True),
                                        jnp.max(s2, axis=-1, keepdims=True)), snk)
            e1 = jnp.exp2(s1 - m)
            e2 = jnp.exp2(s2 - m)
            den = (jnp.sum(e1, axis=-1, keepdims=True) + jnp.sum(e2, axis=-1, keepdims=True)
                   + jnp.exp2(snk - m))
            o = _dot(e1.astype(BF16), vwin[r0:r0 + span, kcols]) + _dot(e2.astype(BF16), vx[:, kcols])
            o = o / den
            for pr in range(2):
                o_low = o[pr * ATTN_BLOCK:(pr + 1) * ATTN_BLOCK]
                o_high = o[(2 + pr) * ATTN_BLOCK:(3 + pr) * ATTN_BLOCK]
                c0 = POOL_WIDTH + (2 * kh + pr) * LANES
                mix_ref[r0:r0 + ATTN_BLOCK, c0:c0 + LANES] = jnp.where(low, o_low, o_high).astype(BF16)

    y = _dot(mix_ref[...], wo_ref[...])
    _tail(y, h_ref, gate_ref, g2_ref, sh_ref, sc_ref, rwt_ref, rb_ref, hout_ref, fx_ref, bk_ref)


def _odd_out_call(sink, up, q, kd, vd, kvx, pool_w, pool_scale, w_out,
                  h, gate, g2, shift, scale, rwt, rb, *, tq):
    b, n, d = h.shape
    row = lambda bb, i: (bb, i, 0)
    nb8, t8 = n // POOL_HALO, tq // POOL_HALO
    nbk, tk = n // ATTN_BLOCK, tq // ATTN_BLOCK
    prev8 = lambda bb, i: (bb, jnp.maximum(i * t8 - 1, 0), 0)
    next8 = lambda bb, i: (bb, jnp.minimum((i + 1) * t8, nb8 - 1), 0)
    prevk = lambda bb, i: (bb, jnp.maximum(i * tk - 1, 0), 0)
    nextk = lambda bb, i: (bb, jnp.minimum((i + 1) * tk, nbk - 1), 0)
    kvw = 2 * KV_WIDTH
    tin, out_shape, out_specs = _tail_specs(h, gate, shift, scale, tq)
    in_specs = [pl.BlockSpec(memory_space=pltpu.SMEM),
                pl.BlockSpec((1, tq, POOL_WIDTH), row),
                pl.BlockSpec((1, POOL_HALO, POOL_WIDTH), prev8),
                pl.BlockSpec((1, POOL_HALO, POOL_WIDTH), next8),
                pl.BlockSpec((1, tq, ATTN_WIDTH), row),
                pl.BlockSpec((1, tq, kvw), row),
                pl.BlockSpec((1, ATTN_BLOCK, kvw), prevk),
                pl.BlockSpec((1, ATTN_BLOCK, kvw), nextk),
                pl.BlockSpec((1, tq, kvw), row),
                pl.BlockSpec((1, ATTN_BLOCK, kvw), prevk),
                pl.BlockSpec((1, ATTN_BLOCK, kvw), nextk),
                pl.BlockSpec((1, kvx.shape[1], 2 * kvw), lambda bb, i: (bb, 0, 0)),
                pl.BlockSpec(pool_w.shape, lambda bb, i: (0, 0, 0)),
                pl.BlockSpec((1, POOL_WIDTH), lambda bb, i: (0, 0)),
                pl.BlockSpec((d, d), lambda bb, i: (0, 0))] + tin
    return pl.pallas_call(
        functools.partial(_odd_out_kernel, n_total=n),
        grid=(b, n // tq),
        in_specs=in_specs,
        out_specs=out_specs,
        out_shape=out_shape,
        scratch_shapes=[pltpu.VMEM((tq + 2 * POOL_HALO, POOL_WIDTH), F32),
                        pltpu.VMEM((tq, d), BF16)],
        compiler_params=_cparams(("arbitrary", "arbitrary")),
        name="odd_out",
    )(sink, up, up, up, q, kd, kd, kd, vd, vd, vd, kvx, pool_w, pool_scale.reshape(1, POOL_WIDTH),
      w_out, h, gate, g2.reshape(1, d), shift, scale, rwt, rb)


def _moe_kernel(e_lo_ref, e_hi_ref, valid_ref, x_ref,
                g1_ref, u1_ref, d1_ref, g2_ref, u2_ref, d2_ref, o_ref):
    j = pl.program_id(0)

    @pl.when(valid_ref[j] != 0)
    def _():
        x = x_ref[:, :D_MODEL].astype(BF16)
        w_lo = x_ref[:, D_MODEL + INFO_W_LO:D_MODEL + INFO_W_LO + 1]
        w_hi = x_ref[:, D_MODEL + INFO_W_HI:D_MODEL + INFO_W_HI + 1]

        def expert(g_ref, u_ref, d_ref):
            gate = _dot(x, g_ref[0, 0].astype(BF16))
            hid = gate * jax.nn.sigmoid(gate) * _dot(x, u_ref[0, 0].astype(BF16))
            return _dot(hid.astype(BF16), d_ref[0, 0].astype(BF16))

        o_lo = expert(g1_ref, u1_ref, d1_ref)
        o_hi = expert(g2_ref, u2_ref, d2_ref)
        o_ref[...] = w_lo * o_lo + w_hi * o_hi

    @pl.when(valid_ref[j] == 0)
    def _():
        o_ref[...] = jnp.zeros(o_ref.shape, o_ref.dtype)


def _moe_call(tile_lo, tile_hi, tile_valid, xs, wg, wu, wd, *, layer, tm):
    p = xs.shape[0]
    d = D_MODEL
    ntiles = p // tm
    lo4 = lambda j, lo, hi, v: (layer, lo[j], 0, 0)
    hi4 = lambda j, lo, hi, v: (layer, hi[j], 0, 0)
    rowm = lambda j, lo, hi, v: (j, 0)
    row_in = lambda j, lo, hi, v: (jnp.where(v[j] != 0, j, 0), 0)
    grid_spec = pltpu.PrefetchScalarGridSpec(
        num_scalar_prefetch=3,
        grid=(ntiles,),
        in_specs=[pl.BlockSpec((tm, ROW_WIDTH), row_in),
                  pl.BlockSpec((1, 1, d, D_EXPERT), lo4), pl.BlockSpec((1, 1, d, D_EXPERT), lo4),
                  pl.BlockSpec((1, 1, D_EXPERT, d), lo4),
                  pl.BlockSpec((1, 1, d, D_EXPERT), hi4), pl.BlockSpec((1, 1, d, D_EXPERT), hi4),
                  pl.BlockSpec((1, 1, D_EXPERT, d), hi4)],
        out_specs=pl.BlockSpec((tm, d), rowm),
    )
    return pl.pallas_call(
        _moe_kernel,
        grid_spec=grid_spec,
        out_shape=jax.ShapeDtypeStruct((p, d), F32),
        compiler_params=_cparams(("arbitrary",)),
        name="moe_pairs",
    )(tile_lo, tile_hi, tile_valid, xs, wg, wu, wd, wg, wu, wd)


RANK_ROWS = 32


def _rank_kernel(bk_ref, tri_ref, rk_ref, cnt_ref, carry_ref):
    @pl.when(pl.program_id(0) == 0)
    def _():
        carry_ref[...] = jnp.zeros(carry_ref.shape, F32)

    tr = bk_ref.shape[1]
    bucket = bk_ref[0:1, :]
    rows = lax.broadcasted_iota(jnp.int32, (RANK_ROWS, tr), 0).astype(F32)
    onehot = jnp.where(rows == bucket, 1.0, 0.0)
    before = _dot(onehot.astype(BF16), tri_ref[...]) + carry_ref[:, 0:1]
    rank = jnp.sum(onehot * before, axis=0, keepdims=True)
    rk_ref[...] = jnp.broadcast_to(rank, rk_ref.shape)
    carry_ref[...] = carry_ref[...] + jnp.sum(onehot, axis=1, keepdims=True)
    cnt_ref[...] = carry_ref[...]


def _rank_call(bk, *, tr):
    t = bk.shape[1]
    tri = jnp.asarray(np.triu(np.ones((tr, tr), np.float32), 1), BF16)
    return pl.pallas_call(
        _rank_kernel,
        grid=(t // tr,),
        in_specs=[pl.BlockSpec((8, tr), lambda j: (0, j)), pl.BlockSpec((tr, tr), lambda j: (0, 0))],
        out_specs=[pl.BlockSpec((8, tr), lambda j: (0, j)), pl.BlockSpec((RANK_ROWS, LANES), lambda j: (0, 0))],
        out_shape=[jax.ShapeDtypeStruct((8, t), F32), jax.ShapeDtypeStruct((RANK_ROWS, LANES), F32)],
        scratch_shapes=[pltpu.VMEM((RANK_ROWS, LANES), F32)],
        compiler_params=_cparams(("arbitrary",)),
        name="bucket_rank",
    )(bk, tri)


ROW_TILE = 512


def _scatter_rows_kernel(tile_end_ref, pos_ref, *refs, tile_starts, tm):
    nsrc = len(tile_starts)
    srcs, out_hbm, zero_ref, sem = refs[:nsrc], refs[nsrc], refs[nsrc + 1], refs[nsrc + 2]
    j = pl.program_id(0)

    @pl.when(j == 0)
    def _():
        zero_ref[...] = jnp.zeros(zero_ref.shape, F32)

        def fill(tile, start):
            copy = pltpu.make_async_copy(zero_ref, out_hbm.at[pl.ds(tile * tm, tm)], sem)
            copy.start() if start else copy.wait()

        n_slot_tiles = out_hbm.shape[0] // tm
        used = tile_end_ref[N_BUCKETS - 1]
        for start in (True, False):
            for b in range(N_BUCKETS):
                first_tile = tile_end_ref[b - 1] if b else 0
                pl.when(tile_end_ref[b] > first_tile)(functools.partial(fill, tile_end_ref[b] - 1, start))
                pl.when(n_slot_tiles - 1 - b >= used)(functools.partial(fill, n_slot_tiles - 1 - b, start))

    def move(src_vmem):
        for r in range(ROW_TILE):
            pltpu.make_async_copy(src_vmem.at[pl.ds(r, 1)],
                                  out_hbm.at[pl.ds(pos_ref[0, 0, r], 1)], sem).start(priority=r % 2)
        pltpu.make_async_copy(src_vmem, out_hbm.at[pl.ds(0, ROW_TILE)], sem).wait()

    for s in range(nsrc):
        first = tile_starts[s]
        if nsrc == 1:
            move(srcs[s])
        else:
            in_range = (j >= first) if s == nsrc - 1 else ((j >= first) & (j < tile_starts[s + 1]))
            pl.when(in_range)(functools.partial(move, srcs[s]))


def _scatter_rows_call(pos, tile_end, sources, nslots, tm):
    w = sources[0].shape[1]
    t = pos.shape[0]
    ntiles = t // ROW_TILE
    tile_starts, src_specs, acc = [], [], 0
    for s in sources:
        first, count = acc, s.shape[0] // ROW_TILE
        tile_starts.append(first)
        src_specs.append(pl.BlockSpec(
            (ROW_TILE, w), lambda j, te, first=first, count=count: (jnp.clip(j - first, 0, count - 1), 0)))
        acc += count
    grid_spec = pltpu.PrefetchScalarGridSpec(
        num_scalar_prefetch=1,
        grid=(ntiles,),
        in_specs=[pl.BlockSpec((1, 1, ROW_TILE), lambda j, te: (j, 0, 0), memory_space=pltpu.SMEM)] + src_specs,
        out_specs=pl.BlockSpec(memory_space=pl.ANY),
        scratch_shapes=[pltpu.VMEM((tm, w), F32), pltpu.SemaphoreType.DMA(())],
    )
    return pl.pallas_call(
        functools.partial(_scatter_rows_kernel, tile_starts=tuple(tile_starts), tm=tm),
        grid_spec=grid_spec,
        out_shape=jax.ShapeDtypeStruct((nslots, w), F32),
        compiler_params=_cparams(("arbitrary",)),
        name="scatter_rows",
    )(tile_end, pos.reshape(ntiles, 1, ROW_TILE), *sources)


def _final_kernel(h_ref, *refs):
    pos_refs = refs[:ROW_BUFFERS]
    moe_hbm, gate_ref, g_ref, o_ref, moe_bufs, moe_sems = refs[ROW_BUFFERS:]
    step = pl.program_id(0) * pl.num_programs(1) + pl.program_id(1)
    nsteps = pl.num_programs(0) * pl.num_programs(1)
    x = h_ref[0] + gate_ref[0] * _gathered_rows(step, pos_refs, moe_hbm, moe_bufs, moe_sems)
    o_ref[0] = x * lax.rsqrt(jnp.mean(x * x, axis=-1, keepdims=True) + EPS) * g_ref[...]
    _request_rows_ahead(step, nsteps, pos_refs, moe_hbm, moe_bufs, moe_sems)


def _final_call(h, moe, pos, gate, g, *, tm):
    b, n, d = h.shape
    row = lambda bb, i: (bb, i, 0)
    pos_tiles, pos_specs = _pos_tiles(pos, tm, n // tm, 0)
    return pl.pallas_call(
        _final_kernel,
        grid=(b, n // tm),
        in_specs=[pl.BlockSpec((1, tm, d), row)] + pos_specs + [
            pl.BlockSpec(memory_space=pl.ANY), _mod_spec(gate), pl.BlockSpec((1, d), lambda bb, i: (0, 0))],
        out_specs=pl.BlockSpec((1, tm, d), row),
        out_shape=jax.ShapeDtypeStruct((b, n, d), F32),
        scratch_shapes=[pltpu.VMEM((ROW_BUFFERS, tm, d), F32), pltpu.SemaphoreType.DMA((ROW_BUFFERS,))],
        compiler_params=_cparams(("arbitrary", "arbitrary")),
        name="final_norm",
    )(h, *([pos_tiles] * ROW_BUFFERS), moe, gate, g.reshape(1, d))


def _channel_dft_table():
    c = np.arange(FOURIER_HEAD_DIM)
    ang = 2.0 * np.pi * ((c[:, None] * c[None, :]) % FOURIER_HEAD_DIM) / FOURIER_HEAD_DIM
    return jnp.asarray(np.concatenate([np.cos(ang), np.sin(ang)], axis=1), F32)


def _position_dft_tables(n):
    n1 = n // DFT_RADIX
    t = np.arange(n // 2)
    a = 2.0 * np.pi * ((np.arange(n1)[:, None] * t[None, :]) % n1) / n1
    bb = 2.0 * np.pi * ((np.arange(DFT_RADIX)[:, None] * t[None, :]) % n) / n
    return tuple(jnp.asarray(v, F32) for v in (np.cos(a), np.sin(a), np.cos(bb), np.sin(bb)))


def _rope_tables(n):
    quarter = HEAD_DIM // 4
    inv = ROPE_THETA ** (-jnp.arange(quarter, dtype=F32) / quarter)
    t = jnp.arange(n)
    ang_r = (t // GRID_W).astype(F32)[:, None] * inv
    ang_c = (t % GRID_W).astype(F32)[:, None] * inv
    cos = jnp.concatenate([jnp.cos(ang_r)] * 2 + [jnp.cos(ang_c)] * 2, axis=1)
    sin = jnp.concatenate([-jnp.sin(ang_r), jnp.sin(ang_r), -jnp.sin(ang_c), jnp.sin(ang_c)], axis=1)
    return jnp.tile(cos, (1, 2)), jnp.tile(sin, (1, 2))


def _dispatch_plan(bucket, rank, counts, tm):
    t = bucket.shape[0]
    ntiles = t // tm + N_BUCKETS
    tiles_per = (counts + tm - 1) // tm
    tile_end = jnp.cumsum(tiles_per)
    tile_start = tile_end - tiles_per
    onehot = bucket[:, None] == jnp.arange(N_BUCKETS, dtype=jnp.int32)[None, :]
    pos = jnp.sum(jnp.where(onehot, (tile_start * tm)[None, :], 0), axis=-1) + rank
    tile_ids = jnp.arange(ntiles, dtype=jnp.int32)
    used = tile_end[-1]
    tile_bucket = jnp.sum((tile_ids[:, None] >= tile_end[None, :]).astype(jnp.int32), axis=1)
    last_bucket = jnp.sum((jnp.maximum(used - 1, 0) >= tile_end).astype(jnp.int32))
    tile_valid = (tile_ids < used).astype(jnp.int32)
    tile_bucket = jnp.where(tile_valid == 1, tile_bucket, last_bucket)
    tile_lo = jnp.asarray(np.asarray(_BUCKET_LO, np.int32))[tile_bucket]
    tile_hi = jnp.asarray(np.asarray(_BUCKET_HI, np.int32))[tile_bucket]
    return pos.astype(jnp.int32), tile_end.astype(jnp.int32), tile_lo, tile_hi, tile_valid, ntiles * tm


def _moe_layer(fx_list, bk_list, wg, wu, wd, *, layer, tm):
    bk = bk_list[0] if len(bk_list) == 1 else jnp.concatenate(bk_list, axis=1)
    rk, cnt = _rank_call(bk, tr=ROW_TILE)
    pos, tile_end, tile_lo, tile_hi, tile_valid, nslots = _dispatch_plan(
        bk[0].astype(jnp.int32), rk[0].astype(jnp.int32), cnt[:N_BUCKETS, 0].astype(jnp.int32), tm)
    xs = _scatter_rows_call(pos, tile_end, fx_list, nslots, tm)
    return _moe_call(tile_lo, tile_hi, tile_valid, xs, wg, wu, wd, layer=layer, tm=tm), pos


def _forward(x, c, ctx, c_ctx, ada_w, ada_b, norm_mix_g, norm_ffn_g, even_w_in, even_conv_w, even_w_out,
             odd_w_in, odd_pool_w, odd_pool_scale, odd_sink, odd_w_out, router_w, router_b,
             moe_w_gate, moe_w_up, moe_w_down, final_g, *, tm_lat, tm_ctx, tq, tm_dft, tm_moe):
    b, n, d = x.shape
    l = ctx.shape[1]

    rows = ((b + 1 + 7) // 8) * 8
    s_rows = jnp.zeros((rows, d), F32).at[:b].set(c).at[b].set(c_ctx)
    mods = _ada_call(s_rows, ada_w, ada_b)

    def mod_vecs(layer):
        m = mods[layer, :b].reshape(b, N_MOD, 1, d)
        mc = mods[layer, b].reshape(N_MOD, 1, 1, d)
        return [m[:, k] for k in range(N_MOD)], [mc[k] for k in range(N_MOD)]

    rw_t = router_w.T
    rw_hi = rw_t.astype(BF16)
    rw_lo = (rw_t - rw_hi.astype(F32)).astype(BF16)
    rwt = jnp.concatenate([rw_hi, rw_lo], axis=0)
    rb = router_b.astype(F32).reshape(N_EXPERTS, 1)
    cs_tab = _channel_dft_table()

    m, mc = mod_vecs(0)
    w_in0 = even_w_in[0].astype(BF16)
    w_out0 = even_w_out[0].astype(BF16)

    def even_stream(h, mv, tm):
        nn = h.shape[1]
        uc, us, gb, gu = _inproj_call(h, norm_mix_g[0], mv[0], mv[1], w_in0, mode="even", tm=tm,
                                      extra=(cs_tab,))
        yf = _dft_call(_position_dft_tables(nn), uc, us, tm=min(tm_dft, nn))
        return _even_out_call(yf, gb, gu, even_conv_w[0], w_out0, h, mv[2], norm_ffn_g[0],
                              mv[3], mv[4], rwt, rb, tm=tm)

    h1, fx_lat, bk_lat = even_stream(x, m, tm_lat)
    hc1, fx_ctx, bk_ctx = even_stream(ctx, mc, tm_ctx)

    moe0, pos0 = _moe_layer([fx_lat.reshape(b * n, ROW_WIDTH), fx_ctx.reshape(b * l, ROW_WIDTH)],
                            [bk_lat, bk_ctx], moe_w_gate, moe_w_up, moe_w_down, layer=0, tm=tm_moe)
    gate_lat0, gate_ctx0 = m[5], mc[5]

    m, mc = mod_vecs(1)
    w_in1 = odd_w_in[0]
    kv0 = POOL_WIDTH + ATTN_WIDTH
    wk, wv = w_in1[:, kv0:kv0 + KV_WIDTH], w_in1[:, kv0 + KV_WIDTH:]

    def dup_heads(wm):
        return jnp.concatenate([wm[:, :HEAD_DIM], wm[:, :HEAD_DIM], wm[:, HEAD_DIM:], wm[:, HEAD_DIM:]], axis=1)

    w_kv_dup = jnp.concatenate([dup_heads(wk), dup_heads(wv)], axis=1)
    w_lat1 = jnp.concatenate([w_in1[:, :kv0], w_kv_dup], axis=1).astype(BF16)
    w_out1 = odd_w_out[0].astype(BF16)

    cos_t, sin_t = _rope_tables(n)
    q_scale = HEAD_DIM ** -0.5 * LOG2_E
    h1b, up, q, kd, vd = _inproj_call(h1, norm_mix_g[1], m[0], m[1], w_lat1, mode="odd", tm=tm_lat,
                                      moe=moe0, pos=pos0, moe_row0=0, gate=gate_lat0,
                                      extra=(cos_t * q_scale, sin_t * q_scale, cos_t, sin_t))
    (kvx,) = _inproj_call(hc1, norm_mix_g[1], mc[0], mc[1], w_kv_dup.astype(BF16), mode="plain", tm=tm_ctx,
                          moe=moe0, pos=pos0, moe_row0=b * n, gate=gate_ctx0)
    h2, fx2, bk2 = _odd_out_call(odd_sink[0], up, q, kd, vd, kvx, odd_pool_w[0].astype(BF16),
                                 odd_pool_scale[0], w_out1, h1b, m[2], norm_ffn_g[1], m[3], m[4], rwt, rb, tq=tq)
    moe1, pos1 = _moe_layer([fx2.reshape(b * n, ROW_WIDTH)], [bk2], moe_w_gate, moe_w_up, moe_w_down,
                            layer=1, tm=tm_moe)
    return _final_call(h2, moe1, pos1, m[5], final_g, tm=tm_lat)


def kernel(x, c, ctx, c_ctx, ada_w, ada_b, norm_mix_g, norm_ffn_g, even_w_in, even_conv_w, even_w_out,
           odd_w_in, odd_pool_w, odd_pool_scale, odd_sink, odd_w_out, router_w, router_b,
           moe_w_gate, moe_w_up, moe_w_down, final_g):
    return _forward(x, c, ctx, c_ctx, ada_w, ada_b, norm_mix_g, norm_ffn_g, even_w_in, even_conv_w,
                    even_w_out, odd_w_in, odd_pool_w, odd_pool_scale, odd_sink, odd_w_out, router_w,
                    router_b, moe_w_gate, moe_w_up, moe_w_down, final_g,
                    tm_lat=512, tm_ctx=256, tq=512, tm_dft=1024, tm_moe=512)
```

```python
import functools
import math

import numpy as np
import jax
import jax.numpy as jnp
from jax import lax
from jax.experimental import pallas as pl
from jax.experimental.pallas import tpu as pltpu

F32 = jnp.float32
BF16 = jnp.bfloat16

D_MODEL = 1024
GRID_W = 64
EPS = 1e-6
N_MOD = 6
FOURIER_HEADS = 4
FOURIER_HEAD_DIM = 128
FOURIER_WIDTH = 512
CONV_WIDTH = 512
CONV_K = 3
POOL_WINDOWS = (2, 4, 8, 16)
POOL_GROUP_DIM = 128
POOL_WIDTH = 512
POOL_HALO = 8
HEAD_DIM = 64
N_Q_HEADS = 8
N_KV_HEADS = 2
ATTN_WIDTH = 512
KV_WIDTH = 128
ATTN_BLOCK = 128
ROPE_THETA = 10000.0
N_EXPERTS = 16
N_GROUPS = 4
EXPERTS_PER_GROUP = 4
D_EXPERT = 512
N_PAIRS = 6
N_BUCKETS = N_GROUPS * N_PAIRS

_PAIRS = [(a, b) for a in range(EXPERTS_PER_GROUP) for b in range(a + 1, EXPERTS_PER_GROUP)]
_BUCKET_LO = [(k // N_PAIRS) * EXPERTS_PER_GROUP + _PAIRS[k % N_PAIRS][0] for k in range(N_BUCKETS)]
_BUCKET_HI = [(k // N_PAIRS) * EXPERTS_PER_GROUP + _PAIRS[k % N_PAIRS][1] for k in range(N_BUCKETS)]

LANES = 128
ROW_WIDTH = D_MODEL + LANES
INFO_BUCKET, INFO_W_LO, INFO_W_HI = 0, 1, 2
VMEM_LIMIT_BYTES = 48 * 1024 * 1024

HIGHEST = lax.Precision.HIGHEST
LOG2_E = math.log2(math.e)


def _cparams(sem):
    return pltpu.CompilerParams(dimension_semantics=sem, vmem_limit_bytes=VMEM_LIMIT_BYTES)


def _rms_mod(x, g, shift, scale):
    y = x * lax.rsqrt(jnp.mean(x * x, axis=-1, keepdims=True) + EPS) * g
    return y * (1.0 + scale) + shift


def _dot(a, b):
    return jnp.dot(a, b, preferred_element_type=F32)


def _dot_nt(a, b):
    return lax.dot_general(a, b, (((1,), (1,)), ((), ())), preferred_element_type=F32)


def _ada_kernel(s_ref, w_ref, b_ref, o_ref):
    s = s_ref[...]
    s = s * jax.nn.sigmoid(s)
    o_ref[0] = jnp.dot(s, w_ref[0], preferred_element_type=F32, precision=HIGHEST) + b_ref[0]


def _ada_call(s_rows, ada_w, ada_b):
    depth, d, n6 = ada_w.shape
    r = s_rows.shape[0]
    tn = 1536
    return pl.pallas_call(
        _ada_kernel,
        grid=(depth, n6 // tn),
        in_specs=[
            pl.BlockSpec((r, d), lambda l, j: (0, 0)),
            pl.BlockSpec((1, d, tn), lambda l, j: (l, 0, j)),
            pl.BlockSpec((1, 1, tn), lambda l, j: (l, 0, j)),
        ],
        out_specs=pl.BlockSpec((1, r, tn), lambda l, j: (l, 0, j)),
        out_shape=jax.ShapeDtypeStruct((depth, r, n6), F32),
        compiler_params=_cparams(("arbitrary", "arbitrary")),
        name="ada_mod",
    )(s_rows, ada_w, ada_b.reshape(depth, 1, n6))


def _rope_group(x, cos, sin_signed):
    lane = lax.broadcasted_iota(jnp.int32, x.shape, 1)
    first_half = (lane % 32) < 16
    partner = jnp.where(first_half, pltpu.roll(x, LANES - 16, 1), pltpu.roll(x, 16, 1))
    return x * cos + partner * sin_signed


def _inproj_kernel(*refs, mode, add_moe):
    it = iter(refs)
    h_ref = next(it)
    if add_moe:
        pos_refs = [next(it) for _ in range(ROW_BUFFERS)]
        moe_hbm, gate_ref = next(it), next(it)
        moe_bufs, moe_sems = refs[-2], refs[-1]
        step = pl.program_id(0) * pl.num_programs(1) + pl.program_id(1)
        nsteps = pl.num_programs(0) * pl.num_programs(1)
        moe_rows = _gathered_rows(step, pos_refs, moe_hbm, moe_bufs, moe_sems)
    g_ref, sh_ref, sc_ref, w_ref = next(it), next(it), next(it), next(it)
    x = h_ref[0]
    if add_moe:
        x = x + gate_ref[0] * moe_rows
    a = _rms_mod(x, g_ref[...], sh_ref[0], sc_ref[0])
    proj = _dot(a.astype(BF16), w_ref[...])
    if mode == "even":
        cs_ref = next(it)
        uc_ref, us_ref, gb_ref, gu_ref = next(it), next(it), next(it), next(it)
        uf = proj[:, :FOURIER_WIDTH].astype(BF16)
        cs = cs_ref[...].astype(BF16)
        for hh in range(FOURIER_HEADS):
            cols = slice(hh * LANES, (hh + 1) * LANES)
            r = _dot(uf[:, cols], cs)
            uc_ref[0, :, cols] = r[:, :LANES].astype(BF16)
            us_ref[0, :, cols] = r[:, LANES:].astype(BF16)
        c0 = FOURIER_WIDTH
        gb_ref[0] = proj[:, c0:c0 + CONV_WIDTH]
        gu_ref[0] = proj[:, c0 + CONV_WIDTH:c0 + 2 * CONV_WIDTH] * proj[:, c0 + 2 * CONV_WIDTH:]
    elif mode == "odd":
        cq_ref, sq_ref, ck_ref, sk_ref = next(it), next(it), next(it), next(it)
        hn_ref, up_ref, q_ref, kd_ref, vd_ref = next(it), next(it), next(it), next(it), next(it)
        hn_ref[0] = x
        up_ref[0] = proj[:, :POOL_WIDTH]
        c0 = POOL_WIDTH
        for gi in range(ATTN_WIDTH // LANES):
            cols = slice(c0 + gi * LANES, c0 + (gi + 1) * LANES)
            q_ref[0, :, gi * LANES:(gi + 1) * LANES] = _rope_group(
                proj[:, cols], cq_ref[...], sq_ref[...]).astype(BF16)
        c0 += ATTN_WIDTH
        for gi in range(2 * KV_WIDTH // LANES):
            cols = slice(c0 + gi * LANES, c0 + (gi + 1) * LANES)
            kd_ref[0, :, gi * LANES:(gi + 1) * LANES] = _rope_group(
                proj[:, cols], ck_ref[...], sk_ref[...]).astype(BF16)
        c0 += 2 * KV_WIDTH
        vd_ref[0] = proj[:, c0:].astype(BF16)
    else:
        kv_ref = next(it)
        kv_ref[0] = proj.astype(BF16)
    if add_moe:
        _request_rows_ahead(step, nsteps, pos_refs, moe_hbm, moe_bufs, moe_sems)


def _mod_spec(arr):
    if arr.shape[0] > 1:
        return pl.BlockSpec((1, 1, arr.shape[2]), lambda b, i: (b, 0, 0))
    return pl.BlockSpec((1, 1, arr.shape[2]), lambda b, i: (0, 0, 0))


ROW_LOOKAHEAD = 2
ROW_BUFFERS = ROW_LOOKAHEAD + 1


def _gathered_rows(step, pos_refs, src_hbm, bufs, sems):
    nrows = bufs.shape[1]

    @pl.when(step == 0)
    def _():
        for ahead in range(ROW_LOOKAHEAD):
            def body(r, carry, ahead=ahead):
                pltpu.make_async_copy(src_hbm.at[pl.ds(pos_refs[ahead][0, 0, r], 1)],
                                      bufs.at[ahead, pl.ds(r, 1)], sems.at[ahead]).start()
                return carry
            lax.fori_loop(0, nrows, body, 0, unroll=8)

    slot = lax.rem(step, ROW_BUFFERS)
    pltpu.make_async_copy(src_hbm.at[pl.ds(0, nrows)], bufs.at[slot], sems.at[slot]).wait()
    return bufs[slot]


def _request_rows_ahead(step, nsteps, pos_refs, src_hbm, bufs, sems):
    nrows = bufs.shape[1]
    slot = lax.rem(step + ROW_LOOKAHEAD, ROW_BUFFERS)
    ahead_pos = pos_refs[ROW_LOOKAHEAD]
    for r in range(nrows):
        pltpu.make_async_copy(src_hbm.at[pl.ds(ahead_pos[0, 0, r], 1)], bufs.at[slot, pl.ds(r, 1)],
                              sems.at[slot]).start(priority=r % 2)

    @pl.when(step == nsteps - 1)
    def _():
        for ahead in range(1, ROW_BUFFERS):
            pending = lax.rem(step + ahead, ROW_BUFFERS)
            pltpu.make_async_copy(src_hbm.at[pl.ds(0, nrows)], bufs.at[pending], sems.at[pending]).wait()


def _pos_tiles(pos, tm, nt, row0):
    tile0 = row0 // tm
    ntiles = pos.shape[0] // tm
    specs = [pl.BlockSpec((1, 1, tm),
                          lambda bb, i, ahead=ahead: (jnp.minimum(tile0 + bb * nt + i + ahead, ntiles - 1), 0, 0),
                          memory_space=pltpu.SMEM) for ahead in range(ROW_BUFFERS)]
    return pos.reshape(ntiles, 1, tm), specs


def _inproj_call(h, g, shift, scale, w, *, mode, tm, moe=None, pos=None, moe_row0=0, gate=None, extra=()):
    b, n, d = h.shape
    nout = w.shape[1]
    add_moe = moe is not None
    row = lambda bb, i: (bb, i, 0)
    full2 = lambda bb, i: (0, 0)
    args = [h]
    in_specs = [pl.BlockSpec((1, tm, d), row)]
    scratch = []
    if add_moe:
        pos_tiles, pos_specs = _pos_tiles(pos, tm, n // tm, moe_row0)
        args += [pos_tiles] * ROW_BUFFERS + [moe, gate]
        in_specs += pos_specs + [pl.BlockSpec(memory_space=pl.ANY), _mod_spec(gate)]
        scratch = [pltpu.VMEM((ROW_BUFFERS, tm, d), F32), pltpu.SemaphoreType.DMA((ROW_BUFFERS,))]
    args += [g.reshape(1, d), shift, scale, w]
    in_specs += [pl.BlockSpec((1, d), full2), _mod_spec(shift), _mod_spec(scale),
                 pl.BlockSpec((d, nout), full2)]
    if mode == "even":
        args += list(extra)
        in_specs += [pl.BlockSpec(extra[0].shape, full2)]
        out_shape = [jax.ShapeDtypeStruct((b, n, FOURIER_WIDTH), BF16),
                     jax.ShapeDtypeStruct((b, n, FOURIER_WIDTH), BF16),
                     jax.ShapeDtypeStruct((b, n, CONV_WIDTH), F32),
                     jax.ShapeDtypeStruct((b, n, CONV_WIDTH), F32)]
        out_specs = [pl.BlockSpec((1, tm, 512), row)] * 4
    elif mode == "odd":
        args += list(extra)
        in_specs += [pl.BlockSpec((tm, LANES), lambda bb, i: (i, 0))] * 4
        out_shape = [jax.ShapeDtypeStruct((b, n, d), F32),
                     jax.ShapeDtypeStruct((b, n, POOL_WIDTH), F32),
                     jax.ShapeDtypeStruct((b, n, ATTN_WIDTH), BF16),
                     jax.ShapeDtypeStruct((b, n, 2 * KV_WIDTH), BF16),
                     jax.ShapeDtypeStruct((b, n, 2 * KV_WIDTH), BF16)]
        out_specs = [pl.BlockSpec((1, tm, d), row), pl.BlockSpec((1, tm, POOL_WIDTH), row),
                     pl.BlockSpec((1, tm, ATTN_WIDTH), row),
                     pl.BlockSpec((1, tm, 2 * KV_WIDTH), row), pl.BlockSpec((1, tm, 2 * KV_WIDTH), row)]
    else:
        out_shape = [jax.ShapeDtypeStruct((b, n, nout), BF16)]
        out_specs = [pl.BlockSpec((1, tm, nout), row)]
    return pl.pallas_call(
        functools.partial(_inproj_kernel, mode=mode, add_moe=add_moe),
        grid=(b, n // tm),
        in_specs=in_specs,
        out_specs=out_specs,
        out_shape=out_shape,
        scratch_shapes=scratch,
        compiler_params=_cparams(("arbitrary", "arbitrary")),
        name="inproj_" + mode,
    )(*args)


DFT_RADIX = 64


BF16_SUBLANES = 16


def _fold_kernel(rev_ref, uc_ref, ucm_ref, ucx_ref, us_ref, usm_ref, usx_ref, ue_ref, uo_ref):
    j = pl.program_id(1)
    tf = uc_ref.shape[1]
    first = lax.broadcasted_iota(jnp.int32, (tf, 1), 0) == 0

    def mirrored(m_ref, x_ref):
        return jnp.where(first, x_ref[0, 0:1, :].astype(F32), _dot(rev_ref[...], m_ref[0]))

    weight = jnp.where(first, jnp.where(j == 0, 0.5, 1.0), 1.0)
    ue_ref[0] = (weight * (uc_ref[0].astype(F32) + mirrored(ucm_ref, ucx_ref))).astype(BF16)
    uo_ref[0] = (us_ref[0].astype(F32) - mirrored(usm_ref, usx_ref)).astype(BF16)


def _fold_call(uc, us):
    b, n, wdt = uc.shape
    half = n // 2
    tf = min(512, half)
    nblk = n // tf
    rev = np.zeros((tf, tf), np.float32)
    rev[np.arange(1, tf), tf - np.arange(1, tf)] = 1.0
    direct = pl.BlockSpec((1, tf, wdt), lambda bb, j: (bb, j, 0))
    mirror = pl.BlockSpec((1, tf, wdt), lambda bb, j: (bb, nblk - 1 - j, 0))
    extra = pl.BlockSpec((1, BF16_SUBLANES, wdt),
                         lambda bb, j: (bb, jnp.where(j == 0, 0, (n - j * tf) // BF16_SUBLANES), 0))
    out = pl.BlockSpec((1, tf, wdt), lambda bb, j: (bb, j, 0))
    return pl.pallas_call(
        _fold_kernel,
        grid=(b, half // tf),
        in_specs=[pl.BlockSpec((tf, tf), lambda bb, j: (0, 0)), direct, mirror, extra, direct, mirror, extra],
        out_specs=[out, out],
        out_shape=[jax.ShapeDtypeStruct((b, half, wdt), BF16)] * 2,
        compiler_params=_cparams(("arbitrary", "arbitrary")),
        name="dft_fold",
    )(jnp.asarray(rev, BF16), uc, uc, uc, us, us, us)


def _dft_kernel(ca_ref, sa_ref, cb_ref, sb_ref, ue_ref, uo_ref, mid_ref, o_ref, c_scr, s_scr, *, norm):
    @pl.when(pl.program_id(1) == 0)
    def _():
        cb, sb = cb_ref[...], sb_ref[...]
        for r in range(ca_ref.shape[0]):
            ca, sa = ca_ref[r:r + 1, :], sa_ref[r:r + 1, :]
            rows = slice(r * DFT_RADIX, (r + 1) * DFT_RADIX)
            c_scr[rows, :] = (ca * cb - sa * sb).astype(BF16)
            s_scr[rows, :] = (-(sa * cb + ca * sb)).astype(BF16)

    tm = o_ref.shape[1]
    acc = _dot(c_scr[...], ue_ref[0]) + _dot(s_scr[...], uo_ref[0])
    k = pl.program_id(0) * tm + lax.broadcasted_iota(jnp.int32, (tm, 1), 0)
    sign = jnp.where((k & 1) == 0, 1.0, -1.0)
    acc = acc + sign * mid_ref[0, 0:1, :].astype(F32)
    o_ref[0] = (acc * norm).astype(BF16)


def _dft_call(tabs, uc, us, *, tm):
    b, n, wdt = uc.shape
    half = n // 2
    ue, uo = _fold_call(uc, us)
    r_tile = tm // DFT_RADIX
    norm = 1.0 / math.sqrt(n * FOURIER_HEAD_DIM)
    a_spec = pl.BlockSpec((r_tile, half), lambda i, bb: (i, 0))
    b_spec = pl.BlockSpec((DFT_RADIX, half), lambda i, bb: (0, 0))
    u_spec = pl.BlockSpec((1, half, wdt), lambda i, bb: (bb, 0, 0))
    mid_spec = pl.BlockSpec((1, BF16_SUBLANES, wdt), lambda i, bb: (bb, half // BF16_SUBLANES, 0))
    return pl.pallas_call(
        functools.partial(_dft_kernel, norm=norm),
        grid=(n // tm, b),
        in_specs=[a_spec, a_spec, b_spec, b_spec, u_spec, u_spec, mid_spec],
        out_specs=pl.BlockSpec((1, tm, wdt), lambda i, bb: (bb, i, 0)),
        out_shape=jax.ShapeDtypeStruct((b, n, wdt), BF16),
        scratch_shapes=[pltpu.VMEM((tm, half), BF16), pltpu.VMEM((tm, half), BF16)],
        compiler_params=_cparams(("arbitrary", "arbitrary")),
        name="dft_rows",
    )(*tabs, ue, uo, uc)


def _tail(y, h_ref, gate_ref, g2_ref, sh_ref, sc_ref, rwt_ref, rb_ref, hout_ref, fx_ref, bk_ref):
    hn = h_ref[0] + gate_ref[0] * y
    hout_ref[0] = hn
    f = _rms_mod(hn, g2_ref[...], sh_ref[0], sc_ref[0])
    tm = f.shape[0]
    f_hi = f.astype(BF16)
    f_lo = (f - f_hi.astype(F32)).astype(BF16)
    both = _dot_nt(rwt_ref[...], f_hi)
    logits = (both[:N_EXPERTS] + both[N_EXPERTS:]) + _dot_nt(rwt_ref[:N_EXPERTS, :], f_lo)
    aff = jax.nn.sigmoid(logits)
    sel = aff + rb_ref[...]
    cands = []
    for bkt in range(N_BUCKETS):
        lo, hi = _BUCKET_LO[bkt], _BUCKET_HI[bkt]
        cands.append((sel[lo:lo + 1, :] + sel[hi:hi + 1, :], jnp.full((1, tm), float(bkt), F32),
                      aff[lo:lo + 1, :], aff[hi:hi + 1, :]))
    while len(cands) > 1:
        merged = []
        for k in range(0, len(cands) - 1, 2):
            left, right = cands[k], cands[k + 1]
            take_right = right[0] > left[0]
            merged.append(tuple(jnp.where(take_right, r, l) for l, r in zip(left, right)))
        if len(cands) % 2:
            merged.append(cands[-1])
        cands = merged
    _, bucket, a_lo, a_hi = cands[0]
    den = a_lo + a_hi
    info = jnp.concatenate([bucket, a_lo / den, a_hi / den, jnp.zeros((LANES - 3, tm), F32)], axis=0)
    fx_ref[0, :, :D_MODEL] = f
    fx_ref[0, :, D_MODEL:] = info.T
    bk_ref[...] = jnp.concatenate([bucket, jnp.zeros((7, tm), F32)], axis=0)


def _tail_specs(h, gate, shift, scale, tm):
    b, n, d = h.shape
    nt = n // tm
    row = lambda bb, i: (bb, i, 0)
    in_specs = [pl.BlockSpec((1, tm, d), row), _mod_spec(gate),
                pl.BlockSpec((1, d), lambda bb, i: (0, 0)), _mod_spec(shift), _mod_spec(scale),
                pl.BlockSpec((2 * N_EXPERTS, d), lambda bb, i: (0, 0)),
                pl.BlockSpec((N_EXPERTS, 1), lambda bb, i: (0, 0))]
    out_shape = [jax.ShapeDtypeStruct((b, n, d), F32), jax.ShapeDtypeStruct((b, n, ROW_WIDTH), F32),
                 jax.ShapeDtypeStruct((8, b * n), F32)]
    out_specs = [pl.BlockSpec((1, tm, d), row), pl.BlockSpec((1, tm, ROW_WIDTH), row),
                 pl.BlockSpec((8, tm), lambda bb, i: (0, bb * nt + i))]
    return in_specs, out_shape, out_specs


def _even_out_kernel(yf_ref, gb_ref, gu_ref, gp_ref, gn_ref, cw_ref, wo_ref,
                     h_ref, gate_ref, g2_ref, sh_ref, sc_ref, rwt_ref, rb_ref,
                     hout_ref, fx_ref, bk_ref):
    i = pl.program_id(1)
    last = pl.num_programs(1) - 1
    gu = gu_ref[0]
    tm = gu.shape[0]
    prev = jnp.where(i > 0, gp_ref[0, 7:8, :], 0.0)
    nxt = jnp.where(i < last, gn_ref[0, 0:1, :], 0.0)
    row = lax.broadcasted_iota(jnp.int32, gu.shape, 0)
    up = jnp.where(row == 0, prev, pltpu.roll(gu, 1, 0))
    dn = jnp.where(row == tm - 1, nxt, pltpu.roll(gu, tm - 1, 0))
    conv = up * cw_ref[0:1, :] + gu * cw_ref[1:2, :] + dn * cw_ref[2:3, :]
    yc = (gb_ref[0] * conv).astype(BF16)
    y = _dot(yf_ref[0], wo_ref[:FOURIER_WIDTH, :]) + _dot(yc, wo_ref[FOURIER_WIDTH:, :])
    _tail(y, h_ref, gate_ref, g2_ref, sh_ref, sc_ref, rwt_ref, rb_ref, hout_ref, fx_ref, bk_ref)


def _even_out_call(yf, gb, gu, conv_w, w_out, h, gate, g2, shift, scale, rwt, rb, *, tm):
    b, n, d = h.shape
    row = lambda bb, i: (bb, i, 0)
    nb8 = n // 8
    t8 = tm // 8
    tin, out_shape, out_specs = _tail_specs(h, gate, shift, scale, tm)
    in_specs = [pl.BlockSpec((1, tm, 512), row), pl.BlockSpec((1, tm, 512), row),
                pl.BlockSpec((1, tm, 512), row),
                pl.BlockSpec((1, 8, 512), lambda bb, i: (bb, jnp.maximum(i * t8 - 1, 0), 0)),
                pl.BlockSpec((1, 8, 512), lambda bb, i: (bb, jnp.minimum((i + 1) * t8, nb8 - 1), 0)),
                pl.BlockSpec((CONV_K, CONV_WIDTH), lambda bb, i: (0, 0)),
                pl.BlockSpec((d, d), lambda bb, i: (0, 0))] + tin
    return pl.pallas_call(
        _even_out_kernel,
        grid=(b, n // tm),
        in_specs=in_specs,
        out_specs=out_specs,
        out_shape=out_shape,
        compiler_params=_cparams(("arbitrary", "arbitrary")),
        name="even_out",
    )(yf, gb, gu, gu, gu, conv_w, w_out, h, gate, g2.reshape(1, d), shift, scale, rwt, rb)


def _odd_out_kernel(sink_ref, up_ref, upp_ref, upn_ref, q_ref, kc_ref, kp_ref, kn_ref,
                    vc_ref, vp_ref, vn_ref, kvx_ref, pw_ref, ps_ref, wo_ref,
                    h_ref, gate_ref, g2_ref, sh_ref, sc_ref, rwt_ref, rb_ref,
                    hout_ref, fx_ref, bk_ref, ext_ref, mix_ref, *, n_total):
    i = pl.program_id(1)
    last = pl.num_programs(1) - 1
    tq = q_ref.shape[1]
    nsub = tq // ATTN_BLOCK

    u = up_ref[0]
    ext_ref[0:POOL_HALO, :] = jnp.where(i > 0, upp_ref[0], 0.0)
    ext_ref[POOL_HALO:POOL_HALO + tq, :] = u
    ext_ref[POOL_HALO + tq:, :] = jnp.where(i < last, upn_ref[0], 0.0)
    t = i * tq + lax.broadcasted_iota(jnp.int32, (tq, LANES), 0)
    ext_rows = tq + 2 * POOL_HALO
    for gi, win in enumerate(POOL_WINDOWS):
        r = win // 2
        cols = slice(gi * LANES, (gi + 1) * LANES)
        x = ext_ref[:, cols]
        ahead = lambda a, k: pltpu.roll(a, ext_rows - k, 0)
        run = x
        span = 1
        while span < r:
            run = run + ahead(run, span)
            span *= 2
        win_sum = pltpu.roll(run, r, 0) + run + ahead(x, r)
        acc = win_sum[POOL_HALO:POOL_HALO + tq]
        cnt = (jnp.minimum(t + r + 1, n_total) - jnp.maximum(t - r, 0)).astype(F32)
        p = acc / cnt - u[:, cols]
        y = _dot(p.astype(BF16), pw_ref[gi]) * ps_ref[:, cols]
        mix_ref[:, cols] = y.astype(BF16)

    kwin = jnp.concatenate([kp_ref[0], kc_ref[0], kn_ref[0]], axis=0)
    vwin = jnp.concatenate([vp_ref[0], vc_ref[0], vn_ref[0]], axis=0)
    kvx = kvx_ref[0]
    kx, vx = kvx[:, :2 * KV_WIDTH], kvx[:, 2 * KV_WIDTH:]
    low = lax.broadcasted_iota(jnp.int32, (1, LANES), 1) < HEAD_DIM
    zero = jnp.zeros((), BF16)

    span = 3 * ATTN_BLOCK
    rows4 = 4 * ATTN_BLOCK
    qi = lax.broadcasted_iota(jnp.int32, (rows4, ATTN_BLOCK), 0) % ATTN_BLOCK
    kj = lax.broadcasted_iota(jnp.int32, (rows4, ATTN_BLOCK), 1)
    neg_inf = jnp.float32(-jnp.inf)
    prev_band = jnp.where(kj < qi, neg_inf, 0.0)
    next_band = jnp.where(kj > qi, neg_inf, 0.0)
    row_block = jnp.right_shift(lax.broadcasted_iota(jnp.int32, (rows4, 1), 0), int(math.log2(ATTN_BLOCK)))

    for j in range(nsub):
        blk = i * nsub + j
        prev_bias = prev_band + jnp.where(blk > 0, 0.0, neg_inf)
        next_bias = next_band + jnp.where(blk < (n_total // ATTN_BLOCK) - 1, 0.0, neg_inf)
        r0 = j * ATTN_BLOCK
        for kh in range(N_KV_HEADS):
            kcols = slice(kh * LANES, (kh + 1) * LANES)
            qa = q_ref[0, r0:r0 + ATTN_BLOCK, (2 * kh) * LANES:(2 * kh + 1) * LANES]
            qb = q_ref[0, r0:r0 + ATTN_BLOCK, (2 * kh + 1) * LANES:(2 * kh + 2) * LANES]
            xq = jnp.concatenate([jnp.where(low, qa, zero), jnp.where(low, qb, zero),
                                  jnp.where(low, zero, qa), jnp.where(low, zero, qb)], axis=0)
            h0 = 4 * kh
            snk = LOG2_E * jnp.where(row_block == 0, sink_ref[h0],
                                     jnp.where(row_block == 1, sink_ref[h0 + 2],
                                               jnp.where(row_block == 2, sink_ref[h0 + 1], sink_ref[h0 + 3])))
            s1 = _dot_nt(xq, kwin[r0:r0 + span, kcols])
            s1 = jnp.concatenate([s1[:, :ATTN_BLOCK] + prev_bias, s1[:, ATTN_BLOCK:2 * ATTN_BLOCK],
                                  s1[:, 2 * ATTN_BLOCK:] + next_bias], axis=1)
            s2 = _dot_nt(xq, kx[:, kcols])
            m = jnp.maximum(jnp.maximum(jnp.max(s1, axis=-1, keepdims=True),
                                        jnp.max(s2, axis=-1, keepdims=True)), snk)
            e1 = jnp.exp2(s1 - m)
            e2 = jnp.exp2(s2 - m)
            den = (jnp.sum(e1, axis=-1, keepdims=True) + jnp.sum(e2, axis=-1, keepdims=True)
                   + jnp.exp2(snk - m))
            o = _dot(e1.astype(BF16), vwin[r0:r0 + span, kcols]) + _dot(e2.astype(BF16), vx[:, kcols])
            o = o / den
            for pr in range(2):
                o_low = o[pr * ATTN_BLOCK:(pr + 1) * ATTN_BLOCK]
                o_high = o[(2 + pr) * ATTN_BLOCK:(3 + pr) * ATTN_BLOCK]
                c0 = POOL_WIDTH + (2 * kh + pr) * LANES
                mix_ref[r0:r0 + ATTN_BLOCK, c0:c0 + LANES] = jnp.where(low, o_low, o_high).astype(BF16)

    y = _dot(mix_ref[...], wo_ref[...])
    _tail(y, h_ref, gate_ref, g2_ref, sh_ref, sc_ref, rwt_ref, rb_ref, hout_ref, fx_ref, bk_ref)


def _odd_out_call(sink, up, q, kd, vd, kvx, pool_w, pool_scale, w_out,
                  h, gate, g2, shift, scale, rwt, rb, *, tq):
    b, n, d = h.shape
    row = lambda bb, i: (bb, i, 0)
    nb8, t8 = n // POOL_HALO, tq // POOL_HALO
    nbk, tk = n // ATTN_BLOCK, tq // ATTN_BLOCK
    prev8 = lambda bb, i: (bb, jnp.maximum(i * t8 - 1, 0), 0)
    next8 = lambda bb, i: (bb, jnp.minimum((i + 1) * t8, nb8 - 1), 0)
    prevk = lambda bb, i: (bb, jnp.maximum(i * tk - 1, 0), 0)
    nextk = lambda bb, i: (bb, jnp.minimum((i + 1) * tk, nbk - 1), 0)
    kvw = 2 * KV_WIDTH
    tin, out_shape, out_specs = _tail_specs(h, gate, shift, scale, tq)
    in_specs = [pl.BlockSpec(memory_space=pltpu.SMEM),
                pl.BlockSpec((1, tq, POOL_WIDTH), row),
                pl.BlockSpec((1, POOL_HALO, POOL_WIDTH), prev8),
                pl.BlockSpec((1, POOL_HALO, POOL_WIDTH), next8),
                pl.BlockSpec((1, tq, ATTN_WIDTH), row),
                pl.BlockSpec((1, tq, kvw), row),
                pl.BlockSpec((1, ATTN_BLOCK, kvw), prevk),
                pl.BlockSpec((1, ATTN_BLOCK, kvw), nextk),
                pl.BlockSpec((1, tq, kvw), row),
                pl.BlockSpec((1, ATTN_BLOCK, kvw), prevk),
                pl.BlockSpec((1, ATTN_BLOCK, kvw), nextk),
                pl.BlockSpec((1, kvx.shape[1], 2 * kvw), lambda bb, i: (bb, 0, 0)),
                pl.BlockSpec(pool_w.shape, lambda bb, i: (0, 0, 0)),
                pl.BlockSpec((1, POOL_WIDTH), lambda bb, i: (0, 0)),
                pl.BlockSpec((d, d), lambda bb, i: (0, 0))] + tin
    return pl.pallas_call(
        functools.partial(_odd_out_kernel, n_total=n),
        grid=(b, n // tq),
        in_specs=in_specs,
        out_specs=out_specs,
        out_shape=out_shape,
        scratch_shapes=[pltpu.VMEM((tq + 2 * POOL_HALO, POOL_WIDTH), F32),
                        pltpu.VMEM((tq, d), BF16)],
        compiler_params=_cparams(("arbitrary", "arbitrary")),
        name="odd_out",
    )(sink, up, up, up, q, kd, kd, kd, vd, vd, vd, kvx, pool_w, pool_scale.reshape(1, POOL_WIDTH),
      w_out, h, gate, g2.reshape(1, d), shift, scale, rwt, rb)


def _moe_kernel(e_lo_ref, e_hi_ref, valid_ref, fresh_ref, x_ref,
                g1_ref, u1_ref, d1_ref, g2_ref, u2_ref, d2_ref, o_ref,
                g1_bf, u1_bf, d1_bf, g2_bf, u2_bf, d2_bf):
    j = pl.program_id(0)

    @pl.when(fresh_ref[j] != 0)
    def _():
        for src, dst in ((g1_ref, g1_bf), (u1_ref, u1_bf), (d1_ref, d1_bf),
                         (g2_ref, g2_bf), (u2_ref, u2_bf), (d2_ref, d2_bf)):
            dst[...] = src[0, 0].astype(BF16)

    @pl.when(valid_ref[j] != 0)
    def _():
        x = x_ref[:, :D_MODEL].astype(BF16)
        w_lo = x_ref[:, D_MODEL + INFO_W_LO:D_MODEL + INFO_W_LO + 1]
        w_hi = x_ref[:, D_MODEL + INFO_W_HI:D_MODEL + INFO_W_HI + 1]

        def expert(g_bf, u_bf, d_bf):
            gate = _dot(x, g_bf[...])
            hid = gate * jax.nn.sigmoid(gate) * _dot(x, u_bf[...])
            return _dot(hid.astype(BF16), d_bf[...])

        o_lo = expert(g1_bf, u1_bf, d1_bf)
        o_hi = expert(g2_bf, u2_bf, d2_bf)
        o_ref[...] = w_lo * o_lo + w_hi * o_hi

    @pl.when(valid_ref[j] == 0)
    def _():
        o_ref[...] = jnp.zeros(o_ref.shape, o_ref.dtype)


def _moe_call(tile_lo, tile_hi, tile_valid, tile_fresh, xs, wg, wu, wd, *, layer, tm):
    p = xs.shape[0]
    d = D_MODEL
    ntiles = p // tm
    lo4 = lambda j, lo, hi, v, f: (layer, lo[j], 0, 0)
    hi4 = lambda j, lo, hi, v, f: (layer, hi[j], 0, 0)
    rowm = lambda j, lo, hi, v, f: (j, 0)
    row_in = lambda j, lo, hi, v, f: (jnp.where(v[j] != 0, j, 0), 0)
    grid_spec = pltpu.PrefetchScalarGridSpec(
        num_scalar_prefetch=4,
        grid=(ntiles,),
        in_specs=[pl.BlockSpec((tm, ROW_WIDTH), row_in),
                  pl.BlockSpec((1, 1, d, D_EXPERT), lo4), pl.BlockSpec((1, 1, d, D_EXPERT), lo4),
                  pl.BlockSpec((1, 1, D_EXPERT, d), lo4),
                  pl.BlockSpec((1, 1, d, D_EXPERT), hi4), pl.BlockSpec((1, 1, d, D_EXPERT), hi4),
                  pl.BlockSpec((1, 1, D_EXPERT, d), hi4)],
        out_specs=pl.BlockSpec((tm, d), rowm),
        scratch_shapes=[pltpu.VMEM((d, D_EXPERT), BF16), pltpu.VMEM((d, D_EXPERT), BF16),
                        pltpu.VMEM((D_EXPERT, d), BF16)] * 2,
    )
    return pl.pallas_call(
        _moe_kernel,
        grid_spec=grid_spec,
        out_shape=jax.ShapeDtypeStruct((p, d), F32),
        compiler_params=_cparams(("arbitrary",)),
        name="moe_pairs",
    )(tile_lo, tile_hi, tile_valid, tile_fresh, xs, wg, wu, wd, wg, wu, wd)


RANK_ROWS = 32


def _rank_kernel(bk_ref, tri_ref, rk_ref, cnt_ref, carry_ref):
    @pl.when(pl.program_id(0) == 0)
    def _():
        carry_ref[...] = jnp.zeros(carry_ref.shape, F32)

    tr = bk_ref.shape[1]
    bucket = bk_ref[0:1, :]
    rows = lax.broadcasted_iota(jnp.int32, (RANK_ROWS, tr), 0).astype(F32)
    onehot = jnp.where(rows == bucket, 1.0, 0.0)
    before = _dot(onehot.astype(BF16), tri_ref[...]) + carry_ref[:, 0:1]
    rank = jnp.sum(onehot * before, axis=0, keepdims=True)
    rk_ref[...] = jnp.broadcast_to(rank, rk_ref.shape)
    carry_ref[...] = carry_ref[...] + jnp.sum(onehot, axis=1, keepdims=True)
    cnt_ref[...] = carry_ref[...]


def _rank_call(bk, *, tr):
    t = bk.shape[1]
    tri = jnp.asarray(np.triu(np.ones((tr, tr), np.float32), 1), BF16)
    return pl.pallas_call(
        _rank_kernel,
        grid=(t // tr,),
        in_specs=[pl.BlockSpec((8, tr), lambda j: (0, j)), pl.BlockSpec((tr, tr), lambda j: (0, 0))],
        out_specs=[pl.BlockSpec((8, tr), lambda j: (0, j)), pl.BlockSpec((RANK_ROWS, LANES), lambda j: (0, 0))],
        out_shape=[jax.ShapeDtypeStruct((8, t), F32), jax.ShapeDtypeStruct((RANK_ROWS, LANES), F32)],
        scratch_shapes=[pltpu.VMEM((RANK_ROWS, LANES), F32)],
        compiler_params=_cparams(("arbitrary",)),
        name="bucket_rank",
    )(bk, tri)


ROW_TILE = 512


def _scatter_rows_kernel(tile_end_ref, pos_ref, *refs, tile_starts, tm):
    nsrc = len(tile_starts)
    srcs, out_hbm, zero_ref, sem = refs[:nsrc], refs[nsrc], refs[nsrc + 1], refs[nsrc + 2]
    j = pl.program_id(0)

    @pl.when(j == 0)
    def _():
        zero_ref[...] = jnp.zeros(zero_ref.shape, F32)

        def fill(tile, start):
            copy = pltpu.make_async_copy(zero_ref, out_hbm.at[pl.ds(tile * tm, tm)], sem)
            copy.start() if start else copy.wait()

        n_slot_tiles = out_hbm.shape[0] // tm
        used = tile_end_ref[N_BUCKETS - 1]
        for start in (True, False):
            for b in range(N_BUCKETS):
                first_tile = tile_end_ref[b - 1] if b else 0
                pl.when(tile_end_ref[b] > first_tile)(functools.partial(fill, tile_end_ref[b] - 1, start))
                pl.when(n_slot_tiles - 1 - b >= used)(functools.partial(fill, n_slot_tiles - 1 - b, start))

    def move(src_vmem):
        for r in range(ROW_TILE):
            pltpu.make_async_copy(src_vmem.at[pl.ds(r, 1)],
                                  out_hbm.at[pl.ds(pos_ref[0, 0, r], 1)], sem).start(priority=r % 2)
        pltpu.make_async_copy(src_vmem, out_hbm.at[pl.ds(0, ROW_TILE)], sem).wait()

    for s in range(nsrc):
        first = tile_starts[s]
        if nsrc == 1:
            move(srcs[s])
        else:
            in_range = (j >= first) if s == nsrc - 1 else ((j >= first) & (j < tile_starts[s + 1]))
            pl.when(in_range)(functools.partial(move, srcs[s]))


def _scatter_rows_call(pos, tile_end, sources, nslots, tm):
    w = sources[0].shape[1]
    t = pos.shape[0]
    ntiles = t // ROW_TILE
    tile_starts, src_specs, acc = [], [], 0
    for s in sources:
        first, count = acc, s.shape[0] // ROW_TILE
        tile_starts.append(first)
        src_specs.append(pl.BlockSpec(
            (ROW_TILE, w), lambda j, te, first=first, count=count: (jnp.clip(j - first, 0, count - 1), 0)))
        acc += count
    grid_spec = pltpu.PrefetchScalarGridSpec(
        num_scalar_prefetch=1,
        grid=(ntiles,),
        in_specs=[pl.BlockSpec((1, 1, ROW_TILE), lambda j, te: (j, 0, 0), memory_space=pltpu.SMEM)] + src_specs,
        out_specs=pl.BlockSpec(memory_space=pl.ANY),
        scratch_shapes=[pltpu.VMEM((tm, w), F32), pltpu.SemaphoreType.DMA(())],
    )
    return pl.pallas_call(
        functools.partial(_scatter_rows_kernel, tile_starts=tuple(tile_starts), tm=tm),
        grid_spec=grid_spec,
        out_shape=jax.ShapeDtypeStruct((nslots, w), F32),
        compiler_params=_cparams(("arbitrary",)),
        name="scatter_rows",
    )(tile_end, pos.reshape(ntiles, 1, ROW_TILE), *sources)


def _final_kernel(h_ref, *refs):
    pos_refs = refs[:ROW_BUFFERS]
    moe_hbm, gate_ref, g_ref, o_ref, moe_bufs, moe_sems = refs[ROW_BUFFERS:]
    step = pl.program_id(0) * pl.num_programs(1) + pl.program_id(1)
    nsteps = pl.num_programs(0) * pl.num_programs(1)
    x = h_ref[0] + gate_ref[0] * _gathered_rows(step, pos_refs, moe_hbm, moe_bufs, moe_sems)
    o_ref[0] = x * lax.rsqrt(jnp.mean(x * x, axis=-1, keepdims=True) + EPS) * g_ref[...]
    _request_rows_ahead(step, nsteps, pos_refs, moe_hbm, moe_bufs, moe_sems)


def _final_call(h, moe, pos, gate, g, *, tm):
    b, n, d = h.shape
    row = lambda bb, i: (bb, i, 0)
    pos_tiles, pos_specs = _pos_tiles(pos, tm, n // tm, 0)
    return pl.pallas_call(
        _final_kernel,
        grid=(b, n // tm),
        in_specs=[pl.BlockSpec((1, tm, d), row)] + pos_specs + [
            pl.BlockSpec(memory_space=pl.ANY), _mod_spec(gate), pl.BlockSpec((1, d), lambda bb, i: (0, 0))],
        out_specs=pl.BlockSpec((1, tm, d), row),
        out_shape=jax.ShapeDtypeStruct((b, n, d), F32),
        scratch_shapes=[pltpu.VMEM((ROW_BUFFERS, tm, d), F32), pltpu.SemaphoreType.DMA((ROW_BUFFERS,))],
        compiler_params=_cparams(("arbitrary", "arbitrary")),
        name="final_norm",
    )(h, *([pos_tiles] * ROW_BUFFERS), moe, gate, g.reshape(1, d))


def _channel_dft_table():
    c = np.arange(FOURIER_HEAD_DIM)
    ang = 2.0 * np.pi * ((c[:, None] * c[None, :]) % FOURIER_HEAD_DIM) / FOURIER_HEAD_DIM
    return jnp.asarray(np.concatenate([np.cos(ang), np.sin(ang)], axis=1), F32)


def _position_dft_tables(n):
    n1 = n // DFT_RADIX
    t = np.arange(n // 2)
    a = 2.0 * np.pi * ((np.arange(n1)[:, None] * t[None, :]) % n1) / n1
    bb = 2.0 * np.pi * ((np.arange(DFT_RADIX)[:, None] * t[None, :]) % n) / n
    return tuple(jnp.asarray(v, F32) for v in (np.cos(a), np.sin(a), np.cos(bb), np.sin(bb)))


def _rope_tables(n):
    quarter = HEAD_DIM // 4
    inv = ROPE_THETA ** (-jnp.arange(quarter, dtype=F32) / quarter)
    t = jnp.arange(n)
    ang_r = (t // GRID_W).astype(F32)[:, None] * inv
    ang_c = (t % GRID_W).astype(F32)[:, None] * inv
    cos = jnp.concatenate([jnp.cos(ang_r)] * 2 + [jnp.cos(ang_c)] * 2, axis=1)
    sin = jnp.concatenate([-jnp.sin(ang_r), jnp.sin(ang_r), -jnp.sin(ang_c), jnp.sin(ang_c)], axis=1)
    return jnp.tile(cos, (1, 2)), jnp.tile(sin, (1, 2))


def _dispatch_plan(bucket, rank, counts, tm):
    t = bucket.shape[0]
    ntiles = t // tm + N_BUCKETS
    tiles_per = (counts + tm - 1) // tm
    tile_end = jnp.cumsum(tiles_per)
    tile_start = tile_end - tiles_per
    onehot = bucket[:, None] == jnp.arange(N_BUCKETS, dtype=jnp.int32)[None, :]
    pos = jnp.sum(jnp.where(onehot, (tile_start * tm)[None, :], 0), axis=-1) + rank
    tile_ids = jnp.arange(ntiles, dtype=jnp.int32)
    used = tile_end[-1]
    tile_bucket = jnp.sum((tile_ids[:, None] >= tile_end[None, :]).astype(jnp.int32), axis=1)
    last_bucket = jnp.sum((jnp.maximum(used - 1, 0) >= tile_end).astype(jnp.int32))
    tile_valid = (tile_ids < used).astype(jnp.int32)
    tile_bucket = jnp.where(tile_valid == 1, tile_bucket, last_bucket)
    tile_lo = jnp.asarray(np.asarray(_BUCKET_LO, np.int32))[tile_bucket]
    tile_hi = jnp.asarray(np.asarray(_BUCKET_HI, np.int32))[tile_bucket]
    tile_fresh = jnp.concatenate([jnp.ones((1,), jnp.int32),
                                  (tile_bucket[1:] != tile_bucket[:-1]).astype(jnp.int32)])
    return (pos.astype(jnp.int32), tile_end.astype(jnp.int32), tile_lo, tile_hi, tile_valid, tile_fresh,
            ntiles * tm)


def _moe_layer(fx_list, bk_list, wg, wu, wd, *, layer, tm):
    bk = bk_list[0] if len(bk_list) == 1 else jnp.concatenate(bk_list, axis=1)
    rk, cnt = _rank_call(bk, tr=ROW_TILE)
    pos, tile_end, tile_lo, tile_hi, tile_valid, tile_fresh, nslots = _dispatch_plan(
        bk[0].astype(jnp.int32), rk[0].astype(jnp.int32), cnt[:N_BUCKETS, 0].astype(jnp.int32), tm)
    xs = _scatter_rows_call(pos, tile_end, fx_list, nslots, tm)
    return _moe_call(tile_lo, tile_hi, tile_valid, tile_fresh, xs, wg, wu, wd, layer=layer, tm=tm), pos


def _forward(x, c, ctx, c_ctx, ada_w, ada_b, norm_mix_g, norm_ffn_g, even_w_in, even_conv_w, even_w_out,
             odd_w_in, odd_pool_w, odd_pool_scale, odd_sink, odd_w_out, router_w, router_b,
             moe_w_gate, moe_w_up, moe_w_down, final_g, *, tm_lat, tm_ctx, tq, tm_dft, tm_moe):
    b, n, d = x.shape
    l = ctx.shape[1]

    rows = ((b + 1 + 7) // 8) * 8
    s_rows = jnp.zeros((rows, d), F32).at[:b].set(c).at[b].set(c_ctx)
    mods = _ada_call(s_rows, ada_w, ada_b)

    def mod_vecs(layer):
        m = mods[layer, :b].reshape(b, N_MOD, 1, d)
        mc = mods[layer, b].reshape(N_MOD, 1, 1, d)
        return [m[:, k] for k in range(N_MOD)], [mc[k] for k in range(N_MOD)]

    rw_t = router_w.T
    rw_hi = rw_t.astype(BF16)
    rw_lo = (rw_t - rw_hi.astype(F32)).astype(BF16)
    rwt = jnp.concatenate([rw_hi, rw_lo], axis=0)
    rb = router_b.astype(F32).reshape(N_EXPERTS, 1)
    cs_tab = _channel_dft_table()

    m, mc = mod_vecs(0)
    w_in0 = even_w_in[0].astype(BF16)
    w_out0 = even_w_out[0].astype(BF16)

    def even_stream(h, mv, tm):
        nn = h.shape[1]
        uc, us, gb, gu = _inproj_call(h, norm_mix_g[0], mv[0], mv[1], w_in0, mode="even", tm=tm,
                                      extra=(cs_tab,))
        yf = _dft_call(_position_dft_tables(nn), uc, us, tm=min(tm_dft, nn))
        return _even_out_call(yf, gb, gu, even_conv_w[0], w_out0, h, mv[2], norm_ffn_g[0],
                              mv[3], mv[4], rwt, rb, tm=tm)

    h1, fx_lat, bk_lat = even_stream(x, m, tm_lat)
    hc1, fx_ctx, bk_ctx = even_stream(ctx, mc, tm_ctx)

    moe0, pos0 = _moe_layer([fx_lat.reshape(b * n, ROW_WIDTH), fx_ctx.reshape(b * l, ROW_WIDTH)],
                            [bk_lat, bk_ctx], moe_w_gate, moe_w_up, moe_w_down, layer=0, tm=tm_moe)
    gate_lat0, gate_ctx0 = m[5], mc[5]

    m, mc = mod_vecs(1)
    w_in1 = odd_w_in[0]
    kv0 = POOL_WIDTH + ATTN_WIDTH
    wk, wv = w_in1[:, kv0:kv0 + KV_WIDTH], w_in1[:, kv0 + KV_WIDTH:]

    def dup_heads(wm):
        return jnp.concatenate([wm[:, :HEAD_DIM], wm[:, :HEAD_DIM], wm[:, HEAD_DIM:], wm[:, HEAD_DIM:]], axis=1)

    w_kv_dup = jnp.concatenate([dup_heads(wk), dup_heads(wv)], axis=1)
    w_lat1 = jnp.concatenate([w_in1[:, :kv0], w_kv_dup], axis=1).astype(BF16)
    w_out1 = odd_w_out[0].astype(BF16)

    cos_t, sin_t = _rope_tables(n)
    q_scale = HEAD_DIM ** -0.5 * LOG2_E
    h1b, up, q, kd, vd = _inproj_call(h1, norm_mix_g[1], m[0], m[1], w_lat1, mode="odd", tm=tm_lat,
                                      moe=moe0, pos=pos0, moe_row0=0, gate=gate_lat0,
                                      extra=(cos_t * q_scale, sin_t * q_scale, cos_t, sin_t))
    (kvx,) = _inproj_call(hc1, norm_mix_g[1], mc[0], mc[1], w_kv_dup.astype(BF16), mode="plain", tm=tm_ctx,
                          moe=moe0, pos=pos0, moe_row0=b * n, gate=gate_ctx0)
    h2, fx2, bk2 = _odd_out_call(odd_sink[0], up, q, kd, vd, kvx, odd_pool_w[0].astype(BF16),
                                 odd_pool_scale[0], w_out1, h1b, m[2], norm_ffn_g[1], m[3], m[4], rwt, rb, tq=tq)
    moe1, pos1 = _moe_layer([fx2.reshape(b * n, ROW_WIDTH)], [bk2], moe_w_gate, moe_w_up, moe_w_down,
                            layer=1, tm=tm_moe)
    return _final_call(h2, moe1, pos1, m[5], final_g, tm=tm_lat)


def kernel(x, c, ctx, c_ctx, ada_w, ada_b, norm_mix_g, norm_ffn_g, even_w_in, even_conv_w, even_w_out,
           odd_w_in, odd_pool_w, odd_pool_scale, odd_sink, odd_w_out, router_w, router_b,
           moe_w_gate, moe_w_up, moe_w_down, final_g):
    return _forward(x, c, ctx, c_ctx, ada_w, ada_b, norm_mix_g, norm_ffn_g, even_w_in, even_conv_w,
                    even_w_out, odd_w_in, odd_pool_w, odd_pool_scale, odd_sink, odd_w_out, router_w,
                    router_b, moe_w_gate, moe_w_up, moe_w_down, final_g,
                    tm_lat=512, tm_ctx=256, tq=512, tm_dft=1024, tm_moe=512)
```

```python
import functools
import math

import numpy as np
import jax
import jax.numpy as jnp
from jax import lax
from jax.experimental import pallas as pl
from jax.experimental.pallas import tpu as pltpu

F32 = jnp.float32
BF16 = jnp.bfloat16

D_MODEL = 1024
GRID_W = 64
EPS = 1e-6
N_MOD = 6
FOURIER_HEADS = 4
FOURIER_HEAD_DIM = 128
FOURIER_WIDTH = 512
CONV_WIDTH = 512
CONV_K = 3
POOL_WINDOWS = (2, 4, 8, 16)
POOL_GROUP_DIM = 128
POOL_WIDTH = 512
POOL_HALO = 8
HEAD_DIM = 64
N_KV_HEADS = 2
ATTN_WIDTH = 512
KV_WIDTH = 128
ATTN_BLOCK = 128
ROPE_THETA = 10000.0
N_EXPERTS = 16
N_GROUPS = 4
EXPERTS_PER_GROUP = 4
D_EXPERT = 512
N_PAIRS = 6
N_BUCKETS = N_GROUPS * N_PAIRS

_PAIRS = [(a, b) for a in range(EXPERTS_PER_GROUP) for b in range(a + 1, EXPERTS_PER_GROUP)]
_BUCKET_LO = [(k // N_PAIRS) * EXPERTS_PER_GROUP + _PAIRS[k % N_PAIRS][0] for k in range(N_BUCKETS)]
_BUCKET_HI = [(k // N_PAIRS) * EXPERTS_PER_GROUP + _PAIRS[k % N_PAIRS][1] for k in range(N_BUCKETS)]

LANES = 128
F32_SUBLANES = 8
ROW_WIDTH = D_MODEL + LANES
INFO_BUCKET, INFO_W_LO, INFO_W_HI = 0, 1, 2
ADA_COL_TILE = 1536
VMEM_LIMIT_BYTES = 48 * 1024 * 1024

HIGHEST = lax.Precision.HIGHEST
LOG2_E = math.log2(math.e)


def _cparams(sem):
    return pltpu.CompilerParams(dimension_semantics=sem, vmem_limit_bytes=VMEM_LIMIT_BYTES)


def _rms_mod(x, g, shift, scale):
    y = x * lax.rsqrt(jnp.mean(x * x, axis=-1, keepdims=True) + EPS) * g
    return y * (1.0 + scale) + shift


def _dot(a, b):
    return jnp.dot(a, b, preferred_element_type=F32)


def _dot_nt(a, b):
    return lax.dot_general(a, b, (((1,), (1,)), ((), ())), preferred_element_type=F32)


def _ada_kernel(s_ref, w_ref, b_ref, o_ref):
    s = s_ref[...]
    s = s * jax.nn.sigmoid(s)
    o_ref[0] = jnp.dot(s, w_ref[0], preferred_element_type=F32, precision=HIGHEST) + b_ref[0]


def _ada_call(s_rows, ada_w, ada_b):
    depth, d, n6 = ada_w.shape
    r = s_rows.shape[0]
    tn = ADA_COL_TILE
    return pl.pallas_call(
        _ada_kernel,
        grid=(depth, n6 // tn),
        in_specs=[
            pl.BlockSpec((r, d), lambda l, j: (0, 0)),
            pl.BlockSpec((1, d, tn), lambda l, j: (l, 0, j)),
            pl.BlockSpec((1, 1, tn), lambda l, j: (l, 0, j)),
        ],
        out_specs=pl.BlockSpec((1, r, tn), lambda l, j: (l, 0, j)),
        out_shape=jax.ShapeDtypeStruct((depth, r, n6), F32),
        compiler_params=_cparams(("arbitrary", "arbitrary")),
        name="ada_mod",
    )(s_rows, ada_w, ada_b.reshape(depth, 1, n6))


def _rope_group(x, cos, sin_signed):
    lane = lax.broadcasted_iota(jnp.int32, x.shape, 1)
    quarter = HEAD_DIM // 4
    first_half = (lane % (2 * quarter)) < quarter
    partner = jnp.where(first_half, pltpu.roll(x, LANES - quarter, 1), pltpu.roll(x, quarter, 1))
    return x * cos + partner * sin_signed


def _inproj_kernel(*refs, mode, add_moe):
    it = iter(refs)
    h_ref = next(it)
    if add_moe:
        pos_refs = [next(it) for _ in range(ROW_BUFFERS)]
        moe_hbm, gate_ref = next(it), next(it)
        moe_bufs, moe_sems = refs[-2], refs[-1]
        step = pl.program_id(0) * pl.num_programs(1) + pl.program_id(1)
        nsteps = pl.num_programs(0) * pl.num_programs(1)
        moe_rows = _gathered_rows(step, pos_refs, moe_hbm, moe_bufs, moe_sems)
    g_ref, sh_ref, sc_ref, w_ref = next(it), next(it), next(it), next(it)
    x = h_ref[0]
    if add_moe:
        x = x + gate_ref[0] * moe_rows
    a = _rms_mod(x, g_ref[...], sh_ref[0], sc_ref[0])
    proj = _dot(a.astype(BF16), w_ref[...])
    if mode == "even":
        cs_ref = next(it)
        uc_ref, us_ref, gb_ref, gu_ref = next(it), next(it), next(it), next(it)
        uf = proj[:, :FOURIER_WIDTH].astype(BF16)
        cs = cs_ref[...].astype(BF16)
        for hh in range(FOURIER_HEADS):
            cols = slice(hh * LANES, (hh + 1) * LANES)
            r = _dot(uf[:, cols], cs)
            uc_ref[0, :, cols] = r[:, :LANES].astype(BF16)
            us_ref[0, :, cols] = r[:, LANES:].astype(BF16)
        c0 = FOURIER_WIDTH
        gb_ref[0] = proj[:, c0:c0 + CONV_WIDTH]
        gu_ref[0] = proj[:, c0 + CONV_WIDTH:c0 + 2 * CONV_WIDTH] * proj[:, c0 + 2 * CONV_WIDTH:]
    elif mode == "odd":
        cq_ref, sq_ref, ck_ref, sk_ref = next(it), next(it), next(it), next(it)
        hn_ref, up_ref, q_ref, kd_ref, vd_ref = next(it), next(it), next(it), next(it), next(it)
        hn_ref[0] = x
        up_ref[0] = proj[:, :POOL_WIDTH]
        c0 = POOL_WIDTH
        for gi in range(ATTN_WIDTH // LANES):
            cols = slice(c0 + gi * LANES, c0 + (gi + 1) * LANES)
            q_ref[0, :, gi * LANES:(gi + 1) * LANES] = _rope_group(
                proj[:, cols], cq_ref[...], sq_ref[...]).astype(BF16)
        c0 += ATTN_WIDTH
        for gi in range(2 * KV_WIDTH // LANES):
            cols = slice(c0 + gi * LANES, c0 + (gi + 1) * LANES)
            kd_ref[0, :, gi * LANES:(gi + 1) * LANES] = _rope_group(
                proj[:, cols], ck_ref[...], sk_ref[...]).astype(BF16)
        c0 += 2 * KV_WIDTH
        vd_ref[0] = proj[:, c0:].astype(BF16)
    else:
        kv_ref = next(it)
        kv_ref[0] = proj.astype(BF16)
    if add_moe:
        _request_rows_ahead(step, nsteps, pos_refs, moe_hbm, moe_bufs, moe_sems)


def _mod_spec(arr):
    if arr.shape[0] > 1:
        return pl.BlockSpec((1, 1, arr.shape[2]), lambda b, i: (b, 0, 0))
    return pl.BlockSpec((1, 1, arr.shape[2]), lambda b, i: (0, 0, 0))


ROW_LOOKAHEAD = 2
ROW_BUFFERS = ROW_LOOKAHEAD + 1


def _gathered_rows(step, pos_refs, src_hbm, bufs, sems):
    nrows = bufs.shape[1]

    @pl.when(step == 0)
    def _():
        for ahead in range(ROW_LOOKAHEAD):
            def body(r, carry, ahead=ahead):
                pltpu.make_async_copy(src_hbm.at[pl.ds(pos_refs[ahead][0, 0, r], 1)],
                                      bufs.at[ahead, pl.ds(r, 1)], sems.at[ahead]).start()
                return carry
            lax.fori_loop(0, nrows, body, 0, unroll=8)

    slot = lax.rem(step, ROW_BUFFERS)
    pltpu.make_async_copy(src_hbm.at[pl.ds(0, nrows)], bufs.at[slot], sems.at[slot]).wait()
    return bufs[slot]


def _request_rows_ahead(step, nsteps, pos_refs, src_hbm, bufs, sems):
    nrows = bufs.shape[1]
    slot = lax.rem(step + ROW_LOOKAHEAD, ROW_BUFFERS)
    ahead_pos = pos_refs[ROW_LOOKAHEAD]
    for r in range(nrows):
        pltpu.make_async_copy(src_hbm.at[pl.ds(ahead_pos[0, 0, r], 1)], bufs.at[slot, pl.ds(r, 1)],
                              sems.at[slot]).start(priority=r % 2)

    @pl.when(step == nsteps - 1)
    def _():
        for ahead in range(1, ROW_BUFFERS):
            pending = lax.rem(step + ahead, ROW_BUFFERS)
            pltpu.make_async_copy(src_hbm.at[pl.ds(0, nrows)], bufs.at[pending], sems.at[pending]).wait()


def _pos_tiles(pos, tm, nt, row0):
    tile0 = row0 // tm
    ntiles = pos.shape[0] // tm
    specs = [pl.BlockSpec((1, 1, tm),
                          lambda bb, i, ahead=ahead: (jnp.minimum(tile0 + bb * nt + i + ahead, ntiles - 1), 0, 0),
                          memory_space=pltpu.SMEM) for ahead in range(ROW_BUFFERS)]
    return pos.reshape(ntiles, 1, tm), specs


def _inproj_call(h, g, shift, scale, w, *, mode, tm, moe=None, pos=None, moe_row0=0, gate=None, extra=()):
    b, n, d = h.shape
    nout = w.shape[1]
    add_moe = moe is not None
    row = lambda bb, i: (bb, i, 0)
    full2 = lambda bb, i: (0, 0)
    args = [h]
    in_specs = [pl.BlockSpec((1, tm, d), row)]
    scratch = []
    if add_moe:
        pos_tiles, pos_specs = _pos_tiles(pos, tm, n // tm, moe_row0)
        args += [pos_tiles] * ROW_BUFFERS + [moe, gate]
        in_specs += pos_specs + [pl.BlockSpec(memory_space=pl.ANY), _mod_spec(gate)]
        scratch = [pltpu.VMEM((ROW_BUFFERS, tm, d), F32), pltpu.SemaphoreType.DMA((ROW_BUFFERS,))]
    args += [g.reshape(1, d), shift, scale, w]
    in_specs += [pl.BlockSpec((1, d), full2), _mod_spec(shift), _mod_spec(scale),
                 pl.BlockSpec((d, nout), full2)]
    if mode == "even":
        args += list(extra)
        in_specs += [pl.BlockSpec(extra[0].shape, full2)]
        out_shape = [jax.ShapeDtypeStruct((b, n, FOURIER_WIDTH), BF16),
                     jax.ShapeDtypeStruct((b, n, FOURIER_WIDTH), BF16),
                     jax.ShapeDtypeStruct((b, n, CONV_WIDTH), F32),
                     jax.ShapeDtypeStruct((b, n, CONV_WIDTH), F32)]
        out_specs = [pl.BlockSpec((1, tm, FOURIER_WIDTH), row)] * 2 + [pl.BlockSpec((1, tm, CONV_WIDTH), row)] * 2
    elif mode == "odd":
        args += list(extra)
        in_specs += [pl.BlockSpec((tm, LANES), lambda bb, i: (i, 0))] * 4
        out_shape = [jax.ShapeDtypeStruct((b, n, d), F32),
                     jax.ShapeDtypeStruct((b, n, POOL_WIDTH), F32),
                     jax.ShapeDtypeStruct((b, n, ATTN_WIDTH), BF16),
                     jax.ShapeDtypeStruct((b, n, 2 * KV_WIDTH), BF16),
                     jax.ShapeDtypeStruct((b, n, 2 * KV_WIDTH), BF16)]
        out_specs = [pl.BlockSpec((1, tm, d), row), pl.BlockSpec((1, tm, POOL_WIDTH), row),
                     pl.BlockSpec((1, tm, ATTN_WIDTH), row),
                     pl.BlockSpec((1, tm, 2 * KV_WIDTH), row), pl.BlockSpec((1, tm, 2 * KV_WIDTH), row)]
    else:
        out_shape = [jax.ShapeDtypeStruct((b, n, nout), BF16)]
        out_specs = [pl.BlockSpec((1, tm, nout), row)]
    return pl.pallas_call(
        functools.partial(_inproj_kernel, mode=mode, add_moe=add_moe),
        grid=(b, n // tm),
        in_specs=in_specs,
        out_specs=out_specs,
        out_shape=out_shape,
        scratch_shapes=scratch,
        compiler_params=_cparams(("arbitrary", "arbitrary")),
        name="inproj_" + mode,
    )(*args)


DFT_RADIX = 64


BF16_SUBLANES = 16


def _fold_kernel(rev_ref, uc_ref, ucm_ref, ucx_ref, us_ref, usm_ref, usx_ref, ue_ref, uo_ref):
    j = pl.program_id(1)
    tf = uc_ref.shape[1]
    first = lax.broadcasted_iota(jnp.int32, (tf, 1), 0) == 0

    def mirrored(m_ref, x_ref):
        return jnp.where(first, x_ref[0, 0:1, :].astype(F32), _dot(rev_ref[...], m_ref[0]))

    weight = jnp.where(first, jnp.where(j == 0, 0.5, 1.0), 1.0)
    ue_ref[0] = (weight * (uc_ref[0].astype(F32) + mirrored(ucm_ref, ucx_ref))).astype(BF16)
    uo_ref[0] = (us_ref[0].astype(F32) - mirrored(usm_ref, usx_ref)).astype(BF16)


def _fold_call(uc, us):
    b, n, wdt = uc.shape
    half = n // 2
    tf = min(512, half)
    nblk = n // tf
    rev = np.zeros((tf, tf), np.float32)
    rev[np.arange(1, tf), tf - np.arange(1, tf)] = 1.0
    direct = pl.BlockSpec((1, tf, wdt), lambda bb, j: (bb, j, 0))
    mirror = pl.BlockSpec((1, tf, wdt), lambda bb, j: (bb, nblk - 1 - j, 0))
    extra = pl.BlockSpec((1, BF16_SUBLANES, wdt),
                         lambda bb, j: (bb, jnp.where(j == 0, 0, (n - j * tf) // BF16_SUBLANES), 0))
    out = pl.BlockSpec((1, tf, wdt), lambda bb, j: (bb, j, 0))
    return pl.pallas_call(
        _fold_kernel,
        grid=(b, half // tf),
        in_specs=[pl.BlockSpec((tf, tf), lambda bb, j: (0, 0)), direct, mirror, extra, direct, mirror, extra],
        out_specs=[out, out],
        out_shape=[jax.ShapeDtypeStruct((b, half, wdt), BF16)] * 2,
        compiler_params=_cparams(("arbitrary", "arbitrary")),
        name="dft_fold",
    )(jnp.asarray(rev, BF16), uc, uc, uc, us, us, us)


def _dft_kernel(ca_ref, sa_ref, cb_ref, sb_ref, ue_ref, uo_ref, mid_ref, o_ref, c_scr, s_scr, *, norm):
    @pl.when(pl.program_id(1) == 0)
    def _():
        cb, sb = cb_ref[...], sb_ref[...]
        for r in range(ca_ref.shape[0]):
            ca, sa = ca_ref[r:r + 1, :], sa_ref[r:r + 1, :]
            rows = slice(r * DFT_RADIX, (r + 1) * DFT_RADIX)
            c_scr[rows, :] = (ca * cb - sa * sb).astype(BF16)
            s_scr[rows, :] = (-(sa * cb + ca * sb)).astype(BF16)

    tm = o_ref.shape[1]
    acc = _dot(c_scr[...], ue_ref[0]) + _dot(s_scr[...], uo_ref[0])
    k = pl.program_id(0) * tm + lax.broadcasted_iota(jnp.int32, (tm, 1), 0)
    sign = jnp.where((k & 1) == 0, 1.0, -1.0)
    acc = acc + sign * mid_ref[0, 0:1, :].astype(F32)
    o_ref[0] = (acc * norm).astype(BF16)


def _dft_call(tabs, uc, us, *, tm):
    b, n, wdt = uc.shape
    half = n // 2
    ue, uo = _fold_call(uc, us)
    r_tile = tm // DFT_RADIX
    norm = 1.0 / math.sqrt(n * FOURIER_HEAD_DIM)
    a_spec = pl.BlockSpec((r_tile, half), lambda i, bb: (i, 0))
    b_spec = pl.BlockSpec((DFT_RADIX, half), lambda i, bb: (0, 0))
    u_spec = pl.BlockSpec((1, half, wdt), lambda i, bb: (bb, 0, 0))
    mid_spec = pl.BlockSpec((1, BF16_SUBLANES, wdt), lambda i, bb: (bb, half // BF16_SUBLANES, 0))
    return pl.pallas_call(
        functools.partial(_dft_kernel, norm=norm),
        grid=(n // tm, b),
        in_specs=[a_spec, a_spec, b_spec, b_spec, u_spec, u_spec, mid_spec],
        out_specs=pl.BlockSpec((1, tm, wdt), lambda i, bb: (bb, i, 0)),
        out_shape=jax.ShapeDtypeStruct((b, n, wdt), BF16),
        scratch_shapes=[pltpu.VMEM((tm, half), BF16), pltpu.VMEM((tm, half), BF16)],
        compiler_params=_cparams(("arbitrary", "arbitrary")),
        name="dft_rows",
    )(*tabs, ue, uo, uc)


def _tail(y, h_ref, gate_ref, g2_ref, sh_ref, sc_ref, rwt_ref, rb_ref, hout_ref, fx_ref, bk_ref):
    hn = h_ref[0] + gate_ref[0] * y
    hout_ref[0] = hn
    f = _rms_mod(hn, g2_ref[...], sh_ref[0], sc_ref[0])
    tm = f.shape[0]
    f_hi = f.astype(BF16)
    f_lo = (f - f_hi.astype(F32)).astype(BF16)
    both = _dot_nt(rwt_ref[...], f_hi)
    logits = (both[:N_EXPERTS] + both[N_EXPERTS:]) + _dot_nt(rwt_ref[:N_EXPERTS, :], f_lo)
    aff = jax.nn.sigmoid(logits)
    sel = aff + rb_ref[...]
    cands = []
    for bkt in range(N_BUCKETS):
        lo, hi = _BUCKET_LO[bkt], _BUCKET_HI[bkt]
        cands.append((sel[lo:lo + 1, :] + sel[hi:hi + 1, :], jnp.full((1, tm), float(bkt), F32),
                      aff[lo:lo + 1, :], aff[hi:hi + 1, :]))
    while len(cands) > 1:
        merged = []
        for k in range(0, len(cands) - 1, 2):
            left, right = cands[k], cands[k + 1]
            take_right = right[0] > left[0]
            merged.append(tuple(jnp.where(take_right, r, l) for l, r in zip(left, right)))
        if len(cands) % 2:
            merged.append(cands[-1])
        cands = merged
    _, bucket, a_lo, a_hi = cands[0]
    den = a_lo + a_hi
    info = jnp.concatenate([bucket, a_lo / den, a_hi / den, jnp.zeros((LANES - 3, tm), F32)], axis=0)
    fx_ref[0, :, :D_MODEL] = f
    fx_ref[0, :, D_MODEL:] = info.T
    bk_ref[...] = jnp.concatenate([bucket, jnp.zeros((F32_SUBLANES - 1, tm), F32)], axis=0)


def _tail_specs(h, gate, shift, scale, tm):
    b, n, d = h.shape
    nt = n // tm
    row = lambda bb, i: (bb, i, 0)
    in_specs = [pl.BlockSpec((1, tm, d), row), _mod_spec(gate),
                pl.BlockSpec((1, d), lambda bb, i: (0, 0)), _mod_spec(shift), _mod_spec(scale),
                pl.BlockSpec((2 * N_EXPERTS, d), lambda bb, i: (0, 0)),
                pl.BlockSpec((N_EXPERTS, 1), lambda bb, i: (0, 0))]
    out_shape = [jax.ShapeDtypeStruct((b, n, d), F32), jax.ShapeDtypeStruct((b, n, ROW_WIDTH), F32),
                 jax.ShapeDtypeStruct((F32_SUBLANES, b * n), F32)]
    out_specs = [pl.BlockSpec((1, tm, d), row), pl.BlockSpec((1, tm, ROW_WIDTH), row),
                 pl.BlockSpec((F32_SUBLANES, tm), lambda bb, i: (0, bb * nt + i))]
    return in_specs, out_shape, out_specs


def _even_out_kernel(yf_ref, gb_ref, gu_ref, gp_ref, gn_ref, cw_ref, wo_ref,
                     h_ref, gate_ref, g2_ref, sh_ref, sc_ref, rwt_ref, rb_ref,
                     hout_ref, fx_ref, bk_ref):
    i = pl.program_id(1)
    last = pl.num_programs(1) - 1
    gu = gu_ref[0]
    tm = gu.shape[0]
    prev = jnp.where(i > 0, gp_ref[0, F32_SUBLANES - 1:F32_SUBLANES, :], 0.0)
    nxt = jnp.where(i < last, gn_ref[0, 0:1, :], 0.0)
    row = lax.broadcasted_iota(jnp.int32, gu.shape, 0)
    up = jnp.where(row == 0, prev, pltpu.roll(gu, 1, 0))
    dn = jnp.where(row == tm - 1, nxt, pltpu.roll(gu, tm - 1, 0))
    conv = up * cw_ref[0:1, :] + gu * cw_ref[1:2, :] + dn * cw_ref[2:3, :]
    yc = (gb_ref[0] * conv).astype(BF16)
    y = _dot(yf_ref[0], wo_ref[:FOURIER_WIDTH, :]) + _dot(yc, wo_ref[FOURIER_WIDTH:, :])
    _tail(y, h_ref, gate_ref, g2_ref, sh_ref, sc_ref, rwt_ref, rb_ref, hout_ref, fx_ref, bk_ref)


def _even_out_call(yf, gb, gu, conv_w, w_out, h, gate, g2, shift, scale, rwt, rb, *, tm):
    b, n, d = h.shape
    row = lambda bb, i: (bb, i, 0)
    nb8 = n // F32_SUBLANES
    t8 = tm // F32_SUBLANES
    tin, out_shape, out_specs = _tail_specs(h, gate, shift, scale, tm)
    halo = (1, F32_SUBLANES, CONV_WIDTH)
    in_specs = [pl.BlockSpec((1, tm, FOURIER_WIDTH), row), pl.BlockSpec((1, tm, CONV_WIDTH), row),
                pl.BlockSpec((1, tm, CONV_WIDTH), row),
                pl.BlockSpec(halo, lambda bb, i: (bb, jnp.maximum(i * t8 - 1, 0), 0)),
                pl.BlockSpec(halo, lambda bb, i: (bb, jnp.minimum((i + 1) * t8, nb8 - 1), 0)),
                pl.BlockSpec((CONV_K, CONV_WIDTH), lambda bb, i: (0, 0)),
                pl.BlockSpec((d, d), lambda bb, i: (0, 0))] + tin
    return pl.pallas_call(
        _even_out_kernel,
        grid=(b, n // tm),
        in_specs=in_specs,
        out_specs=out_specs,
        out_shape=out_shape,
        compiler_params=_cparams(("arbitrary", "arbitrary")),
        name="even_out",
    )(yf, gb, gu, gu, gu, conv_w, w_out, h, gate, g2.reshape(1, d), shift, scale, rwt, rb)


def _odd_out_kernel(sink_ref, up_ref, upp_ref, upn_ref, q_ref, kc_ref, kp_ref, kn_ref,
                    vc_ref, vp_ref, vn_ref, kvx_ref, pw_ref, ps_ref, wo_ref,
                    h_ref, gate_ref, g2_ref, sh_ref, sc_ref, rwt_ref, rb_ref,
                    hout_ref, fx_ref, bk_ref, ext_ref, mix_ref, *, n_total):
    i = pl.program_id(1)
    last = pl.num_programs(1) - 1
    tq = q_ref.shape[1]
    nsub = tq // ATTN_BLOCK

    u = up_ref[0]
    ext_ref[0:POOL_HALO, :] = jnp.where(i > 0, upp_ref[0], 0.0)
    ext_ref[POOL_HALO:POOL_HALO + tq, :] = u
    ext_ref[POOL_HALO + tq:, :] = jnp.where(i < last, upn_ref[0], 0.0)
    t = i * tq + lax.broadcasted_iota(jnp.int32, (tq, LANES), 0)
    ext_rows = tq + 2 * POOL_HALO
    for gi, win in enumerate(POOL_WINDOWS):
        r = win // 2
        cols = slice(gi * LANES, (gi + 1) * LANES)
        x = ext_ref[:, cols]
        ahead = lambda a, k: pltpu.roll(a, ext_rows - k, 0)
        run = x
        span = 1
        while span < r:
            run = run + ahead(run, span)
            span *= 2
        win_sum = pltpu.roll(run, r, 0) + run + ahead(x, r)
        acc = win_sum[POOL_HALO:POOL_HALO + tq]
        cnt = (jnp.minimum(t + r + 1, n_total) - jnp.maximum(t - r, 0)).astype(F32)
        p = acc / cnt - u[:, cols]
        y = _dot(p.astype(BF16), pw_ref[gi]) * ps_ref[:, cols]
        mix_ref[:, cols] = y.astype(BF16)

    kwin = jnp.concatenate([kp_ref[0], kc_ref[0], kn_ref[0]], axis=0)
    vwin = jnp.concatenate([vp_ref[0], vc_ref[0], vn_ref[0]], axis=0)
    kvx = kvx_ref[0]
    kx, vx = kvx[:, :2 * KV_WIDTH], kvx[:, 2 * KV_WIDTH:]
    low = lax.broadcasted_iota(jnp.int32, (1, LANES), 1) < HEAD_DIM
    zero = jnp.zeros((), BF16)

    span = 3 * ATTN_BLOCK
    rows4 = 4 * ATTN_BLOCK
    qi = lax.broadcasted_iota(jnp.int32, (rows4, ATTN_BLOCK), 0) % ATTN_BLOCK
    kj = lax.broadcasted_iota(jnp.int32, (rows4, ATTN_BLOCK), 1)
    neg_inf = jnp.float32(-jnp.inf)
    prev_band = jnp.where(kj < qi, neg_inf, 0.0)
    next_band = jnp.where(kj > qi, neg_inf, 0.0)
    row_block = jnp.right_shift(lax.broadcasted_iota(jnp.int32, (rows4, 1), 0), int(math.log2(ATTN_BLOCK)))

    for j in range(nsub):
        blk = i * nsub + j
        prev_bias = prev_band + jnp.where(blk > 0, 0.0, neg_inf)
        next_bias = next_band + jnp.where(blk < (n_total // ATTN_BLOCK) - 1, 0.0, neg_inf)
        r0 = j * ATTN_BLOCK
        for kh in range(N_KV_HEADS):
            kcols = slice(kh * LANES, (kh + 1) * LANES)
            qa = q_ref[0, r0:r0 + ATTN_BLOCK, (2 * kh) * LANES:(2 * kh + 1) * LANES]
            qb = q_ref[0, r0:r0 + ATTN_BLOCK, (2 * kh + 1) * LANES:(2 * kh + 2) * LANES]
            xq = jnp.concatenate([jnp.where(low, qa, zero), jnp.where(low, qb, zero),
                                  jnp.where(low, zero, qa), jnp.where(low, zero, qb)], axis=0)
            h0 = 4 * kh
            snk = LOG2_E * jnp.where(row_block == 0, sink_ref[h0],
                                     jnp.where(row_block == 1, sink_ref[h0 + 2],
                                               jnp.where(row_block == 2, sink_ref[h0 + 1], sink_ref[h0 + 3])))
            s1 = _dot_nt(xq, kwin[r0:r0 + span, kcols])
            s1 = jnp.concatenate([s1[:, :ATTN_BLOCK] + prev_bias, s1[:, ATTN_BLOCK:2 * ATTN_BLOCK],
                                  s1[:, 2 * ATTN_BLOCK:] + next_bias], axis=1)
            s2 = _dot_nt(xq, kx[:, kcols])
            m = jnp.maximum(jnp.maximum(jnp.max(s1, axis=-1, keepdims=True),
                                        jnp.max(s2, axis=-1, keepdims=True)), snk)
            e1 = jnp.exp2(s1 - m)
            e2 = jnp.exp2(s2 - m)
            den = (jnp.sum(e1, axis=-1, keepdims=True) + jnp.sum(e2, axis=-1, keepdims=True)
                   + jnp.exp2(snk - m))
            o = _dot(e1.astype(BF16), vwin[r0:r0 + span, kcols]) + _dot(e2.astype(BF16), vx[:, kcols])
            o = o / den
            for pr in range(2):
                o_low = o[pr * ATTN_BLOCK:(pr + 1) * ATTN_BLOCK]
                o_high = o[(2 + pr) * ATTN_BLOCK:(3 + pr) * ATTN_BLOCK]
                c0 = POOL_WIDTH + (2 * kh + pr) * LANES
                mix_ref[r0:r0 + ATTN_BLOCK, c0:c0 + LANES] = jnp.where(low, o_low, o_high).astype(BF16)

    y = _dot(mix_ref[...], wo_ref[...])
    _tail(y, h_ref, gate_ref, g2_ref, sh_ref, sc_ref, rwt_ref, rb_ref, hout_ref, fx_ref, bk_ref)


def _odd_out_call(sink, up, q, kd, vd, kvx, pool_w, pool_scale, w_out,
                  h, gate, g2, shift, scale, rwt, rb, *, tq):
    b, n, d = h.shape
    row = lambda bb, i: (bb, i, 0)
    nb8, t8 = n // POOL_HALO, tq // POOL_HALO
    nbk, tk = n // ATTN_BLOCK, tq // ATTN_BLOCK
    prev8 = lambda bb, i: (bb, jnp.maximum(i * t8 - 1, 0), 0)
    next8 = lambda bb, i: (bb, jnp.minimum((i + 1) * t8, nb8 - 1), 0)
    prevk = lambda bb, i: (bb, jnp.maximum(i * tk - 1, 0), 0)
    nextk = lambda bb, i: (bb, jnp.minimum((i + 1) * tk, nbk - 1), 0)
    kvw = 2 * KV_WIDTH
    tin, out_shape, out_specs = _tail_specs(h, gate, shift, scale, tq)
    in_specs = [pl.BlockSpec(memory_space=pltpu.SMEM),
                pl.BlockSpec((1, tq, POOL_WIDTH), row),
                pl.BlockSpec((1, POOL_HALO, POOL_WIDTH), prev8),
                pl.BlockSpec((1, POOL_HALO, POOL_WIDTH), next8),
                pl.BlockSpec((1, tq, ATTN_WIDTH), row),
                pl.BlockSpec((1, tq, kvw), row),
                pl.BlockSpec((1, ATTN_BLOCK, kvw), prevk),
                pl.BlockSpec((1, ATTN_BLOCK, kvw), nextk),
                pl.BlockSpec((1, tq, kvw), row),
                pl.BlockSpec((1, ATTN_BLOCK, kvw), prevk),
                pl.BlockSpec((1, ATTN_BLOCK, kvw), nextk),
                pl.BlockSpec((1, kvx.shape[1], 2 * kvw), lambda bb, i: (bb, 0, 0)),
                pl.BlockSpec(pool_w.shape, lambda bb, i: (0, 0, 0)),
                pl.BlockSpec((1, POOL_WIDTH), lambda bb, i: (0, 0)),
                pl.BlockSpec((d, d), lambda bb, i: (0, 0))] + tin
    return pl.pallas_call(
        functools.partial(_odd_out_kernel, n_total=n),
        grid=(b, n // tq),
        in_specs=in_specs,
        out_specs=out_specs,
        out_shape=out_shape,
        scratch_shapes=[pltpu.VMEM((tq + 2 * POOL_HALO, POOL_WIDTH), F32),
                        pltpu.VMEM((tq, d), BF16)],
        compiler_params=_cparams(("arbitrary", "arbitrary")),
        name="odd_out",
    )(sink, up, up, up, q, kd, kd, kd, vd, vd, vd, kvx, pool_w, pool_scale.reshape(1, POOL_WIDTH),
      w_out, h, gate, g2.reshape(1, d), shift, scale, rwt, rb)


def _moe_kernel(e_lo_ref, e_hi_ref, valid_ref, x_ref,
                g1_ref, u1_ref, d1_ref, g2_ref, u2_ref, d2_ref, o_ref):
    j = pl.program_id(0)

    @pl.when(valid_ref[j] != 0)
    def _():
        x = x_ref[:, :D_MODEL].astype(BF16)
        w_lo = x_ref[:, D_MODEL + INFO_W_LO:D_MODEL + INFO_W_LO + 1]
        w_hi = x_ref[:, D_MODEL + INFO_W_HI:D_MODEL + INFO_W_HI + 1]

        def expert(g_ref, u_ref, d_ref):
            gate = _dot(x, g_ref[0, 0].astype(BF16))
            hid = gate * jax.nn.sigmoid(gate) * _dot(x, u_ref[0, 0].astype(BF16))
            return _dot(hid.astype(BF16), d_ref[0, 0].astype(BF16))

        o_lo = expert(g1_ref, u1_ref, d1_ref)
        o_hi = expert(g2_ref, u2_ref, d2_ref)
        o_ref[...] = w_lo * o_lo + w_hi * o_hi

    @pl.when(valid_ref[j] == 0)
    def _():
        o_ref[...] = jnp.zeros(o_ref.shape, o_ref.dtype)


def _moe_call(tile_lo, tile_hi, tile_valid, xs, wg, wu, wd, *, layer, tm):
    p = xs.shape[0]
    d = D_MODEL
    ntiles = p // tm
    lo4 = lambda j, lo, hi, v: (layer, lo[j], 0, 0)
    hi4 = lambda j, lo, hi, v: (layer, hi[j], 0, 0)
    rowm = lambda j, lo, hi, v: (j, 0)
    row_in = lambda j, lo, hi, v: (jnp.where(v[j] != 0, j, 0), 0)
    grid_spec = pltpu.PrefetchScalarGridSpec(
        num_scalar_prefetch=3,
        grid=(ntiles,),
        in_specs=[pl.BlockSpec((tm, ROW_WIDTH), row_in),
                  pl.BlockSpec((1, 1, d, D_EXPERT), lo4), pl.BlockSpec((1, 1, d, D_EXPERT), lo4),
                  pl.BlockSpec((1, 1, D_EXPERT, d), lo4),
                  pl.BlockSpec((1, 1, d, D_EXPERT), hi4), pl.BlockSpec((1, 1, d, D_EXPERT), hi4),
                  pl.BlockSpec((1, 1, D_EXPERT, d), hi4)],
        out_specs=pl.BlockSpec((tm, d), rowm),
    )
    return pl.pallas_call(
        _moe_kernel,
        grid_spec=grid_spec,
        out_shape=jax.ShapeDtypeStruct((p, d), F32),
        compiler_params=_cparams(("arbitrary",)),
        name="moe_pairs",
    )(tile_lo, tile_hi, tile_valid, xs, wg, wu, wd, wg, wu, wd)


RANK_ROWS = 32


def _rank_kernel(bk_ref, tri_ref, rk_ref, cnt_ref, carry_ref):
    @pl.when(pl.program_id(0) == 0)
    def _():
        carry_ref[...] = jnp.zeros(carry_ref.shape, F32)

    tr = bk_ref.shape[1]
    bucket = bk_ref[0:1, :]
    rows = lax.broadcasted_iota(jnp.int32, (RANK_ROWS, tr), 0).astype(F32)
    onehot = jnp.where(rows == bucket, 1.0, 0.0)
    before = _dot(onehot.astype(BF16), tri_ref[...]) + carry_ref[:, 0:1]
    rank = jnp.sum(onehot * before, axis=0, keepdims=True)
    rk_ref[...] = jnp.broadcast_to(rank, rk_ref.shape)
    carry_ref[...] = carry_ref[...] + jnp.sum(onehot, axis=1, keepdims=True)
    cnt_ref[...] = carry_ref[...]


def _rank_call(bk, *, tr):
    t = bk.shape[1]
    tri = jnp.asarray(np.triu(np.ones((tr, tr), np.float32), 1), BF16)
    return pl.pallas_call(
        _rank_kernel,
        grid=(t // tr,),
        in_specs=[pl.BlockSpec((F32_SUBLANES, tr), lambda j: (0, j)), pl.BlockSpec((tr, tr), lambda j: (0, 0))],
        out_specs=[pl.BlockSpec((F32_SUBLANES, tr), lambda j: (0, j)),
                   pl.BlockSpec((RANK_ROWS, LANES), lambda j: (0, 0))],
        out_shape=[jax.ShapeDtypeStruct((F32_SUBLANES, t), F32), jax.ShapeDtypeStruct((RANK_ROWS, LANES), F32)],
        scratch_shapes=[pltpu.VMEM((RANK_ROWS, LANES), F32)],
        compiler_params=_cparams(("arbitrary",)),
        name="bucket_rank",
    )(bk, tri)


ROW_TILE = 512


SCATTER_STAGES = 3


def _scatter_rows_kernel(tile_end_ref, pos_ref, *refs, tile_starts, tm):
    nsrc = len(tile_starts)
    srcs, out_hbm = refs[:nsrc], refs[nsrc]
    zero_ref, stage, load_sems, row_sems, fill_sem = refs[nsrc + 1:]
    j = pl.program_id(0)
    nsteps = pl.num_programs(0)

    def load_start(tile):
        slot = lax.rem(tile, SCATTER_STAGES)
        for s in range(nsrc):
            def go(s=s):
                row0 = pl.multiple_of((tile - tile_starts[s]) * ROW_TILE, ROW_TILE)
                pltpu.make_async_copy(srcs[s].at[pl.ds(row0, ROW_TILE)], stage.at[slot],
                                      load_sems.at[slot]).start()
            if nsrc == 1:
                go()
            else:
                upper = tile < tile_starts[s + 1] if s + 1 < nsrc else True
                pl.when((tile >= tile_starts[s]) & upper)(go)

    def rows_wait(tile):
        slot = lax.rem(tile, SCATTER_STAGES)
        pltpu.make_async_copy(stage.at[slot], out_hbm.at[pl.ds(0, ROW_TILE)], row_sems.at[slot]).wait()

    @pl.when(j == 0)
    def _():
        zero_ref[...] = jnp.zeros(zero_ref.shape, F32)

        def fill(tile, start):
            copy = pltpu.make_async_copy(zero_ref, out_hbm.at[pl.ds(tile * tm, tm)], fill_sem)
            copy.start() if start else copy.wait()

        n_slot_tiles = out_hbm.shape[0] // tm
        used = tile_end_ref[N_BUCKETS - 1]
        for start in (True, False):
            for b in range(N_BUCKETS):
                first_tile = tile_end_ref[b - 1] if b else 0
                pl.when(tile_end_ref[b] > first_tile)(functools.partial(fill, tile_end_ref[b] - 1, start))
                pl.when(n_slot_tiles - 1 - b >= used)(functools.partial(fill, n_slot_tiles - 1 - b, start))
        load_start(j)

    pl.when(j + 1 < nsteps)(functools.partial(load_start, j + 1))

    slot = lax.rem(j, SCATTER_STAGES)
    pltpu.make_async_copy(srcs[0].at[pl.ds(0, ROW_TILE)], stage.at[slot], load_sems.at[slot]).wait()
    for r in range(ROW_TILE):
        pltpu.make_async_copy(stage.at[slot, pl.ds(r, 1)], out_hbm.at[pl.ds(pos_ref[0, 0, r], 1)],
                              row_sems.at[slot]).start(priority=r % 2)

    pl.when(j >= 1)(functools.partial(rows_wait, j - 1))
    pl.when(j == nsteps - 1)(functools.partial(rows_wait, j))


def _scatter_rows_call(pos, tile_end, sources, nslots, tm):
    w = sources[0].shape[1]
    t = pos.shape[0]
    ntiles = t // ROW_TILE
    tile_starts, acc = [], 0
    for s in sources:
        tile_starts.append(acc)
        acc += s.shape[0] // ROW_TILE
    any_spec = pl.BlockSpec(memory_space=pl.ANY)
    grid_spec = pltpu.PrefetchScalarGridSpec(
        num_scalar_prefetch=1,
        grid=(ntiles,),
        in_specs=[pl.BlockSpec((1, 1, ROW_TILE), lambda j, te: (j, 0, 0), memory_space=pltpu.SMEM)]
        + [any_spec] * len(sources),
        out_specs=any_spec,
        scratch_shapes=[pltpu.VMEM((tm, w), F32), pltpu.VMEM((SCATTER_STAGES, ROW_TILE, w), F32),
                        pltpu.SemaphoreType.DMA((SCATTER_STAGES,)), pltpu.SemaphoreType.DMA((SCATTER_STAGES,)),
                        pltpu.SemaphoreType.DMA(())],
    )
    return pl.pallas_call(
        functools.partial(_scatter_rows_kernel, tile_starts=tuple(tile_starts), tm=tm),
        grid_spec=grid_spec,
        out_shape=jax.ShapeDtypeStruct((nslots, w), F32),
        compiler_params=_cparams(("arbitrary",)),
        name="scatter_rows",
    )(tile_end, pos.reshape(ntiles, 1, ROW_TILE), *sources)


def _final_kernel(h_ref, *refs):
    pos_refs = refs[:ROW_BUFFERS]
    moe_hbm, gate_ref, g_ref, o_ref, moe_bufs, moe_sems = refs[ROW_BUFFERS:]
    step = pl.program_id(0) * pl.num_programs(1) + pl.program_id(1)
    nsteps = pl.num_programs(0) * pl.num_programs(1)
    x = h_ref[0] + gate_ref[0] * _gathered_rows(step, pos_refs, moe_hbm, moe_bufs, moe_sems)
    o_ref[0] = x * lax.rsqrt(jnp.mean(x * x, axis=-1, keepdims=True) + EPS) * g_ref[...]
    _request_rows_ahead(step, nsteps, pos_refs, moe_hbm, moe_bufs, moe_sems)


def _final_call(h, moe, pos, gate, g, *, tm):
    b, n, d = h.shape
    row = lambda bb, i: (bb, i, 0)
    pos_tiles, pos_specs = _pos_tiles(pos, tm, n // tm, 0)
    return pl.pallas_call(
        _final_kernel,
        grid=(b, n // tm),
        in_specs=[pl.BlockSpec((1, tm, d), row)] + pos_specs + [
            pl.BlockSpec(memory_space=pl.ANY), _mod_spec(gate), pl.BlockSpec((1, d), lambda bb, i: (0, 0))],
        out_specs=pl.BlockSpec((1, tm, d), row),
        out_shape=jax.ShapeDtypeStruct((b, n, d), F32),
        scratch_shapes=[pltpu.VMEM((ROW_BUFFERS, tm, d), F32), pltpu.SemaphoreType.DMA((ROW_BUFFERS,))],
        compiler_params=_cparams(("arbitrary", "arbitrary")),
        name="final_norm",
    )(h, *([pos_tiles] * ROW_BUFFERS), moe, gate, g.reshape(1, d))


def _channel_dft_table():
    c = np.arange(FOURIER_HEAD_DIM)
    ang = 2.0 * np.pi * ((c[:, None] * c[None, :]) % FOURIER_HEAD_DIM) / FOURIER_HEAD_DIM
    return jnp.asarray(np.concatenate([np.cos(ang), np.sin(ang)], axis=1), F32)


def _position_dft_tables(n):
    n1 = n // DFT_RADIX
    t = np.arange(n // 2)
    a = 2.0 * np.pi * ((np.arange(n1)[:, None] * t[None, :]) % n1) / n1
    bb = 2.0 * np.pi * ((np.arange(DFT_RADIX)[:, None] * t[None, :]) % n) / n
    return tuple(jnp.asarray(v, F32) for v in (np.cos(a), np.sin(a), np.cos(bb), np.sin(bb)))


def _rope_tables(n):
    quarter = HEAD_DIM // 4
    inv = ROPE_THETA ** (-jnp.arange(quarter, dtype=F32) / quarter)
    t = jnp.arange(n)
    ang_r = (t // GRID_W).astype(F32)[:, None] * inv
    ang_c = (t % GRID_W).astype(F32)[:, None] * inv
    cos = jnp.concatenate([jnp.cos(ang_r)] * 2 + [jnp.cos(ang_c)] * 2, axis=1)
    sin = jnp.concatenate([-jnp.sin(ang_r), jnp.sin(ang_r), -jnp.sin(ang_c), jnp.sin(ang_c)], axis=1)
    return jnp.tile(cos, (1, 2)), jnp.tile(sin, (1, 2))


def _dispatch_plan(bucket, rank, counts, tm):
    t = bucket.shape[0]
    ntiles = t // tm + N_BUCKETS
    tiles_per = (counts + tm - 1) // tm
    tile_end = jnp.cumsum(tiles_per)
    tile_start = tile_end - tiles_per
    onehot = bucket[:, None] == jnp.arange(N_BUCKETS, dtype=jnp.int32)[None, :]
    pos = jnp.sum(jnp.where(onehot, (tile_start * tm)[None, :], 0), axis=-1) + rank
    tile_ids = jnp.arange(ntiles, dtype=jnp.int32)
    used = tile_end[-1]
    tile_bucket = jnp.sum((tile_ids[:, None] >= tile_end[None, :]).astype(jnp.int32), axis=1)
    last_bucket = jnp.sum((jnp.maximum(used - 1, 0) >= tile_end).astype(jnp.int32))
    tile_valid = (tile_ids < used).astype(jnp.int32)
    tile_bucket = jnp.where(tile_valid == 1, tile_bucket, last_bucket)
    tile_lo = jnp.asarray(np.asarray(_BUCKET_LO, np.int32))[tile_bucket]
    tile_hi = jnp.asarray(np.asarray(_BUCKET_HI, np.int32))[tile_bucket]
    return pos.astype(jnp.int32), tile_end.astype(jnp.int32), tile_lo, tile_hi, tile_valid, ntiles * tm


def _moe_layer(fx_list, bk_list, wg, wu, wd, *, layer, tm):
    bk = bk_list[0] if len(bk_list) == 1 else jnp.concatenate(bk_list, axis=1)
    rk, cnt = _rank_call(bk, tr=ROW_TILE)
    pos, tile_end, tile_lo, tile_hi, tile_valid, nslots = _dispatch_plan(
        bk[0].astype(jnp.int32), rk[0].astype(jnp.int32), cnt[:N_BUCKETS, 0].astype(jnp.int32), tm)
    xs = _scatter_rows_call(pos, tile_end, fx_list, nslots, tm)
    return _moe_call(tile_lo, tile_hi, tile_valid, xs, wg, wu, wd, layer=layer, tm=tm), pos


def _forward(x, c, ctx, c_ctx, ada_w, ada_b, norm_mix_g, norm_ffn_g, even_w_in, even_conv_w, even_w_out,
             odd_w_in, odd_pool_w, odd_pool_scale, odd_sink, odd_w_out, router_w, router_b,
             moe_w_gate, moe_w_up, moe_w_down, final_g, *, tm_lat, tm_in_even, tm_ctx, tq, tm_dft, tm_moe):
    b, n, d = x.shape
    l = ctx.shape[1]

    rows = ((b + 1 + 7) // 8) * 8
    s_rows = jnp.zeros((rows, d), F32).at[:b].set(c).at[b].set(c_ctx)
    mods = _ada_call(s_rows, ada_w, ada_b)

    def mod_vecs(layer):
        m = mods[layer, :b].reshape(b, N_MOD, 1, d)
        mc = mods[layer, b].reshape(N_MOD, 1, 1, d)
        return [m[:, k] for k in range(N_MOD)], [mc[k] for k in range(N_MOD)]

    rw_t = router_w.T
    rw_hi = rw_t.astype(BF16)
    rw_lo = (rw_t - rw_hi.astype(F32)).astype(BF16)
    rwt = jnp.concatenate([rw_hi, rw_lo], axis=0)
    rb = router_b.astype(F32).reshape(N_EXPERTS, 1)
    cs_tab = _channel_dft_table()

    m, mc = mod_vecs(0)
    w_in0 = even_w_in[0].astype(BF16)
    w_out0 = even_w_out[0].astype(BF16)

    def even_stream(h, mv, tm_in, tm):
        nn = h.shape[1]
        uc, us, gb, gu = _inproj_call(h, norm_mix_g[0], mv[0], mv[1], w_in0, mode="even", tm=tm_in,
                                      extra=(cs_tab,))
        yf = _dft_call(_position_dft_tables(nn), uc, us, tm=min(tm_dft, nn))
        return _even_out_call(yf, gb, gu, even_conv_w[0], w_out0, h, mv[2], norm_ffn_g[0],
                              mv[3], mv[4], rwt, rb, tm=tm)

    h1, fx_lat, bk_lat = even_stream(x, m, tm_in_even, tm_lat)
    hc1, fx_ctx, bk_ctx = even_stream(ctx, mc, tm_ctx, tm_ctx)

    moe0, pos0 = _moe_layer([fx_lat.reshape(b * n, ROW_WIDTH), fx_ctx.reshape(b * l, ROW_WIDTH)],
                            [bk_lat, bk_ctx], moe_w_gate, moe_w_up, moe_w_down, layer=0, tm=tm_moe)
    gate_lat0, gate_ctx0 = m[5], mc[5]

    m, mc = mod_vecs(1)
    w_in1 = odd_w_in[0]
    kv0 = POOL_WIDTH + ATTN_WIDTH
    wk, wv = w_in1[:, kv0:kv0 + KV_WIDTH], w_in1[:, kv0 + KV_WIDTH:]

    def dup_heads(wm):
        return jnp.concatenate([wm[:, :HEAD_DIM], wm[:, :HEAD_DIM], wm[:, HEAD_DIM:], wm[:, HEAD_DIM:]], axis=1)

    w_kv_dup = jnp.concatenate([dup_heads(wk), dup_heads(wv)], axis=1)
    w_lat1 = jnp.concatenate([w_in1[:, :kv0], w_kv_dup], axis=1).astype(BF16)
    w_out1 = odd_w_out[0].astype(BF16)

    cos_t, sin_t = _rope_tables(n)
    q_scale = HEAD_DIM ** -0.5 * LOG2_E
    h1b, up, q, kd, vd = _inproj_call(h1, norm_mix_g[1], m[0], m[1], w_lat1, mode="odd", tm=tm_lat,
                                      moe=moe0, pos=pos0, moe_row0=0, gate=gate_lat0,
                                      extra=(cos_t * q_scale, sin_t * q_scale, cos_t, sin_t))
    (kvx,) = _inproj_call(hc1, norm_mix_g[1], mc[0], mc[1], w_kv_dup.astype(BF16), mode="plain", tm=tm_ctx,
                          moe=moe0, pos=pos0, moe_row0=b * n, gate=gate_ctx0)
    h2, fx2, bk2 = _odd_out_call(odd_sink[0], up, q, kd, vd, kvx, odd_pool_w[0].astype(BF16),
                                 odd_pool_scale[0], w_out1, h1b, m[2], norm_ffn_g[1], m[3], m[4], rwt, rb, tq=tq)
    moe1, pos1 = _moe_layer([fx2.reshape(b * n, ROW_WIDTH)], [bk2], moe_w_gate, moe_w_up, moe_w_down,
                            layer=1, tm=tm_moe)
    return _final_call(h2, moe1, pos1, m[5], final_g, tm=tm_lat)


def kernel(x, c, ctx, c_ctx, ada_w, ada_b, norm_mix_g, norm_ffn_g, even_w_in, even_conv_w, even_w_out,
           odd_w_in, odd_pool_w, odd_pool_scale, odd_sink, odd_w_out, router_w, router_b,
           moe_w_gate, moe_w_up, moe_w_down, final_g):
    return _forward(x, c, ctx, c_ctx, ada_w, ada_b, norm_mix_g, norm_ffn_g, even_w_in, even_conv_w,
                    even_w_out, odd_w_in, odd_pool_w, odd_pool_scale, odd_sink, odd_w_out, router_w,
                    router_b, moe_w_gate, moe_w_up, moe_w_down, final_g, **_tile_sizes(x.shape[1], ctx.shape[1]))


def _tile_sizes(n, l):
    return dict(
        tm_lat=min(ROW_TILE, n),
        tm_in_even=min(2 * ROW_TILE, n),
        tm_ctx=min(ROW_TILE // 2, l),
        tq=min(ROW_TILE, n),
        tm_dft=min(4 * ROW_TILE, n),
        tm_moe=ROW_TILE,
    )
```
